```python
import jax, jax.numpy as jnp
from jax import lax
import numpy as np

D_MODEL = 1024
BATCH = 2
SEQ = 16384
DEPTH = 4

GRID_W = 64
CTX_LEN = 256

N_GROUPS = 4
GROUP_W = D_MODEL // N_GROUPS
HEAD_DIM = GROUP_W // 4

SGU_HEADS = 4
SGU_CHUNK = 128
MLSTM_HEADS = 4
MLSTM_CHUNK = 128
QK_CONV = 3
ATT_HEADS = 4
ATT_KV_HEADS = 2
ATT_GROUP = ATT_HEADS // ATT_KV_HEADS
ATT_BLOCK = 128
WINDOW = 128
ROPE_BASE = 10000.0
AXIS_ROT = HEAD_DIM // 2
AXIS_FREQ = AXIS_ROT // 2
POOL_WINDOWS = (2, 4, 8, 16)
POOL_CH = GROUP_W // len(POOL_WINDOWS)
FFN_HIDDEN = -(-(8 * D_MODEL) // (3 * 256)) * 256

EPS = 1e-6

SGU_COLS = 2 * GROUP_W
MLSTM_COLS = 4 * GROUP_W + 4 * MLSTM_HEADS
ATT_Q = ATT_HEADS * HEAD_DIM
ATT_KV = ATT_KV_HEADS * HEAD_DIM
ATT_COLS = ATT_Q + 2 * ATT_KV
POOL_COLS = GROUP_W
OFF_MLSTM = SGU_COLS
OFF_ATT = OFF_MLSTM + MLSTM_COLS
OFF_POOL = OFF_ATT + ATT_COLS
IN_COLS = OFF_POOL + POOL_COLS

kernel_name = 'hybrid_parallel_groups_flow_backbone'

F32 = jnp.float32


def rms_norm(x, g):
    xf = x.astype(F32)
    y = xf * lax.rsqrt(jnp.mean(xf * xf, axis=-1, keepdims=True) + EPS)
    return (y * g.astype(F32)).astype(x.dtype)


def layer_norm(x):
    xf = x.astype(F32)
    mu = jnp.mean(xf, axis=-1, keepdims=True)
    xc = xf - mu
    return xc * lax.rsqrt(jnp.mean(xc * xc, axis=-1, keepdims=True) + EPS)


def modulate(h, shift, scale):
    return h * (1 + scale) + shift


def axial_rope_tables(n_tokens):
    rows = n_tokens // GRID_W
    row = jnp.broadcast_to(jnp.arange(rows)[:, None], (rows, GRID_W)).reshape(-1).astype(F32)
    col = jnp.broadcast_to(jnp.arange(GRID_W)[None, :], (rows, GRID_W)).reshape(-1).astype(F32)
    inv = jnp.power(ROPE_BASE, -jnp.arange(AXIS_FREQ, dtype=F32) * 2.0 / AXIS_ROT)
    ar = row[:, None] * inv
    ac = col[:, None] * inv
    ang = jnp.concatenate([ar, ar, ac, ac], axis=-1)
    return jnp.cos(ang), jnp.sin(ang)


def apply_rope(x, cos, sin):
    xs = x.reshape(x.shape[:-1] + (2, 2, AXIS_FREQ))
    rot = jnp.stack([-xs[..., 1, :], xs[..., 0, :]], axis=-2).reshape(x.shape)
    c = cos[None, :, None, :].astype(x.dtype)
    s = sin[None, :, None, :].astype(x.dtype)
    return x * c + rot * s


def spatial_gating(z, w_s, b_s):
    B, N, _ = z.shape
    z = jax.nn.gelu(z)
    u, v = z[..., :GROUP_W], z[..., GROUP_W:]
    nc = N // SGU_CHUNK
    vh = layer_norm(v.reshape(B, nc, SGU_CHUNK, SGU_HEADS, GROUP_W // SGU_HEADS)).astype(z.dtype)
    mixed = jnp.einsum('hts,bnshd->bnthd', w_s, vh) + b_s.T[None, None, :, :, None]
    return u * mixed.reshape(B, N, GROUP_W)


def short_conv(x, w):
    n = x.shape[1]
    p = QK_CONV // 2
    xp = jnp.pad(x, ((0, 0), (p, p), (0, 0)))
    out = xp[:, 0:n] * w[0]
    for j in range(1, QK_CONV):
        out = out + xp[:, j:j + n] * w[j]
    return out


def mlstm_chunk_scan(q, k, v, log_i, log_f, state):
    B, H, N, dh = q.shape
    nc = N // MLSTM_CHUNK

    def chunks(a):
        a = a.reshape((B, H, nc, MLSTM_CHUNK) + a.shape[3:])
        return jnp.moveaxis(a, 2, 0)

    tril = jnp.tril(jnp.ones((MLSTM_CHUNK, MLSTM_CHUNK), dtype=bool))

    def step(carry, xs):
        C, n, m = carry
        qc, kc, vc, li, lf = xs
        b = jnp.cumsum(lf, axis=-1)
        d_ts = jnp.where(tril, b[..., :, None] - b[..., None, :] + li[..., None, :], -jnp.inf)
        m_t = jnp.maximum(b + m[..., None], jnp.max(d_ts, axis=-1))
        inter = jnp.exp(b + m[..., None] - m_t)
        s = jnp.einsum('bhtd,bhsd->bhts', qc, kc) * jnp.exp(d_ts - m_t[..., None])
        num = inter[..., None] * jnp.einsum('bhtd,bhde->bhte', qc, C) + jnp.einsum('bhts,bhse->bhte', s, vc)
        den = inter * jnp.einsum('bhtd,bhd->bht', qc, n) + jnp.sum(s, axis=-1)
        h = num / jnp.maximum(jnp.abs(den), jnp.exp(-m_t))[..., None]
        b_end = b[..., -1]
        g = b_end[..., None] - b + li
        m_new = jnp.maximum(b_end + m, jnp.max(g, axis=-1))
        decay = jnp.exp(b_end + m - m_new)
        w = jnp.exp(g - m_new[..., None])
        C_new = decay[..., None, None] * C + jnp.einsum('bhs,bhsd,bhse->bhde', w, kc, vc)
        n_new = decay[..., None] * n + jnp.einsum('bhs,bhsd->bhd', w, kc)
        return (C_new, n_new, m_new), h

    state, hs = lax.scan(step, state, (chunks(q), chunks(k), chunks(v), chunks(log_i), chunks(log_f)))
    h = jnp.moveaxis(hs, 0, 2).reshape(B, H, N, dh)
    return h, state


def mlstm_prep(z, conv_w, gate_b):
    B, N, _ = z.shape
    G = GROUP_W
    qk = jax.nn.silu(short_conv(z[..., :2 * G], conv_w))

    def heads(a):
        return a.reshape(B, N, MLSTM_HEADS, -1).transpose(0, 2, 1, 3).astype(F32)

    q = heads(qk[..., :G])
    k = heads(qk[..., G:]) * (HEAD_DIM ** -0.5)
    v = heads(z[..., 2 * G:3 * G])
    o = jax.nn.sigmoid(z[..., 3 * G:4 * G])
    gates = (z[..., 4 * G:].astype(F32).reshape(B, N, 4, MLSTM_HEADS) + gate_b.astype(F32)).transpose(2, 0, 3, 1)
    gate_logs = (gates[0], jax.nn.log_sigmoid(gates[1]), gates[2], jax.nn.log_sigmoid(gates[3]))
    return q, k, v, o, gate_logs


def mlstm_directions(q, k, v, gate_logs, state_f, state_b):
    li_f, lf_f, li_b, lf_b = gate_logs
    h_f, st_f = mlstm_chunk_scan(q, k, v, li_f, lf_f, state_f)
    flip = lambda a: jnp.flip(a, axis=2)
    h_b, st_b = mlstm_chunk_scan(flip(q), flip(k), flip(v), flip(li_b), flip(lf_b), state_b)
    return h_f + flip(h_b), st_f, st_b


def mlstm_out(h, o, norm_g):
    B, H, N, dh = h.shape
    hn = layer_norm(h.transpose(0, 2, 1, 3)) * norm_g.astype(F32).reshape(H, dh)
    return hn.reshape(B, N, H * dh).astype(o.dtype) * o


def mlstm_mixer(zx, zc, conv_w, gate_b, norm_g, update_ctx):
    B = zx.shape[0]
    zero = (jnp.zeros((B, MLSTM_HEADS, HEAD_DIM, HEAD_DIM), F32),
            jnp.zeros((B, MLSTM_HEADS, HEAD_DIM), F32),
            jnp.zeros((B, MLSTM_HEADS), F32))
    qc, kc, vc, oc, gc = mlstm_prep(zc, conv_w, gate_b)
    h_c, st_f, st_b = mlstm_directions(qc, kc, vc, gc, zero, zero)
    qx, kx, vx, ox, gx = mlstm_prep(zx, conv_w, gate_b)
    h_x, _, _ = mlstm_directions(qx, kx, vx, gx, st_f, st_b)
    out_x = mlstm_out(h_x, ox, norm_g)
    out_c = mlstm_out(h_c, oc, norm_g) if update_ctx else None
    return out_x, out_c


def band_attention(q, k, v, k_ctx, v_ctx, sink):
    B, N = q.shape[:2]
    L = ATT_BLOCK
    nb = N // L
    scale = HEAD_DIM ** -0.5
    qb = q.reshape(B, nb, L, ATT_KV_HEADS, ATT_GROUP, HEAD_DIM)
    pad = ((0, 0), (L, L), (0, 0), (0, 0))
    kp = jnp.pad(k, pad).reshape(B, nb + 2, L, ATT_KV_HEADS, HEAD_DIM)
    vp = jnp.pad(v, pad).reshape(B, nb + 2, L, ATT_KV_HEADS, HEAD_DIM)
    band = lambda a: jnp.concatenate([a[:, :-2], a[:, 1:-1], a[:, 2:]], axis=2)
    kb, vb = band(kp), band(vp)
    s_loc = jnp.einsum('bnqhgd,bnshd->bnhgqs', qb, kb).astype(F32) * scale
    s_ctx = jnp.einsum('bnqhgd,bchd->bnhgqc', qb, k_ctx).astype(F32) * scale
    qpos = jnp.arange(nb)[:, None] * L + jnp.arange(L)[None, :]
    kpos = (jnp.arange(nb)[:, None] - 1) * L + jnp.arange(3 * L)[None, :]
    valid = (jnp.abs(qpos[:, :, None] - kpos[:, None, :]) <= WINDOW) & ((kpos >= 0) & (kpos < N))[:, None, :]
    s_loc = jnp.where(valid[None, :, None, None], s_loc, -jnp.inf)
    s_sink = jnp.broadcast_to(sink.astype(F32).reshape(ATT_KV_HEADS, ATT_GROUP)[None, None, :, :, None, None],
                              s_loc.shape[:-1] + (1,))
    p = jax.nn.softmax(jnp.concatenate([s_loc, s_ctx, s_sink], axis=-1), axis=-1).astype(v.dtype)
    n_loc = 3 * L
    n_ctx = k_ctx.shape[1]
    o = (jnp.einsum('bnhgqs,bnshd->bnqhgd', p[..., :n_loc], vb)
         + jnp.einsum('bnhgqc,bchd->bnqhgd', p[..., n_loc:n_loc + n_ctx], v_ctx))
    return o.reshape(B, N, ATT_HEADS * HEAD_DIM)


def context_attention(q, k, v, sink):
    B, C = q.shape[:2]
    qg = q.reshape(B, C, ATT_KV_HEADS, ATT_GROUP, HEAD_DIM)
    s = jnp.einsum('bqhgd,bchd->bhgqc', qg, k).astype(F32) * (HEAD_DIM ** -0.5)
    s_sink = jnp.broadcast_to(sink.astype(F32).reshape(ATT_KV_HEADS, ATT_GROUP)[None, :, :, None, None],
                              s.shape[:-1] + (1,))
    p = jax.nn.softmax(jnp.concatenate([s, s_sink], axis=-1), axis=-1)[..., :-1].astype(v.dtype)
    return jnp.einsum('bhgqc,bchd->bqhgd', p, v).reshape(B, C, ATT_HEADS * HEAD_DIM)


def attn_mixer(zx, zc, cos, sin, sink, update_ctx):
    def split(z):
        B, N, _ = z.shape
        q = z[..., :ATT_Q].reshape(B, N, ATT_HEADS, HEAD_DIM)
        k = z[..., ATT_Q:ATT_Q + ATT_KV].reshape(B, N, ATT_KV_HEADS, HEAD_DIM)
        v = z[..., ATT_Q + ATT_KV:].reshape(B, N, ATT_KV_HEADS, HEAD_DIM)
        return q, k, v
    qx, kx, vx = split(zx)
    qc, kc, vc = split(zc)
    out_x = band_attention(apply_rope(qx, cos, sin), apply_rope(kx, cos, sin), vx, kc, vc, sink)
    out_c = context_attention(qc, kc, vc, sink) if update_ctx else None
    return out_x, out_c


def pool_mixer(z, pool_w, pool_scale):
    B, N, _ = z.shape
    zg = z.astype(F32).reshape(B, N, len(POOL_WINDOWS), POOL_CH)
    csum = jnp.concatenate([jnp.zeros((B, 1) + zg.shape[2:], F32), jnp.cumsum(zg, axis=1)], axis=1)
    t = jnp.arange(N)
    means = []
    for gi, w in enumerate(POOL_WINDOWS):
        lo = jnp.clip(t - w // 2, 0, N)
        hi = jnp.clip(t + w - w // 2, 0, N)
        cnt = (hi - lo).astype(F32)
        means.append((csum[:, hi, gi] - csum[:, lo, gi]) / cnt[None, :, None])
    pooled = jnp.stack(means, axis=2)
    y = jnp.einsum('bngc,gcd->bngd', (pooled - zg).astype(z.dtype), pool_w)
    return y.reshape(B, N, GROUP_W) * pool_scale


def token_mixers(zx, zc, cos, sin, sgu_w, sgu_b, conv_w, gate_b, mnorm_g, sink, pool_w, pool_scale, update_ctx):
    seg = lambda z, a, b: z[..., a:b]
    a_x = spatial_gating(seg(zx, 0, OFF_MLSTM), sgu_w, sgu_b)
    b_x, b_c = mlstm_mixer(seg(zx, OFF_MLSTM, OFF_ATT), seg(zc, OFF_MLSTM, OFF_ATT), conv_w, gate_b, mnorm_g, update_ctx)
    c_x, c_c = attn_mixer(seg(zx, OFF_ATT, OFF_POOL), seg(zc, OFF_ATT, OFF_POOL), cos, sin, sink, update_ctx)
    d_x = pool_mixer(seg(zx, OFF_POOL, IN_COLS), pool_w, pool_scale)
    out_x = jnp.concatenate([a_x, b_x, c_x, d_x], axis=-1)
    out_c = None
    if update_ctx:
        a_c = spatial_gating(seg(zc, 0, OFF_MLSTM), sgu_w, sgu_b)
        d_c = pool_mixer(seg(zc, OFF_POOL, IN_COLS), pool_w, pool_scale)
        out_c = jnp.concatenate([a_c, b_c, c_c, d_c], axis=-1)
    return out_x, out_c


def swiglu(h, w_in_, w_out_):
    gate, up = jnp.split(h @ w_in_, 2, axis=-1)
    return (jax.nn.silu(gate) * up) @ w_out_


def trunk_layer(x, ctx, c, c_ctx, cos, sin, w_mod, b_mod, norm_g, w_in, w_out, sgu_w, sgu_b, conv_w, gate_b,
                mnorm_g, sink, pool_w, pool_scale, w_ffn_in, w_ffn_out, update_ctx):
    sh_a, sc_a, g_a, sh_f, sc_f, g_f = jnp.split((jax.nn.silu(c) @ w_mod + b_mod)[:, None, :], 6, axis=-1)
    csh_a, csc_a, cg_a, csh_f, csc_f, cg_f = jnp.split((jax.nn.silu(c_ctx) @ w_mod + b_mod)[None, None, :], 6, axis=-1)
    zx = modulate(rms_norm(x, norm_g[0]), sh_a, sc_a) @ w_in
    zc = modulate(rms_norm(ctx, norm_g[0]), csh_a, csc_a) @ w_in
    mix_x, mix_c = token_mixers(zx, zc, cos, sin, sgu_w, sgu_b, conv_w, gate_b, mnorm_g, sink, pool_w, pool_scale,
                                update_ctx)
    x = x + g_a * rms_norm(mix_x @ w_out, norm_g[1])
    x = x + g_f * rms_norm(swiglu(modulate(rms_norm(x, norm_g[2]), sh_f, sc_f), w_ffn_in, w_ffn_out), norm_g[3])
    if update_ctx:
        ctx = ctx + cg_a * rms_norm(mix_c @ w_out, norm_g[1])
        ctx = ctx + cg_f * rms_norm(swiglu(modulate(rms_norm(ctx, norm_g[2]), csh_f, csc_f), w_ffn_in, w_ffn_out),
                                    norm_g[3])
    return x, ctx


def setup_inputs(seed: int = 0) -> dict:
    key = jax.random.key(seed)
    ks = jax.random.split(key, 19)
    nrm = lambda k, shape, s: jax.random.normal(k, shape, F32) * s
    x = nrm(ks[0], (BATCH, SEQ, D_MODEL), 1.0)
    c = nrm(ks[1], (BATCH, D_MODEL), 1.0)
    ctx = nrm(ks[2], (BATCH, CTX_LEN, D_MODEL), 1.0)
    c_ctx = nrm(ks[3], (D_MODEL,), 1.0)
    w_mod = nrm(ks[4], (DEPTH, D_MODEL, 6 * D_MODEL), 0.5 * D_MODEL ** -0.5)
    b_mod = nrm(ks[5], (DEPTH, 6 * D_MODEL), 0.02)
    norm_g = 1.0 + nrm(ks[6], (DEPTH, 4, D_MODEL), 0.05)
    w_in = nrm(ks[7], (DEPTH, D_MODEL, IN_COLS), D_MODEL ** -0.5)
    w_out = nrm(ks[8], (DEPTH, D_MODEL, D_MODEL), D_MODEL ** -0.5)
    sgu_w = nrm(ks[9], (DEPTH, SGU_HEADS, SGU_CHUNK, SGU_CHUNK), SGU_CHUNK ** -0.5)
    sgu_b = 1.0 + nrm(ks[10], (DEPTH, SGU_HEADS, SGU_CHUNK), 0.05)
    mlstm_conv_w = nrm(ks[11], (DEPTH, QK_CONV, 2 * GROUP_W), QK_CONV ** -0.5)
    f_bias = jnp.linspace(3.0, 6.0, MLSTM_HEADS)
    is_forget = jnp.array([0.0, 1.0, 0.0, 1.0], dtype=F32)
    mlstm_gate_b = nrm(ks[12], (DEPTH, 4, MLSTM_HEADS), 0.1) + is_forget[None, :, None] * f_bias[None, None, :]
    mlstm_norm_g = 1.0 + nrm(ks[13], (DEPTH, GROUP_W), 0.05)
    attn_sink = nrm(ks[14], (DEPTH, ATT_HEADS), 1.0)
    pool_w = nrm(ks[15], (DEPTH, len(POOL_WINDOWS), POOL_CH, POOL_CH), POOL_CH ** -0.5)
    pool_scale = 1.0 + nrm(ks[16], (DEPTH, GROUP_W), 0.1)
    w_ffn_in = nrm(ks[17], (DEPTH, D_MODEL, 2 * FFN_HIDDEN), D_MODEL ** -0.5)
    w_ffn_out = nrm(ks[18], (DEPTH, FFN_HIDDEN, D_MODEL), FFN_HIDDEN ** -0.5)
    return {'x': x, 'c': c, 'ctx': ctx, 'c_ctx': c_ctx, 'w_mod': w_mod, 'b_mod': b_mod, 'norm_g': norm_g,
            'w_in': w_in, 'w_out': w_out, 'sgu_w': sgu_w, 'sgu_b': sgu_b, 'mlstm_conv_w': mlstm_conv_w,
            'mlstm_gate_b': mlstm_gate_b, 'mlstm_norm_g': mlstm_norm_g, 'attn_sink': attn_sink,
            'pool_w': pool_w, 'pool_scale': pool_scale, 'w_ffn_in': w_ffn_in, 'w_ffn_out': w_ffn_out}


def reference(x, c, ctx, c_ctx, w_mod, b_mod, norm_g, w_in, w_out, sgu_w, sgu_b, mlstm_conv_w, mlstm_gate_b,
              mlstm_norm_g, attn_sink, pool_w, pool_scale, w_ffn_in, w_ffn_out):
    cos, sin = axial_rope_tables(x.shape[1])
    for l in range(DEPTH):
        x, ctx = trunk_layer(x, ctx, c, c_ctx, cos, sin, w_mod[l], b_mod[l], norm_g[l], w_in[l], w_out[l],
                             sgu_w[l], sgu_b[l], mlstm_conv_w[l], mlstm_gate_b[l], mlstm_norm_g[l], attn_sink[l],
                             pool_w[l], pool_scale[l], w_ffn_in[l], w_ffn_out[l], l < DEPTH - 1)
    return x
```

```python
import functools

import jax
import jax.numpy as jnp
from jax import lax
from jax.experimental import pallas as pl
from jax.experimental.pallas import tpu as pltpu

F32 = jnp.float32
BF16 = jnp.bfloat16

GRID_W = 64
ROPE_BASE = 10000.0
EPS = 1e-6
LANES = 128
SUBLANES = 8
CHUNK = 128
HEAD_DIM = 64
N_HEADS = 4
GROUP_W = N_HEADS * HEAD_DIM
POOL_WINDOWS = (2, 4, 8, 16)
HALO = 8
ROW_TILE = 256
NEG = -1e30
VMEM_LIMIT = 56 * 1024 * 1024

C_SGU = 0
C_ML = 512
C_ATT = 1536
C_POOL = 2304
C_GATE = 2560
IN_COLS_PADDED = 2688

HI = lax.Precision.HIGHEST


def _dot(a, b):
    return jnp.dot(a, b, preferred_element_type=F32)


def _dot_hi(a, b):
    return jnp.dot(a, b, preferred_element_type=F32, precision=HI)


def _dot_nt(a, b):
    return lax.dot_general(a, b, (((1,), (1,)), ((), ())), preferred_element_type=F32)


def _dot_tn(a, b):
    return lax.dot_general(a, b, (((0,), (0,)), ((), ())), preferred_element_type=F32)


def _rms(x, g):
    return x * lax.rsqrt(jnp.mean(x * x, axis=-1, keepdims=True) + EPS) * g


def _in_head(shape, h):
    lane = lax.broadcasted_iota(jnp.int32, shape, 1)
    return (lane >= h * HEAD_DIM) & (lane < (h + 1) * HEAD_DIM)


def _keep_head(x, h):
    return jnp.where(_in_head(x.shape, h), x, jnp.zeros_like(x))


def _group_layer_norm(v, p_ref):
    mu = _dot_hi(v, p_ref[...])
    vc = v - mu
    var = _dot_hi(vc * vc, p_ref[...])
    return vc * lax.rsqrt(var + EPS)


def _mod_kernel(c_ref, w_ref, b_ref, o_ref):
    c = c_ref[...]
    s = (c * jax.nn.sigmoid(c)).astype(BF16)
    o_ref[...] = _dot(s, w_ref[...].astype(BF16)) + b_ref[...]


def _modulation(cc, w_mod, b_mod):
    depth, d, n6 = w_mod.shape
    rows = cc.shape[0]
    tn = 1536
    return pl.pallas_call(
        _mod_kernel,
        grid=(depth, n6 // tn),
        in_specs=[
            pl.BlockSpec((rows, d), lambda l, j: (0, 0)),
            pl.BlockSpec((None, d, tn), lambda l, j: (l, 0, j)),
            pl.BlockSpec((None, 1, tn), lambda l, j: (l, 0, j)),
        ],
        out_specs=pl.BlockSpec((None, rows, tn), lambda l, j: (l, 0, j)),
        out_shape=jax.ShapeDtypeStruct((depth, rows, n6), F32),
        compiler_params=pltpu.CompilerParams(
            dimension_semantics=("arbitrary", "arbitrary"), vmem_limit_bytes=VMEM_LIMIT),
        name="modulation",
    )(cc, w_mod, b_mod.reshape(depth, 1, n6))


def _in_kernel(x_ref, mod_ref, g_ref, w_ref, cos_ref, sin_ref,
               zs_ref, zm_ref, zg_ref, za_ref, zp_ref):
    x = x_ref[...]
    h = _rms(x, g_ref[...]) * (1.0 + mod_ref[1:2, :]) + mod_ref[0:1, :]
    hb = h.astype(BF16)
    zs_ref[...] = _dot(hb, w_ref[:, C_SGU:C_ML])
    zm_ref[...] = _dot(hb, w_ref[:, C_ML:C_ATT])
    zp_ref[...] = _dot(hb, w_ref[:, C_POOL:C_GATE])
    zg_ref[...] = _dot(hb, w_ref[:, C_GATE:IN_COLS_PADDED])

    cos = jnp.concatenate([cos_ref[...], cos_ref[...]], axis=1)
    sin = jnp.concatenate([sin_ref[...], sin_ref[...]], axis=1)
    lane = lax.broadcasted_iota(jnp.int32, cos.shape, 1)
    first_half = (lane & 31) < 16

    def rope(t):
        rot = jnp.where(first_half, pltpu.roll(t, GROUP_W - 16, axis=1), pltpu.roll(t, 16, axis=1))
        return t * cos + rot * sin

    q = _dot(hb, w_ref[:, C_ATT:C_ATT + GROUP_W])
    za_ref[:, 0:GROUP_W] = rope(q) * (HEAD_DIM ** -0.5)
    k = _dot(hb, w_ref[:, C_ATT + GROUP_W:C_ATT + 2 * GROUP_W])
    za_ref[:, GROUP_W:2 * GROUP_W] = rope(k)
    za_ref[:, 2 * GROUP_W:3 * GROUP_W] = _dot(hb, w_ref[:, C_ATT + 2 * GROUP_W:C_POOL])


def _in_proj(xs, mods, g0, w_in, cos_t, sin_t, n_batch_rows):
    B, S, D = xs.shape
    tm = ROW_TILE
    row = lambda b, j: (b, j, 0)
    const2 = lambda b, j: (0, 0)
    outs = [(512, "zs"), (1024, "zm"), (LANES, "zg"), (3 * GROUP_W, "za"), (GROUP_W, "zp")]
    return pl.pallas_call(
        _in_kernel,
        grid=(B, S // tm),
        in_specs=[
            pl.BlockSpec((None, tm, D), row),
            pl.BlockSpec((None, 6, D), lambda b, j: (jnp.where(j == 0, n_batch_rows, b), 0, 0)),
            pl.BlockSpec((1, D), const2),
            pl.BlockSpec((D, IN_COLS_PADDED), const2, pipeline_mode=pl.Buffered(1)),
            pl.BlockSpec((tm, LANES), lambda b, j: (j, 0)),
            pl.BlockSpec((tm, LANES), lambda b, j: (j, 0)),
        ],
        out_specs=[pl.BlockSpec((None, tm, w), row) for w, _ in outs],
        out_shape=[jax.ShapeDtypeStruct((B, S, w), F32) for w, _ in outs],
        compiler_params=pltpu.CompilerParams(
            dimension_semantics=("arbitrary", "arbitrary"), vmem_limit_bytes=VMEM_LIMIT),
        name="in_proj",
    )(xs, mods, g0, w_in, cos_t, sin_t)


def _sgu_kernel(z_ref, w_ref, bias_ref, p_ref, o_ref):
    g = jax.nn.gelu(z_ref[...])
    u = g[:, 0:GROUP_W]
    vh = _group_layer_norm(g[:, GROUP_W:2 * GROUP_W], p_ref).astype(BF16)
    mixed = bias_ref[...]
    for h in range(N_HEADS):
        mixed = mixed + _dot(w_ref[h], _keep_head(vh, h))
    o_ref[...] = u * mixed


def _sgu(zs, sgu_w, sgu_bias, p_avg):
    B, S, _ = zs.shape
    return pl.pallas_call(
        _sgu_kernel,
        grid=(B, S // CHUNK),
        in_specs=[
            pl.BlockSpec((None, CHUNK, 2 * GROUP_W), lambda b, c: (b, c, 0)),
            pl.BlockSpec((N_HEADS, CHUNK, CHUNK), lambda b, c: (0, 0, 0)),
            pl.BlockSpec((CHUNK, GROUP_W), lambda b, c: (0, 0)),
            pl.BlockSpec((GROUP_W, GROUP_W), lambda b, c: (0, 0)),
        ],
        out_specs=pl.BlockSpec((None, CHUNK, GROUP_W), lambda b, c: (b, c, 0)),
        out_shape=jax.ShapeDtypeStruct((B, S, GROUP_W), F32),
        compiler_params=pltpu.CompilerParams(dimension_semantics=("arbitrary", "arbitrary")),
        name="sgu",
    )(zs, sgu_w, sgu_bias, p_avg)


def _pool_kernel(z_ref, zp_ref, zn_ref, w_ref, scale_ref, o_ref, *, n_ctx_chunks, n_chunks, ctx_len, seq_len):
    c = pl.program_id(1)
    is_ctx = c < n_ctx_chunks
    has_prev = jnp.logical_and(c != 0, c != n_ctx_chunks).astype(F32)
    has_next = jnp.logical_and(c != n_ctx_chunks - 1, c != n_chunks - 1).astype(F32)
    z = z_ref[...]
    rows = CHUNK + 2 * HALO
    ext = jnp.concatenate([zp_ref[...] * has_prev, z, zn_ref[...] * has_next], axis=0)
    s2 = ext + pltpu.roll(ext, 1, axis=0)
    s4 = pltpu.roll(s2, 1, axis=0) + pltpu.roll(s2, rows - 1, axis=0)
    s8 = pltpu.roll(s4, 2, axis=0) + pltpu.roll(s4, rows - 2, axis=0)
    s16 = pltpu.roll(s8, 4, axis=0) + pltpu.roll(s8, rows - 4, axis=0)
    sums = [s[HALO:HALO + CHUNK] for s in (s2, s4, s8, s16)]

    lane = lax.broadcasted_iota(jnp.int32, z.shape, 1)
    t = lax.broadcasted_iota(jnp.int32, z.shape, 0) + jnp.where(is_ctx, c, c - n_ctx_chunks) * CHUNK
    seg_len = jnp.where(is_ctx, ctx_len, seq_len)
    pooled = jnp.zeros_like(z)
    for gi, w in enumerate(POOL_WINDOWS):
        cnt = (jnp.minimum(t + w // 2, seg_len) - jnp.maximum(t - w // 2, 0)).astype(F32)
        in_group = (lane >= gi * HEAD_DIM) & (lane < (gi + 1) * HEAD_DIM)
        pooled = jnp.where(in_group, sums[gi] / cnt, pooled)
    y = _dot((pooled - z).astype(BF16), w_ref[...])
    o_ref[...] = y * scale_ref[...]


def _pool(zp, pool_w_bd, pool_scale, n_ctx_chunks, ctx_len, seq_len):
    B, S, _ = zp.shape
    n_chunks = S // CHUNK
    per = CHUNK // HALO
    kern = functools.partial(_pool_kernel, n_ctx_chunks=n_ctx_chunks, n_chunks=n_chunks,
                             ctx_len=ctx_len, seq_len=seq_len)
    return pl.pallas_call(
        kern,
        grid=(B, n_chunks),
        in_specs=[
            pl.BlockSpec((None, CHUNK, GROUP_W), lambda b, c: (b, c, 0)),
            pl.BlockSpec((None, HALO, GROUP_W), lambda b, c: (b, jnp.maximum(c * per - 1, 0), 0)),
            pl.BlockSpec((None, HALO, GROUP_W), lambda b, c: (b, jnp.minimum((c + 1) * per, S // HALO - 1), 0)),
            pl.BlockSpec((GROUP_W, GROUP_W), lambda b, c: (0, 0)),
            pl.BlockSpec((1, GROUP_W), lambda b, c: (0, 0)),
        ],
        out_specs=pl.BlockSpec((None, CHUNK, GROUP_W), lambda b, c: (b, c, 0)),
        out_shape=jax.ShapeDtypeStruct((B, S, GROUP_W), F32),
        compiler_params=pltpu.CompilerParams(dimension_semantics=("arbitrary", "arbitrary")),
        name="pool",
    )(zp, zp, zp, pool_w_bd, pool_scale)


def _att_kernel(q_ref, kp_ref, kc_ref, kn_ref, kx_ref, vp_ref, vc_ref, vn_ref, vx_ref, sink_ref, o_ref,
                *, n_ctx_chunks, n_chunks):
    c = pl.program_id(1)
    prev_ok = c > n_ctx_chunks
    cur_ok = c >= n_ctx_chunks
    next_ok = jnp.logical_and(cur_ok, c < n_chunks - 1)
    qb = q_ref[...].astype(BF16)
    kcat = jnp.concatenate([kp_ref[...], kc_ref[...], kn_ref[...], kx_ref[...]], axis=0).astype(BF16)
    vcat = jnp.concatenate([vp_ref[...], vc_ref[...], vn_ref[...], vx_ref[...]], axis=0).astype(BF16)
    n_keys = kcat.shape[0]
    i = lax.broadcasted_iota(jnp.int32, (CHUNK, n_keys), 0)
    j = lax.broadcasted_iota(jnp.int32, (CHUNK, n_keys), 1)
    lo = jnp.where(cur_ok, jnp.where(prev_ok, 0, CHUNK), 3 * CHUNK)
    hi = jnp.where(next_ok, 3 * CHUNK, 2 * CHUNK)
    valid = ((j >= i) & (j <= i + 2 * CHUNK) & (j >= lo) & (j < hi)) | (j >= 3 * CHUNK)
    acc = jnp.zeros((CHUNK, GROUP_W), F32)
    for h in range(N_HEADS):
        s = _dot_nt(_keep_head(qb, h), kcat)
        s = jnp.where(valid, s, NEG)
        sink = sink_ref[0:1, h:h + 1]
        m = jnp.maximum(jnp.max(s, axis=1, keepdims=True), sink)
        p = jnp.exp(s - m)
        denom = jnp.sum(p, axis=1, keepdims=True) + jnp.exp(sink - m)
        o = _dot(p.astype(BF16), vcat)
        acc = acc + _keep_head(o / denom, h)
    o_ref[...] = acc


def _attention(za, sink_row, n_ctx_chunks):
    B, S, _ = za.shape
    n_chunks = S // CHUNK
    kern = functools.partial(_att_kernel, n_ctx_chunks=n_ctx_chunks, n_chunks=n_chunks)
    blk = (None, CHUNK, GROUP_W)
    ctx_blk = (None, n_ctx_chunks * CHUNK, GROUP_W)

    def specs(col):
        return [
            pl.BlockSpec(blk, lambda b, c: (b, jnp.maximum(c - 1, 0), col)),
            pl.BlockSpec(blk, lambda b, c: (b, c, col)),
            pl.BlockSpec(blk, lambda b, c: (b, jnp.minimum(c + 1, n_chunks - 1), col)),
            pl.BlockSpec(ctx_blk, lambda b, c: (b, 0, col)),
        ]

    return pl.pallas_call(
        kern,
        grid=(B, n_chunks),
        in_specs=[pl.BlockSpec(blk, lambda b, c: (b, c, 0))] + specs(1) + specs(2)
                 + [pl.BlockSpec((1, LANES), lambda b, c: (0, 0))],
        out_specs=pl.BlockSpec(blk, lambda b, c: (b, c, 0)),
        out_shape=jax.ShapeDtypeStruct((B, S, GROUP_W), F32),
        compiler_params=pltpu.CompilerParams(dimension_semantics=("arbitrary", "arbitrary")),
        name="attention",
    )(*([za] * 9), sink_row)


STATE_W = GROUP_W + LANES


def _log_sigmoid(x):
    return jnp.minimum(x, 0.0) - jnp.log1p(jnp.exp(-jnp.abs(x)))


def _mlstm_direction(zm_ref, zprev_ref, znext_ref, zg_ref, conv_ref, gb_ref, st_ref, m_ref, o_ref,
                     *, forward, has_prev, has_next):
    rows = CHUNK + 2 * HALO
    zqk = zm_ref[:, 0:2 * GROUP_W]
    ext = jnp.concatenate([zprev_ref[...] * has_prev, zqk, znext_ref[...] * has_next], axis=0)
    before = pltpu.roll(ext, 1, axis=0)[HALO:HALO + CHUNK]
    after = pltpu.roll(ext, rows - 1, axis=0)[HALO:HALO + CHUNK]
    conv = before * conv_ref[0:1, :] + zqk * conv_ref[1:2, :] + after * conv_ref[2:3, :]
    qk = conv * jax.nn.sigmoid(conv)
    qb = qk[:, 0:GROUP_W].astype(BF16)
    k = qk[:, GROUP_W:2 * GROUP_W] * (HEAD_DIM ** -0.5)
    kb = k.astype(BF16)
    vb = zm_ref[:, 2 * GROUP_W:3 * GROUP_W].astype(BF16)

    pre = zg_ref[...] + gb_ref[...]
    glane = lax.broadcasted_iota(jnp.int32, pre.shape, 1)
    gates = jnp.where((glane & (2 * N_HEADS - 1)) >= N_HEADS, _log_sigmoid(pre), pre)
    r = lax.broadcasted_iota(jnp.int32, (CHUNK, CHUNK), 0)
    s_ = lax.broadcasted_iota(jnp.int32, (CHUNK, CHUNK), 1)
    causal = (s_ <= r) if forward else (s_ >= r)
    cum = _dot_hi(jnp.where(causal, 1.0, 0.0).astype(F32), gates)
    gates_t = gates.T
    cum_t = cum.T
    li0 = 0 if forward else 2 * N_HEADS
    lf0 = li0 + N_HEADS
    end = CHUNK - 1 if forward else 0

    st = st_ref[...]
    q_st = _dot(qb, st.astype(BF16))
    num = jnp.zeros((CHUNK, GROUP_W), F32)
    w_full = jnp.zeros((CHUNK, GROUP_W), F32)
    row_id = lax.broadcasted_iota(jnp.int32, (GROUP_W, 1), 0)
    decay_rows = jnp.zeros((GROUP_W, 1), F32)
    for h in range(N_HEADS):
        b_col = cum[:, lf0 + h:lf0 + h + 1]
        li_col = gates[:, li0 + h:li0 + h + 1]
        b_row = cum_t[lf0 + h:lf0 + h + 1, :]
        li_row = gates_t[li0 + h:li0 + h + 1, :]
        b_end = b_row[:, end:end + 1]
        m_row = (0 if forward else N_HEADS) + h
        m = m_ref[m_row:m_row + 1, 0:1]
        d = jnp.where(causal, b_col + (li_row - b_row), NEG)
        m_t = jnp.maximum(b_col + m, jnp.max(d, axis=1, keepdims=True))
        inter = jnp.exp(b_col + m - m_t)
        s = _dot_nt(_keep_head(qb, h), kb) * jnp.exp(d - m_t)
        sv = _dot(s.astype(BF16), vb)
        den = inter * q_st[:, GROUP_W + h:GROUP_W + h + 1] + jnp.sum(s, axis=1, keepdims=True)
        scale = 1.0 / jnp.maximum(jnp.abs(den), jnp.exp(-m_t))
        num = num + _keep_head((inter * q_st[:, 0:GROUP_W] + sv) * scale, h)
        g_col = b_end - b_col + li_col
        m_new = jnp.maximum(b_end + m, jnp.max(g_col, axis=0, keepdims=True))
        decay = jnp.exp(b_end + m - m_new)
        w_full = jnp.where(_in_head(w_full.shape, h), jnp.exp(g_col - m_new), w_full)
        in_head = (row_id >= h * HEAD_DIM) & (row_id < (h + 1) * HEAD_DIM)
        decay_rows = jnp.where(in_head, decay, decay_rows)
        m_ref[m_row:m_row + 1, :] = jnp.broadcast_to(m_new, (1, LANES))
    o_ref[...] = num

    kw = (k * w_full).astype(BF16)
    v_ext = jnp.concatenate([vb, jnp.ones((CHUNK, LANES), BF16)], axis=1)
    upd = _dot_tn(kw, v_ext)
    srow = lax.broadcasted_iota(jnp.int32, upd.shape, 0) >> 6
    scol = lax.broadcasted_iota(jnp.int32, upd.shape, 1)
    keep = ((scol < GROUP_W) & ((scol >> 6) == srow)) | (scol - GROUP_W == srow)
    st_ref[...] = st * decay_rows + jnp.where(keep, upd, 0.0)


def _mlstm_kernel(zmf_ref, zpf_ref, znf_ref, zgf_ref, zmb_ref, zpb_ref, znb_ref, zgb_ref,
                  conv_ref, gb_ref, hf_ref, hb_ref, stf_ref, stb_ref, m_ref, *, n_ctx_chunks, n_chunks):
    i = pl.program_id(1)

    @pl.when(i == 0)
    def _():
        stf_ref[...] = jnp.zeros_like(stf_ref)
        stb_ref[...] = jnp.zeros_like(stb_ref)
        m_ref[...] = jnp.zeros_like(m_ref)

    def halo_flags(c):
        has_prev = jnp.logical_and(c != 0, c != n_ctx_chunks).astype(F32)
        has_next = jnp.logical_and(c != n_ctx_chunks - 1, c != n_chunks - 1).astype(F32)
        return has_prev, has_next

    cf = i
    cb = jnp.where(i < n_ctx_chunks, n_ctx_chunks - 1 - i, n_chunks - 1 + n_ctx_chunks - i)
    pf, nf = halo_flags(cf)
    pb, nb = halo_flags(cb)
    _mlstm_direction(zmf_ref, zpf_ref, znf_ref, zgf_ref, conv_ref, gb_ref, stf_ref, m_ref, hf_ref,
                     forward=True, has_prev=pf, has_next=nf)
    _mlstm_direction(zmb_ref, zpb_ref, znb_ref, zgb_ref, conv_ref, gb_ref, stb_ref, m_ref, hb_ref,
                     forward=False, has_prev=pb, has_next=nb)


def _mlstm(zm, zg, conv_w, gate_b_row, n_ctx_chunks):
    B, S, _ = zm.shape
    n_chunks = S // CHUNK
    per = CHUNK // HALO
    kern = functools.partial(_mlstm_kernel, n_ctx_chunks=n_ctx_chunks, n_chunks=n_chunks)

    def fwd(i):
        return i

    def bwd(i):
        return jnp.where(i < n_ctx_chunks, n_ctx_chunks - 1 - i, n_chunks - 1 + n_ctx_chunks - i)

    def specs(order):
        return [
            pl.BlockSpec((None, CHUNK, 4 * GROUP_W), lambda b, i: (b, order(i), 0)),
            pl.BlockSpec((None, HALO, 2 * GROUP_W), lambda b, i: (b, jnp.maximum(order(i) * per - 1, 0), 0)),
            pl.BlockSpec((None, HALO, 2 * GROUP_W),
                         lambda b, i: (b, jnp.minimum((order(i) + 1) * per, S // HALO - 1), 0)),
            pl.BlockSpec((None, CHUNK, LANES), lambda b, i: (b, order(i), 0)),
        ]

    return pl.pallas_call(
        kern,
        grid=(B, n_chunks),
        in_specs=specs(fwd) + specs(bwd) + [
            pl.BlockSpec((3, 2 * GROUP_W), lambda b, i: (0, 0)),
            pl.BlockSpec((1, LANES), lambda b, i: (0, 0)),
        ],
        out_specs=[
            pl.BlockSpec((None, CHUNK, GROUP_W), lambda b, i: (b, fwd(i), 0)),
            pl.BlockSpec((None, CHUNK, GROUP_W), lambda b, i: (b, bwd(i), 0)),
        ],
        out_shape=[jax.ShapeDtypeStruct((B, S, GROUP_W), F32)] * 2,
        scratch_shapes=[
            pltpu.VMEM((GROUP_W, STATE_W), F32),
            pltpu.VMEM((GROUP_W, STATE_W), F32),
            pltpu.VMEM((2 * N_HEADS, LANES), F32),
        ],
        compiler_params=pltpu.CompilerParams(dimension_semantics=("arbitrary", "arbitrary")),
        name="mlstm",
    )(zm, zm, zm, zg, zm, zm, zm, zg, conv_w, gate_b_row)


def _out_kernel(x_ref, a_ref, hf_ref, hb_ref, zo_ref, c_ref, d_ref, mod_ref, g_ref, mg_ref, p_ref,
                wo_ref, wi_ref, wf_ref, o_ref, *, hidden):
    hn = _group_layer_norm(hf_ref[...] + hb_ref[...], p_ref) * mg_ref[...]
    bm = hn * jax.nn.sigmoid(zo_ref[...])
    y = (_dot(a_ref[...].astype(BF16), wo_ref[0:GROUP_W, :])
         + _dot(bm.astype(BF16), wo_ref[GROUP_W:2 * GROUP_W, :])
         + _dot(c_ref[...].astype(BF16), wo_ref[2 * GROUP_W:3 * GROUP_W, :])
         + _dot(d_ref[...].astype(BF16), wo_ref[3 * GROUP_W:4 * GROUP_W, :]))
    x1 = x_ref[...] + mod_ref[2:3, :] * _rms(y, g_ref[1:2, :])
    t = _rms(x1, g_ref[2:3, :]) * (1.0 + mod_ref[4:5, :]) + mod_ref[3:4, :]
    tb = t.astype(BF16)
    gate = _dot(tb, wi_ref[:, 0:hidden])
    up = _dot(tb, wi_ref[:, hidden:2 * hidden])
    act = (gate * jax.nn.sigmoid(gate) * up).astype(BF16)
    f = _dot(act, wf_ref[...])
    o_ref[...] = x1 + mod_ref[5:6, :] * _rms(f, g_ref[3:4, :])


def _out_ffn(xs, a, hf, hb, zm, c, d, mods, norm_g, mnorm_g, p_avg, w_out, w_ffn_in, w_ffn_out, n_batch_rows):
    B, S, D = xs.shape
    tm = ROW_TILE
    hidden = w_ffn_out.shape[0]
    row = lambda b, j: (b, j, 0)
    const2 = lambda b, j: (0, 0)
    mix_spec = pl.BlockSpec((None, tm, GROUP_W), row)
    kern = functools.partial(_out_kernel, hidden=hidden)
    return pl.pallas_call(
        kern,
        grid=(B, S // tm),
        in_specs=[
            pl.BlockSpec((None, tm, D), row),
            mix_spec, mix_spec, mix_spec,
            pl.BlockSpec((None, tm, GROUP_W), lambda b, j: (b, j, 3)),
            mix_spec, mix_spec,
            pl.BlockSpec((None, 6, D), lambda b, j: (jnp.where(j == 0, n_batch_rows, b), 0, 0)),
            pl.BlockSpec((4, D), const2),
            pl.BlockSpec((1, GROUP_W), const2),
            pl.BlockSpec((GROUP_W, GROUP_W), const2),
            pl.BlockSpec((D, D), const2, pipeline_mode=pl.Buffered(1)),
            pl.BlockSpec((D, 2 * hidden), const2, pipeline_mode=pl.Buffered(1)),
            pl.BlockSpec((hidden, D), const2, pipeline_mode=pl.Buffered(1)),
        ],
        out_specs=pl.BlockSpec((None, tm, D), row),
        out_shape=jax.ShapeDtypeStruct((B, S, D), F32),
        compiler_params=pltpu.CompilerParams(
            dimension_semantics=("arbitrary", "arbitrary"), vmem_limit_bytes=VMEM_LIMIT),
        name="out_ffn",
    )(xs, a, hf, hb, zm, c, d, mods, norm_g, mnorm_g, p_avg, w_out, w_ffn_in, w_ffn_out)


def _rope_tables(n_tokens, ctx_len):
    rows = n_tokens // GRID_W
    axis_freq = HEAD_DIM // 4
    row = jnp.broadcast_to(jnp.arange(rows)[:, None], (rows, GRID_W)).reshape(-1).astype(F32)
    col = jnp.broadcast_to(jnp.arange(GRID_W)[None, :], (rows, GRID_W)).reshape(-1).astype(F32)
    inv = jnp.power(ROPE_BASE, -jnp.arange(axis_freq, dtype=F32) * 2.0 / (2 * axis_freq))
    ar = row[:, None] * inv
    ac = col[:, None] * inv
    ang = jnp.concatenate([ar, ar, ac, ac], axis=-1)
    cos, sin = jnp.cos(ang), jnp.sin(ang)
    sign = jnp.where((jnp.arange(HEAD_DIM) % 32) < 16, -1.0, 1.0).astype(F32)
    cos = jnp.concatenate([jnp.ones((ctx_len, HEAD_DIM), F32), cos], axis=0)
    sin = jnp.concatenate([jnp.zeros((ctx_len, HEAD_DIM), F32), sin * sign], axis=0)
    return jnp.tile(cos, (1, 2)), jnp.tile(sin, (1, 2))


def _arrange_w_in(w_in):
    d = w_in.shape[0]
    ml0 = 2 * GROUP_W
    gate0 = ml0 + 4 * GROUP_W
    att0 = gate0 + 4 * N_HEADS
    kv_w = GROUP_W // 2
    pool0 = att0 + GROUP_W + 2 * kv_w

    def expand_kv(w):
        w = w.reshape(d, 2, 1, HEAD_DIM)
        return jnp.broadcast_to(w, (d, 2, 2, HEAD_DIM)).reshape(d, GROUP_W)

    cols = [
        w_in[:, 0:gate0],
        w_in[:, att0:att0 + GROUP_W],
        expand_kv(w_in[:, att0 + GROUP_W:att0 + GROUP_W + kv_w]),
        expand_kv(w_in[:, att0 + GROUP_W + kv_w:pool0]),
        w_in[:, pool0:pool0 + GROUP_W],
        w_in[:, gate0:att0],
        jnp.zeros((d, IN_COLS_PADDED - C_GATE - 4 * N_HEADS), w_in.dtype),
    ]
    return jnp.concatenate(cols, axis=1).astype(BF16)


def _block_diag(w):
    g = w.shape[0]
    eye = jnp.eye(g, dtype=w.dtype)
    return (eye[:, None, :, None] * w[:, :, None, :]).reshape(g * w.shape[1], g * w.shape[2])


def kernel(x, c, ctx, c_ctx, w_mod, b_mod, norm_g, w_in, w_out, sgu_w, sgu_b, mlstm_conv_w, mlstm_gate_b,
           mlstm_norm_g, attn_sink, pool_w, pool_scale, w_ffn_in, w_ffn_out):
    B, N, D = x.shape
    ctx_len = ctx.shape[1]
    depth = w_mod.shape[0]
    assert D == 4 * GROUP_W and N % ROW_TILE == 0 and ctx_len == ROW_TILE
    assert w_in.shape[2] == 2 * GROUP_W + 4 * GROUP_W + 4 * N_HEADS + 2 * GROUP_W + GROUP_W
    n_ctx_chunks = ctx_len // CHUNK

    xs = jnp.concatenate([ctx, x], axis=1)
    rows = -(-(B + 1) // SUBLANES) * SUBLANES
    cc = jnp.zeros((rows, D), F32).at[0:B].set(c).at[B].set(c_ctx)
    mods = _modulation(cc, w_mod, b_mod).reshape(depth, rows, 6, D)

    cos_t, sin_t = _rope_tables(N, ctx_len)
    p_avg = _block_diag(jnp.full((N_HEADS, HEAD_DIM, HEAD_DIM), 1.0 / HEAD_DIM, F32))
    gate_b_row = jnp.zeros((depth, 1, LANES), F32).at[:, 0, 0:4 * N_HEADS].set(mlstm_gate_b.reshape(depth, -1))
    sink_row = jnp.zeros((depth, 1, LANES), F32).at[:, 0, 0:N_HEADS].set(attn_sink)

    for l in range(depth):
        w_in_l = _arrange_w_in(w_in[l])
        zs, zm, zg, za, zp = _in_proj(xs, mods[l], norm_g[l, 0:1], w_in_l, cos_t, sin_t, B)
        sgu_bias = jnp.repeat(sgu_b[l].T, HEAD_DIM, axis=1)
        a = _sgu(zs, sgu_w[l].astype(BF16), sgu_bias, p_avg)
        hf, hb = _mlstm(zm, zg, mlstm_conv_w[l], gate_b_row[l], n_ctx_chunks)
        cm = _attention(za, sink_row[l], n_ctx_chunks)
        dm = _pool(zp, _block_diag(pool_w[l]).astype(BF16), pool_scale[l].reshape(1, GROUP_W),
                   n_ctx_chunks, ctx_len, N)
        xs = _out_ffn(xs, a, hf, hb, zm, cm, dm, mods[l], norm_g[l], mlstm_norm_g[l].reshape(1, GROUP_W), p_avg,
                      w_out[l].astype(BF16), w_ffn_in[l].astype(BF16), w_ffn_out[l].astype(BF16), B)
    return xs[:, ctx_len:, :]
```

```python
import functools

import jax
import jax.numpy as jnp
from jax import lax
from jax.experimental import pallas as pl
from jax.experimental.pallas import tpu as pltpu

F32 = jnp.float32
BF16 = jnp.bfloat16

GRID_W = 64
ROPE_BASE = 10000.0
EPS = 1e-6
LANES = 128
SUBLANES = 8
BF16_ROWS = 16
CHUNK = 128
HEAD_DIM = 64
N_HEADS = 4
KV_HEADS = 2
GROUP_W = N_HEADS * HEAD_DIM
KV_W = KV_HEADS * HEAD_DIM
POOL_WINDOWS = (2, 4, 8, 16)
HALO = 8
ROW_TILE = 256
NEG = -1e30
VMEM_LIMIT = 56 * 1024 * 1024

R_SGU = 0
R_QK = 512
R_VO = 1024
R_AQ = 1536
R_AV = 1792
R_GATE = 1920
R_END = 1936
T_AK = 0
T_POOL = KV_W
T_END = KV_W + GROUP_W

ST_ROWS = 80

HI = lax.Precision.HIGHEST


def _dot(a, b):
    return jnp.dot(a, b, preferred_element_type=F32)


def _dot_hi(a, b):
    return jnp.dot(a, b, preferred_element_type=F32, precision=HI)


def _dot_nt(a, b):
    return lax.dot_general(a, b, (((1,), (1,)), ((), ())), preferred_element_type=F32)


def _dot_tn(a, b):
    return lax.dot_general(a, b, (((0,), (0,)), ((), ())), preferred_element_type=F32)


def _rms(x, g):
    return x * lax.rsqrt(jnp.mean(x * x, axis=-1, keepdims=True) + EPS) * g


def _silu(x):
    return x * jax.nn.sigmoid(x)


def _head_layer_norm_t(v):
    mu = jnp.mean(v, axis=0, keepdims=True)
    vc = v - mu
    return vc * lax.rsqrt(jnp.mean(vc * vc, axis=0, keepdims=True) + EPS)


def _mod_kernel(c_ref, w_ref, b_ref, o_ref):
    o_ref[...] = _dot(_silu(c_ref[...]).astype(BF16), w_ref[...].astype(BF16)) + b_ref[...]


def _modulation(cc, w_mod, b_mod):
    depth, d, n6 = w_mod.shape
    rows = cc.shape[0]
    tn = 1536
    return pl.pallas_call(
        _mod_kernel,
        grid=(depth, n6 // tn),
        in_specs=[
            pl.BlockSpec((rows, d), lambda l, j: (0, 0)),
            pl.BlockSpec((None, d, tn), lambda l, j: (l, 0, j)),
            pl.BlockSpec((None, 1, tn), lambda l, j: (l, 0, j)),
        ],
        out_specs=pl.BlockSpec((None, rows, tn), lambda l, j: (l, 0, j)),
        out_shape=jax.ShapeDtypeStruct((depth, rows, n6), F32),
        compiler_params=pltpu.CompilerParams(
            dimension_semantics=("arbitrary", "arbitrary"), vmem_limit_bytes=VMEM_LIMIT),
        name="modulation",
    )(cc, w_mod, b_mod.reshape(depth, 1, n6))


def _in_kernel(x_ref, xp_ref, xn_ref, mod_ref, g_ref, wt_ref, wk_ref, cos_ref, sin_ref, cost_ref, sint_ref,
               conv_ref, pw_ref, ps_ref,
               zs_ref, qk_ref, vo_ref, gt_ref, aq_ref, av_ref, ak_ref, d_ref, *, n_tiles, ctx_len, seq_len):
    tm = x_ref.shape[0]
    j = pl.program_id(1)
    has_prev = (j >= 2).astype(F32)
    has_next = jnp.logical_and(j >= 1, j < n_tiles - 1).astype(F32)

    def prep(x):
        return (_rms(x, g_ref[...]) * (1.0 + mod_ref[1:2, :]) + mod_ref[0:1, :]).astype(BF16)

    hb = prep(x_ref[...])
    hh = prep(jnp.concatenate([xp_ref[...], xn_ref[...]], axis=0))

    zs_ref[...] = _dot_nt(wt_ref[R_SGU:R_QK, :], hb).astype(BF16)
    vo_ref[...] = _dot_nt(wt_ref[R_VO:R_AQ, :], hb).astype(BF16)
    gt_ref[...] = _dot_nt(wt_ref[R_GATE:R_END, :], hb)
    av_ref[...] = _dot_nt(wt_ref[R_AV:R_GATE, :], hb).astype(BF16)

    z = _dot_nt(wt_ref[R_QK:R_VO, :], hb)
    zh = _dot_nt(wt_ref[R_QK:R_VO, :], hh)
    lane = lax.broadcasted_iota(jnp.int32, z.shape, 1)
    before = jnp.where(lane == 0, zh[:, HALO - 1:HALO] * has_prev, pltpu.roll(z, 1, axis=1))
    after = jnp.where(lane == tm - 1, zh[:, HALO:HALO + 1] * has_next, pltpu.roll(z, tm - 1, axis=1))
    qk = _silu(before * conv_ref[0] + z * conv_ref[1] + after * conv_ref[2])
    qk_ref[0:GROUP_W, :] = qk[0:GROUP_W].astype(BF16)
    qk_ref[GROUP_W:2 * GROUP_W, :] = (qk[GROUP_W:2 * GROUP_W] * (HEAD_DIM ** -0.5)).astype(BF16)

    q = _dot_nt(wt_ref[R_AQ:R_AV, :], hb)
    cos_t = jnp.concatenate([cost_ref[...]] * N_HEADS, axis=0)
    sin_t = jnp.concatenate([sint_ref[...]] * N_HEADS, axis=0)
    row = lax.broadcasted_iota(jnp.int32, q.shape, 0)
    rot_q = jnp.where((row & 31) < 16, pltpu.roll(q, GROUP_W - 16, axis=0), pltpu.roll(q, 16, axis=0))
    aq_ref[...] = ((q * cos_t + rot_q * sin_t) * (HEAD_DIM ** -0.5)).astype(BF16)

    k = _dot(hb, wk_ref[:, T_AK:T_POOL])
    klane = lax.broadcasted_iota(jnp.int32, k.shape, 1)
    rot_k = jnp.where((klane & 31) < 16, pltpu.roll(k, KV_W - 16, axis=1), pltpu.roll(k, 16, axis=1))
    ak_ref[...] = (k * cos_ref[...] + rot_k * sin_ref[...]).astype(BF16)

    zp = _dot(hb, wk_ref[:, T_POOL:T_END])
    zph = _dot(hh, wk_ref[:, T_POOL:T_END])
    rows = tm + 2 * HALO
    ext = jnp.concatenate([zph[0:HALO] * has_prev, zp, zph[HALO:2 * HALO] * has_next], axis=0)
    s2 = ext + pltpu.roll(ext, 1, axis=0)
    s4 = pltpu.roll(s2, 1, axis=0) + pltpu.roll(s2, rows - 1, axis=0)
    s8 = pltpu.roll(s4, 2, axis=0) + pltpu.roll(s4, rows - 2, axis=0)
    s16 = pltpu.roll(s8, 4, axis=0) + pltpu.roll(s8, rows - 4, axis=0)
    sums = [s[HALO:HALO + tm] for s in (s2, s4, s8, s16)]
    plane = lax.broadcasted_iota(jnp.int32, zp.shape, 1)
    t = lax.broadcasted_iota(jnp.int32, zp.shape, 0) + jnp.where(j == 0, 0, (j - 1) * tm)
    seg_len = jnp.where(j == 0, ctx_len, seq_len)
    pooled = jnp.zeros_like(zp)
    for gi, w in enumerate(POOL_WINDOWS):
        cnt = (jnp.minimum(t + w // 2, seg_len) - jnp.maximum(t - w // 2, 0)).astype(F32)
        in_group = (plane >= gi * HEAD_DIM) & (plane < (gi + 1) * HEAD_DIM)
        pooled = jnp.where(in_group, sums[gi] / cnt, pooled)
    d_ref[...] = (_dot((pooled - zp).astype(BF16), pw_ref[...]) * ps_ref[...]).astype(BF16)


def _in_proj(xs, mods, g0, wt, wk, tables, conv_b, pool_w_bd, pool_scale, n_batch_rows, ctx_len, seq_len):
    B, S, D = xs.shape
    tm = ROW_TILE
    n_tiles = S // tm
    per = tm // HALO
    cos_k, sin_k, cos_t, sin_t = tables
    row = lambda b, j: (b, j, 0)
    colm = lambda b, j: (b, 0, j)
    c2 = lambda b, j: (0, 0)
    kern = functools.partial(_in_kernel, n_tiles=n_tiles, ctx_len=ctx_len, seq_len=seq_len)
    fm = lambda rows, dt: (pl.BlockSpec((None, rows, tm), colm), jax.ShapeDtypeStruct((B, rows, S), dt))
    tk = lambda cols, dt: (pl.BlockSpec((None, tm, cols), row), jax.ShapeDtypeStruct((B, S, cols), dt))
    outs = [fm(2 * GROUP_W, BF16), fm(2 * GROUP_W, BF16), fm(2 * GROUP_W, BF16), fm(4 * N_HEADS, F32),
            fm(GROUP_W, BF16), fm(KV_W, BF16), tk(KV_W, BF16), tk(GROUP_W, BF16)]
    return pl.pallas_call(
        kern,
        grid=(B, n_tiles),
        in_specs=[
            pl.BlockSpec((None, tm, D), row),
            pl.BlockSpec((None, HALO, D), lambda b, j: (b, jnp.maximum(j * per - 1, 0), 0)),
            pl.BlockSpec((None, HALO, D), lambda b, j: (b, jnp.minimum((j + 1) * per, S // HALO - 1), 0)),
            pl.BlockSpec((None, 6, D), lambda b, j: (jnp.where(j == 0, n_batch_rows, b), 0, 0)),
            pl.BlockSpec((1, D), c2),
            pl.BlockSpec((R_END, D), c2, pipeline_mode=pl.Buffered(1)),
            pl.BlockSpec((D, T_END), c2, pipeline_mode=pl.Buffered(1)),
            pl.BlockSpec((tm, KV_W), lambda b, j: (j, 0)),
            pl.BlockSpec((tm, KV_W), lambda b, j: (j, 0)),
            pl.BlockSpec((HEAD_DIM, tm), lambda b, j: (0, j)),
            pl.BlockSpec((HEAD_DIM, tm), lambda b, j: (0, j)),
            pl.BlockSpec((3, 2 * GROUP_W, tm), lambda b, j: (0, 0, 0)),
            pl.BlockSpec((GROUP_W, GROUP_W), c2),
            pl.BlockSpec((1, GROUP_W), c2),
        ],
        out_specs=[o[0] for o in outs],
        out_shape=[o[1] for o in outs],
        compiler_params=pltpu.CompilerParams(
            dimension_semantics=("arbitrary", "arbitrary"), vmem_limit_bytes=VMEM_LIMIT),
        name="in_proj",
    )(xs, xs, xs, mods, g0, wt, wk, cos_k, sin_k, cos_t, sin_t, conv_b, pool_w_bd, pool_scale)


def _sgu_kernel(z_ref, w_ref, bias_ref, o_ref):
    for b in range(z_ref.shape[0]):
        g = jax.nn.gelu(z_ref[b].astype(F32))
        for h in range(N_HEADS):
            u = g[h * HEAD_DIM:(h + 1) * HEAD_DIM]
            vh = _head_layer_norm_t(g[GROUP_W + h * HEAD_DIM:GROUP_W + (h + 1) * HEAD_DIM]).astype(BF16)
            mixed = _dot(vh, w_ref[h]) + bias_ref[h:h + 1, :]
            o_ref[b, h * HEAD_DIM:(h + 1) * HEAD_DIM, :] = (u * mixed).astype(BF16)


def _sgu(zs, sgu_wt, sgu_b):
    B, _, S = zs.shape
    return pl.pallas_call(
        _sgu_kernel,
        grid=(S // CHUNK,),
        in_specs=[
            pl.BlockSpec((B, 2 * GROUP_W, CHUNK), lambda c: (0, 0, c)),
            pl.BlockSpec((N_HEADS, CHUNK, CHUNK), lambda c: (0, 0, 0)),
            pl.BlockSpec((N_HEADS, CHUNK), lambda c: (0, 0)),
        ],
        out_specs=pl.BlockSpec((B, GROUP_W, CHUNK), lambda c: (0, 0, c)),
        out_shape=jax.ShapeDtypeStruct((B, GROUP_W, S), BF16),
        compiler_params=pltpu.CompilerParams(dimension_semantics=("arbitrary",)),
        name="sgu",
    )(zs, sgu_wt, sgu_b)


def _att_kernel(q_ref, kp_ref, kc_ref, kn_ref, kx_ref, vp_ref, vc_ref, vn_ref, vx_ref, sink_ref, o_ref,
                *, n_ctx_chunks, n_chunks):
    c = pl.program_id(0)
    prev_ok = c > n_ctx_chunks
    cur_ok = c >= n_ctx_chunks
    next_ok = jnp.logical_and(cur_ok, c < n_chunks - 1)
    n_keys = 3 * CHUNK + n_ctx_chunks * CHUNK
    j = lax.broadcasted_iota(jnp.int32, (n_keys, CHUNK), 0)
    i = lax.broadcasted_iota(jnp.int32, (n_keys, CHUNK), 1)
    lo = jnp.where(cur_ok, jnp.where(prev_ok, 0, CHUNK), 3 * CHUNK)
    hi = jnp.where(next_ok, 3 * CHUNK, 2 * CHUNK)
    valid = ((j >= i) & (j <= i + 2 * CHUNK) & (j >= lo) & (j < hi)) | (j >= 3 * CHUNK)
    zeros = jnp.zeros((HEAD_DIM, CHUNK), BF16)
    for b in range(q_ref.shape[0]):
        kcat = jnp.concatenate([kp_ref[b], kc_ref[b], kn_ref[b], kx_ref[b]], axis=0)
        vcat = jnp.concatenate([vp_ref[b], vc_ref[b], vn_ref[b], vx_ref[b]], axis=1)
        for h in range(N_HEADS):
            kv = h // (N_HEADS // KV_HEADS)
            qh = q_ref[b, h * HEAD_DIM:(h + 1) * HEAD_DIM, :]
            qm = jnp.concatenate([qh, zeros] if kv == 0 else [zeros, qh], axis=0)
            s = jnp.where(valid, _dot(kcat, qm), NEG)
            sink = sink_ref[h:h + 1, :]
            m = jnp.maximum(jnp.max(s, axis=0, keepdims=True), sink)
            p = jnp.exp(s - m)
            denom = jnp.sum(p, axis=0, keepdims=True) + jnp.exp(sink - m)
            o = _dot(vcat[kv * HEAD_DIM:(kv + 1) * HEAD_DIM, :], p.astype(BF16))
            o_ref[b, h * HEAD_DIM:(h + 1) * HEAD_DIM, :] = (o / denom).astype(BF16)


def _attention(aq, ak, av, sink_rows, n_ctx_chunks):
    B, _, S = aq.shape
    n_chunks = S // CHUNK
    ctx_w = n_ctx_chunks * CHUNK
    kern = functools.partial(_att_kernel, n_ctx_chunks=n_ctx_chunks, n_chunks=n_chunks)
    prev = lambda c: jnp.maximum(c - 1, 0)
    nxt = lambda c: jnp.minimum(c + 1, n_chunks - 1)
    kspec = lambda f: pl.BlockSpec((B, CHUNK, KV_W), lambda c: (0, f(c), 0))
    vspec = lambda f: pl.BlockSpec((B, KV_W, CHUNK), lambda c: (0, 0, f(c)))
    ident = lambda c: c
    return pl.pallas_call(
        kern,
        grid=(n_chunks,),
        in_specs=[
            pl.BlockSpec((B, GROUP_W, CHUNK), lambda c: (0, 0, c)),
            kspec(prev), kspec(ident), kspec(nxt), pl.BlockSpec((B, ctx_w, KV_W), lambda c: (0, 0, 0)),
            vspec(prev), vspec(ident), vspec(nxt), pl.BlockSpec((B, KV_W, ctx_w), lambda c: (0, 0, 0)),
            pl.BlockSpec((SUBLANES, CHUNK), lambda c: (0, 0)),
        ],
        out_specs=pl.BlockSpec((B, GROUP_W, CHUNK), lambda c: (0, 0, c)),
        out_shape=jax.ShapeDtypeStruct((B, GROUP_W, S), BF16),
        compiler_params=pltpu.CompilerParams(dimension_semantics=("arbitrary",)),
        name="attention",
    )(aq, ak, ak, ak, ak, av, av, av, av, sink_rows)


def _log_sigmoid(x):
    return jnp.minimum(x, 0.0) - jnp.log1p(jnp.exp(-jnp.abs(x)))


def _mlstm_direction(qk, v, gates, st_ref, m_ref, o_ref, b, slot0, *, forward):
    s_i = lax.broadcasted_iota(jnp.int32, (CHUNK, CHUNK), 0)
    t_i = lax.broadcasted_iota(jnp.int32, (CHUNK, CHUNK), 1)
    causal = (s_i <= t_i) if forward else (s_i >= t_i)
    li8 = gates[0:2 * N_HEADS]
    lf8 = _log_sigmoid(gates[2 * N_HEADS:4 * N_HEADS])
    cum8 = _dot_hi(lf8, jnp.where(causal, 1.0, 0.0).astype(F32))
    tot8 = _dot_hi(lf8, jnp.ones((CHUNK, CHUNK), F32))
    a8 = li8 - cum8
    ones_row = jnp.where(lax.broadcasted_iota(jnp.int32, (ST_ROWS - HEAD_DIM, CHUNK), 0) == 0, 1.0, 0.0).astype(BF16)
    r0 = 0 if forward else N_HEADS
    for h in range(N_HEADS):
        r = r0 + h
        slot = slot0 + h
        a = a8[r:r + 1]
        bcum = cum8[r:r + 1]
        b_end = tot8[r:r + 1]
        m = m_ref[slot:slot + 1, :]
        a_col = jnp.broadcast_to(a, (CHUNK, CHUNK)).T
        a_max = jnp.max(a_col, axis=0, keepdims=True)
        big_m = jnp.maximum(jnp.max(jnp.where(causal, a_col, NEG), axis=0, keepdims=True), m)
        inter = jnp.exp(m - big_m)
        e = jnp.exp(jnp.where(causal, a_col - big_m, NEG))
        qh = qk[h * HEAD_DIM:(h + 1) * HEAD_DIM]
        kh = qk[GROUP_W + h * HEAD_DIM:GROUP_W + (h + 1) * HEAD_DIM]
        vh = v[h * HEAD_DIM:(h + 1) * HEAD_DIM]
        s = _dot_tn(kh, qh) * e
        st = st_ref[slot]
        cq = _dot(st[:, 0:HEAD_DIM].astype(BF16), qh)
        num = cq[0:HEAD_DIM] * inter + _dot(vh, s.astype(BF16))
        den = inter * cq[HEAD_DIM:HEAD_DIM + 1] + jnp.sum(s, axis=0, keepdims=True)
        floor = jnp.exp(-(bcum + big_m))
        o_ref[b, h * HEAD_DIM:(h + 1) * HEAD_DIM, :] = num / jnp.maximum(jnp.abs(den), floor)
        m_new = jnp.maximum(b_end + m, b_end + a_max)
        decay = jnp.exp(b_end + m - m_new)
        w = jnp.exp(b_end + a - m_new)
        kw = (kh.astype(F32) * w).astype(BF16)
        v_ext = jnp.concatenate([vh, ones_row], axis=0)
        upd = _dot_nt(v_ext, kw)
        st_ref[slot, :, 0:HEAD_DIM] = st[:, 0:HEAD_DIM] * decay[:, 0:HEAD_DIM] + upd
        m_ref[slot:slot + 1, :] = m_new


def _mlstm_kernel(qkf_ref, vf_ref, gf_ref, qkb_ref, vb_ref, gb_ref, bias_ref, hf_ref, hb_ref, st_ref, m_ref):
    @pl.when(pl.program_id(0) == 0)
    def _():
        st_ref[...] = jnp.zeros_like(st_ref)
        m_ref[...] = jnp.zeros_like(m_ref)

    for b in range(qkf_ref.shape[0]):
        slot0 = b * 2 * N_HEADS
        _mlstm_direction(qkf_ref[b], vf_ref[b], gf_ref[b] + bias_ref[...], st_ref, m_ref, hf_ref, b, slot0,
                         forward=True)
        _mlstm_direction(qkb_ref[b], vb_ref[b], gb_ref[b] + bias_ref[...], st_ref, m_ref, hb_ref, b,
                         slot0 + N_HEADS, forward=False)


def _mlstm(qk, vo, gt, gate_bias, n_ctx_chunks):
    B, _, S = qk.shape
    n_chunks = S // CHUNK
    fwd = lambda i: i
    bwd = lambda i: jnp.where(i < n_ctx_chunks, n_ctx_chunks - 1 - i, n_chunks - 1 + n_ctx_chunks - i)

    def specs(order):
        return [
            pl.BlockSpec((B, 2 * GROUP_W, CHUNK), lambda i: (0, 0, order(i))),
            pl.BlockSpec((B, GROUP_W, CHUNK), lambda i: (0, 0, order(i))),
            pl.BlockSpec((B, 4 * N_HEADS, CHUNK), lambda i: (0, 0, order(i))),
        ]

    n_slots = B * 2 * N_HEADS
    return pl.pallas_call(
        _mlstm_kernel,
        grid=(n_chunks,),
        in_specs=specs(fwd) + specs(bwd) + [pl.BlockSpec((4 * N_HEADS, CHUNK), lambda i: (0, 0))],
        out_specs=[
            pl.BlockSpec((B, GROUP_W, CHUNK), lambda i: (0, 0, fwd(i))),
            pl.BlockSpec((B, GROUP_W, CHUNK), lambda i: (0, 0, bwd(i))),
        ],
        out_shape=[jax.ShapeDtypeStruct((B, GROUP_W, S), F32)] * 2,
        scratch_shapes=[
            pltpu.VMEM((n_slots, ST_ROWS, LANES), F32),
            pltpu.VMEM((n_slots, LANES), F32),
        ],
        compiler_params=pltpu.CompilerParams(dimension_semantics=("arbitrary",)),
        name="mlstm",
    )(qk, vo, gt, qk, vo, gt, gate_bias)


def _out_kernel(x_ref, a_ref, hf_ref, hb_ref, zo_ref, c_ref, d_ref, mod_ref, g_ref, mg_ref,
                wo_ref, wi_ref, wf_ref, o_ref, *, hidden):
    hs = hf_ref[...] + hb_ref[...]
    hn = jnp.concatenate([_head_layer_norm_t(hs[h * HEAD_DIM:(h + 1) * HEAD_DIM]) for h in range(N_HEADS)], axis=0)
    bm = (hn * mg_ref[...] * jax.nn.sigmoid(zo_ref[...].astype(F32))).astype(BF16)
    y = (_dot_tn(a_ref[...], wo_ref[0:GROUP_W, :])
         + _dot_tn(bm, wo_ref[GROUP_W:2 * GROUP_W, :])
         + _dot_tn(c_ref[...], wo_ref[2 * GROUP_W:3 * GROUP_W, :])
         + _dot(d_ref[...], wo_ref[3 * GROUP_W:4 * GROUP_W, :]))
    x1 = x_ref[...] + mod_ref[2:3, :] * _rms(y, g_ref[1:2, :])
    t = _rms(x1, g_ref[2:3, :]) * (1.0 + mod_ref[4:5, :]) + mod_ref[3:4, :]
    tb = t.astype(BF16)
    gate = _dot(tb, wi_ref[:, 0:hidden])
    up = _dot(tb, wi_ref[:, hidden:2 * hidden])
    act = (_silu(gate) * up).astype(BF16)
    f = _dot(act, wf_ref[...])
    o_ref[...] = x1 + mod_ref[5:6, :] * _rms(f, g_ref[3:4, :])


def _out_ffn(xs, a, hf, hb, vo, c, d, mods, norm_g, mnorm_g, w_out, w_ffn_in, w_ffn_out, n_batch_rows):
    B, S, D = xs.shape
    tm = ROW_TILE
    hidden = w_ffn_out.shape[0]
    row = lambda b, j: (b, j, 0)
    colm = lambda b, j: (b, 0, j)
    c2 = lambda b, j: (0, 0)
    fm = pl.BlockSpec((None, GROUP_W, tm), colm)
    kern = functools.partial(_out_kernel, hidden=hidden)
    return pl.pallas_call(
        kern,
        grid=(B, S // tm),
        in_specs=[
            pl.BlockSpec((None, tm, D), row),
            fm, fm, fm,
            pl.BlockSpec((None, GROUP_W, tm), lambda b, j: (b, 1, j)),
            fm,
            pl.BlockSpec((None, tm, GROUP_W), row),
            pl.BlockSpec((None, 6, D), lambda b, j: (jnp.where(j == 0, n_batch_rows, b), 0, 0)),
            pl.BlockSpec((4, D), c2),
            pl.BlockSpec((GROUP_W, tm), c2),
            pl.BlockSpec((D, D), c2, pipeline_mode=pl.Buffered(1)),
            pl.BlockSpec((D, 2 * hidden), c2, pipeline_mode=pl.Buffered(1)),
            pl.BlockSpec((hidden, D), c2, pipeline_mode=pl.Buffered(1)),
        ],
        out_specs=pl.BlockSpec((None, tm, D), row),
        out_shape=jax.ShapeDtypeStruct((B, S, D), F32),
        compiler_params=pltpu.CompilerParams(
            dimension_semantics=("arbitrary", "arbitrary"), vmem_limit_bytes=VMEM_LIMIT),
        name="out_ffn",
    )(xs, a, hf, hb, vo, c, d, mods, norm_g, mnorm_g, w_out, w_ffn_in, w_ffn_out)


def _rope_tables(n_tokens, ctx_len):
    rows = n_tokens // GRID_W
    axis_freq = HEAD_DIM // 4
    row = jnp.broadcast_to(jnp.arange(rows)[:, None], (rows, GRID_W)).reshape(-1).astype(F32)
    col = jnp.broadcast_to(jnp.arange(GRID_W)[None, :], (rows, GRID_W)).reshape(-1).astype(F32)
    inv = jnp.power(ROPE_BASE, -jnp.arange(axis_freq, dtype=F32) * 2.0 / (2 * axis_freq))
    ar = row[:, None] * inv
    ac = col[:, None] * inv
    ang = jnp.concatenate([ar, ar, ac, ac], axis=-1)
    cos, sin = jnp.cos(ang), jnp.sin(ang)
    sign = jnp.where((jnp.arange(HEAD_DIM) % 32) < 16, -1.0, 1.0).astype(F32)
    cos = jnp.concatenate([jnp.ones((ctx_len, HEAD_DIM), F32), cos], axis=0)
    sin = jnp.concatenate([jnp.zeros((ctx_len, HEAD_DIM), F32), sin * sign], axis=0)
    return jnp.tile(cos, (1, KV_HEADS)), jnp.tile(sin, (1, KV_HEADS)), cos.T, sin.T


def _arrange_w_in(w_in):
    ml0 = 2 * GROUP_W
    gate0 = ml0 + 4 * GROUP_W
    att0 = gate0 + 4 * N_HEADS
    pool0 = att0 + GROUP_W + 2 * KV_W
    gates = w_in[:, gate0:att0].reshape(-1, 2, 2, N_HEADS)
    gates = gates.transpose(0, 2, 1, 3).reshape(-1, 4 * N_HEADS)
    feat = jnp.concatenate([
        w_in[:, 0:ml0],
        w_in[:, ml0:ml0 + 2 * GROUP_W],
        w_in[:, ml0 + 2 * GROUP_W:gate0],
        w_in[:, att0:att0 + GROUP_W],
        w_in[:, att0 + GROUP_W + KV_W:pool0],
        gates,
    ], axis=1)
    tok = jnp.concatenate([w_in[:, att0 + GROUP_W:att0 + GROUP_W + KV_W], w_in[:, pool0:pool0 + GROUP_W]], axis=1)
    return feat.T.astype(BF16), tok.astype(BF16)


def _block_diag(w):
    g = w.shape[0]
    eye = jnp.eye(g, dtype=w.dtype)
    return (eye[:, None, :, None] * w[:, :, None, :]).reshape(g * w.shape[1], g * w.shape[2])


def kernel(x, c, ctx, c_ctx, w_mod, b_mod, norm_g, w_in, w_out, sgu_w, sgu_b, mlstm_conv_w, mlstm_gate_b,
           mlstm_norm_g, attn_sink, pool_w, pool_scale, w_ffn_in, w_ffn_out):
    B, N, D = x.shape
    ctx_len = ctx.shape[1]
    depth = w_mod.shape[0]
    assert D == 4 * GROUP_W and N % ROW_TILE == 0 and ctx_len == ROW_TILE
    assert w_in.shape[2] == 2 * GROUP_W + 4 * GROUP_W + 4 * N_HEADS + GROUP_W + 2 * KV_W + GROUP_W
    n_ctx_chunks = ctx_len // CHUNK

    xs = jnp.concatenate([ctx, x], axis=1)
    rows = -(-(B + 1) // SUBLANES) * SUBLANES
    cc = jnp.zeros((rows, D), F32).at[0:B].set(c).at[B].set(c_ctx)
    mods = _modulation(cc, w_mod, b_mod).reshape(depth, rows, 6, D)
    tables = _rope_tables(N, ctx_len)

    for l in range(depth):
        wt, wk = _arrange_w_in(w_in[l])
        conv_b = jnp.broadcast_to(mlstm_conv_w[l][:, :, None], (3, 2 * GROUP_W, ROW_TILE))
        gate_bias = jnp.broadcast_to(
            mlstm_gate_b[l].reshape(2, 2, N_HEADS).transpose(1, 0, 2).reshape(4 * N_HEADS, 1), (4 * N_HEADS, CHUNK))
        sink_rows = jnp.zeros((SUBLANES, CHUNK), F32).at[0:N_HEADS].set(
            jnp.broadcast_to(attn_sink[l][:, None], (N_HEADS, CHUNK)))
        mnorm_b = jnp.broadcast_to(mlstm_norm_g[l][:, None], (GROUP_W, ROW_TILE))

        zs, qk, vo, gt, aq, av, ak, dm = _in_proj(
            xs, mods[l], norm_g[l, 0:1], wt, wk, tables, conv_b, _block_diag(pool_w[l]).astype(BF16),
            pool_scale[l].reshape(1, GROUP_W), B, ctx_len, N)
        a = _sgu(zs, sgu_w[l].transpose(0, 2, 1).astype(BF16), sgu_b[l])
        hf, hb = _mlstm(qk, vo, gt, gate_bias, n_ctx_chunks)
        cm = _attention(aq, ak, av, sink_rows, n_ctx_chunks)
        xs = _out_ffn(xs, a, hf, hb, vo, cm, dm, mods[l], norm_g[l], mnorm_b,
                      w_out[l].astype(BF16), w_ffn_in[l].astype(BF16), w_ffn_out[l].astype(BF16), B)
    return xs[:, ctx_len:, :]
```

```python
import functools

import jax
import jax.numpy as jnp
from jax import lax
from jax.experimental import pallas as pl
from jax.experimental.pallas import tpu as pltpu

F32 = jnp.float32
BF16 = jnp.bfloat16

GRID_W = 64
ROPE_BASE = 10000.0
EPS = 1e-6
LANES = 128
SUBLANES = 8
BF16_ROWS = 16
CHUNK = 128
HEAD_DIM = 64
N_HEADS = 4
KV_HEADS = 2
GROUP_W = N_HEADS * HEAD_DIM
KV_W = KV_HEADS * HEAD_DIM
POOL_WINDOWS = (2, 4, 8, 16)
HALO = 8
ROW_TILE = 256
NEG = -1e30
VMEM_LIMIT = 56 * 1024 * 1024

R_SGU = 0
R_QK = 512
R_VO = 1024
R_AQ = 1536
R_AV = 1792
R_GATE = 1920
R_END = 1936
T_AK = 0
T_POOL = KV_W
T_END = KV_W + GROUP_W

ST_ROWS = 80

HI = lax.Precision.HIGHEST


def _dot(a, b):
    return jnp.dot(a, b, preferred_element_type=F32)


def _dot_hi(a, b):
    return jnp.dot(a, b, preferred_element_type=F32, precision=HI)


def _dot_nt(a, b):
    return lax.dot_general(a, b, (((1,), (1,)), ((), ())), preferred_element_type=F32)


def _dot_tn(a, b):
    return lax.dot_general(a, b, (((0,), (0,)), ((), ())), preferred_element_type=F32)


def _rms(x, g):
    return x * lax.rsqrt(jnp.mean(x * x, axis=-1, keepdims=True) + EPS) * g


def _silu(x):
    return x * jax.nn.sigmoid(x)


def _head_layer_norm_t(v):
    mu = jnp.mean(v, axis=0, keepdims=True)
    vc = v - mu
    return vc * lax.rsqrt(jnp.mean(vc * vc, axis=0, keepdims=True) + EPS)


def _mod_kernel(c_ref, w_ref, b_ref, o_ref):
    o_ref[...] = _dot(_silu(c_ref[...]).astype(BF16), w_ref[...].astype(BF16)) + b_ref[...]


def _modulation(cc, w_mod, b_mod):
    depth, d, n6 = w_mod.shape
    rows = cc.shape[0]
    tn = 1536
    return pl.pallas_call(
        _mod_kernel,
        grid=(depth, n6 // tn),
        in_specs=[
            pl.BlockSpec((rows, d), lambda l, j: (0, 0)),
            pl.BlockSpec((None, d, tn), lambda l, j: (l, 0, j)),
            pl.BlockSpec((None, 1, tn), lambda l, j: (l, 0, j)),
        ],
        out_specs=pl.BlockSpec((None, rows, tn), lambda l, j: (l, 0, j)),
        out_shape=jax.ShapeDtypeStruct((depth, rows, n6), F32),
        compiler_params=pltpu.CompilerParams(
            dimension_semantics=("arbitrary", "arbitrary"), vmem_limit_bytes=VMEM_LIMIT),
        name="modulation",
    )(cc, w_mod, b_mod.reshape(depth, 1, n6))


def _in_kernel(x_ref, xp_ref, xn_ref, mod_ref, g_ref, wt_ref, wk_ref, cos_ref, sin_ref, cost_ref, sint_ref,
               conv_ref, pw_ref, ps_ref,
               zs_ref, qk_ref, vo_ref, gt_ref, aq_ref, av_ref, ak_ref, d_ref, *, n_tiles, ctx_len, seq_len):
    tm = x_ref.shape[0]
    j = pl.program_id(1)
    has_prev = (j >= 2).astype(F32)
    has_next = jnp.logical_and(j >= 1, j < n_tiles - 1).astype(F32)

    def prep(x):
        return (_rms(x, g_ref[...]) * (1.0 + mod_ref[1:2, :]) + mod_ref[0:1, :]).astype(BF16)

    hb = prep(x_ref[...])
    hh = prep(jnp.concatenate([xp_ref[...], xn_ref[...]], axis=0))

    zs_ref[...] = _dot_nt(wt_ref[R_SGU:R_QK, :], hb).astype(BF16)
    vo_ref[...] = _dot_nt(wt_ref[R_VO:R_AQ, :], hb).astype(BF16)
    gt_ref[...] = _dot_nt(wt_ref[R_GATE:R_END, :], hb)
    av_ref[...] = _dot_nt(wt_ref[R_AV:R_GATE, :], hb).astype(BF16)

    z = _dot_nt(wt_ref[R_QK:R_VO, :], hb)
    zh = _dot_nt(wt_ref[R_QK:R_VO, :], hh)
    lane = lax.broadcasted_iota(jnp.int32, z.shape, 1)
    before = jnp.where(lane == 0, zh[:, HALO - 1:HALO] * has_prev, pltpu.roll(z, 1, axis=1))
    after = jnp.where(lane == tm - 1, zh[:, HALO:HALO + 1] * has_next, pltpu.roll(z, tm - 1, axis=1))
    qk = _silu(before * conv_ref[0] + z * conv_ref[1] + after * conv_ref[2])
    qk_ref[0:GROUP_W, :] = qk[0:GROUP_W].astype(BF16)
    qk_ref[GROUP_W:2 * GROUP_W, :] = (qk[GROUP_W:2 * GROUP_W] * (HEAD_DIM ** -0.5)).astype(BF16)

    q = _dot_nt(wt_ref[R_AQ:R_AV, :], hb)
    cos_t = jnp.concatenate([cost_ref[...]] * N_HEADS, axis=0)
    sin_t = jnp.concatenate([sint_ref[...]] * N_HEADS, axis=0)
    row = lax.broadcasted_iota(jnp.int32, q.shape, 0)
    rot_q = jnp.where((row & 31) < 16, pltpu.roll(q, GROUP_W - 16, axis=0), pltpu.roll(q, 16, axis=0))
    aq_ref[...] = ((q * cos_t + rot_q * sin_t) * (HEAD_DIM ** -0.5)).astype(BF16)

    k = _dot(hb, wk_ref[:, T_AK:T_POOL])
    klane = lax.broadcasted_iota(jnp.int32, k.shape, 1)
    rot_k = jnp.where((klane & 31) < 16, pltpu.roll(k, KV_W - 16, axis=1), pltpu.roll(k, 16, axis=1))
    ak_ref[...] = (k * cos_ref[...] + rot_k * sin_ref[...]).astype(BF16)

    zp = _dot(hb, wk_ref[:, T_POOL:T_END])
    zph = _dot(hh, wk_ref[:, T_POOL:T_END])
    rows = tm + 2 * HALO
    ext = jnp.concatenate([zph[0:HALO] * has_prev, zp, zph[HALO:2 * HALO] * has_next], axis=0)
    s2 = ext + pltpu.roll(ext, 1, axis=0)
    s4 = pltpu.roll(s2, 1, axis=0) + pltpu.roll(s2, rows - 1, axis=0)
    s8 = pltpu.roll(s4, 2, axis=0) + pltpu.roll(s4, rows - 2, axis=0)
    s16 = pltpu.roll(s8, 4, axis=0) + pltpu.roll(s8, rows - 4, axis=0)
    sums = [s[HALO:HALO + tm] for s in (s2, s4, s8, s16)]
    plane = lax.broadcasted_iota(jnp.int32, zp.shape, 1)
    t = lax.broadcasted_iota(jnp.int32, zp.shape, 0) + jnp.where(j == 0, 0, (j - 1) * tm)
    seg_len = jnp.where(j == 0, ctx_len, seq_len)
    pooled = jnp.zeros_like(zp)
    for gi, w in enumerate(POOL_WINDOWS):
        cnt = (jnp.minimum(t + w // 2, seg_len) - jnp.maximum(t - w // 2, 0)).astype(F32)
        in_group = (plane >= gi * HEAD_DIM) & (plane < (gi + 1) * HEAD_DIM)
        pooled = jnp.where(in_group, sums[gi] / cnt, pooled)
    d_ref[...] = (_dot((pooled - zp).astype(BF16), pw_ref[...]) * ps_ref[...]).astype(BF16)


def _in_proj(xs, mods, g0, wt, wk, tables, conv_b, pool_w_bd, pool_scale, n_batch_rows, ctx_len, seq_len):
    B, S, D = xs.shape
    tm = ROW_TILE
    n_tiles = S // tm
    per = tm // HALO
    cos_k, sin_k, cos_t, sin_t = tables
    row = lambda b, j: (b, j, 0)
    colm = lambda b, j: (b, 0, j)
    c2 = lambda b, j: (0, 0)
    kern = functools.partial(_in_kernel, n_tiles=n_tiles, ctx_len=ctx_len, seq_len=seq_len)
    fm = lambda rows, dt: (pl.BlockSpec((None, rows, tm), colm), jax.ShapeDtypeStruct((B, rows, S), dt))
    tk = lambda cols, dt: (pl.BlockSpec((None, tm, cols), row), jax.ShapeDtypeStruct((B, S, cols), dt))
    outs = [fm(2 * GROUP_W, BF16), fm(2 * GROUP_W, BF16), fm(2 * GROUP_W, BF16), fm(4 * N_HEADS, F32),
            fm(GROUP_W, BF16), fm(KV_W, BF16), tk(KV_W, BF16), tk(GROUP_W, BF16)]
    return pl.pallas_call(
        kern,
        grid=(B, n_tiles),
        in_specs=[
            pl.BlockSpec((None, tm, D), row),
            pl.BlockSpec((None, HALO, D), lambda b, j: (b, jnp.maximum(j * per - 1, 0), 0)),
            pl.BlockSpec((None, HALO, D), lambda b, j: (b, jnp.minimum((j + 1) * per, S // HALO - 1), 0)),
            pl.BlockSpec((None, 6, D), lambda b, j: (jnp.where(j == 0, n_batch_rows, b), 0, 0)),
            pl.BlockSpec((1, D), c2),
            pl.BlockSpec((R_END, D), c2, pipeline_mode=pl.Buffered(1)),
            pl.BlockSpec((D, T_END), c2, pipeline_mode=pl.Buffered(1)),
            pl.BlockSpec((tm, KV_W), lambda b, j: (j, 0)),
            pl.BlockSpec((tm, KV_W), lambda b, j: (j, 0)),
            pl.BlockSpec((HEAD_DIM, tm), lambda b, j: (0, j)),
            pl.BlockSpec((HEAD_DIM, tm), lambda b, j: (0, j)),
            pl.BlockSpec((3, 2 * GROUP_W, tm), lambda b, j: (0, 0, 0)),
            pl.BlockSpec((GROUP_W, GROUP_W), c2),
            pl.BlockSpec((1, GROUP_W), c2),
        ],
        out_specs=[o[0] for o in outs],
        out_shape=[o[1] for o in outs],
        compiler_params=pltpu.CompilerParams(
            dimension_semantics=("arbitrary", "arbitrary"), vmem_limit_bytes=VMEM_LIMIT),
        name="in_proj",
    )(xs, xs, xs, mods, g0, wt, wk, cos_k, sin_k, cos_t, sin_t, conv_b, pool_w_bd, pool_scale)


def _sgu_kernel(z_ref, w_ref, bias_ref, o_ref):
    for b in range(z_ref.shape[0]):
        g = jax.nn.gelu(z_ref[b].astype(F32))
        for h in range(N_HEADS):
            u = g[h * HEAD_DIM:(h + 1) * HEAD_DIM]
            vh = _head_layer_norm_t(g[GROUP_W + h * HEAD_DIM:GROUP_W + (h + 1) * HEAD_DIM]).astype(BF16)
            mixed = _dot(vh, w_ref[h]) + bias_ref[h:h + 1, :]
            o_ref[b, h * HEAD_DIM:(h + 1) * HEAD_DIM, :] = (u * mixed).astype(BF16)


def _sgu(zs, sgu_wt, sgu_b):
    B, _, S = zs.shape
    return pl.pallas_call(
        _sgu_kernel,
        grid=(S // CHUNK,),
        in_specs=[
            pl.BlockSpec((B, 2 * GROUP_W, CHUNK), lambda c: (0, 0, c)),
            pl.BlockSpec((N_HEADS, CHUNK, CHUNK), lambda c: (0, 0, 0)),
            pl.BlockSpec((N_HEADS, CHUNK), lambda c: (0, 0)),
        ],
        out_specs=pl.BlockSpec((B, GROUP_W, CHUNK), lambda c: (0, 0, c)),
        out_shape=jax.ShapeDtypeStruct((B, GROUP_W, S), BF16),
        compiler_params=pltpu.CompilerParams(dimension_semantics=("arbitrary",)),
        name="sgu",
    )(zs, sgu_wt, sgu_b)


def _att_kernel(q_ref, kp_ref, kc_ref, kn_ref, kx_ref, vp_ref, vc_ref, vn_ref, vx_ref, sink_ref, o_ref,
                *, n_ctx_chunks, n_chunks):
    c = pl.program_id(0)
    prev_ok = c > n_ctx_chunks
    cur_ok = c >= n_ctx_chunks
    next_ok = jnp.logical_and(cur_ok, c < n_chunks - 1)
    n_keys = 3 * CHUNK + n_ctx_chunks * CHUNK
    group = N_HEADS // KV_HEADS
    j = lax.broadcasted_iota(jnp.int32, (n_keys, group * CHUNK), 0)
    i = lax.broadcasted_iota(jnp.int32, (n_keys, group * CHUNK), 1) & (CHUNK - 1)
    lo = jnp.where(cur_ok, jnp.where(prev_ok, 0, CHUNK), 3 * CHUNK)
    hi = jnp.where(next_ok, 3 * CHUNK, 2 * CHUNK)
    valid = ((j >= i) & (j <= i + 2 * CHUNK) & (j >= lo) & (j < hi)) | (j >= 3 * CHUNK)
    zeros = jnp.zeros((HEAD_DIM, group * CHUNK), BF16)
    n_b = q_ref.shape[0]
    pairs = [(b, kv) for b in range(n_b) for kv in range(KV_HEADS)]
    scores = []
    for b, kv in pairs:
        kcat = jnp.concatenate([kp_ref[b], kc_ref[b], kn_ref[b], kx_ref[b]], axis=0)
        q2 = jnp.concatenate([q_ref[b, (kv * group + g) * HEAD_DIM:(kv * group + g + 1) * HEAD_DIM, :]
                              for g in range(group)], axis=1)
        qm = jnp.concatenate([q2, zeros] if kv == 0 else [zeros, q2], axis=0)
        scores.append(_dot(kcat, qm))
    probs = []
    for (b, kv), s in zip(pairs, scores):
        s = jnp.where(valid, s, NEG)
        sink = sink_ref[kv:kv + 1, :]
        m = jnp.maximum(jnp.max(s, axis=0, keepdims=True), sink)
        p = jnp.exp(s - m)
        denom = jnp.sum(p, axis=0, keepdims=True) + jnp.exp(sink - m)
        probs.append((p.astype(BF16), denom))
    for (b, kv), (p, denom) in zip(pairs, probs):
        vcat = jnp.concatenate([vp_ref[b], vc_ref[b], vn_ref[b], vx_ref[b]], axis=1)
        o = _dot(vcat[kv * HEAD_DIM:(kv + 1) * HEAD_DIM, :], p) / denom
        for g in range(group):
            h = kv * group + g
            o_ref[b, h * HEAD_DIM:(h + 1) * HEAD_DIM, :] = o[:, g * CHUNK:(g + 1) * CHUNK].astype(BF16)


def _attention(aq, ak, av, sink_rows, n_ctx_chunks):
    B, _, S = aq.shape
    n_chunks = S // CHUNK
    ctx_w = n_ctx_chunks * CHUNK
    kern = functools.partial(_att_kernel, n_ctx_chunks=n_ctx_chunks, n_chunks=n_chunks)
    prev = lambda c: jnp.maximum(c - 1, 0)
    nxt = lambda c: jnp.minimum(c + 1, n_chunks - 1)
    kspec = lambda f: pl.BlockSpec((B, CHUNK, KV_W), lambda c: (0, f(c), 0))
    vspec = lambda f: pl.BlockSpec((B, KV_W, CHUNK), lambda c: (0, 0, f(c)))
    ident = lambda c: c
    return pl.pallas_call(
        kern,
        grid=(n_chunks,),
        in_specs=[
            pl.BlockSpec((B, GROUP_W, CHUNK), lambda c: (0, 0, c)),
            kspec(prev), kspec(ident), kspec(nxt), pl.BlockSpec((B, ctx_w, KV_W), lambda c: (0, 0, 0)),
            vspec(prev), vspec(ident), vspec(nxt), pl.BlockSpec((B, KV_W, ctx_w), lambda c: (0, 0, 0)),
            pl.BlockSpec((SUBLANES, (N_HEADS // KV_HEADS) * CHUNK), lambda c: (0, 0)),
        ],
        out_specs=pl.BlockSpec((B, GROUP_W, CHUNK), lambda c: (0, 0, c)),
        out_shape=jax.ShapeDtypeStruct((B, GROUP_W, S), BF16),
        compiler_params=pltpu.CompilerParams(dimension_semantics=("arbitrary",)),
        name="attention",
    )(aq, ak, ak, ak, ak, av, av, av, av, sink_rows)


def _log_sigmoid(x):
    return jnp.minimum(x, 0.0) - jnp.log1p(jnp.exp(-jnp.abs(x)))


def _mlstm_kernel(qkf_ref, vf_ref, gf_ref, qkb_ref, vb_ref, gb_ref, bias_ref, hf_ref, hb_ref, st_ref, m_ref):
    @pl.when(pl.program_id(0) == 0)
    def _():
        st_ref[...] = jnp.zeros_like(st_ref)
        m_ref[...] = jnp.zeros_like(m_ref)

    s_i = lax.broadcasted_iota(jnp.int32, (CHUNK, CHUNK), 0)
    t_i = lax.broadcasted_iota(jnp.int32, (CHUNK, CHUNK), 1)
    causal = {True: s_i <= t_i, False: s_i >= t_i}
    ones_row = jnp.where(lax.broadcasted_iota(jnp.int32, (ST_ROWS - HEAD_DIM, CHUNK), 0) == 0, 1.0, 0.0).astype(BF16)
    all_ones = jnp.ones((CHUNK, CHUNK), F32)

    chains = []
    for b in range(qkf_ref.shape[0]):
        for forward, qk_ref, v_ref, g_ref, o_ref in ((True, qkf_ref, vf_ref, gf_ref, hf_ref),
                                                      (False, qkb_ref, vb_ref, gb_ref, hb_ref)):
            gates = g_ref[b] + bias_ref[...]
            li8 = gates[0:2 * N_HEADS]
            lf8 = _log_sigmoid(gates[2 * N_HEADS:4 * N_HEADS])
            cum8 = _dot_hi(lf8, jnp.where(causal[forward], 1.0, 0.0).astype(F32))
            tot8 = _dot_hi(lf8, all_ones)
            a8 = li8 - cum8
            for h in range(N_HEADS):
                r = (0 if forward else N_HEADS) + h
                chains.append(dict(
                    b=b, h=h, mask=causal[forward], o_ref=o_ref, slot=(b * 2 + (0 if forward else 1)) * N_HEADS + h,
                    a=a8[r:r + 1], bcum=cum8[r:r + 1], b_end=tot8[r:r + 1],
                    q=qk_ref[b, h * HEAD_DIM:(h + 1) * HEAD_DIM, :],
                    k=qk_ref[b, GROUP_W + h * HEAD_DIM:GROUP_W + (h + 1) * HEAD_DIM, :],
                    v=v_ref[b, h * HEAD_DIM:(h + 1) * HEAD_DIM, :]))

    for ch in chains:
        ch["st"] = st_ref[ch["slot"]]
        ch["m"] = m_ref[ch["slot"]:ch["slot"] + 1, :]
        ch["kq"] = _dot_tn(ch["k"], ch["q"])
        ch["cq"] = _dot(ch["st"][:, 0:HEAD_DIM].astype(BF16), ch["q"])

    for ch in chains:
        a, m, mask = ch["a"], ch["m"], ch["mask"]
        a_col = jnp.broadcast_to(a, (CHUNK, CHUNK)).T
        a_max = jnp.max(a_col, axis=0, keepdims=True)
        big_m = jnp.maximum(jnp.max(jnp.where(mask, a_col, NEG), axis=0, keepdims=True), m)
        ch["inter"] = jnp.exp(m - big_m)
        ch["floor"] = jnp.exp(-(ch["bcum"] + big_m))
        s = ch["kq"] * jnp.exp(jnp.where(mask, a_col - big_m, NEG))
        ch["rowsum"] = jnp.sum(s, axis=0, keepdims=True)
        ch["s"] = s.astype(BF16)
        m_new = jnp.maximum(ch["b_end"] + m, ch["b_end"] + a_max)
        ch["decay"] = jnp.exp(ch["b_end"] + m - m_new)
        ch["kw"] = (ch["k"].astype(F32) * jnp.exp(ch["b_end"] + a - m_new)).astype(BF16)
        ch["m_new"] = m_new

    for ch in chains:
        ch["vs"] = _dot(ch["v"], ch["s"])
        v_ext = jnp.concatenate([ch["v"], ones_row], axis=0)
        ch["upd"] = _dot_nt(v_ext, ch["kw"])

    for ch in chains:
        h, slot, cq, inter = ch["h"], ch["slot"], ch["cq"], ch["inter"]
        num = cq[0:HEAD_DIM] * inter + ch["vs"]
        den = inter * cq[HEAD_DIM:HEAD_DIM + 1] + ch["rowsum"]
        ch["o_ref"][ch["b"], h * HEAD_DIM:(h + 1) * HEAD_DIM, :] = num / jnp.maximum(jnp.abs(den), ch["floor"])
        st_ref[slot, :, 0:HEAD_DIM] = ch["st"][:, 0:HEAD_DIM] * ch["decay"][:, 0:HEAD_DIM] + ch["upd"]
        m_ref[slot:slot + 1, :] = ch["m_new"]


def _mlstm(qk, vo, gt, gate_bias, n_ctx_chunks):
    B, _, S = qk.shape
    n_chunks = S // CHUNK
    fwd = lambda i: i
    bwd = lambda i: jnp.where(i < n_ctx_chunks, n_ctx_chunks - 1 - i, n_chunks - 1 + n_ctx_chunks - i)

    def specs(order):
        return [
            pl.BlockSpec((B, 2 * GROUP_W, CHUNK), lambda i: (0, 0, order(i))),
            pl.BlockSpec((B, GROUP_W, CHUNK), lambda i: (0, 0, order(i))),
            pl.BlockSpec((B, 4 * N_HEADS, CHUNK), lambda i: (0, 0, order(i))),
        ]

    n_slots = B * 2 * N_HEADS
    return pl.pallas_call(
        _mlstm_kernel,
        grid=(n_chunks,),
        in_specs=specs(fwd) + specs(bwd) + [pl.BlockSpec((4 * N_HEADS, CHUNK), lambda i: (0, 0))],
        out_specs=[
            pl.BlockSpec((B, GROUP_W, CHUNK), lambda i: (0, 0, fwd(i))),
            pl.BlockSpec((B, GROUP_W, CHUNK), lambda i: (0, 0, bwd(i))),
        ],
        out_shape=[jax.ShapeDtypeStruct((B, GROUP_W, S), F32)] * 2,
        scratch_shapes=[
            pltpu.VMEM((n_slots, ST_ROWS, LANES), F32),
            pltpu.VMEM((n_slots, LANES), F32),
        ],
        compiler_params=pltpu.CompilerParams(dimension_semantics=("arbitrary",)),
        name="mlstm",
    )(qk, vo, gt, qk, vo, gt, gate_bias)


def _out_kernel(x_ref, a_ref, hf_ref, hb_ref, zo_ref, c_ref, d_ref, mod_ref, g_ref, mg_ref,
                wo_ref, wi_ref, wf_ref, o_ref, *, hidden):
    hs = hf_ref[...] + hb_ref[...]
    hn = jnp.concatenate([_head_layer_norm_t(hs[h * HEAD_DIM:(h + 1) * HEAD_DIM]) for h in range(N_HEADS)], axis=0)
    bm = (hn * mg_ref[...] * jax.nn.sigmoid(zo_ref[...].astype(F32))).astype(BF16)
    y = (_dot_tn(a_ref[...], wo_ref[0:GROUP_W, :])
         + _dot_tn(bm, wo_ref[GROUP_W:2 * GROUP_W, :])
         + _dot_tn(c_ref[...], wo_ref[2 * GROUP_W:3 * GROUP_W, :])
         + _dot(d_ref[...], wo_ref[3 * GROUP_W:4 * GROUP_W, :]))
    x1 = x_ref[...] + mod_ref[2:3, :] * _rms(y, g_ref[1:2, :])
    t = _rms(x1, g_ref[2:3, :]) * (1.0 + mod_ref[4:5, :]) + mod_ref[3:4, :]
    tb = t.astype(BF16)
    gate = _dot(tb, wi_ref[:, 0:hidden])
    up = _dot(tb, wi_ref[:, hidden:2 * hidden])
    act = (_silu(gate) * up).astype(BF16)
    f = _dot(act, wf_ref[...])
    o_ref[...] = x1 + mod_ref[5:6, :] * _rms(f, g_ref[3:4, :])


def _out_ffn(xs, a, hf, hb, vo, c, d, mods, norm_g, mnorm_g, w_out, w_ffn_in, w_ffn_out, n_batch_rows):
    B, S, D = xs.shape
    tm = ROW_TILE
    hidden = w_ffn_out.shape[0]
    row = lambda b, j: (b, j, 0)
    colm = lambda b, j: (b, 0, j)
    c2 = lambda b, j: (0, 0)
    fm = pl.BlockSpec((None, GROUP_W, tm), colm)
    kern = functools.partial(_out_kernel, hidden=hidden)
    return pl.pallas_call(
        kern,
        grid=(B, S // tm),
        in_specs=[
            pl.BlockSpec((None, tm, D), row),
            fm, fm, fm,
            pl.BlockSpec((None, GROUP_W, tm), lambda b, j: (b, 1, j)),
            fm,
            pl.BlockSpec((None, tm, GROUP_W), row),
            pl.BlockSpec((None, 6, D), lambda b, j: (jnp.where(j == 0, n_batch_rows, b), 0, 0)),
            pl.BlockSpec((4, D), c2),
            pl.BlockSpec((GROUP_W, tm), c2),
            pl.BlockSpec((D, D), c2, pipeline_mode=pl.Buffered(1)),
            pl.BlockSpec((D, 2 * hidden), c2, pipeline_mode=pl.Buffered(1)),
            pl.BlockSpec((hidden, D), c2, pipeline_mode=pl.Buffered(1)),
        ],
        out_specs=pl.BlockSpec((None, tm, D), row),
        out_shape=jax.ShapeDtypeStruct((B, S, D), F32),
        compiler_params=pltpu.CompilerParams(
            dimension_semantics=("arbitrary", "arbitrary"), vmem_limit_bytes=VMEM_LIMIT),
        name="out_ffn",
    )(xs, a, hf, hb, vo, c, d, mods, norm_g, mnorm_g, w_out, w_ffn_in, w_ffn_out)


def _rope_tables(n_tokens, ctx_len):
    rows = n_tokens // GRID_W
    axis_freq = HEAD_DIM // 4
    row = jnp.broadcast_to(jnp.arange(rows)[:, None], (rows, GRID_W)).reshape(-1).astype(F32)
    col = jnp.broadcast_to(jnp.arange(GRID_W)[None, :], (rows, GRID_W)).reshape(-1).astype(F32)
    inv = jnp.power(ROPE_BASE, -jnp.arange(axis_freq, dtype=F32) * 2.0 / (2 * axis_freq))
    ar = row[:, None] * inv
    ac = col[:, None] * inv
    ang = jnp.concatenate([ar, ar, ac, ac], axis=-1)
    cos, sin = jnp.cos(ang), jnp.sin(ang)
    sign = jnp.where((jnp.arange(HEAD_DIM) % 32) < 16, -1.0, 1.0).astype(F32)
    cos = jnp.concatenate([jnp.ones((ctx_len, HEAD_DIM), F32), cos], axis=0)
    sin = jnp.concatenate([jnp.zeros((ctx_len, HEAD_DIM), F32), sin * sign], axis=0)
    return jnp.tile(cos, (1, KV_HEADS)), jnp.tile(sin, (1, KV_HEADS)), cos.T, sin.T


def _arrange_w_in(w_in):
    ml0 = 2 * GROUP_W
    gate0 = ml0 + 4 * GROUP_W
    att0 = gate0 + 4 * N_HEADS
    pool0 = att0 + GROUP_W + 2 * KV_W
    gates = w_in[:, gate0:att0].reshape(-1, 2, 2, N_HEADS)
    gates = gates.transpose(0, 2, 1, 3).reshape(-1, 4 * N_HEADS)
    feat = jnp.concatenate([
        w_in[:, 0:ml0],
        w_in[:, ml0:ml0 + 2 * GROUP_W],
        w_in[:, ml0 + 2 * GROUP_W:gate0],
        w_in[:, att0:att0 + GROUP_W],
        w_in[:, att0 + GROUP_W + KV_W:pool0],
        gates,
    ], axis=1)
    tok = jnp.concatenate([w_in[:, att0 + GROUP_W:att0 + GROUP_W + KV_W], w_in[:, pool0:pool0 + GROUP_W]], axis=1)
    return feat.T.astype(BF16), tok.astype(BF16)


def _block_diag(w):
    g = w.shape[0]
    eye = jnp.eye(g, dtype=w.dtype)
    return (eye[:, None, :, None] * w[:, :, None, :]).reshape(g * w.shape[1], g * w.shape[2])


def kernel(x, c, ctx, c_ctx, w_mod, b_mod, norm_g, w_in, w_out, sgu_w, sgu_b, mlstm_conv_w, mlstm_gate_b,
           mlstm_norm_g, attn_sink, pool_w, pool_scale, w_ffn_in, w_ffn_out):
    B, N, D = x.shape
    ctx_len = ctx.shape[1]
    depth = w_mod.shape[0]
    assert D == 4 * GROUP_W and N % ROW_TILE == 0 and ctx_len == ROW_TILE
    assert w_in.shape[2] == 2 * GROUP_W + 4 * GROUP_W + 4 * N_HEADS + GROUP_W + 2 * KV_W + GROUP_W
    n_ctx_chunks = ctx_len // CHUNK

    xs = jnp.concatenate([ctx, x], axis=1)
    rows = -(-(B + 1) // SUBLANES) * SUBLANES
    cc = jnp.zeros((rows, D), F32).at[0:B].set(c).at[B].set(c_ctx)
    mods = _modulation(cc, w_mod, b_mod).reshape(depth, rows, 6, D)
    tables = _rope_tables(N, ctx_len)

    for l in range(depth):
        wt, wk = _arrange_w_in(w_in[l])
        conv_b = jnp.broadcast_to(mlstm_conv_w[l][:, :, None], (3, 2 * GROUP_W, ROW_TILE))
        gate_bias = jnp.broadcast_to(
            mlstm_gate_b[l].reshape(2, 2, N_HEADS).transpose(1, 0, 2).reshape(4 * N_HEADS, 1), (4 * N_HEADS, CHUNK))
        sink_rows = jnp.zeros((SUBLANES, (N_HEADS // KV_HEADS) * CHUNK), F32).at[0:KV_HEADS].set(
            jnp.repeat(attn_sink[l].reshape(KV_HEADS, N_HEADS // KV_HEADS), CHUNK, axis=1))
        mnorm_b = jnp.broadcast_to(mlstm_norm_g[l][:, None], (GROUP_W, ROW_TILE))

        zs, qk, vo, gt, aq, av, ak, dm = _in_proj(
            xs, mods[l], norm_g[l, 0:1], wt, wk, tables, conv_b, _block_diag(pool_w[l]).astype(BF16),
            pool_scale[l].reshape(1, GROUP_W), B, ctx_len, N)
        a = _sgu(zs, sgu_w[l].transpose(0, 2, 1).astype(BF16), sgu_b[l])
        hf, hb = _mlstm(qk, vo, gt, gate_bias, n_ctx_chunks)
        cm = _attention(aq, ak, av, sink_rows, n_ctx_chunks)
        xs = _out_ffn(xs, a, hf, hb, vo, cm, dm, mods[l], norm_g[l], mnorm_b,
                      w_out[l].astype(BF16), w_ffn_in[l].astype(BF16), w_ffn_out[l].astype(BF16), B)
    return xs[:, ctx_len:, :]
```

```python
import functools

import jax
import jax.numpy as jnp
from jax import lax
from jax.experimental import pallas as pl
from jax.experimental.pallas import tpu as pltpu

F32 = jnp.float32
BF16 = jnp.bfloat16

GRID_W = 64
ROPE_BASE = 10000.0
EPS = 1e-6
LANES = 128
SUBLANES = 8
BF16_ROWS = 16
CHUNK = 128
HEAD_DIM = 64
N_HEADS = 4
KV_HEADS = 2
GROUP_W = N_HEADS * HEAD_DIM
KV_W = KV_HEADS * HEAD_DIM
POOL_WINDOWS = (2, 4, 8, 16)
HALO = 8
ROW_TILE = 256
NEG = -1e30
VMEM_LIMIT = 56 * 1024 * 1024

R_SGU = 0
R_QK = 512
R_VO = 1024
R_AQ = 1536
R_AV = 1792
R_GATE = 1920
R_END = 1936
T_AK = 0
T_POOL = KV_W
T_END = KV_W + GROUP_W

ST_ROWS = 80

HI = lax.Precision.HIGHEST


def _dot(a, b):
    return jnp.dot(a, b, preferred_element_type=F32)


def _dot_hi(a, b):
    return jnp.dot(a, b, preferred_element_type=F32, precision=HI)


def _dot_nt(a, b):
    return lax.dot_general(a, b, (((1,), (1,)), ((), ())), preferred_element_type=F32)


def _dot_tn(a, b):
    return lax.dot_general(a, b, (((0,), (0,)), ((), ())), preferred_element_type=F32)


def _rms(x, g):
    return x * lax.rsqrt(jnp.mean(x * x, axis=-1, keepdims=True) + EPS) * g


def _silu(x):
    return x * jax.nn.sigmoid(x)


def _head_layer_norm_t(v):
    mu = jnp.mean(v, axis=0, keepdims=True)
    vc = v - mu
    return vc * lax.rsqrt(jnp.mean(vc * vc, axis=0, keepdims=True) + EPS)


def _mod_kernel(c_ref, w_ref, b_ref, o_ref):
    o_ref[...] = _dot(_silu(c_ref[...]).astype(BF16), w_ref[...].astype(BF16)) + b_ref[...]


def _modulation(cc, w_mod, b_mod):
    depth, d, n6 = w_mod.shape
    rows = cc.shape[0]
    tn = 1536
    return pl.pallas_call(
        _mod_kernel,
        grid=(depth, n6 // tn),
        in_specs=[
            pl.BlockSpec((rows, d), lambda l, j: (0, 0)),
            pl.BlockSpec((None, d, tn), lambda l, j: (l, 0, j)),
            pl.BlockSpec((None, 1, tn), lambda l, j: (l, 0, j)),
        ],
        out_specs=pl.BlockSpec((None, rows, tn), lambda l, j: (l, 0, j)),
        out_shape=jax.ShapeDtypeStruct((depth, rows, n6), F32),
        compiler_params=pltpu.CompilerParams(
            dimension_semantics=("arbitrary", "arbitrary"), vmem_limit_bytes=VMEM_LIMIT),
        name="modulation",
    )(cc, w_mod, b_mod.reshape(depth, 1, n6))


def _in_kernel(x_ref, xp_ref, xn_ref, mod_ref, g_ref, wt_ref, wk_ref, cos_ref, sin_ref, cost_ref, sint_ref,
               conv_ref, pw_ref, ps_ref, invc_ref,
               zs_ref, qk_ref, vo_ref, gt_ref, aq_ref, av_ref, ak_ref, d_ref, hb_ref, hh_ref, *, npb):
    tm = x_ref.shape[0]

    @pl.when(pl.program_id(0) == 0)
    def _():
        hb_ref[...] = jnp.zeros_like(hb_ref)
        hh_ref[...] = jnp.zeros_like(hh_ref)

    j = jnp.maximum(pl.program_id(0) - 1, 0) % npb
    has_prev = (j >= 2).astype(F32)
    has_next = jnp.logical_and(j >= 1, j < npb - 1).astype(F32)
    hb = hb_ref[...]
    hh = hh_ref[...]
    z = _dot_nt(wt_ref[R_QK:R_VO, :], hb)
    zh = _dot_nt(wt_ref[R_QK:R_VO, :], hh)
    lane = lax.broadcasted_iota(jnp.int32, z.shape, 1)
    before = jnp.where(lane == 0, zh[:, HALO - 1:HALO] * has_prev, pltpu.roll(z, 1, axis=1))
    after = jnp.where(lane == tm - 1, zh[:, HALO:HALO + 1] * has_next, pltpu.roll(z, tm - 1, axis=1))
    qk = _silu(before * conv_ref[0] + z * conv_ref[1] + after * conv_ref[2])
    qk_ref[0:GROUP_W, :] = qk[0:GROUP_W].astype(BF16)
    qk_ref[GROUP_W:2 * GROUP_W, :] = (qk[GROUP_W:2 * GROUP_W] * (HEAD_DIM ** -0.5)).astype(BF16)

    q = _dot_nt(wt_ref[R_AQ:R_AV, :], hb)
    cos_t = jnp.concatenate([cost_ref[...]] * N_HEADS, axis=0)
    sin_t = jnp.concatenate([sint_ref[...]] * N_HEADS, axis=0)
    row = lax.broadcasted_iota(jnp.int32, q.shape, 0)
    rot_q = jnp.where((row & 31) < 16, pltpu.roll(q, GROUP_W - 16, axis=0), pltpu.roll(q, 16, axis=0))
    aq_ref[...] = ((q * cos_t + rot_q * sin_t) * (HEAD_DIM ** -0.5)).astype(BF16)

    k = _dot(hb, wk_ref[:, T_AK:T_POOL])
    klane = lax.broadcasted_iota(jnp.int32, k.shape, 1)
    rot_k = jnp.where((klane & 31) < 16, pltpu.roll(k, KV_W - 16, axis=1), pltpu.roll(k, 16, axis=1))
    ak_ref[...] = (k * cos_ref[...] + rot_k * sin_ref[...]).astype(BF16)

    zp = _dot(hb, wk_ref[:, T_POOL:T_END])
    zph = _dot(hh, wk_ref[:, T_POOL:T_END])
    rows = tm + 2 * HALO
    ext = jnp.concatenate([zph[0:HALO] * has_prev, zp, zph[HALO:2 * HALO] * has_next], axis=0)
    s2 = ext + pltpu.roll(ext, 1, axis=0)
    s4 = pltpu.roll(s2, 1, axis=0) + pltpu.roll(s2, rows - 1, axis=0)
    s8 = pltpu.roll(s4, 2, axis=0) + pltpu.roll(s4, rows - 2, axis=0)
    s16 = pltpu.roll(s8, 4, axis=0) + pltpu.roll(s8, rows - 4, axis=0)
    sums = [s[HALO:HALO + tm] for s in (s2, s4, s8, s16)]
    plane = lax.broadcasted_iota(jnp.int32, zp.shape, 1)
    pooled = sums[-1]
    for gi in range(len(POOL_WINDOWS) - 2, -1, -1):
        pooled = jnp.where(plane < (gi + 1) * HEAD_DIM, sums[gi], pooled)
    pooled = pooled * invc_ref[...]
    d_ref[...] = (_dot((pooled - zp).astype(BF16), pw_ref[...]) * ps_ref[...]).astype(BF16)

    zs_ref[...] = _dot_nt(wt_ref[R_SGU:R_QK, :], hb).astype(BF16)
    vo_ref[...] = _dot_nt(wt_ref[R_VO:R_AQ, :], hb).astype(BF16)
    av_ref[...] = _dot_nt(wt_ref[R_AV:R_GATE, :], hb).astype(BF16)
    gt_ref[...] = _dot_nt(wt_ref[R_GATE:R_END, :], hb)

    gain = g_ref[...] * (1.0 + mod_ref[1:2, :])
    hb_ref[...] = (_rms(x_ref[...], gain) + mod_ref[0:1, :]).astype(BF16)
    hh_ref[...] = (_rms(jnp.concatenate([xp_ref[...], xn_ref[...]], axis=0), gain) + mod_ref[0:1, :]).astype(BF16)


def _in_proj(xs, mods, g0, wt, wk, tables, conv_b, pool_w_bd, pool_scale, n_batch_rows):
    B, S, D = xs.shape
    tm = ROW_TILE
    npb = S // tm
    n_tiles = B * npb
    per = tm // HALO
    cos_k, sin_k, cos_t, sin_t, inv_cnt = tables
    cur = lambda s: jnp.minimum(s, n_tiles - 1)
    prv = lambda s: jnp.maximum(s - 1, 0)
    row = lambda s: (prv(s) // npb, prv(s) % npb, 0)
    colm = lambda s: (prv(s) // npb, 0, prv(s) % npb)
    c2 = lambda s: (0, 0)
    kern = functools.partial(_in_kernel, npb=npb)
    fm = lambda rows, dt: (pl.BlockSpec((None, rows, tm), colm), jax.ShapeDtypeStruct((B, rows, S), dt))
    tk = lambda cols, dt: (pl.BlockSpec((None, tm, cols), row), jax.ShapeDtypeStruct((B, S, cols), dt))
    outs = [fm(2 * GROUP_W, BF16), fm(2 * GROUP_W, BF16), fm(2 * GROUP_W, BF16), fm(4 * N_HEADS, F32),
            fm(GROUP_W, BF16), fm(KV_W, BF16), tk(KV_W, BF16), tk(GROUP_W, BF16)]
    return pl.pallas_call(
        kern,
        grid=(n_tiles + 1,),
        in_specs=[
            pl.BlockSpec((None, tm, D), lambda s: (cur(s) // npb, cur(s) % npb, 0)),
            pl.BlockSpec((None, HALO, D), lambda s: (cur(s) // npb, jnp.maximum(cur(s) % npb * per - 1, 0), 0)),
            pl.BlockSpec((None, HALO, D),
                         lambda s: (cur(s) // npb, jnp.minimum((cur(s) % npb + 1) * per, S // HALO - 1), 0)),
            pl.BlockSpec((None, 6, D), lambda s: (jnp.where(cur(s) % npb == 0, n_batch_rows, cur(s) // npb), 0, 0)),
            pl.BlockSpec((1, D), c2),
            pl.BlockSpec((R_END, D), c2, pipeline_mode=pl.Buffered(1)),
            pl.BlockSpec((D, T_END), c2, pipeline_mode=pl.Buffered(1)),
            pl.BlockSpec((tm, KV_W), lambda s: (prv(s) % npb, 0)),
            pl.BlockSpec((tm, KV_W), lambda s: (prv(s) % npb, 0)),
            pl.BlockSpec((HEAD_DIM, tm), lambda s: (0, prv(s) % npb)),
            pl.BlockSpec((HEAD_DIM, tm), lambda s: (0, prv(s) % npb)),
            pl.BlockSpec((3, 2 * GROUP_W, tm), lambda s: (0, 0, 0)),
            pl.BlockSpec((GROUP_W, GROUP_W), c2),
            pl.BlockSpec((1, GROUP_W), c2),
            pl.BlockSpec((tm, GROUP_W), lambda s: (prv(s) % npb, 0)),
        ],
        out_specs=[o[0] for o in outs],
        out_shape=[o[1] for o in outs],
        scratch_shapes=[pltpu.VMEM((tm, D), BF16), pltpu.VMEM((2 * HALO, D), BF16)],
        compiler_params=pltpu.CompilerParams(dimension_semantics=("arbitrary",), vmem_limit_bytes=VMEM_LIMIT),
        name="in_proj",
    )(xs, xs, xs, mods, g0, wt, wk, cos_k, sin_k, cos_t, sin_t, conv_b, pool_w_bd, pool_scale, inv_cnt)


def _sgu_kernel(z_ref, w_ref, bias_ref, o_ref):
    for b in range(z_ref.shape[0]):
        g = jax.nn.gelu(z_ref[b].astype(F32))
        for h in range(N_HEADS):
            u = g[h * HEAD_DIM:(h + 1) * HEAD_DIM]
            vh = _head_layer_norm_t(g[GROUP_W + h * HEAD_DIM:GROUP_W + (h + 1) * HEAD_DIM]).astype(BF16)
            mixed = _dot(vh, w_ref[h]) + bias_ref[h:h + 1, :]
            o_ref[b, h * HEAD_DIM:(h + 1) * HEAD_DIM, :] = (u * mixed).astype(BF16)


def _sgu(zs, sgu_wt, sgu_b):
    B, _, S = zs.shape
    return pl.pallas_call(
        _sgu_kernel,
        grid=(S // CHUNK,),
        in_specs=[
            pl.BlockSpec((B, 2 * GROUP_W, CHUNK), lambda c: (0, 0, c)),
            pl.BlockSpec((N_HEADS, CHUNK, CHUNK), lambda c: (0, 0, 0)),
            pl.BlockSpec((N_HEADS, CHUNK), lambda c: (0, 0)),
        ],
        out_specs=pl.BlockSpec((B, GROUP_W, CHUNK), lambda c: (0, 0, c)),
        out_shape=jax.ShapeDtypeStruct((B, GROUP_W, S), BF16),
        compiler_params=pltpu.CompilerParams(dimension_semantics=("arbitrary",)),
        name="sgu",
    )(zs, sgu_wt, sgu_b)


def _att_kernel(q_ref, kp_ref, kc_ref, kn_ref, kx_ref, vp_ref, vc_ref, vn_ref, vx_ref, sink_ref, o_ref,
                *, n_ctx_chunks, n_chunks):
    c = pl.program_id(0)
    prev_ok = c > n_ctx_chunks
    cur_ok = c >= n_ctx_chunks
    next_ok = jnp.logical_and(cur_ok, c < n_chunks - 1)
    n_keys = 3 * CHUNK + n_ctx_chunks * CHUNK
    group = N_HEADS // KV_HEADS
    j = lax.broadcasted_iota(jnp.int32, (n_keys, group * CHUNK), 0)
    i = lax.broadcasted_iota(jnp.int32, (n_keys, group * CHUNK), 1) & (CHUNK - 1)
    lo = jnp.where(cur_ok, jnp.where(prev_ok, 0, CHUNK), 3 * CHUNK)
    hi = jnp.where(next_ok, 3 * CHUNK, 2 * CHUNK)
    valid = ((j >= i) & (j <= i + 2 * CHUNK) & (j >= lo) & (j < hi)) | (j >= 3 * CHUNK)
    zeros = jnp.zeros((HEAD_DIM, group * CHUNK), BF16)
    n_b = q_ref.shape[0]
    pairs = [(b, kv) for b in range(n_b) for kv in range(KV_HEADS)]
    scores = []
    for b, kv in pairs:
        kcat = jnp.concatenate([kp_ref[b], kc_ref[b], kn_ref[b], kx_ref[b]], axis=0)
        q2 = jnp.concatenate([q_ref[b, (kv * group + g) * HEAD_DIM:(kv * group + g + 1) * HEAD_DIM, :]
                              for g in range(group)], axis=1)
        qm = jnp.concatenate([q2, zeros] if kv == 0 else [zeros, q2], axis=0)
        scores.append(_dot(kcat, qm))
    probs = []
    for (b, kv), s in zip(pairs, scores):
        s = jnp.where(valid, s, NEG)
        sink = sink_ref[kv:kv + 1, :]
        m = jnp.maximum(jnp.max(s, axis=0, keepdims=True), sink)
        p = jnp.exp(s - m)
        denom = jnp.sum(p, axis=0, keepdims=True) + jnp.exp(sink - m)
        probs.append((p.astype(BF16), denom))
    for (b, kv), (p, denom) in zip(pairs, probs):
        vcat = jnp.concatenate([vp_ref[b], vc_ref[b], vn_ref[b], vx_ref[b]], axis=1)
        o = _dot(vcat[kv * HEAD_DIM:(kv + 1) * HEAD_DIM, :], p) / denom
        for g in range(group):
            h = kv * group + g
            o_ref[b, h * HEAD_DIM:(h + 1) * HEAD_DIM, :] = o[:, g * CHUNK:(g + 1) * CHUNK].astype(BF16)


def _attention(aq, ak, av, sink_rows, n_ctx_chunks):
    B, _, S = aq.shape
    n_chunks = S // CHUNK
    ctx_w = n_ctx_chunks * CHUNK
    kern = functools.partial(_att_kernel, n_ctx_chunks=n_ctx_chunks, n_chunks=n_chunks)
    prev = lambda c: jnp.maximum(c - 1, 0)
    nxt = lambda c: jnp.minimum(c + 1, n_chunks - 1)
    kspec = lambda f: pl.BlockSpec((B, CHUNK, KV_W), lambda c: (0, f(c), 0))
    vspec = lambda f: pl.BlockSpec((B, KV_W, CHUNK), lambda c: (0, 0, f(c)))
    ident = lambda c: c
    return pl.pallas_call(
        kern,
        grid=(n_chunks,),
        in_specs=[
            pl.BlockSpec((B, GROUP_W, CHUNK), lambda c: (0, 0, c)),
            kspec(prev), kspec(ident), kspec(nxt), pl.BlockSpec((B, ctx_w, KV_W), lambda c: (0, 0, 0)),
            vspec(prev), vspec(ident), vspec(nxt), pl.BlockSpec((B, KV_W, ctx_w), lambda c: (0, 0, 0)),
            pl.BlockSpec((SUBLANES, (N_HEADS // KV_HEADS) * CHUNK), lambda c: (0, 0)),
        ],
        out_specs=pl.BlockSpec((B, GROUP_W, CHUNK), lambda c: (0, 0, c)),
        out_shape=jax.ShapeDtypeStruct((B, GROUP_W, S), BF16),
        compiler_params=pltpu.CompilerParams(dimension_semantics=("arbitrary",)),
        name="attention",
    )(aq, ak, ak, ak, ak, av, av, av, av, sink_rows)


def _log_sigmoid(x):
    return jnp.minimum(x, 0.0) - jnp.log1p(jnp.exp(-jnp.abs(x)))


def _mlstm_kernel(qkf_ref, vf_ref, gf_ref, qkb_ref, vb_ref, gb_ref, bias_ref, hf_ref, hb_ref, st_ref, m_ref):
    @pl.when(pl.program_id(0) == 0)
    def _():
        st_ref[...] = jnp.zeros_like(st_ref)
        m_ref[...] = jnp.zeros_like(m_ref)

    s_i = lax.broadcasted_iota(jnp.int32, (CHUNK, CHUNK), 0)
    t_i = lax.broadcasted_iota(jnp.int32, (CHUNK, CHUNK), 1)
    causal = {True: s_i <= t_i, False: s_i >= t_i}
    ones_row = jnp.where(lax.broadcasted_iota(jnp.int32, (ST_ROWS - HEAD_DIM, CHUNK), 0) == 0, 1.0, 0.0).astype(BF16)
    all_ones = jnp.ones((CHUNK, CHUNK), F32)

    chains = []
    for b in range(qkf_ref.shape[0]):
        for forward, qk_ref, v_ref, g_ref, o_ref in ((True, qkf_ref, vf_ref, gf_ref, hf_ref),
                                                      (False, qkb_ref, vb_ref, gb_ref, hb_ref)):
            gates = g_ref[b] + bias_ref[...]
            li8 = gates[0:2 * N_HEADS]
            lf8 = _log_sigmoid(gates[2 * N_HEADS:4 * N_HEADS])
            cum8 = _dot_hi(lf8, jnp.where(causal[forward], 1.0, 0.0).astype(F32))
            tot8 = _dot_hi(lf8, all_ones)
            a8 = li8 - cum8
            for h in range(N_HEADS):
                r = (0 if forward else N_HEADS) + h
                chains.append(dict(
                    b=b, h=h, mask=causal[forward], o_ref=o_ref, slot=(b * 2 + (0 if forward else 1)) * N_HEADS + h,
                    a=a8[r:r + 1], bcum=cum8[r:r + 1], b_end=tot8[r:r + 1],
                    q=qk_ref[b, h * HEAD_DIM:(h + 1) * HEAD_DIM, :],
                    k=qk_ref[b, GROUP_W + h * HEAD_DIM:GROUP_W + (h + 1) * HEAD_DIM, :],
                    v=v_ref[b, h * HEAD_DIM:(h + 1) * HEAD_DIM, :]))

    for ch in chains:
        ch["st"] = st_ref[ch["slot"]]
        ch["m"] = m_ref[ch["slot"]:ch["slot"] + 1, :]
        ch["kq"] = _dot_tn(ch["k"], ch["q"])
        ch["cq"] = _dot(ch["st"][:, 0:HEAD_DIM].astype(BF16), ch["q"])

    for ch in chains:
        a, m, mask = ch["a"], ch["m"], ch["mask"]
        a_col = jnp.broadcast_to(a, (CHUNK, CHUNK)).T
        a_max = jnp.max(a_col, axis=0, keepdims=True)
        big_m = jnp.maximum(jnp.max(jnp.where(mask, a_col, NEG), axis=0, keepdims=True), m)
        ch["inter"] = jnp.exp(m - big_m)
        ch["floor"] = jnp.exp(-(ch["bcum"] + big_m))
        s = ch["kq"] * jnp.exp(jnp.where(mask, a_col - big_m, NEG))
        ch["rowsum"] = jnp.sum(s, axis=0, keepdims=True)
        ch["s"] = s.astype(BF16)
        m_new = jnp.maximum(ch["b_end"] + m, ch["b_end"] + a_max)
        ch["decay"] = jnp.exp(ch["b_end"] + m - m_new)
        ch["kw"] = (ch["k"].astype(F32) * jnp.exp(ch["b_end"] + a - m_new)).astype(BF16)
        ch["m_new"] = m_new

    for ch in chains:
        ch["vs"] = _dot(ch["v"], ch["s"])
        v_ext = jnp.concatenate([ch["v"], ones_row], axis=0)
        ch["upd"] = _dot_nt(v_ext, ch["kw"])

    for ch in chains:
        h, slot, cq, inter = ch["h"], ch["slot"], ch["cq"], ch["inter"]
        num = cq[0:HEAD_DIM] * inter + ch["vs"]
        den = inter * cq[HEAD_DIM:HEAD_DIM + 1] + ch["rowsum"]
        ch["o_ref"][ch["b"], h * HEAD_DIM:(h + 1) * HEAD_DIM, :] = num / jnp.maximum(jnp.abs(den), ch["floor"])
        st_ref[slot, :, 0:HEAD_DIM] = ch["st"][:, 0:HEAD_DIM] * ch["decay"][:, 0:HEAD_DIM] + ch["upd"]
        m_ref[slot:slot + 1, :] = ch["m_new"]


def _mlstm(qk, vo, gt, gate_bias, n_ctx_chunks):
    B, _, S = qk.shape
    n_chunks = S // CHUNK
    fwd = lambda i: i
    bwd = lambda i: jnp.where(i < n_ctx_chunks, n_ctx_chunks - 1 - i, n_chunks - 1 + n_ctx_chunks - i)

    def specs(order):
        return [
            pl.BlockSpec((B, 2 * GROUP_W, CHUNK), lambda i: (0, 0, order(i))),
            pl.BlockSpec((B, GROUP_W, CHUNK), lambda i: (0, 0, order(i))),
            pl.BlockSpec((B, 4 * N_HEADS, CHUNK), lambda i: (0, 0, order(i))),
        ]

    n_slots = B * 2 * N_HEADS
    return pl.pallas_call(
        _mlstm_kernel,
        grid=(n_chunks,),
        in_specs=specs(fwd) + specs(bwd) + [pl.BlockSpec((4 * N_HEADS, CHUNK), lambda i: (0, 0))],
        out_specs=[
            pl.BlockSpec((B, GROUP_W, CHUNK), lambda i: (0, 0, fwd(i))),
            pl.BlockSpec((B, GROUP_W, CHUNK), lambda i: (0, 0, bwd(i))),
        ],
        out_shape=[jax.ShapeDtypeStruct((B, GROUP_W, S), F32)] * 2,
        scratch_shapes=[
            pltpu.VMEM((n_slots, ST_ROWS, LANES), F32),
            pltpu.VMEM((n_slots, LANES), F32),
        ],
        compiler_params=pltpu.CompilerParams(dimension_semantics=("arbitrary",)),
        name="mlstm",
    )(qk, vo, gt, qk, vo, gt, gate_bias)


def _hidden_chunks(hidden):
    step = 3 * 2 * LANES
    return [(lo, min(lo + step, hidden)) for lo in range(0, hidden, step)]


def _out_kernel(x_ref, a_ref, hf_ref, hb_ref, zo_ref, c_ref, d_ref, moda_ref, modc_ref, g_ref, mg_ref,
                wo_ref, wi_ref, wf_ref, o_ref, x1_ref, x1b_ref, tb_ref, f_ref, act_ref, bm_ref, *, hidden):
    @pl.when(pl.program_id(0) == 0)
    def _():
        x1_ref[...] = jnp.zeros_like(x1_ref)
        x1b_ref[...] = jnp.zeros_like(x1b_ref)
        tb_ref[...] = jnp.zeros_like(tb_ref)
        f_ref[...] = jnp.zeros_like(f_ref)

    tb = tb_ref[...]
    chunks = _hidden_chunks(hidden)

    def ffn_in(lo, hi):
        return _dot(tb, wi_ref[:, lo:hi]), _dot(tb, wi_ref[:, hidden + lo:hidden + hi])

    def act(gate_up, lo, hi):
        gate, up = gate_up
        act_ref[:, lo:hi] = (_silu(gate) * up).astype(BF16)

    def ffn_out(lo, hi):
        return _dot(act_ref[:, lo:hi], wf_ref[lo:hi, :])

    gu0 = ffn_in(*chunks[0])
    o_ref[...] = x1b_ref[...] + modc_ref[5:6, :] * _rms(f_ref[...], g_ref[3:4, :])
    gu1 = ffn_in(*chunks[1])
    hs = hf_ref[...] + hb_ref[...]
    hn = jnp.concatenate([_head_layer_norm_t(hs[h * HEAD_DIM:(h + 1) * HEAD_DIM]) for h in range(N_HEADS)], axis=0)
    bm_ref[...] = (hn * mg_ref[...] * jax.nn.sigmoid(zo_ref[...].astype(F32))).astype(BF16)
    act(gu0, *chunks[0])
    gu2 = ffn_in(*chunks[2])
    act(gu1, *chunks[1])
    y = (_dot_tn(a_ref[...], wo_ref[0:GROUP_W, :])
         + _dot_tn(bm_ref[...], wo_ref[GROUP_W:2 * GROUP_W, :])
         + _dot_tn(c_ref[...], wo_ref[2 * GROUP_W:3 * GROUP_W, :])
         + _dot(d_ref[...], wo_ref[3 * GROUP_W:4 * GROUP_W, :]))
    rest = [ffn_in(*ch) for ch in chunks[3:]]
    act(gu2, *chunks[2])
    f = ffn_out(*chunks[0])
    x1b_ref[...] = x1_ref[...]
    x1 = x_ref[...] + moda_ref[2:3, :] * _rms(y, g_ref[1:2, :])
    t = _rms(x1, g_ref[2:3, :] * (1.0 + moda_ref[4:5, :])) + moda_ref[3:4, :]
    x1_ref[...] = x1
    tb_ref[...] = t.astype(BF16)
    f = f + ffn_out(*chunks[1]) + ffn_out(*chunks[2])
    for gate_up, ch in zip(rest, chunks[3:]):
        act(gate_up, *ch)
        f = f + ffn_out(*ch)
    f_ref[...] = f


def _out_ffn(xs, a, hf, hb, vo, c, d, mods, norm_g, mnorm_g, w_out, w_ffn_in, w_ffn_out, n_batch_rows, drop_ctx):
    B, S, D = xs.shape
    tm = ROW_TILE
    npb = S // tm
    n_tiles = B * npb
    hidden = w_ffn_out.shape[0]
    assert len(_hidden_chunks(hidden)) >= 3
    cur = lambda s: jnp.minimum(s, n_tiles - 1)
    prv = lambda s: jnp.maximum(s - 2, 0)
    row = lambda s: (cur(s) // npb, cur(s) % npb, 0)
    colm = lambda s: (cur(s) // npb, 0, cur(s) % npb)
    mod_row = lambda t: (jnp.where(t % npb == 0, n_batch_rows, t // npb), 0, 0)
    c2 = lambda s: (0, 0)
    fm = pl.BlockSpec((None, GROUP_W, tm), colm)
    if drop_ctx:
        out_spec = pl.BlockSpec((None, tm, D), lambda s: (prv(s) // npb, jnp.maximum(prv(s) % npb - 1, 0), 0))
        out_shape = jax.ShapeDtypeStruct((B, S - tm, D), F32)
    else:
        out_spec = pl.BlockSpec((None, tm, D), lambda s: (prv(s) // npb, prv(s) % npb, 0))
        out_shape = jax.ShapeDtypeStruct((B, S, D), F32)
    kern = functools.partial(_out_kernel, hidden=hidden)
    return pl.pallas_call(
        kern,
        grid=(n_tiles + 2,),
        in_specs=[
            pl.BlockSpec((None, tm, D), row),
            fm, fm, fm,
            pl.BlockSpec((None, GROUP_W, tm), lambda s: (cur(s) // npb, 1, cur(s) % npb)),
            fm,
            pl.BlockSpec((None, tm, GROUP_W), row),
            pl.BlockSpec((None, 6, D), lambda s: mod_row(cur(s))),
            pl.BlockSpec((None, 6, D), lambda s: mod_row(prv(s))),
            pl.BlockSpec((4, D), c2),
            pl.BlockSpec((GROUP_W, tm), c2),
            pl.BlockSpec((D, D), c2, pipeline_mode=pl.Buffered(1)),
            pl.BlockSpec((D, 2 * hidden), c2, pipeline_mode=pl.Buffered(1)),
            pl.BlockSpec((hidden, D), c2, pipeline_mode=pl.Buffered(1)),
        ],
        out_specs=out_spec,
        out_shape=out_shape,
        scratch_shapes=[pltpu.VMEM((tm, D), F32), pltpu.VMEM((tm, D), F32), pltpu.VMEM((tm, D), BF16),
                        pltpu.VMEM((tm, D), F32), pltpu.VMEM((tm, hidden), BF16), pltpu.VMEM((GROUP_W, tm), BF16)],
        compiler_params=pltpu.CompilerParams(dimension_semantics=("arbitrary",), vmem_limit_bytes=VMEM_LIMIT),
        name="out_ffn",
    )(xs, a, hf, hb, vo, c, d, mods, mods, norm_g, mnorm_g, w_out, w_ffn_in, w_ffn_out)


def _rope_tables(n_tokens, ctx_len):
    rows = n_tokens // GRID_W
    axis_freq = HEAD_DIM // 4
    row = jnp.broadcast_to(jnp.arange(rows)[:, None], (rows, GRID_W)).reshape(-1).astype(F32)
    col = jnp.broadcast_to(jnp.arange(GRID_W)[None, :], (rows, GRID_W)).reshape(-1).astype(F32)
    inv = jnp.power(ROPE_BASE, -jnp.arange(axis_freq, dtype=F32) * 2.0 / (2 * axis_freq))
    ar = row[:, None] * inv
    ac = col[:, None] * inv
    ang = jnp.concatenate([ar, ar, ac, ac], axis=-1)
    cos, sin = jnp.cos(ang), jnp.sin(ang)
    sign = jnp.where((jnp.arange(HEAD_DIM) % 32) < 16, -1.0, 1.0).astype(F32)
    cos = jnp.concatenate([jnp.ones((ctx_len, HEAD_DIM), F32), cos], axis=0)
    sin = jnp.concatenate([jnp.zeros((ctx_len, HEAD_DIM), F32), sin * sign], axis=0)
    inv_cnt = []
    for seg in (ctx_len, n_tokens):
        t = jnp.arange(seg)[:, None]
        half = jnp.asarray([w // 2 for w in POOL_WINDOWS])[None, :]
        cnt = jnp.minimum(t + half, seg) - jnp.maximum(t - half, 0)
        inv_cnt.append(jnp.repeat(1.0 / cnt.astype(F32), HEAD_DIM, axis=1))
    return (jnp.tile(cos, (1, KV_HEADS)), jnp.tile(sin, (1, KV_HEADS)), cos.T, sin.T,
            jnp.concatenate(inv_cnt, axis=0))


def _arrange_w_in(w_in):
    ml0 = 2 * GROUP_W
    gate0 = ml0 + 4 * GROUP_W
    att0 = gate0 + 4 * N_HEADS
    pool0 = att0 + GROUP_W + 2 * KV_W
    gates = w_in[:, gate0:att0].reshape(-1, 2, 2, N_HEADS)
    gates = gates.transpose(0, 2, 1, 3).reshape(-1, 4 * N_HEADS)
    feat = jnp.concatenate([
        w_in[:, 0:ml0],
        w_in[:, ml0:ml0 + 2 * GROUP_W],
        w_in[:, ml0 + 2 * GROUP_W:gate0],
        w_in[:, att0:att0 + GROUP_W],
        w_in[:, att0 + GROUP_W + KV_W:pool0],
        gates,
    ], axis=1)
    tok = jnp.concatenate([w_in[:, att0 + GROUP_W:att0 + GROUP_W + KV_W], w_in[:, pool0:pool0 + GROUP_W]], axis=1)
    return feat.T.astype(BF16), tok.astype(BF16)


def _block_diag(w):
    g = w.shape[0]
    eye = jnp.eye(g, dtype=w.dtype)
    return (eye[:, None, :, None] * w[:, :, None, :]).reshape(g * w.shape[1], g * w.shape[2])


def kernel(x, c, ctx, c_ctx, w_mod, b_mod, norm_g, w_in, w_out, sgu_w, sgu_b, mlstm_conv_w, mlstm_gate_b,
           mlstm_norm_g, attn_sink, pool_w, pool_scale, w_ffn_in, w_ffn_out):
    B, N, D = x.shape
    ctx_len = ctx.shape[1]
    depth = w_mod.shape[0]
    assert D == 4 * GROUP_W and N % ROW_TILE == 0 and ctx_len == ROW_TILE
    assert w_in.shape[2] == 2 * GROUP_W + 4 * GROUP_W + 4 * N_HEADS + GROUP_W + 2 * KV_W + GROUP_W
    n_ctx_chunks = ctx_len // CHUNK

    xs = jnp.concatenate([ctx, x], axis=1)
    rows = -(-(B + 1) // SUBLANES) * SUBLANES
    cc = jnp.zeros((rows, D), F32).at[0:B].set(c).at[B].set(c_ctx)
    mods = _modulation(cc, w_mod, b_mod).reshape(depth, rows, 6, D)
    tables = _rope_tables(N, ctx_len)

    for l in range(depth):
        wt, wk = _arrange_w_in(w_in[l])
        conv_b = jnp.broadcast_to(mlstm_conv_w[l][:, :, None], (3, 2 * GROUP_W, ROW_TILE))
        gate_bias = jnp.broadcast_to(
            mlstm_gate_b[l].reshape(2, 2, N_HEADS).transpose(1, 0, 2).reshape(4 * N_HEADS, 1), (4 * N_HEADS, CHUNK))
        sink_rows = jnp.zeros((SUBLANES, (N_HEADS // KV_HEADS) * CHUNK), F32).at[0:KV_HEADS].set(
            jnp.repeat(attn_sink[l].reshape(KV_HEADS, N_HEADS // KV_HEADS), CHUNK, axis=1))
        mnorm_b = jnp.broadcast_to(mlstm_norm_g[l][:, None], (GROUP_W, ROW_TILE))

        zs, qk, vo, gt, aq, av, ak, dm = _in_proj(
            xs, mods[l], norm_g[l, 0:1], wt, wk, tables, conv_b, _block_diag(pool_w[l]).astype(BF16),
            pool_scale[l].reshape(1, GROUP_W), B)
        a = _sgu(zs, sgu_w[l].transpose(0, 2, 1).astype(BF16), sgu_b[l])
        hf, hb = _mlstm(qk, vo, gt, gate_bias, n_ctx_chunks)
        cm = _attention(aq, ak, av, sink_rows, n_ctx_chunks)
        xs = _out_ffn(xs, a, hf, hb, vo, cm, dm, mods[l], norm_g[l], mnorm_b,
                      w_out[l].astype(BF16), w_ffn_in[l].astype(BF16), w_ffn_out[l].astype(BF16), B,
                      drop_ctx=(l == depth - 1))
    return xs
```

```python
import functools

import jax
import jax.numpy as jnp
from jax import lax
from jax.experimental import pallas as pl
from jax.experimental.pallas import tpu as pltpu

F32 = jnp.float32
BF16 = jnp.bfloat16

GRID_W = 64
ROPE_BASE = 10000.0
EPS = 1e-6
LANES = 128
SUBLANES = 8
BF16_ROWS = 16
CHUNK = 128
HEAD_DIM = 64
N_HEADS = 4
KV_HEADS = 2
GROUP_W = N_HEADS * HEAD_DIM
KV_W = KV_HEADS * HEAD_DIM
POOL_WINDOWS = (2, 4, 8, 16)
HALO = 8
ROW_TILE = 256
NEG = -1e30
VMEM_LIMIT = 56 * 1024 * 1024

R_SGU = 0
R_QK = 512
R_VO = 1024
R_AQ = 1536
R_AV = 1792
R_GATE = 1920
R_END = 1936
T_AK = 0
T_POOL = KV_W
T_END = KV_W + GROUP_W

ST_ROWS = 80

HI = lax.Precision.HIGHEST


def _dot(a, b):
    return jnp.dot(a, b, preferred_element_type=F32)


def _dot_hi(a, b):
    return jnp.dot(a, b, preferred_element_type=F32, precision=HI)


def _dot_nt(a, b):
    return lax.dot_general(a, b, (((1,), (1,)), ((), ())), preferred_element_type=F32)


def _dot_tn(a, b):
    return lax.dot_general(a, b, (((0,), (0,)), ((), ())), preferred_element_type=F32)


def _rms(x, g):
    return x * lax.rsqrt(jnp.mean(x * x, axis=-1, keepdims=True) + EPS) * g


def _silu(x):
    return x * jax.nn.sigmoid(x)


def _head_layer_norm_t(v):
    mu = jnp.mean(v, axis=0, keepdims=True)
    vc = v - mu
    return vc * lax.rsqrt(jnp.mean(vc * vc, axis=0, keepdims=True) + EPS)


def _mod_kernel(c_ref, w_ref, b_ref, o_ref):
    o_ref[...] = _dot(_silu(c_ref[...]).astype(BF16), w_ref[...].astype(BF16)) + b_ref[...]


def _modulation(cc, w_mod, b_mod):
    depth, d, n6 = w_mod.shape
    rows = cc.shape[0]
    tn = 1536
    return pl.pallas_call(
        _mod_kernel,
        grid=(depth, n6 // tn),
        in_specs=[
            pl.BlockSpec((rows, d), lambda l, j: (0, 0)),
            pl.BlockSpec((None, d, tn), lambda l, j: (l, 0, j)),
            pl.BlockSpec((None, 1, tn), lambda l, j: (l, 0, j)),
        ],
        out_specs=pl.BlockSpec((None, rows, tn), lambda l, j: (l, 0, j)),
        out_shape=jax.ShapeDtypeStruct((depth, rows, n6), F32),
        compiler_params=pltpu.CompilerParams(
            dimension_semantics=("arbitrary", "arbitrary"), vmem_limit_bytes=VMEM_LIMIT),
        name="modulation",
    )(cc, w_mod, b_mod.reshape(depth, 1, n6))


def _in_kernel(x_ref, xp_ref, xn_ref, mod_ref, g_ref, wt_ref, wk_ref, cos_ref, sin_ref, cost_ref, sint_ref,
               conv_ref, pw_ref, ps_ref, invc_ref,
               zs_ref, qk_ref, vo_ref, gt_ref, aq_ref, av_ref, ak_ref, d_ref, hb_ref, hh_ref, z_ref, pcol_ref,
               *, npb, n_tiles):
    tm = x_ref.shape[0]
    s = pl.program_id(0)

    @pl.when(s == 0)
    def _():
        hb_ref[...] = jnp.zeros_like(hb_ref)
        hh_ref[...] = jnp.zeros_like(hh_ref)
        z_ref[...] = jnp.zeros_like(z_ref)
        pcol_ref[...] = jnp.zeros_like(pcol_ref)

    def halo_flags(jj):
        return (jj >= 2).astype(F32), jnp.logical_and(jj >= 1, jj < npb - 1).astype(F32)

    has_prev, has_next = halo_flags(jnp.clip(s - 1, 0, n_tiles - 1) % npb)
    conv_prev, conv_next = halo_flags(jnp.maximum(s - 2, 0) % npb)
    hb = hb_ref[...]
    hh = hh_ref[...]

    zc = _dot_nt(wt_ref[R_QK:R_VO, :], hb)
    gain = g_ref[...] * (1.0 + mod_ref[1:2, :])
    hb_next = (_rms(x_ref[...], gain) + mod_ref[0:1, :]).astype(BF16)
    hh_next = (_rms(jnp.concatenate([xp_ref[...], xn_ref[...]], axis=0), gain) + mod_ref[0:1, :]).astype(BF16)

    zs_ref[...] = _dot_nt(wt_ref[R_SGU:R_QK, :], hb).astype(BF16)

    z = z_ref[...]
    lane = lax.broadcasted_iota(jnp.int32, z.shape, 1)
    before = jnp.where(lane == 0, pcol_ref[:, 0:1] * conv_prev, pltpu.roll(z, 1, axis=1))
    after = jnp.where(lane == tm - 1, zc[:, 0:1] * conv_next, pltpu.roll(z, tm - 1, axis=1))
    qk = _silu(before * conv_ref[0] + z * conv_ref[1] + after * conv_ref[2])
    qk_ref[0:GROUP_W, :] = qk[0:GROUP_W].astype(BF16)
    qk_ref[GROUP_W:2 * GROUP_W, :] = (qk[GROUP_W:2 * GROUP_W] * (HEAD_DIM ** -0.5)).astype(BF16)
    pcol_ref[...] = jnp.broadcast_to(z[:, tm - 1:tm], pcol_ref.shape)
    z_ref[...] = zc

    zp = _dot(hb, wk_ref[:, T_POOL:T_END])
    zph = _dot(hh, wk_ref[:, T_POOL:T_END])
    vo_ref[...] = _dot_nt(wt_ref[R_VO:R_AQ, :], hb).astype(BF16)
    rows = tm + 2 * HALO
    ext = jnp.concatenate([zph[0:HALO] * has_prev, zp, zph[HALO:2 * HALO] * has_next], axis=0)
    s2 = ext + pltpu.roll(ext, 1, axis=0)
    s4 = pltpu.roll(s2, 1, axis=0) + pltpu.roll(s2, rows - 1, axis=0)
    s8 = pltpu.roll(s4, 2, axis=0) + pltpu.roll(s4, rows - 2, axis=0)
    s16 = pltpu.roll(s8, 4, axis=0) + pltpu.roll(s8, rows - 4, axis=0)
    sums = [s[HALO:HALO + tm] for s in (s2, s4, s8, s16)]
    plane = lax.broadcasted_iota(jnp.int32, zp.shape, 1)
    pooled = sums[-1]
    for gi in range(len(POOL_WINDOWS) - 2, -1, -1):
        pooled = jnp.where(plane < (gi + 1) * HEAD_DIM, sums[gi], pooled)
    pooled = pooled * invc_ref[...]
    d_ref[...] = (_dot((pooled - zp).astype(BF16), pw_ref[...]) * ps_ref[...]).astype(BF16)

    q = _dot_nt(wt_ref[R_AQ:R_AV, :], hb)
    k = _dot(hb, wk_ref[:, T_AK:T_POOL])
    av_ref[...] = _dot_nt(wt_ref[R_AV:R_GATE, :], hb).astype(BF16)
    gt_ref[...] = _dot_nt(wt_ref[R_GATE:R_END, :], hb)
    cos_t = jnp.concatenate([cost_ref[...]] * N_HEADS, axis=0)
    sin_t = jnp.concatenate([sint_ref[...]] * N_HEADS, axis=0)
    row = lax.broadcasted_iota(jnp.int32, q.shape, 0)
    rot_q = jnp.where((row & 31) < 16, pltpu.roll(q, GROUP_W - 16, axis=0), pltpu.roll(q, 16, axis=0))
    aq_ref[...] = ((q * cos_t + rot_q * sin_t) * (HEAD_DIM ** -0.5)).astype(BF16)
    klane = lax.broadcasted_iota(jnp.int32, k.shape, 1)
    rot_k = jnp.where((klane & 31) < 16, pltpu.roll(k, KV_W - 16, axis=1), pltpu.roll(k, 16, axis=1))
    ak_ref[...] = (k * cos_ref[...] + rot_k * sin_ref[...]).astype(BF16)

    hb_ref[...] = hb_next
    hh_ref[...] = hh_next


def _in_proj(xs, mods, g0, wt, wk, tables, conv_b, pool_w_bd, pool_scale, n_batch_rows):
    B, S, D = xs.shape
    tm = ROW_TILE
    npb = S // tm
    n_tiles = B * npb
    per = tm // HALO
    cos_k, sin_k, cos_t, sin_t, inv_cnt = tables
    cur = lambda s: jnp.minimum(s, n_tiles - 1)
    prv = lambda s: jnp.clip(s - 1, 0, n_tiles - 1)
    old = lambda s: jnp.maximum(s - 2, 0)
    row = lambda s: (prv(s) // npb, prv(s) % npb, 0)
    colm = lambda s: (prv(s) // npb, 0, prv(s) % npb)
    c2 = lambda s: (0, 0)
    kern = functools.partial(_in_kernel, npb=npb, n_tiles=n_tiles)
    fm = lambda rows, dt: (pl.BlockSpec((None, rows, tm), colm), jax.ShapeDtypeStruct((B, rows, S), dt))
    tk = lambda cols, dt: (pl.BlockSpec((None, tm, cols), row), jax.ShapeDtypeStruct((B, S, cols), dt))
    qk_out = (pl.BlockSpec((None, 2 * GROUP_W, tm), lambda s: (old(s) // npb, 0, old(s) % npb)),
              jax.ShapeDtypeStruct((B, 2 * GROUP_W, S), BF16))
    outs = [fm(2 * GROUP_W, BF16), qk_out, fm(2 * GROUP_W, BF16), fm(4 * N_HEADS, F32),
            fm(GROUP_W, BF16), fm(KV_W, BF16), tk(KV_W, BF16), tk(GROUP_W, BF16)]
    return pl.pallas_call(
        kern,
        grid=(n_tiles + 2,),
        in_specs=[
            pl.BlockSpec((None, tm, D), lambda s: (cur(s) // npb, cur(s) % npb, 0)),
            pl.BlockSpec((None, HALO, D), lambda s: (cur(s) // npb, jnp.maximum(cur(s) % npb * per - 1, 0), 0)),
            pl.BlockSpec((None, HALO, D),
                         lambda s: (cur(s) // npb, jnp.minimum((cur(s) % npb + 1) * per, S // HALO - 1), 0)),
            pl.BlockSpec((None, 6, D), lambda s: (jnp.where(cur(s) % npb == 0, n_batch_rows, cur(s) // npb), 0, 0)),
            pl.BlockSpec((1, D), c2),
            pl.BlockSpec((R_END, D), c2, pipeline_mode=pl.Buffered(1)),
            pl.BlockSpec((D, T_END), c2, pipeline_mode=pl.Buffered(1)),
            pl.BlockSpec((tm, KV_W), lambda s: (prv(s) % npb, 0)),
            pl.BlockSpec((tm, KV_W), lambda s: (prv(s) % npb, 0)),
            pl.BlockSpec((HEAD_DIM, tm), lambda s: (0, prv(s) % npb)),
            pl.BlockSpec((HEAD_DIM, tm), lambda s: (0, prv(s) % npb)),
            pl.BlockSpec((3, 2 * GROUP_W, tm), lambda s: (0, 0, 0)),
            pl.BlockSpec((GROUP_W, GROUP_W), c2),
            pl.BlockSpec((1, GROUP_W), c2),
            pl.BlockSpec((tm, GROUP_W), lambda s: (prv(s) % npb, 0)),
        ],
        out_specs=[o[0] for o in outs],
        out_shape=[o[1] for o in outs],
        scratch_shapes=[pltpu.VMEM((tm, D), BF16), pltpu.VMEM((2 * HALO, D), BF16),
                        pltpu.VMEM((2 * GROUP_W, tm), F32), pltpu.VMEM((2 * GROUP_W, LANES), F32)],
        compiler_params=pltpu.CompilerParams(dimension_semantics=("arbitrary",), vmem_limit_bytes=VMEM_LIMIT),
        name="in_proj",
    )(xs, xs, xs, mods, g0, wt, wk, cos_k, sin_k, cos_t, sin_t, conv_b, pool_w_bd, pool_scale, inv_cnt)


def _sgu_kernel(z_ref, w_ref, bias_ref, o_ref):
    for b in range(z_ref.shape[0]):
        g = jax.nn.gelu(z_ref[b].astype(F32))
        for h in range(N_HEADS):
            u = g[h * HEAD_DIM:(h + 1) * HEAD_DIM]
            vh = _head_layer_norm_t(g[GROUP_W + h * HEAD_DIM:GROUP_W + (h + 1) * HEAD_DIM]).astype(BF16)
            mixed = _dot(vh, w_ref[h]) + bias_ref[h:h + 1, :]
            o_ref[b, h * HEAD_DIM:(h + 1) * HEAD_DIM, :] = (u * mixed).astype(BF16)


def _sgu(zs, sgu_wt, sgu_b):
    B, _, S = zs.shape
    return pl.pallas_call(
        _sgu_kernel,
        grid=(S // CHUNK,),
        in_specs=[
            pl.BlockSpec((B, 2 * GROUP_W, CHUNK), lambda c: (0, 0, c)),
            pl.BlockSpec((N_HEADS, CHUNK, CHUNK), lambda c: (0, 0, 0)),
            pl.BlockSpec((N_HEADS, CHUNK), lambda c: (0, 0)),
        ],
        out_specs=pl.BlockSpec((B, GROUP_W, CHUNK), lambda c: (0, 0, c)),
        out_shape=jax.ShapeDtypeStruct((B, GROUP_W, S), BF16),
        compiler_params=pltpu.CompilerParams(dimension_semantics=("arbitrary",)),
        name="sgu",
    )(zs, sgu_wt, sgu_b)


def _att_kernel(q_ref, kp_ref, kc_ref, kn_ref, kx_ref, vp_ref, vc_ref, vn_ref, vx_ref, sink_ref, o_ref,
                *, n_ctx_chunks, n_chunks):
    c = pl.program_id(0)
    prev_ok = c > n_ctx_chunks
    cur_ok = c >= n_ctx_chunks
    next_ok = jnp.logical_and(cur_ok, c < n_chunks - 1)
    n_keys = 3 * CHUNK + n_ctx_chunks * CHUNK
    group = N_HEADS // KV_HEADS
    j = lax.broadcasted_iota(jnp.int32, (n_keys, group * CHUNK), 0)
    i = lax.broadcasted_iota(jnp.int32, (n_keys, group * CHUNK), 1) & (CHUNK - 1)
    lo = jnp.where(cur_ok, jnp.where(prev_ok, 0, CHUNK), 3 * CHUNK)
    hi = jnp.where(next_ok, 3 * CHUNK, 2 * CHUNK)
    valid = ((j >= i) & (j <= i + 2 * CHUNK) & (j >= lo) & (j < hi)) | (j >= 3 * CHUNK)
    zeros = jnp.zeros((HEAD_DIM, group * CHUNK), BF16)
    n_b = q_ref.shape[0]
    pairs = [(b, kv) for b in range(n_b) for kv in range(KV_HEADS)]
    scores = []
    for b, kv in pairs:
        kcat = jnp.concatenate([kp_ref[b], kc_ref[b], kn_ref[b], kx_ref[b]], axis=0)
        q2 = jnp.concatenate([q_ref[b, (kv * group + g) * HEAD_DIM:(kv * group + g + 1) * HEAD_DIM, :]
                              for g in range(group)], axis=1)
        qm = jnp.concatenate([q2, zeros] if kv == 0 else [zeros, q2], axis=0)
        scores.append(_dot(kcat, qm))
    probs = []
    for (b, kv), s in zip(pairs, scores):
        s = jnp.where(valid, s, NEG)
        sink = sink_ref[kv:kv + 1, :]
        m = jnp.maximum(jnp.max(s, axis=0, keepdims=True), sink)
        p = jnp.exp(s - m)
        denom = jnp.sum(p, axis=0, keepdims=True) + jnp.exp(sink - m)
        probs.append((p.astype(BF16), denom))
    for (b, kv), (p, denom) in zip(pairs, probs):
        vcat = jnp.concatenate([vp_ref[b], vc_ref[b], vn_ref[b], vx_ref[b]], axis=1)
        o = _dot(vcat[kv * HEAD_DIM:(kv + 1) * HEAD_DIM, :], p) / denom
        for g in range(group):
            h = kv * group + g
            o_ref[b, h * HEAD_DIM:(h + 1) * HEAD_DIM, :] = o[:, g * CHUNK:(g + 1) * CHUNK].astype(BF16)


def _attention(aq, ak, av, sink_rows, n_ctx_chunks):
    B, _, S = aq.shape
    n_chunks = S // CHUNK
    ctx_w = n_ctx_chunks * CHUNK
    kern = functools.partial(_att_kernel, n_ctx_chunks=n_ctx_chunks, n_chunks=n_chunks)
    prev = lambda c: jnp.maximum(c - 1, 0)
    nxt = lambda c: jnp.minimum(c + 1, n_chunks - 1)
    kspec = lambda f: pl.BlockSpec((B, CHUNK, KV_W), lambda c: (0, f(c), 0))
    vspec = lambda f: pl.BlockSpec((B, KV_W, CHUNK), lambda c: (0, 0, f(c)))
    ident = lambda c: c
    return pl.pallas_call(
        kern,
        grid=(n_chunks,),
        in_specs=[
            pl.BlockSpec((B, GROUP_W, CHUNK), lambda c: (0, 0, c)),
            kspec(prev), kspec(ident), kspec(nxt), pl.BlockSpec((B, ctx_w, KV_W), lambda c: (0, 0, 0)),
            vspec(prev), vspec(ident), vspec(nxt), pl.BlockSpec((B, KV_W, ctx_w), lambda c: (0, 0, 0)),
            pl.BlockSpec((SUBLANES, (N_HEADS // KV_HEADS) * CHUNK), lambda c: (0, 0)),
        ],
        out_specs=pl.BlockSpec((B, GROUP_W, CHUNK), lambda c: (0, 0, c)),
        out_shape=jax.ShapeDtypeStruct((B, GROUP_W, S), BF16),
        compiler_params=pltpu.CompilerParams(dimension_semantics=("arbitrary",)),
        name="attention",
    )(aq, ak, ak, ak, ak, av, av, av, av, sink_rows)


def _log_sigmoid(x):
    return jnp.minimum(x, 0.0) - jnp.log1p(jnp.exp(-jnp.abs(x)))


def _mlstm_kernel(qkf_ref, vf_ref, gf_ref, qkb_ref, vb_ref, gb_ref, bias_ref, hf_ref, hb_ref, st_ref, m_ref):
    @pl.when(pl.program_id(0) == 0)
    def _():
        st_ref[...] = jnp.zeros_like(st_ref)
        m_ref[...] = jnp.zeros_like(m_ref)

    s_i = lax.broadcasted_iota(jnp.int32, (CHUNK, CHUNK), 0)
    t_i = lax.broadcasted_iota(jnp.int32, (CHUNK, CHUNK), 1)
    causal = {True: s_i <= t_i, False: s_i >= t_i}
    ones_row = jnp.where(lax.broadcasted_iota(jnp.int32, (ST_ROWS - HEAD_DIM, CHUNK), 0) == 0, 1.0, 0.0).astype(BF16)
    all_ones = jnp.ones((CHUNK, CHUNK), F32)

    chains = []
    for b in range(qkf_ref.shape[0]):
        for forward, qk_ref, v_ref, g_ref, o_ref in ((True, qkf_ref, vf_ref, gf_ref, hf_ref),
                                                      (False, qkb_ref, vb_ref, gb_ref, hb_ref)):
            gates = g_ref[b] + bias_ref[...]
            li8 = gates[0:2 * N_HEADS]
            lf8 = _log_sigmoid(gates[2 * N_HEADS:4 * N_HEADS])
            cum8 = _dot_hi(lf8, jnp.where(causal[forward], 1.0, 0.0).astype(F32))
            tot8 = _dot_hi(lf8, all_ones)
            a8 = li8 - cum8
            for h in range(N_HEADS):
                r = (0 if forward else N_HEADS) + h
                chains.append(dict(
                    b=b, h=h, mask=causal[forward], o_ref=o_ref, slot=(b * 2 + (0 if forward else 1)) * N_HEADS + h,
                    a=a8[r:r + 1], bcum=cum8[r:r + 1], b_end=tot8[r:r + 1],
                    q=qk_ref[b, h * HEAD_DIM:(h + 1) * HEAD_DIM, :],
                    k=qk_ref[b, GROUP_W + h * HEAD_DIM:GROUP_W + (h + 1) * HEAD_DIM, :],
                    v=v_ref[b, h * HEAD_DIM:(h + 1) * HEAD_DIM, :]))

    for ch in chains:
        ch["st"] = st_ref[ch["slot"]]
        ch["m"] = m_ref[ch["slot"]:ch["slot"] + 1, :]
        ch["kq"] = _dot_tn(ch["k"], ch["q"])
        ch["cq"] = _dot(ch["st"][:, 0:HEAD_DIM].astype(BF16), ch["q"])

    for ch in chains:
        a, m, mask = ch["a"], ch["m"], ch["mask"]
        a_col = jnp.broadcast_to(a, (CHUNK, CHUNK)).T
        a_max = jnp.max(a_col, axis=0, keepdims=True)
        big_m = jnp.maximum(jnp.max(jnp.where(mask, a_col, NEG), axis=0, keepdims=True), m)
        ch["inter"] = jnp.exp(m - big_m)
        ch["floor"] = jnp.exp(-(ch["bcum"] + big_m))
        s = ch["kq"] * jnp.exp(jnp.where(mask, a_col - big_m, NEG))
        ch["rowsum"] = jnp.sum(s, axis=0, keepdims=True)
        ch["s"] = s.astype(BF16)
        m_new = jnp.maximum(ch["b_end"] + m, ch["b_end"] + a_max)
        ch["decay"] = jnp.exp(ch["b_end"] + m - m_new)
        ch["kw"] = (ch["k"].astype(F32) * jnp.exp(ch["b_end"] + a - m_new)).astype(BF16)
        ch["m_new"] = m_new

    for ch in chains:
        ch["vs"] = _dot(ch["v"], ch["s"])
        v_ext = jnp.concatenate([ch["v"], ones_row], axis=0)
        ch["upd"] = _dot_nt(v_ext, ch["kw"])

    for ch in chains:
        h, slot, cq, inter = ch["h"], ch["slot"], ch["cq"], ch["inter"]
        num = cq[0:HEAD_DIM] * inter + ch["vs"]
        den = inter * cq[HEAD_DIM:HEAD_DIM + 1] + ch["rowsum"]
        ch["o_ref"][ch["b"], h * HEAD_DIM:(h + 1) * HEAD_DIM, :] = num / jnp.maximum(jnp.abs(den), ch["floor"])
        st_ref[slot, :, 0:HEAD_DIM] = ch["st"][:, 0:HEAD_DIM] * ch["decay"][:, 0:HEAD_DIM] + ch["upd"]
        m_ref[slot:slot + 1, :] = ch["m_new"]


def _mlstm(qk, vo, gt, gate_bias, n_ctx_chunks):
    B, _, S = qk.shape
    n_chunks = S // CHUNK
    fwd = lambda i: i
    bwd = lambda i: jnp.where(i < n_ctx_chunks, n_ctx_chunks - 1 - i, n_chunks - 1 + n_ctx_chunks - i)

    def specs(order):
        return [
            pl.BlockSpec((B, 2 * GROUP_W, CHUNK), lambda i: (0, 0, order(i))),
            pl.BlockSpec((B, GROUP_W, CHUNK), lambda i: (0, 0, order(i))),
            pl.BlockSpec((B, 4 * N_HEADS, CHUNK), lambda i: (0, 0, order(i))),
        ]

    n_slots = B * 2 * N_HEADS
    return pl.pallas_call(
        _mlstm_kernel,
        grid=(n_chunks,),
        in_specs=specs(fwd) + specs(bwd) + [pl.BlockSpec((4 * N_HEADS, CHUNK), lambda i: (0, 0))],
        out_specs=[
            pl.BlockSpec((B, GROUP_W, CHUNK), lambda i: (0, 0, fwd(i))),
            pl.BlockSpec((B, GROUP_W, CHUNK), lambda i: (0, 0, bwd(i))),
        ],
        out_shape=[jax.ShapeDtypeStruct((B, GROUP_W, S), F32)] * 2,
        scratch_shapes=[
            pltpu.VMEM((n_slots, ST_ROWS, LANES), F32),
            pltpu.VMEM((n_slots, LANES), F32),
        ],
        compiler_params=pltpu.CompilerParams(dimension_semantics=("arbitrary",)),
        name="mlstm",
    )(qk, vo, gt, qk, vo, gt, gate_bias)


def _hidden_chunks(hidden):
    step = 3 * 2 * LANES
    return [(lo, min(lo + step, hidden)) for lo in range(0, hidden, step)]


def _out_kernel(x_ref, a_ref, hf_ref, hb_ref, zo_ref, c_ref, d_ref, moda_ref, modc_ref, g_ref, mg_ref,
                wo_ref, wi_ref, wf_ref, o_ref, x1_ref, x1b_ref, tb_ref, f_ref, act_ref, bm_ref, *, hidden):
    @pl.when(pl.program_id(0) == 0)
    def _():
        x1_ref[...] = jnp.zeros_like(x1_ref)
        x1b_ref[...] = jnp.zeros_like(x1b_ref)
        tb_ref[...] = jnp.zeros_like(tb_ref)
        f_ref[...] = jnp.zeros_like(f_ref)

    tb = tb_ref[...]
    chunks = _hidden_chunks(hidden)

    def ffn_in(lo, hi):
        return _dot(tb, wi_ref[:, lo:hi]), _dot(tb, wi_ref[:, hidden + lo:hidden + hi])

    def act(gate_up, lo, hi):
        gate, up = gate_up
        act_ref[:, lo:hi] = (_silu(gate) * up).astype(BF16)

    def ffn_out(lo, hi):
        return _dot(act_ref[:, lo:hi], wf_ref[lo:hi, :])

    gu0 = ffn_in(*chunks[0])
    o_ref[...] = x1b_ref[...] + modc_ref[5:6, :] * _rms(f_ref[...], g_ref[3:4, :])
    gu1 = ffn_in(*chunks[1])
    hs = hf_ref[...] + hb_ref[...]
    hn = jnp.concatenate([_head_layer_norm_t(hs[h * HEAD_DIM:(h + 1) * HEAD_DIM]) for h in range(N_HEADS)], axis=0)
    bm_ref[...] = (hn * mg_ref[...] * jax.nn.sigmoid(zo_ref[...].astype(F32))).astype(BF16)
    act(gu0, *chunks[0])
    gu2 = ffn_in(*chunks[2])
    act(gu1, *chunks[1])
    y = (_dot_tn(a_ref[...], wo_ref[0:GROUP_W, :])
         + _dot_tn(bm_ref[...], wo_ref[GROUP_W:2 * GROUP_W, :])
         + _dot_tn(c_ref[...], wo_ref[2 * GROUP_W:3 * GROUP_W, :])
         + _dot(d_ref[...], wo_ref[3 * GROUP_W:4 * GROUP_W, :]))
    rest = [ffn_in(*ch) for ch in chunks[3:]]
    act(gu2, *chunks[2])
    f = ffn_out(*chunks[0])
    x1b_ref[...] = x1_ref[...]
    x1 = x_ref[...] + moda_ref[2:3, :] * _rms(y, g_ref[1:2, :])
    t = _rms(x1, g_ref[2:3, :] * (1.0 + moda_ref[4:5, :])) + moda_ref[3:4, :]
    x1_ref[...] = x1
    tb_ref[...] = t.astype(BF16)
    f = f + ffn_out(*chunks[1]) + ffn_out(*chunks[2])
    for gate_up, ch in zip(rest, chunks[3:]):
        act(gate_up, *ch)
        f = f + ffn_out(*ch)
    f_ref[...] = f


def _out_ffn(xs, a, hf, hb, vo, c, d, mods, norm_g, mnorm_g, w_out, w_ffn_in, w_ffn_out, n_batch_rows, drop_ctx):
    B, S, D = xs.shape
    tm = ROW_TILE
    npb = S // tm
    n_tiles = B * npb
    hidden = w_ffn_out.shape[0]
    assert len(_hidden_chunks(hidden)) >= 3
    cur = lambda s: jnp.minimum(s, n_tiles - 1)
    prv = lambda s: jnp.maximum(s - 2, 0)
    row = lambda s: (cur(s) // npb, cur(s) % npb, 0)
    colm = lambda s: (cur(s) // npb, 0, cur(s) % npb)
    mod_row = lambda t: (jnp.where(t % npb == 0, n_batch_rows, t // npb), 0, 0)
    c2 = lambda s: (0, 0)
    fm = pl.BlockSpec((None, GROUP_W, tm), colm)
    if drop_ctx:
        out_spec = pl.BlockSpec((None, tm, D), lambda s: (prv(s) // npb, jnp.maximum(prv(s) % npb - 1, 0), 0))
        out_shape = jax.ShapeDtypeStruct((B, S - tm, D), F32)
    else:
        out_spec = pl.BlockSpec((None, tm, D), lambda s: (prv(s) // npb, prv(s) % npb, 0))
        out_shape = jax.ShapeDtypeStruct((B, S, D), F32)
    kern = functools.partial(_out_kernel, hidden=hidden)
    return pl.pallas_call(
        kern,
        grid=(n_tiles + 2,),
        in_specs=[
            pl.BlockSpec((None, tm, D), row),
            fm, fm, fm,
            pl.BlockSpec((None, GROUP_W, tm), lambda s: (cur(s) // npb, 1, cur(s) % npb)),
            fm,
            pl.BlockSpec((None, tm, GROUP_W), row),
            pl.BlockSpec((None, 6, D), lambda s: mod_row(cur(s))),
            pl.BlockSpec((None, 6, D), lambda s: mod_row(prv(s))),
            pl.BlockSpec((4, D), c2),
            pl.BlockSpec((GROUP_W, tm), c2),
            pl.BlockSpec((D, D), c2, pipeline_mode=pl.Buffered(1)),
            pl.BlockSpec((D, 2 * hidden), c2, pipeline_mode=pl.Buffered(1)),
            pl.BlockSpec((hidden, D), c2, pipeline_mode=pl.Buffered(1)),
        ],
        out_specs=out_spec,
        out_shape=out_shape,
        scratch_shapes=[pltpu.VMEM((tm, D), F32), pltpu.VMEM((tm, D), F32), pltpu.VMEM((tm, D), BF16),
                        pltpu.VMEM((tm, D), F32), pltpu.VMEM((tm, hidden), BF16), pltpu.VMEM((GROUP_W, tm), BF16)],
        compiler_params=pltpu.CompilerParams(dimension_semantics=("arbitrary",), vmem_limit_bytes=VMEM_LIMIT),
        name="out_ffn",
    )(xs, a, hf, hb, vo, c, d, mods, mods, norm_g, mnorm_g, w_out, w_ffn_in, w_ffn_out)


def _rope_tables(n_tokens, ctx_len):
    rows = n_tokens // GRID_W
    axis_freq = HEAD_DIM // 4
    row = jnp.broadcast_to(jnp.arange(rows)[:, None], (rows, GRID_W)).reshape(-1).astype(F32)
    col = jnp.broadcast_to(jnp.arange(GRID_W)[None, :], (rows, GRID_W)).reshape(-1).astype(F32)
    inv = jnp.power(ROPE_BASE, -jnp.arange(axis_freq, dtype=F32) * 2.0 / (2 * axis_freq))
    ar = row[:, None] * inv
    ac = col[:, None] * inv
    ang = jnp.concatenate([ar, ar, ac, ac], axis=-1)
    cos, sin = jnp.cos(ang), jnp.sin(ang)
    sign = jnp.where((jnp.arange(HEAD_DIM) % 32) < 16, -1.0, 1.0).astype(F32)
    cos = jnp.concatenate([jnp.ones((ctx_len, HEAD_DIM), F32), cos], axis=0)
    sin = jnp.concatenate([jnp.zeros((ctx_len, HEAD_DIM), F32), sin * sign], axis=0)
    inv_cnt = []
    for seg in (ctx_len, n_tokens):
        t = jnp.arange(seg)[:, None]
        half = jnp.asarray([w // 2 for w in POOL_WINDOWS])[None, :]
        cnt = jnp.minimum(t + half, seg) - jnp.maximum(t - half, 0)
        inv_cnt.append(jnp.repeat(1.0 / cnt.astype(F32), HEAD_DIM, axis=1))
    return (jnp.tile(cos, (1, KV_HEADS)), jnp.tile(sin, (1, KV_HEADS)), cos.T, sin.T,
            jnp.concatenate(inv_cnt, axis=0))


def _arrange_w_in(w_in):
    ml0 = 2 * GROUP_W
    gate0 = ml0 + 4 * GROUP_W
    att0 = gate0 + 4 * N_HEADS
    pool0 = att0 + GROUP_W + 2 * KV_W
    gates = w_in[:, gate0:att0].reshape(-1, 2, 2, N_HEADS)
    gates = gates.transpose(0, 2, 1, 3).reshape(-1, 4 * N_HEADS)
    feat = jnp.concatenate([
        w_in[:, 0:ml0],
        w_in[:, ml0:ml0 + 2 * GROUP_W],
        w_in[:, ml0 + 2 * GROUP_W:gate0],
        w_in[:, att0:att0 + GROUP_W],
        w_in[:, att0 + GROUP_W + KV_W:pool0],
        gates,
    ], axis=1)
    tok = jnp.concatenate([w_in[:, att0 + GROUP_W:att0 + GROUP_W + KV_W], w_in[:, pool0:pool0 + GROUP_W]], axis=1)
    return feat.T.astype(BF16), tok.astype(BF16)


def _block_diag(w):
    g = w.shape[0]
    eye = jnp.eye(g, dtype=w.dtype)
    return (eye[:, None, :, None] * w[:, :, None, :]).reshape(g * w.shape[1], g * w.shape[2])


def kernel(x, c, ctx, c_ctx, w_mod, b_mod, norm_g, w_in, w_out, sgu_w, sgu_b, mlstm_conv_w, mlstm_gate_b,
           mlstm_norm_g, attn_sink, pool_w, pool_scale, w_ffn_in, w_ffn_out):
    B, N, D = x.shape
    ctx_len = ctx.shape[1]
    depth = w_mod.shape[0]
    assert D == 4 * GROUP_W and N % ROW_TILE == 0 and ctx_len == ROW_TILE
    assert w_in.shape[2] == 2 * GROUP_W + 4 * GROUP_W + 4 * N_HEADS + GROUP_W + 2 * KV_W + GROUP_W
    n_ctx_chunks = ctx_len // CHUNK

    xs = jnp.concatenate([ctx, x], axis=1)
    rows = -(-(B + 1) // SUBLANES) * SUBLANES
    cc = jnp.zeros((rows, D), F32).at[0:B].set(c).at[B].set(c_ctx)
    mods = _modulation(cc, w_mod, b_mod).reshape(depth, rows, 6, D)
    tables = _rope_tables(N, ctx_len)

    for l in range(depth):
        wt, wk = _arrange_w_in(w_in[l])
        conv_b = jnp.broadcast_to(mlstm_conv_w[l][:, :, None], (3, 2 * GROUP_W, ROW_TILE))
        gate_bias = jnp.broadcast_to(
            mlstm_gate_b[l].reshape(2, 2, N_HEADS).transpose(1, 0, 2).reshape(4 * N_HEADS, 1), (4 * N_HEADS, CHUNK))
        sink_rows = jnp.zeros((SUBLANES, (N_HEADS // KV_HEADS) * CHUNK), F32).at[0:KV_HEADS].set(
            jnp.repeat(attn_sink[l].reshape(KV_HEADS, N_HEADS // KV_HEADS), CHUNK, axis=1))
        mnorm_b = jnp.broadcast_to(mlstm_norm_g[l][:, None], (GROUP_W, ROW_TILE))

        zs, qk, vo, gt, aq, av, ak, dm = _in_proj(
            xs, mods[l], norm_g[l, 0:1], wt, wk, tables, conv_b, _block_diag(pool_w[l]).astype(BF16),
            pool_scale[l].reshape(1, GROUP_W), B)
        a = _sgu(zs, sgu_w[l].transpose(0, 2, 1).astype(BF16), sgu_b[l])
        hf, hb = _mlstm(qk, vo, gt, gate_bias, n_ctx_chunks)
        cm = _attention(aq, ak, av, sink_rows, n_ctx_chunks)
        xs = _out_ffn(xs, a, hf, hb, vo, cm, dm, mods[l], norm_g[l], mnorm_b,
                      w_out[l].astype(BF16), w_ffn_in[l].astype(BF16), w_ffn_out[l].astype(BF16), B,
                      drop_ctx=(l == depth - 1))
    return xs
```

```python
import functools

import jax
import jax.numpy as jnp
from jax import lax
from jax.experimental import pallas as pl
from jax.experimental.pallas import tpu as pltpu

F32 = jnp.float32
BF16 = jnp.bfloat16

GRID_W = 64
ROPE_BASE = 10000.0
EPS = 1e-6
LANES = 128
SUBLANES = 8
BF16_ROWS = 16
CHUNK = 128
HEAD_DIM = 64
N_HEADS = 4
KV_HEADS = 2
GROUP_W = N_HEADS * HEAD_DIM
KV_W = KV_HEADS * HEAD_DIM
POOL_WINDOWS = (2, 4, 8, 16)
HALO = 8
ROW_TILE = 256
NEG = -1e30
VMEM_LIMIT = 56 * 1024 * 1024

R_SGU = 0
R_QK = 512
R_VO = 1024
R_AQ = 1536
R_AV = 1792
R_GATE = 1920
R_END = 1936
T_AK = 0
T_POOL = KV_W
T_END = KV_W + GROUP_W

ST_ROWS = 80

HI = lax.Precision.HIGHEST


def _dot(a, b):
    return jnp.dot(a, b, preferred_element_type=F32)


def _dot_hi(a, b):
    return jnp.dot(a, b, preferred_element_type=F32, precision=HI)


def _dot_nt(a, b):
    return lax.dot_general(a, b, (((1,), (1,)), ((), ())), preferred_element_type=F32)


def _dot_tn(a, b):
    return lax.dot_general(a, b, (((0,), (0,)), ((), ())), preferred_element_type=F32)


def _rms(x, g):
    return x * lax.rsqrt(jnp.mean(x * x, axis=-1, keepdims=True) + EPS) * g


def _silu(x):
    return x * jax.nn.sigmoid(x)


def _put_chunks(ref, x):
    for i in range(ref.shape[0]):
        ref[i] = x[:, i * CHUNK:(i + 1) * CHUNK].astype(ref.dtype)


def _get_chunks(ref):
    return jnp.concatenate([ref[i] for i in range(ref.shape[0])], axis=1)


def _head_layer_norm_t(v):
    mu = jnp.mean(v, axis=0, keepdims=True)
    vc = v - mu
    return vc * lax.rsqrt(jnp.mean(vc * vc, axis=0, keepdims=True) + EPS)


def _mod_kernel(c_ref, w_ref, b_ref, o_ref):
    o_ref[...] = _dot(_silu(c_ref[...]).astype(BF16), w_ref[...].astype(BF16)) + b_ref[...]


def _modulation(cc, w_mod, b_mod):
    depth, d, n6 = w_mod.shape
    rows = cc.shape[0]
    tn = 1536
    return pl.pallas_call(
        _mod_kernel,
        grid=(depth, n6 // tn),
        in_specs=[
            pl.BlockSpec((rows, d), lambda l, j: (0, 0)),
            pl.BlockSpec((None, d, tn), lambda l, j: (l, 0, j)),
            pl.BlockSpec((None, 1, tn), lambda l, j: (l, 0, j)),
        ],
        out_specs=pl.BlockSpec((None, rows, tn), lambda l, j: (l, 0, j)),
        out_shape=jax.ShapeDtypeStruct((depth, rows, n6), F32),
        compiler_params=pltpu.CompilerParams(
            dimension_semantics=("arbitrary", "arbitrary"), vmem_limit_bytes=VMEM_LIMIT),
        name="modulation",
    )(cc, w_mod, b_mod.reshape(depth, 1, n6))


def _in_kernel(x_ref, xp_ref, xn_ref, mod_ref, g_ref, wt_ref, wk_ref, cos_ref, sin_ref, cost_ref, sint_ref,
               conv_ref, pw_ref, ps_ref, invc_ref,
               zs_ref, qk_ref, vo_ref, gt_ref, aq_ref, av_ref, ak_ref, d_ref, hb_ref, hh_ref, z_ref, pcol_ref,
               *, npb, n_tiles):
    tm = x_ref.shape[0]
    s = pl.program_id(0)

    @pl.when(s == 0)
    def _():
        hb_ref[...] = jnp.zeros_like(hb_ref)
        hh_ref[...] = jnp.zeros_like(hh_ref)
        z_ref[...] = jnp.zeros_like(z_ref)
        pcol_ref[...] = jnp.zeros_like(pcol_ref)

    def halo_flags(jj):
        return (jj >= 2).astype(F32), jnp.logical_and(jj >= 1, jj < npb - 1).astype(F32)

    has_prev, has_next = halo_flags(jnp.clip(s - 1, 0, n_tiles - 1) % npb)
    conv_prev, conv_next = halo_flags(jnp.maximum(s - 2, 0) % npb)
    hb = hb_ref[...]
    hh = hh_ref[...]

    zc = _dot_nt(wt_ref[R_QK:R_VO, :], hb)
    gain = g_ref[...] * (1.0 + mod_ref[1:2, :])
    hb_next = (_rms(x_ref[...], gain) + mod_ref[0:1, :]).astype(BF16)
    hh_next = (_rms(jnp.concatenate([xp_ref[...], xn_ref[...]], axis=0), gain) + mod_ref[0:1, :]).astype(BF16)

    _put_chunks(zs_ref, _dot_nt(wt_ref[R_SGU:R_QK, :], hb))

    z = z_ref[...]
    lane = lax.broadcasted_iota(jnp.int32, z.shape, 1)
    before = jnp.where(lane == 0, pcol_ref[:, 0:1] * conv_prev, pltpu.roll(z, 1, axis=1))
    after = jnp.where(lane == tm - 1, zc[:, 0:1] * conv_next, pltpu.roll(z, tm - 1, axis=1))
    qk = _silu(before * conv_ref[0] + z * conv_ref[1] + after * conv_ref[2])
    _put_chunks(qk_ref, jnp.concatenate([qk[0:GROUP_W], qk[GROUP_W:2 * GROUP_W] * (HEAD_DIM ** -0.5)], axis=0))
    pcol_ref[...] = jnp.broadcast_to(z[:, tm - 1:tm], pcol_ref.shape)
    z_ref[...] = zc

    zp = _dot(hb, wk_ref[:, T_POOL:T_END])
    zph = _dot(hh, wk_ref[:, T_POOL:T_END])
    _put_chunks(vo_ref, _dot_nt(wt_ref[R_VO:R_AQ, :], hb))
    rows = tm + 2 * HALO
    ext = jnp.concatenate([zph[0:HALO] * has_prev, zp, zph[HALO:2 * HALO] * has_next], axis=0)
    s2 = ext + pltpu.roll(ext, 1, axis=0)
    s4 = pltpu.roll(s2, 1, axis=0) + pltpu.roll(s2, rows - 1, axis=0)
    s8 = pltpu.roll(s4, 2, axis=0) + pltpu.roll(s4, rows - 2, axis=0)
    s16 = pltpu.roll(s8, 4, axis=0) + pltpu.roll(s8, rows - 4, axis=0)
    sums = [s[HALO:HALO + tm] for s in (s2, s4, s8, s16)]
    plane = lax.broadcasted_iota(jnp.int32, zp.shape, 1)
    pooled = sums[-1]
    for gi in range(len(POOL_WINDOWS) - 2, -1, -1):
        pooled = jnp.where(plane < (gi + 1) * HEAD_DIM, sums[gi], pooled)
    pooled = pooled * invc_ref[...]
    d_ref[...] = (_dot((pooled - zp).astype(BF16), pw_ref[...]) * ps_ref[...]).astype(BF16)

    q = _dot_nt(wt_ref[R_AQ:R_AV, :], hb)
    k = _dot(hb, wk_ref[:, T_AK:T_POOL])
    _put_chunks(av_ref, _dot_nt(wt_ref[R_AV:R_GATE, :], hb))
    _put_chunks(gt_ref, _dot_nt(wt_ref[R_GATE:R_END, :], hb))
    cos_t = jnp.concatenate([cost_ref[...]] * N_HEADS, axis=0)
    sin_t = jnp.concatenate([sint_ref[...]] * N_HEADS, axis=0)
    row = lax.broadcasted_iota(jnp.int32, q.shape, 0)
    rot_q = jnp.where((row & 31) < 16, pltpu.roll(q, GROUP_W - 16, axis=0), pltpu.roll(q, 16, axis=0))
    _put_chunks(aq_ref, (q * cos_t + rot_q * sin_t) * (HEAD_DIM ** -0.5))
    klane = lax.broadcasted_iota(jnp.int32, k.shape, 1)
    rot_k = jnp.where((klane & 31) < 16, pltpu.roll(k, KV_W - 16, axis=1), pltpu.roll(k, 16, axis=1))
    ak_ref[...] = (k * cos_ref[...] + rot_k * sin_ref[...]).astype(BF16)

    hb_ref[...] = hb_next
    hh_ref[...] = hh_next


def _in_proj(xs, mods, g0, wt, wk, tables, conv_b, pool_w_bd, pool_scale, n_batch_rows):
    B, S, D = xs.shape
    tm = ROW_TILE
    npb = S // tm
    n_tiles = B * npb
    per = tm // HALO
    cos_k, sin_k, cos_t, sin_t, inv_cnt = tables
    cur = lambda s: jnp.minimum(s, n_tiles - 1)
    prv = lambda s: jnp.clip(s - 1, 0, n_tiles - 1)
    old = lambda s: jnp.maximum(s - 2, 0)
    row = lambda s: (prv(s) // npb, prv(s) % npb, 0)
    colm = lambda s: (prv(s) // npb, prv(s) % npb, 0, 0)
    c2 = lambda s: (0, 0)
    kern = functools.partial(_in_kernel, npb=npb, n_tiles=n_tiles)
    cpt = tm // CHUNK
    fm = lambda rows, dt: (pl.BlockSpec((None, cpt, rows, CHUNK), colm),
                           jax.ShapeDtypeStruct((B, S // CHUNK, rows, CHUNK), dt))
    tk = lambda cols, dt: (pl.BlockSpec((None, tm, cols), row), jax.ShapeDtypeStruct((B, S, cols), dt))
    qk_out = (pl.BlockSpec((None, cpt, 2 * GROUP_W, CHUNK), lambda s: (old(s) // npb, old(s) % npb, 0, 0)),
              jax.ShapeDtypeStruct((B, S // CHUNK, 2 * GROUP_W, CHUNK), BF16))
    outs = [fm(2 * GROUP_W, BF16), qk_out, fm(2 * GROUP_W, BF16), fm(4 * N_HEADS, F32),
            fm(GROUP_W, BF16), fm(KV_W, BF16), tk(KV_W, BF16), tk(GROUP_W, BF16)]
    return pl.pallas_call(
        kern,
        grid=(n_tiles + 2,),
        in_specs=[
            pl.BlockSpec((None, tm, D), lambda s: (cur(s) // npb, cur(s) % npb, 0)),
            pl.BlockSpec((None, HALO, D), lambda s: (cur(s) // npb, jnp.maximum(cur(s) % npb * per - 1, 0), 0)),
            pl.BlockSpec((None, HALO, D),
                         lambda s: (cur(s) // npb, jnp.minimum((cur(s) % npb + 1) * per, S // HALO - 1), 0)),
            pl.BlockSpec((None, 6, D), lambda s: (jnp.where(cur(s) % npb == 0, n_batch_rows, cur(s) // npb), 0, 0)),
            pl.BlockSpec((1, D), c2),
            pl.BlockSpec((R_END, D), c2, pipeline_mode=pl.Buffered(1)),
            pl.BlockSpec((D, T_END), c2, pipeline_mode=pl.Buffered(1)),
            pl.BlockSpec((tm, KV_W), lambda s: (prv(s) % npb, 0)),
            pl.BlockSpec((tm, KV_W), lambda s: (prv(s) % npb, 0)),
            pl.BlockSpec((HEAD_DIM, tm), lambda s: (0, prv(s) % npb)),
            pl.BlockSpec((HEAD_DIM, tm), lambda s: (0, prv(s) % npb)),
            pl.BlockSpec((3, 2 * GROUP_W, tm), lambda s: (0, 0, 0)),
            pl.BlockSpec((GROUP_W, GROUP_W), c2),
            pl.BlockSpec((1, GROUP_W), c2),
            pl.BlockSpec((tm, GROUP_W), lambda s: (prv(s) % npb, 0)),
        ],
        out_specs=[o[0] for o in outs],
        out_shape=[o[1] for o in outs],
        scratch_shapes=[pltpu.VMEM((tm, D), BF16), pltpu.VMEM((2 * HALO, D), BF16),
                        pltpu.VMEM((2 * GROUP_W, tm), F32), pltpu.VMEM((2 * GROUP_W, LANES), F32)],
        compiler_params=pltpu.CompilerParams(dimension_semantics=("arbitrary",), vmem_limit_bytes=VMEM_LIMIT),
        name="in_proj",
    )(xs, xs, xs, mods, g0, wt, wk, cos_k, sin_k, cos_t, sin_t, conv_b, pool_w_bd, pool_scale, inv_cnt)


def _sgu_kernel(z_ref, w_ref, bias_ref, o_ref):
    for b in range(z_ref.shape[0]):
        g = jax.nn.gelu(z_ref[b].astype(F32))
        for h in range(N_HEADS):
            u = g[h * HEAD_DIM:(h + 1) * HEAD_DIM]
            vh = _head_layer_norm_t(g[GROUP_W + h * HEAD_DIM:GROUP_W + (h + 1) * HEAD_DIM]).astype(BF16)
            mixed = _dot(vh, w_ref[h]) + bias_ref[h:h + 1, :]
            o_ref[b, h * HEAD_DIM:(h + 1) * HEAD_DIM, :] = (u * mixed).astype(BF16)


def _chunk_spec(n_b, rows, order=lambda c: c, row_block=0):
    return pl.BlockSpec((n_b, None, rows, CHUNK), lambda c: (0, order(c), row_block, 0))


def _sgu(zs, sgu_wt, sgu_b):
    B, n_chunks = zs.shape[:2]
    return pl.pallas_call(
        _sgu_kernel,
        grid=(n_chunks,),
        in_specs=[
            _chunk_spec(B, 2 * GROUP_W),
            pl.BlockSpec((N_HEADS, CHUNK, CHUNK), lambda c: (0, 0, 0)),
            pl.BlockSpec((N_HEADS, CHUNK), lambda c: (0, 0)),
        ],
        out_specs=_chunk_spec(B, GROUP_W),
        out_shape=jax.ShapeDtypeStruct((B, n_chunks, GROUP_W, CHUNK), BF16),
        compiler_params=pltpu.CompilerParams(dimension_semantics=("arbitrary",)),
        name="sgu",
    )(zs, sgu_wt, sgu_b)


def _att_kernel(q_ref, kp_ref, kc_ref, kn_ref, kx_ref, vp_ref, vc_ref, vn_ref, vx_ref, sink_ref, o_ref,
                *, n_ctx_chunks, n_chunks):
    c = pl.program_id(0)
    prev_ok = c > n_ctx_chunks
    cur_ok = c >= n_ctx_chunks
    next_ok = jnp.logical_and(cur_ok, c < n_chunks - 1)
    n_keys = 3 * CHUNK + n_ctx_chunks * CHUNK
    group = N_HEADS // KV_HEADS
    j = lax.broadcasted_iota(jnp.int32, (n_keys, group * CHUNK), 0)
    i = lax.broadcasted_iota(jnp.int32, (n_keys, group * CHUNK), 1) & (CHUNK - 1)
    lo = jnp.where(cur_ok, jnp.where(prev_ok, 0, CHUNK), 3 * CHUNK)
    hi = jnp.where(next_ok, 3 * CHUNK, 2 * CHUNK)
    valid = ((j >= i) & (j <= i + 2 * CHUNK) & (j >= lo) & (j < hi)) | (j >= 3 * CHUNK)
    zeros = jnp.zeros((HEAD_DIM, group * CHUNK), BF16)
    n_b = q_ref.shape[0]
    pairs = [(b, kv) for b in range(n_b) for kv in range(KV_HEADS)]
    scores = []
    for b, kv in pairs:
        kcat = jnp.concatenate([kp_ref[b], kc_ref[b], kn_ref[b], kx_ref[b]], axis=0)
        q2 = jnp.concatenate([q_ref[b, (kv * group + g) * HEAD_DIM:(kv * group + g + 1) * HEAD_DIM, :]
                              for g in range(group)], axis=1)
        qm = jnp.concatenate([q2, zeros] if kv == 0 else [zeros, q2], axis=0)
        scores.append(_dot(kcat, qm))
    probs = []
    for (b, kv), s in zip(pairs, scores):
        s = jnp.where(valid, s, NEG)
        sink = sink_ref[kv:kv + 1, :]
        m = jnp.maximum(jnp.max(s, axis=0, keepdims=True), sink)
        p = jnp.exp(s - m)
        denom = jnp.sum(p, axis=0, keepdims=True) + jnp.exp(sink - m)
        probs.append((p.astype(BF16), denom))
    for (b, kv), (p, denom) in zip(pairs, probs):
        vcat = jnp.concatenate([vp_ref[b], vc_ref[b], vn_ref[b]] + [vx_ref[b, i] for i in range(n_ctx_chunks)],
                               axis=1)
        o = _dot(vcat[kv * HEAD_DIM:(kv + 1) * HEAD_DIM, :], p) / denom
        for g in range(group):
            h = kv * group + g
            o_ref[b, h * HEAD_DIM:(h + 1) * HEAD_DIM, :] = o[:, g * CHUNK:(g + 1) * CHUNK].astype(BF16)


def _attention(aq, ak, av, sink_rows, n_ctx_chunks):
    B, n_chunks = aq.shape[:2]
    ctx_w = n_ctx_chunks * CHUNK
    kern = functools.partial(_att_kernel, n_ctx_chunks=n_ctx_chunks, n_chunks=n_chunks)
    prev = lambda c: jnp.maximum(c - 1, 0)
    nxt = lambda c: jnp.minimum(c + 1, n_chunks - 1)
    kspec = lambda f: pl.BlockSpec((B, CHUNK, KV_W), lambda c: (0, f(c), 0))
    vspec = lambda f: _chunk_spec(B, KV_W, f)
    ident = lambda c: c
    return pl.pallas_call(
        kern,
        grid=(n_chunks,),
        in_specs=[
            _chunk_spec(B, GROUP_W),
            kspec(prev), kspec(ident), kspec(nxt), pl.BlockSpec((B, ctx_w, KV_W), lambda c: (0, 0, 0)),
            vspec(prev), vspec(ident), vspec(nxt),
            pl.BlockSpec((B, n_ctx_chunks, KV_W, CHUNK), lambda c: (0, 0, 0, 0)),
            pl.BlockSpec((SUBLANES, (N_HEADS // KV_HEADS) * CHUNK), lambda c: (0, 0)),
        ],
        out_specs=_chunk_spec(B, GROUP_W),
        out_shape=jax.ShapeDtypeStruct((B, n_chunks, GROUP_W, CHUNK), BF16),
        compiler_params=pltpu.CompilerParams(dimension_semantics=("arbitrary",)),
        name="attention",
    )(aq, ak, ak, ak, ak, av, av, av, av, sink_rows)


def _log_sigmoid(x):
    return jnp.minimum(x, 0.0) - jnp.log1p(jnp.exp(-jnp.abs(x)))


def _mlstm_kernel(qkf_ref, vf_ref, gf_ref, qkb_ref, vb_ref, gb_ref, bias_ref, hf_ref, hb_ref, st_ref, m_ref):
    @pl.when(pl.program_id(0) == 0)
    def _():
        st_ref[...] = jnp.zeros_like(st_ref)
        m_ref[...] = jnp.zeros_like(m_ref)

    s_i = lax.broadcasted_iota(jnp.int32, (CHUNK, CHUNK), 0)
    t_i = lax.broadcasted_iota(jnp.int32, (CHUNK, CHUNK), 1)
    causal = {True: s_i <= t_i, False: s_i >= t_i}
    ones_row = jnp.where(lax.broadcasted_iota(jnp.int32, (ST_ROWS - HEAD_DIM, CHUNK), 0) == 0, 1.0, 0.0).astype(BF16)
    all_ones = jnp.ones((CHUNK, CHUNK), F32)

    chains = []
    for b in range(qkf_ref.shape[0]):
        for forward, qk_ref, v_ref, g_ref, o_ref in ((True, qkf_ref, vf_ref, gf_ref, hf_ref),
                                                      (False, qkb_ref, vb_ref, gb_ref, hb_ref)):
            gates = g_ref[b] + bias_ref[...]
            li8 = gates[0:2 * N_HEADS]
            lf8 = _log_sigmoid(gates[2 * N_HEADS:4 * N_HEADS])
            cum8 = _dot_hi(lf8, jnp.where(causal[forward], 1.0, 0.0).astype(F32))
            tot8 = _dot_hi(lf8, all_ones)
            a8 = li8 - cum8
            for h in range(N_HEADS):
                r = (0 if forward else N_HEADS) + h
                chains.append(dict(
                    b=b, h=h, mask=causal[forward], o_ref=o_ref, slot=(b * 2 + (0 if forward else 1)) * N_HEADS + h,
                    a=a8[r:r + 1], bcum=cum8[r:r + 1], b_end=tot8[r:r + 1],
                    q=qk_ref[b, h * HEAD_DIM:(h + 1) * HEAD_DIM, :],
                    k=qk_ref[b, GROUP_W + h * HEAD_DIM:GROUP_W + (h + 1) * HEAD_DIM, :],
                    v=v_ref[b, h * HEAD_DIM:(h + 1) * HEAD_DIM, :]))

    for ch in chains:
        ch["st"] = st_ref[ch["slot"]]
        ch["m"] = m_ref[ch["slot"]:ch["slot"] + 1, :]
        ch["kq"] = _dot_tn(ch["k"], ch["q"])
        ch["cq"] = _dot(ch["st"][:, 0:HEAD_DIM].astype(BF16), ch["q"])

    for ch in chains:
        a, m, mask = ch["a"], ch["m"], ch["mask"]
        a_col = jnp.broadcast_to(a, (CHUNK, CHUNK)).T
        a_max = jnp.max(a_col, axis=0, keepdims=True)
        big_m = jnp.maximum(jnp.max(jnp.where(mask, a_col, NEG), axis=0, keepdims=True), m)
        ch["inter"] = jnp.exp(m - big_m)
        ch["floor"] = jnp.exp(-(ch["bcum"] + big_m))
        s = ch["kq"] * jnp.exp(jnp.where(mask, a_col - big_m, NEG))
        ch["rowsum"] = jnp.sum(s, axis=0, keepdims=True)
        ch["s"] = s.astype(BF16)
        m_new = jnp.maximum(ch["b_end"] + m, ch["b_end"] + a_max)
        ch["decay"] = jnp.exp(ch["b_end"] + m - m_new)
        ch["kw"] = (ch["k"].astype(F32) * jnp.exp(ch["b_end"] + a - m_new)).astype(BF16)
        ch["m_new"] = m_new

    for ch in chains:
        ch["vs"] = _dot(ch["v"], ch["s"])
        v_ext = jnp.concatenate([ch["v"], ones_row], axis=0)
        ch["upd"] = _dot_nt(v_ext, ch["kw"])

    for ch in chains:
        h, slot, cq, inter = ch["h"], ch["slot"], ch["cq"], ch["inter"]
        num = cq[0:HEAD_DIM] * inter + ch["vs"]
        den = inter * cq[HEAD_DIM:HEAD_DIM + 1] + ch["rowsum"]
        ch["o_ref"][ch["b"], h * HEAD_DIM:(h + 1) * HEAD_DIM, :] = num / jnp.maximum(jnp.abs(den), ch["floor"])
        st_ref[slot, :, 0:HEAD_DIM] = ch["st"][:, 0:HEAD_DIM] * ch["decay"][:, 0:HEAD_DIM] + ch["upd"]
        m_ref[slot:slot + 1, :] = ch["m_new"]


def _mlstm(qk, vo, gt, gate_bias, n_ctx_chunks):
    B, n_chunks = qk.shape[:2]
    fwd = lambda i: i
    bwd = lambda i: jnp.where(i < n_ctx_chunks, n_ctx_chunks - 1 - i, n_chunks - 1 + n_ctx_chunks - i)

    def specs(order):
        return [
            _chunk_spec(B, 2 * GROUP_W, order),
            _chunk_spec(B, GROUP_W, order),
            _chunk_spec(B, 4 * N_HEADS, order),
        ]

    n_slots = B * 2 * N_HEADS
    return pl.pallas_call(
        _mlstm_kernel,
        grid=(n_chunks,),
        in_specs=specs(fwd) + specs(bwd) + [pl.BlockSpec((4 * N_HEADS, CHUNK), lambda i: (0, 0))],
        out_specs=[_chunk_spec(B, GROUP_W, fwd), _chunk_spec(B, GROUP_W, bwd)],
        out_shape=[jax.ShapeDtypeStruct((B, n_chunks, GROUP_W, CHUNK), F32)] * 2,
        scratch_shapes=[
            pltpu.VMEM((n_slots, ST_ROWS, LANES), F32),
            pltpu.VMEM((n_slots, LANES), F32),
        ],
        compiler_params=pltpu.CompilerParams(dimension_semantics=("arbitrary",)),
        name="mlstm",
    )(qk, vo, gt, qk, vo, gt, gate_bias)


def _hidden_chunks(hidden):
    step = 3 * 2 * LANES
    return [(lo, min(lo + step, hidden)) for lo in range(0, hidden, step)]


def _out_kernel(x_ref, a_ref, hf_ref, hb_ref, zo_ref, c_ref, d_ref, moda_ref, modc_ref, g_ref, mg_ref,
                wo_ref, wi_ref, wf_ref, o_ref, x1_ref, x1b_ref, tb_ref, f_ref, act_ref, bm_ref, *, hidden):
    @pl.when(pl.program_id(0) == 0)
    def _():
        x1_ref[...] = jnp.zeros_like(x1_ref)
        x1b_ref[...] = jnp.zeros_like(x1b_ref)
        tb_ref[...] = jnp.zeros_like(tb_ref)
        f_ref[...] = jnp.zeros_like(f_ref)

    tb = tb_ref[...]
    chunks = _hidden_chunks(hidden)

    def ffn_in(lo, hi):
        return _dot(tb, wi_ref[:, lo:hi]), _dot(tb, wi_ref[:, hidden + lo:hidden + hi])

    def act(gate_up, lo, hi):
        gate, up = gate_up
        act_ref[:, lo:hi] = (_silu(gate) * up).astype(BF16)

    def ffn_out(lo, hi):
        return _dot(act_ref[:, lo:hi], wf_ref[lo:hi, :])

    gu0 = ffn_in(*chunks[0])
    o_ref[...] = x1b_ref[...] + modc_ref[5:6, :] * _rms(f_ref[...], g_ref[3:4, :])
    gu1 = ffn_in(*chunks[1])
    hs = _get_chunks(hf_ref) + _get_chunks(hb_ref)
    hn = jnp.concatenate([_head_layer_norm_t(hs[h * HEAD_DIM:(h + 1) * HEAD_DIM]) for h in range(N_HEADS)], axis=0)
    bm_ref[...] = (hn * mg_ref[...] * jax.nn.sigmoid(_get_chunks(zo_ref).astype(F32))).astype(BF16)
    act(gu0, *chunks[0])
    gu2 = ffn_in(*chunks[2])
    act(gu1, *chunks[1])
    y = (_dot_tn(_get_chunks(a_ref), wo_ref[0:GROUP_W, :])
         + _dot_tn(bm_ref[...], wo_ref[GROUP_W:2 * GROUP_W, :])
         + _dot_tn(_get_chunks(c_ref), wo_ref[2 * GROUP_W:3 * GROUP_W, :])
         + _dot(d_ref[...], wo_ref[3 * GROUP_W:4 * GROUP_W, :]))
    rest = [ffn_in(*ch) for ch in chunks[3:]]
    act(gu2, *chunks[2])
    f = ffn_out(*chunks[0])
    x1b_ref[...] = x1_ref[...]
    x1 = x_ref[...] + moda_ref[2:3, :] * _rms(y, g_ref[1:2, :])
    t = _rms(x1, g_ref[2:3, :] * (1.0 + moda_ref[4:5, :])) + moda_ref[3:4, :]
    x1_ref[...] = x1
    tb_ref[...] = t.astype(BF16)
    f = f + ffn_out(*chunks[1]) + ffn_out(*chunks[2])
    for gate_up, ch in zip(rest, chunks[3:]):
        act(gate_up, *ch)
        f = f + ffn_out(*ch)
    f_ref[...] = f


def _out_ffn(xs, a, hf, hb, vo, c, d, mods, norm_g, mnorm_g, w_out, w_ffn_in, w_ffn_out, n_batch_rows, drop_ctx):
    B, S, D = xs.shape
    tm = ROW_TILE
    npb = S // tm
    n_tiles = B * npb
    hidden = w_ffn_out.shape[0]
    assert len(_hidden_chunks(hidden)) >= 3
    cur = lambda s: jnp.minimum(s, n_tiles - 1)
    prv = lambda s: jnp.maximum(s - 2, 0)
    row = lambda s: (cur(s) // npb, cur(s) % npb, 0)
    mod_row = lambda t: (jnp.where(t % npb == 0, n_batch_rows, t // npb), 0, 0)
    c2 = lambda s: (0, 0)
    cpt = tm // CHUNK
    fm = pl.BlockSpec((None, cpt, GROUP_W, CHUNK), lambda s: (cur(s) // npb, cur(s) % npb, 0, 0))
    if drop_ctx:
        out_spec = pl.BlockSpec((None, tm, D), lambda s: (prv(s) // npb, jnp.maximum(prv(s) % npb - 1, 0), 0))
        out_shape = jax.ShapeDtypeStruct((B, S - tm, D), F32)
    else:
        out_spec = pl.BlockSpec((None, tm, D), lambda s: (prv(s) // npb, prv(s) % npb, 0))
        out_shape = jax.ShapeDtypeStruct((B, S, D), F32)
    kern = functools.partial(_out_kernel, hidden=hidden)
    return pl.pallas_call(
        kern,
        grid=(n_tiles + 2,),
        in_specs=[
            pl.BlockSpec((None, tm, D), row),
            fm, fm, fm,
            pl.BlockSpec((None, cpt, GROUP_W, CHUNK),
                         lambda s: (cur(s) // npb, cur(s) % npb, 1, 0)),
            fm,
            pl.BlockSpec((None, tm, GROUP_W), row),
            pl.BlockSpec((None, 6, D), lambda s: mod_row(cur(s))),
            pl.BlockSpec((None, 6, D), lambda s: mod_row(prv(s))),
            pl.BlockSpec((4, D), c2),
            pl.BlockSpec((GROUP_W, tm), c2),
            pl.BlockSpec((D, D), c2, pipeline_mode=pl.Buffered(1)),
            pl.BlockSpec((D, 2 * hidden), c2, pipeline_mode=pl.Buffered(1)),
            pl.BlockSpec((hidden, D), c2, pipeline_mode=pl.Buffered(1)),
        ],
        out_specs=out_spec,
        out_shape=out_shape,
        scratch_shapes=[pltpu.VMEM((tm, D), F32), pltpu.VMEM((tm, D), F32), pltpu.VMEM((tm, D), BF16),
                        pltpu.VMEM((tm, D), F32), pltpu.VMEM((tm, hidden), BF16), pltpu.VMEM((GROUP_W, tm), BF16)],
        compiler_params=pltpu.CompilerParams(dimension_semantics=("arbitrary",), vmem_limit_bytes=VMEM_LIMIT),
        name="out_ffn",
    )(xs, a, hf, hb, vo, c, d, mods, mods, norm_g, mnorm_g, w_out, w_ffn_in, w_ffn_out)


def _rope_tables(n_tokens, ctx_len):
    rows = n_tokens // GRID_W
    axis_freq = HEAD_DIM // 4
    row = jnp.broadcast_to(jnp.arange(rows)[:, None], (rows, GRID_W)).reshape(-1).astype(F32)
    col = jnp.broadcast_to(jnp.arange(GRID_W)[None, :], (rows, GRID_W)).reshape(-1).astype(F32)
    inv = jnp.power(ROPE_BASE, -jnp.arange(axis_freq, dtype=F32) * 2.0 / (2 * axis_freq))
    ar = row[:, None] * inv
    ac = col[:, None] * inv
    ang = jnp.concatenate([ar, ar, ac, ac], axis=-1)
    cos, sin = jnp.cos(ang), jnp.sin(ang)
    sign = jnp.where((jnp.arange(HEAD_DIM) % 32) < 16, -1.0, 1.0).astype(F32)
    cos = jnp.concatenate([jnp.ones((ctx_len, HEAD_DIM), F32), cos], axis=0)
    sin = jnp.concatenate([jnp.zeros((ctx_len, HEAD_DIM), F32), sin * sign], axis=0)
    inv_cnt = []
    for seg in (ctx_len, n_tokens):
        t = jnp.arange(seg)[:, None]
        half = jnp.asarray([w // 2 for w in POOL_WINDOWS])[None, :]
        cnt = jnp.minimum(t + half, seg) - jnp.maximum(t - half, 0)
        inv_cnt.append(jnp.repeat(1.0 / cnt.astype(F32), HEAD_DIM, axis=1))
    return (jnp.tile(cos, (1, KV_HEADS)), jnp.tile(sin, (1, KV_HEADS)), cos.T, sin.T,
            jnp.concatenate(inv_cnt, axis=0))


def _arrange_w_in(w_in):
    ml0 = 2 * GROUP_W
    gate0 = ml0 + 4 * GROUP_W
    att0 = gate0 + 4 * N_HEADS
    pool0 = att0 + GROUP_W + 2 * KV_W
    gates = w_in[:, gate0:att0].reshape(-1, 2, 2, N_HEADS)
    gates = gates.transpose(0, 2, 1, 3).reshape(-1, 4 * N_HEADS)
    feat = jnp.concatenate([
        w_in[:, 0:ml0],
        w_in[:, ml0:ml0 + 2 * GROUP_W],
        w_in[:, ml0 + 2 * GROUP_W:gate0],
        w_in[:, att0:att0 + GROUP_W],
        w_in[:, att0 + GROUP_W + KV_W:pool0],
        gates,
    ], axis=1)
    tok = jnp.concatenate([w_in[:, att0 + GROUP_W:att0 + GROUP_W + KV_W], w_in[:, pool0:pool0 + GROUP_W]], axis=1)
    return feat.T.astype(BF16), tok.astype(BF16)


def _block_diag(w):
    g = w.shape[0]
    eye = jnp.eye(g, dtype=w.dtype)
    return (eye[:, None, :, None] * w[:, :, None, :]).reshape(g * w.shape[1], g * w.shape[2])


def kernel(x, c, ctx, c_ctx, w_mod, b_mod, norm_g, w_in, w_out, sgu_w, sgu_b, mlstm_conv_w, mlstm_gate_b,
           mlstm_norm_g, attn_sink, pool_w, pool_scale, w_ffn_in, w_ffn_out):
    B, N, D = x.shape
    ctx_len = ctx.shape[1]
    depth = w_mod.shape[0]
    assert D == 4 * GROUP_W and N % ROW_TILE == 0 and ctx_len == ROW_TILE
    assert w_in.shape[2] == 2 * GROUP_W + 4 * GROUP_W + 4 * N_HEADS + GROUP_W + 2 * KV_W + GROUP_W
    n_ctx_chunks = ctx_len // CHUNK

    xs = jnp.concatenate([ctx, x], axis=1)
    rows = -(-(B + 1) // SUBLANES) * SUBLANES
    cc = jnp.zeros((rows, D), F32).at[0:B].set(c).at[B].set(c_ctx)
    mods = _modulation(cc, w_mod, b_mod).reshape(depth, rows, 6, D)
    tables = _rope_tables(N, ctx_len)

    for l in range(depth):
        wt, wk = _arrange_w_in(w_in[l])
        conv_b = jnp.broadcast_to(mlstm_conv_w[l][:, :, None], (3, 2 * GROUP_W, ROW_TILE))
        gate_bias = jnp.broadcast_to(
            mlstm_gate_b[l].reshape(2, 2, N_HEADS).transpose(1, 0, 2).reshape(4 * N_HEADS, 1), (4 * N_HEADS, CHUNK))
        sink_rows = jnp.zeros((SUBLANES, (N_HEADS // KV_HEADS) * CHUNK), F32).at[0:KV_HEADS].set(
            jnp.repeat(attn_sink[l].reshape(KV_HEADS, N_HEADS // KV_HEADS), CHUNK, axis=1))
        mnorm_b = jnp.broadcast_to(mlstm_norm_g[l][:, None], (GROUP_W, ROW_TILE))

        zs, qk, vo, gt, aq, av, ak, dm = _in_proj(
            xs, mods[l], norm_g[l, 0:1], wt, wk, tables, conv_b, _block_diag(pool_w[l]).astype(BF16),
            pool_scale[l].reshape(1, GROUP_W), B)
        a = _sgu(zs, sgu_w[l].transpose(0, 2, 1).astype(BF16), sgu_b[l])
        hf, hb = _mlstm(qk, vo, gt, gate_bias, n_ctx_chunks)
        cm = _attention(aq, ak, av, sink_rows, n_ctx_chunks)
        xs = _out_ffn(xs, a, hf, hb, vo, cm, dm, mods[l], norm_g[l], mnorm_b,
                      w_out[l].astype(BF16), w_ffn_in[l].astype(BF16), w_ffn_out[l].astype(BF16), B,
                      drop_ctx=(l == depth - 1))
    return xs
```

```python
import functools

import jax
import jax.numpy as jnp
from jax import lax
from jax.experimental import pallas as pl
from jax.experimental.pallas import tpu as pltpu

F32 = jnp.float32
BF16 = jnp.bfloat16

GRID_W = 64
ROPE_BASE = 10000.0
EPS = 1e-6
LANES = 128
SUBLANES = 8
BF16_ROWS = 16
CHUNK = 128
HEAD_DIM = 64
N_HEADS = 4
KV_HEADS = 2
GROUP_W = N_HEADS * HEAD_DIM
KV_W = KV_HEADS * HEAD_DIM
POOL_WINDOWS = (2, 4, 8, 16)
HALO = 8
ROW_TILE = 256
NEG = -1e30
VMEM_LIMIT = 56 * 1024 * 1024

R_SGU = 0
R_QK = 512
R_VO = 1024
R_AQ = 1536
R_AV = 1792
R_GATE = 1920
R_END = 1936
T_AK = 0
T_POOL = KV_W
T_END = KV_W + GROUP_W

ST_ROWS = 80

HI = lax.Precision.HIGHEST


def _dot(a, b):
    return jnp.dot(a, b, preferred_element_type=F32)


def _dot_hi(a, b):
    return jnp.dot(a, b, preferred_element_type=F32, precision=HI)


def _dot_nt(a, b):
    return lax.dot_general(a, b, (((1,), (1,)), ((), ())), preferred_element_type=F32)


def _dot_tn(a, b):
    return lax.dot_general(a, b, (((0,), (0,)), ((), ())), preferred_element_type=F32)


def _rms(x, g):
    return x * lax.rsqrt(jnp.mean(x * x, axis=-1, keepdims=True) + EPS) * g


def _silu(x):
    return x * jax.nn.sigmoid(x)


def _put_chunks(ref, x):
    for i in range(ref.shape[0]):
        ref[i] = x[:, i * CHUNK:(i + 1) * CHUNK].astype(ref.dtype)


def _get_chunks(ref):
    return jnp.concatenate([ref[i] for i in range(ref.shape[0])], axis=1)


def _head_layer_norm_t(v):
    mu = jnp.mean(v, axis=0, keepdims=True)
    vc = v - mu
    return vc * lax.rsqrt(jnp.mean(vc * vc, axis=0, keepdims=True) + EPS)


def _mod_kernel(c_ref, w_ref, b_ref, o_ref):
    o_ref[...] = _dot(_silu(c_ref[...]).astype(BF16), w_ref[...].astype(BF16)) + b_ref[...]


def _modulation(cc, w_mod, b_mod):
    depth, d, n6 = w_mod.shape
    rows = cc.shape[0]
    tn = 1536
    return pl.pallas_call(
        _mod_kernel,
        grid=(depth, n6 // tn),
        in_specs=[
            pl.BlockSpec((rows, d), lambda l, j: (0, 0)),
            pl.BlockSpec((None, d, tn), lambda l, j: (l, 0, j)),
            pl.BlockSpec((None, 1, tn), lambda l, j: (l, 0, j)),
        ],
        out_specs=pl.BlockSpec((None, rows, tn), lambda l, j: (l, 0, j)),
        out_shape=jax.ShapeDtypeStruct((depth, rows, n6), F32),
        compiler_params=pltpu.CompilerParams(
            dimension_semantics=("arbitrary", "arbitrary"), vmem_limit_bytes=VMEM_LIMIT),
        name="modulation",
    )(cc, w_mod, b_mod.reshape(depth, 1, n6))


def _first_tile_from_ctx(ctx_ref, x_ref, tile):
    rows = lax.broadcasted_iota(jnp.int32, x_ref.shape, 0)
    return jnp.where(rows < jnp.where(tile == 0, x_ref.shape[0], 0), ctx_ref[...], x_ref[...])


def _in_kernel(*refs, npb, n_tiles, split_ctx):
    ctx_ref = refs[0] if split_ctx else None
    (x_ref, xp_ref, xn_ref, mod_ref, g_ref, wt_ref, wk_ref, cos_ref, sin_ref, cost_ref, sint_ref,
     conv_ref, pw_ref, ps_ref, invc_ref,
     zs_ref, qk_ref, vo_ref, gt_ref, aq_ref, av_ref, ak_ref, d_ref,
     hb_ref, hh_ref, z_ref, pcol_ref) = refs[1:] if split_ctx else refs
    tm = x_ref.shape[0]
    s = pl.program_id(0)

    @pl.when(s == 0)
    def _():
        hb_ref[...] = jnp.zeros_like(hb_ref)
        hh_ref[...] = jnp.zeros_like(hh_ref)
        z_ref[...] = jnp.zeros_like(z_ref)
        pcol_ref[...] = jnp.zeros_like(pcol_ref)

    def halo_flags(jj):
        return (jj >= 2).astype(F32), jnp.logical_and(jj >= 1, jj < npb - 1).astype(F32)

    has_prev, has_next = halo_flags(jnp.clip(s - 1, 0, n_tiles - 1) % npb)
    conv_prev, conv_next = halo_flags(jnp.maximum(s - 2, 0) % npb)
    hb = hb_ref[...]
    hh = hh_ref[...]

    zc = _dot_nt(wt_ref[R_QK:R_VO, :], hb)
    gain = g_ref[...] * (1.0 + mod_ref[1:2, :])
    x_tile = _first_tile_from_ctx(ctx_ref, x_ref, jnp.minimum(s, n_tiles - 1) % npb) if split_ctx else x_ref[...]
    hb_next = (_rms(x_tile, gain) + mod_ref[0:1, :]).astype(BF16)
    hh_next = (_rms(jnp.concatenate([xp_ref[...], xn_ref[...]], axis=0), gain) + mod_ref[0:1, :]).astype(BF16)

    _put_chunks(zs_ref, _dot_nt(wt_ref[R_SGU:R_QK, :], hb))

    z = z_ref[...]
    lane = lax.broadcasted_iota(jnp.int32, z.shape, 1)
    before = jnp.where(lane == 0, pcol_ref[:, 0:1] * conv_prev, pltpu.roll(z, 1, axis=1))
    after = jnp.where(lane == tm - 1, zc[:, 0:1] * conv_next, pltpu.roll(z, tm - 1, axis=1))
    qk = _silu(before * conv_ref[0] + z * conv_ref[1] + after * conv_ref[2])
    _put_chunks(qk_ref, jnp.concatenate([qk[0:GROUP_W], qk[GROUP_W:2 * GROUP_W] * (HEAD_DIM ** -0.5)], axis=0))
    pcol_ref[...] = jnp.broadcast_to(z[:, tm - 1:tm], pcol_ref.shape)
    z_ref[...] = zc

    zp = _dot(hb, wk_ref[:, T_POOL:T_END])
    zph = _dot(hh, wk_ref[:, T_POOL:T_END])
    _put_chunks(vo_ref, _dot_nt(wt_ref[R_VO:R_AQ, :], hb))
    rows = tm + 2 * HALO
    ext = jnp.concatenate([zph[0:HALO] * has_prev, zp, zph[HALO:2 * HALO] * has_next], axis=0)
    s2 = ext + pltpu.roll(ext, 1, axis=0)
    s4 = pltpu.roll(s2, 1, axis=0) + pltpu.roll(s2, rows - 1, axis=0)
    s8 = pltpu.roll(s4, 2, axis=0) + pltpu.roll(s4, rows - 2, axis=0)
    s16 = pltpu.roll(s8, 4, axis=0) + pltpu.roll(s8, rows - 4, axis=0)
    sums = [s[HALO:HALO + tm] for s in (s2, s4, s8, s16)]
    plane = lax.broadcasted_iota(jnp.int32, zp.shape, 1)
    pooled = sums[-1]
    for gi in range(len(POOL_WINDOWS) - 2, -1, -1):
        pooled = jnp.where(plane < (gi + 1) * HEAD_DIM, sums[gi], pooled)
    pooled = pooled * invc_ref[...]
    d_ref[...] = (_dot((pooled - zp).astype(BF16), pw_ref[...]) * ps_ref[...]).astype(BF16)

    q = _dot_nt(wt_ref[R_AQ:R_AV, :], hb)
    k = _dot(hb, wk_ref[:, T_AK:T_POOL])
    _put_chunks(av_ref, _dot_nt(wt_ref[R_AV:R_GATE, :], hb))
    _put_chunks(gt_ref, _dot_nt(wt_ref[R_GATE:R_END, :], hb))
    cos_t = jnp.concatenate([cost_ref[...]] * N_HEADS, axis=0)
    sin_t = jnp.concatenate([sint_ref[...]] * N_HEADS, axis=0)
    row = lax.broadcasted_iota(jnp.int32, q.shape, 0)
    rot_q = jnp.where((row & 31) < 16, pltpu.roll(q, GROUP_W - 16, axis=0), pltpu.roll(q, 16, axis=0))
    _put_chunks(aq_ref, (q * cos_t + rot_q * sin_t) * (HEAD_DIM ** -0.5))
    klane = lax.broadcasted_iota(jnp.int32, k.shape, 1)
    rot_k = jnp.where((klane & 31) < 16, pltpu.roll(k, KV_W - 16, axis=1), pltpu.roll(k, 16, axis=1))
    ak_ref[...] = (k * cos_ref[...] + rot_k * sin_ref[...]).astype(BF16)

    hb_ref[...] = hb_next
    hh_ref[...] = hh_next


def _stream_specs(xs, tm, tile_of_step, npb, with_halo):
    per = tm // HALO
    bj = lambda s: (tile_of_step(s) // npb, tile_of_step(s) % npb)
    if isinstance(xs, tuple):
        ctx, x = xs
        D, last = x.shape[2], x.shape[1] // HALO - 1
        specs = [pl.BlockSpec((None, tm, D), lambda s: (bj(s)[0], 0, 0)),
                 pl.BlockSpec((None, tm, D), lambda s: (bj(s)[0], jnp.maximum(bj(s)[1] - 1, 0), 0))]
        halo = [pl.BlockSpec((None, HALO, D), lambda s: (bj(s)[0], jnp.clip((bj(s)[1] - 1) * per - 1, 0, last), 0)),
                pl.BlockSpec((None, HALO, D), lambda s: (bj(s)[0], jnp.clip(bj(s)[1] * per, 0, last), 0))]
        return specs + (halo if with_halo else []), [ctx, x] + ([x, x] if with_halo else [])
    D, last = xs.shape[2], xs.shape[1] // HALO - 1
    specs = [pl.BlockSpec((None, tm, D), lambda s: (bj(s)[0], bj(s)[1], 0))]
    halo = [pl.BlockSpec((None, HALO, D), lambda s: (bj(s)[0], jnp.maximum(bj(s)[1] * per - 1, 0), 0)),
            pl.BlockSpec((None, HALO, D), lambda s: (bj(s)[0], jnp.minimum((bj(s)[1] + 1) * per, last), 0))]
    return specs + (halo if with_halo else []), [xs] + ([xs, xs] if with_halo else [])


def _stream_shape(xs):
    if isinstance(xs, tuple):
        return xs[1].shape[0], xs[0].shape[1] + xs[1].shape[1], xs[1].shape[2]
    return xs.shape


def _in_proj(xs, mods, g0, wt, wk, tables, conv_b, pool_w_bd, pool_scale, n_batch_rows):
    B, S, D = _stream_shape(xs)
    tm = ROW_TILE
    npb = S // tm
    n_tiles = B * npb
    cos_k, sin_k, cos_t, sin_t, inv_cnt = tables
    cur = lambda s: jnp.minimum(s, n_tiles - 1)
    prv = lambda s: jnp.clip(s - 1, 0, n_tiles - 1)
    old = lambda s: jnp.maximum(s - 2, 0)
    row = lambda s: (prv(s) // npb, prv(s) % npb, 0)
    colm = lambda s: (prv(s) // npb, prv(s) % npb, 0, 0)
    c2 = lambda s: (0, 0)
    kern = functools.partial(_in_kernel, npb=npb, n_tiles=n_tiles, split_ctx=isinstance(xs, tuple))
    x_specs, x_args = _stream_specs(xs, tm, cur, npb, with_halo=True)
    cpt = tm // CHUNK
    fm = lambda rows, dt: (pl.BlockSpec((None, cpt, rows, CHUNK), colm),
                           jax.ShapeDtypeStruct((B, S // CHUNK, rows, CHUNK), dt))
    tk = lambda cols, dt: (pl.BlockSpec((None, tm, cols), row), jax.ShapeDtypeStruct((B, S, cols), dt))
    qk_out = (pl.BlockSpec((None, cpt, 2 * GROUP_W, CHUNK), lambda s: (old(s) // npb, old(s) % npb, 0, 0)),
              jax.ShapeDtypeStruct((B, S // CHUNK, 2 * GROUP_W, CHUNK), BF16))
    outs = [fm(2 * GROUP_W, BF16), qk_out, fm(2 * GROUP_W, BF16), fm(4 * N_HEADS, F32),
            fm(GROUP_W, BF16), fm(KV_W, BF16), tk(KV_W, BF16), tk(GROUP_W, BF16)]
    return pl.pallas_call(
        kern,
        grid=(n_tiles + 2,),
        in_specs=x_specs + [
            pl.BlockSpec((None, 6, D), lambda s: (jnp.where(cur(s) % npb == 0, n_batch_rows, cur(s) // npb), 0, 0)),
            pl.BlockSpec((1, D), c2),
            pl.BlockSpec((R_END, D), c2, pipeline_mode=pl.Buffered(1)),
            pl.BlockSpec((D, T_END), c2, pipeline_mode=pl.Buffered(1)),
            pl.BlockSpec((tm, KV_W), lambda s: (prv(s) % npb, 0)),
            pl.BlockSpec((tm, KV_W), lambda s: (prv(s) % npb, 0)),
            pl.BlockSpec((HEAD_DIM, tm), lambda s: (0, prv(s) % npb)),
            pl.BlockSpec((HEAD_DIM, tm), lambda s: (0, prv(s) % npb)),
            pl.BlockSpec((3, 2 * GROUP_W, tm), lambda s: (0, 0, 0)),
            pl.BlockSpec((GROUP_W, GROUP_W), c2),
            pl.BlockSpec((1, GROUP_W), c2),
            pl.BlockSpec((tm, GROUP_W), lambda s: (prv(s) % npb, 0)),
        ],
        out_specs=[o[0] for o in outs],
        out_shape=[o[1] for o in outs],
        scratch_shapes=[pltpu.VMEM((tm, D), BF16), pltpu.VMEM((2 * HALO, D), BF16),
                        pltpu.VMEM((2 * GROUP_W, tm), F32), pltpu.VMEM((2 * GROUP_W, LANES), F32)],
        compiler_params=pltpu.CompilerParams(dimension_semantics=("arbitrary",), vmem_limit_bytes=VMEM_LIMIT),
        name="in_proj",
    )(*x_args, mods, g0, wt, wk, cos_k, sin_k, cos_t, sin_t, conv_b, pool_w_bd, pool_scale, inv_cnt)


def _log_sigmoid(x):
    return jnp.minimum(x, 0.0) - jnp.log1p(jnp.exp(-jnp.abs(x)))


def _chunk_spec(n_b, rows, order=lambda c: c, row_block=0):
    return pl.BlockSpec((n_b, None, rows, CHUNK), lambda c: (0, order(c), row_block, 0))


def _mixers_kernel(zs_ref, sw_ref, sb_ref,
                   q_ref, kp_ref, kc_ref, kn_ref, kx_ref, vp_ref, vc_ref, vn_ref, vx_ref, sink_ref, mask_ref,
                   qkf_ref, vf_ref, gf_ref, qkb_ref, vb_ref, gb_ref, gbias_ref,
                   a_ref, c_ref, hf_ref, hb_ref, st_ref, m_ref, *, n_ctx_chunks):
    n_b = zs_ref.shape[0]

    @pl.when(pl.program_id(0) == 0)
    def _():
        st_ref[...] = jnp.zeros_like(st_ref)
        m_ref[...] = jnp.zeros_like(m_ref)

    group = N_HEADS // KV_HEADS
    zeros = jnp.zeros((HEAD_DIM, group * CHUNK), BF16)
    pairs = [(b, kv) for b in range(n_b) for kv in range(KV_HEADS)]
    scores = []
    for b, kv in pairs:
        kcat = jnp.concatenate([kp_ref[b], kc_ref[b], kn_ref[b], kx_ref[b]], axis=0)
        q2 = jnp.concatenate([q_ref[b, (kv * group + g) * HEAD_DIM:(kv * group + g + 1) * HEAD_DIM, :]
                              for g in range(group)], axis=1)
        qm = jnp.concatenate([q2, zeros] if kv == 0 else [zeros, q2], axis=0)
        scores.append(_dot(kcat, qm))

    s_i = lax.broadcasted_iota(jnp.int32, (CHUNK, CHUNK), 0)
    t_i = lax.broadcasted_iota(jnp.int32, (CHUNK, CHUNK), 1)
    causal = {True: s_i <= t_i, False: s_i >= t_i}
    ones_row = jnp.where(lax.broadcasted_iota(jnp.int32, (ST_ROWS - HEAD_DIM, CHUNK), 0) == 0, 1.0, 0.0).astype(BF16)
    all_ones = jnp.ones((CHUNK, CHUNK), F32)
    chains = []
    for b in range(n_b):
        for forward, qk_ref, v_ref, g_ref, o_ref in ((True, qkf_ref, vf_ref, gf_ref, hf_ref),
                                                      (False, qkb_ref, vb_ref, gb_ref, hb_ref)):
            gates = g_ref[b] + gbias_ref[...]
            li8 = gates[0:2 * N_HEADS]
            lf8 = _log_sigmoid(gates[2 * N_HEADS:4 * N_HEADS])
            cum8 = _dot_hi(lf8, jnp.where(causal[forward], 1.0, 0.0).astype(F32))
            tot8 = _dot_hi(lf8, all_ones)
            a8 = li8 - cum8
            for h in range(N_HEADS):
                r = (0 if forward else N_HEADS) + h
                chains.append(dict(
                    b=b, h=h, mask=causal[forward], o_ref=o_ref, slot=(b * 2 + (0 if forward else 1)) * N_HEADS + h,
                    a=a8[r:r + 1], bcum=cum8[r:r + 1], b_end=tot8[r:r + 1],
                    q=qk_ref[b, h * HEAD_DIM:(h + 1) * HEAD_DIM, :],
                    k=qk_ref[b, GROUP_W + h * HEAD_DIM:GROUP_W + (h + 1) * HEAD_DIM, :],
                    v=v_ref[b, h * HEAD_DIM:(h + 1) * HEAD_DIM, :]))
    for ch in chains:
        ch["st"] = st_ref[ch["slot"]]
        ch["m"] = m_ref[ch["slot"]:ch["slot"] + 1, :]
        ch["kq"] = _dot_tn(ch["k"], ch["q"])
        ch["cq"] = _dot(ch["st"][:, 0:HEAD_DIM].astype(BF16), ch["q"])

    for b in range(n_b):
        g = jax.nn.gelu(zs_ref[b].astype(F32))
        for h in range(N_HEADS):
            u = g[h * HEAD_DIM:(h + 1) * HEAD_DIM]
            vh = _head_layer_norm_t(g[GROUP_W + h * HEAD_DIM:GROUP_W + (h + 1) * HEAD_DIM]).astype(BF16)
            mixed = _dot(vh, sw_ref[h]) + sb_ref[h:h + 1, :]
            a_ref[b, h * HEAD_DIM:(h + 1) * HEAD_DIM, :] = (u * mixed).astype(BF16)

    probs = []
    for (b, kv), s in zip(pairs, scores):
        s = s + mask_ref[...]
        sink = sink_ref[kv:kv + 1, :]
        m = jnp.maximum(jnp.max(s, axis=0, keepdims=True), sink)
        p = jnp.exp(s - m)
        denom = jnp.sum(p, axis=0, keepdims=True) + jnp.exp(sink - m)
        probs.append((p.astype(BF16), denom))

    for ch in chains:
        a, m, mask = ch["a"], ch["m"], ch["mask"]
        a_col = jnp.broadcast_to(a, (CHUNK, CHUNK)).T
        a_max = jnp.max(a_col, axis=0, keepdims=True)
        big_m = jnp.maximum(jnp.max(jnp.where(mask, a_col, NEG), axis=0, keepdims=True), m)
        ch["inter"] = jnp.exp(m - big_m)
        ch["floor"] = jnp.exp(-(ch["bcum"] + big_m))
        s = ch["kq"] * jnp.exp(jnp.where(mask, a_col - big_m, NEG))
        ch["rowsum"] = jnp.sum(s, axis=0, keepdims=True)
        ch["s"] = s.astype(BF16)
        m_new = jnp.maximum(ch["b_end"] + m, ch["b_end"] + a_max)
        ch["decay"] = jnp.exp(ch["b_end"] + m - m_new)
        ch["kw"] = (ch["k"].astype(F32) * jnp.exp(ch["b_end"] + a - m_new)).astype(BF16)
        ch["m_new"] = m_new

    for (b, kv), (p, denom) in zip(pairs, probs):
        vcat = jnp.concatenate([vp_ref[b], vc_ref[b], vn_ref[b]] + [vx_ref[b, i] for i in range(n_ctx_chunks)],
                               axis=1)
        o = _dot(vcat[kv * HEAD_DIM:(kv + 1) * HEAD_DIM, :], p) / denom
        for g in range(group):
            h = kv * group + g
            c_ref[b, h * HEAD_DIM:(h + 1) * HEAD_DIM, :] = o[:, g * CHUNK:(g + 1) * CHUNK].astype(BF16)

    for ch in chains:
        ch["vs"] = _dot(ch["v"], ch["s"])
        v_ext = jnp.concatenate([ch["v"], ones_row], axis=0)
        ch["upd"] = _dot_nt(v_ext, ch["kw"])
    for ch in chains:
        h, slot, cq, inter = ch["h"], ch["slot"], ch["cq"], ch["inter"]
        num = cq[0:HEAD_DIM] * inter + ch["vs"]
        den = inter * cq[HEAD_DIM:HEAD_DIM + 1] + ch["rowsum"]
        ch["o_ref"][ch["b"], h * HEAD_DIM:(h + 1) * HEAD_DIM, :] = num / jnp.maximum(jnp.abs(den), ch["floor"])
        st_ref[slot, :, 0:HEAD_DIM] = ch["st"][:, 0:HEAD_DIM] * ch["decay"][:, 0:HEAD_DIM] + ch["upd"]
        m_ref[slot:slot + 1, :] = ch["m_new"]


def _attention_masks(n_ctx_chunks):
    group = N_HEADS // KV_HEADS
    n_keys = (3 + n_ctx_chunks) * CHUNK
    j = jnp.arange(n_keys)[:, None]
    i = (jnp.arange(group * CHUNK) & (CHUNK - 1))[None, :]
    masks = []
    for cur_ok, prev_ok, next_ok in ((False, False, False), (True, True, True), (True, False, True),
                                     (True, True, False), (True, False, False)):
        lo = (0 if prev_ok else CHUNK) if cur_ok else 3 * CHUNK
        hi = 3 * CHUNK if next_ok else 2 * CHUNK
        valid = ((j >= i) & (j <= i + 2 * CHUNK) & (j >= lo) & (j < hi)) | (j >= 3 * CHUNK)
        masks.append(jnp.where(valid, 0.0, NEG).astype(F32))
    return jnp.stack(masks)


def _mixers(zs, sgu_wt, sgu_b, aq, ak, av, sink_rows, masks, qk, vo, gt, gate_bias, n_ctx_chunks):
    B, n_chunks = zs.shape[:2]
    ctx_w = n_ctx_chunks * CHUNK
    group = N_HEADS // KV_HEADS
    n_keys = 3 * CHUNK + ctx_w
    kern = functools.partial(_mixers_kernel, n_ctx_chunks=n_ctx_chunks)
    ident = lambda c: c
    prev = lambda c: jnp.maximum(c - 1, 0)
    nxt = lambda c: jnp.minimum(c + 1, n_chunks - 1)
    bwd = lambda c: jnp.where(c < n_ctx_chunks, n_ctx_chunks - 1 - c, n_chunks - 1 + n_ctx_chunks - c)
    kspec = lambda f: pl.BlockSpec((B, CHUNK, KV_W), lambda c: (0, f(c), 0))
    vspec = lambda f: _chunk_spec(B, KV_W, f)

    def mask_kind(c):
        no_prev = (c == n_ctx_chunks).astype(jnp.int32)
        no_next = (c == n_chunks - 1).astype(jnp.int32)
        return jnp.where(c < n_ctx_chunks, 0, 1 + no_prev + 2 * no_next)

    def scan_specs(order):
        return [_chunk_spec(B, 2 * GROUP_W, order), _chunk_spec(B, GROUP_W, order),
                _chunk_spec(B, 4 * N_HEADS, order)]

    n_slots = B * 2 * N_HEADS
    return pl.pallas_call(
        kern,
        grid=(n_chunks,),
        in_specs=[
            _chunk_spec(B, 2 * GROUP_W),
            pl.BlockSpec((N_HEADS, CHUNK, CHUNK), lambda c: (0, 0, 0)),
            pl.BlockSpec((N_HEADS, CHUNK), lambda c: (0, 0)),
            _chunk_spec(B, GROUP_W),
            kspec(prev), kspec(ident), kspec(nxt), pl.BlockSpec((B, ctx_w, KV_W), lambda c: (0, 0, 0)),
            vspec(prev), vspec(ident), vspec(nxt),
            pl.BlockSpec((B, n_ctx_chunks, KV_W, CHUNK), lambda c: (0, 0, 0, 0)),
            pl.BlockSpec((SUBLANES, group * CHUNK), lambda c: (0, 0)),
            pl.BlockSpec((None, n_keys, group * CHUNK), lambda c: (mask_kind(c), 0, 0)),
        ] + scan_specs(ident) + scan_specs(bwd) + [pl.BlockSpec((4 * N_HEADS, CHUNK), lambda c: (0, 0))],
        out_specs=[_chunk_spec(B, GROUP_W), _chunk_spec(B, GROUP_W),
                   _chunk_spec(B, GROUP_W, ident), _chunk_spec(B, GROUP_W, bwd)],
        out_shape=[jax.ShapeDtypeStruct((B, n_chunks, GROUP_W, CHUNK), BF16)] * 2
                  + [jax.ShapeDtypeStruct((B, n_chunks, GROUP_W, CHUNK), F32)] * 2,
        scratch_shapes=[
            pltpu.VMEM((n_slots, ST_ROWS, LANES), F32),
            pltpu.VMEM((n_slots, LANES), F32),
        ],
        compiler_params=pltpu.CompilerParams(dimension_semantics=("arbitrary",)),
        name="mixers",
    )(zs, sgu_wt, sgu_b, aq, ak, ak, ak, ak, av, av, av, av, sink_rows, masks, qk, vo, gt, qk, vo, gt, gate_bias)


def _hidden_chunks(hidden):
    step = 3 * 2 * LANES
    return [(lo, min(lo + step, hidden)) for lo in range(0, hidden, step)]


def _out_kernel(*refs, hidden, npb, n_tiles, split_ctx):
    ctx_ref = refs[0] if split_ctx else None
    (x_ref, a_ref, hf_ref, hb_ref, zo_ref, c_ref, d_ref, moda_ref, modc_ref, g_ref, mg_ref,
     wo_ref, wi_ref, wf_ref, o_ref, x1_ref, x1b_ref, tb_ref, f_ref, act_ref, bm_ref) = refs[1:] if split_ctx else refs
    @pl.when(pl.program_id(0) == 0)
    def _():
        x1_ref[...] = jnp.zeros_like(x1_ref)
        x1b_ref[...] = jnp.zeros_like(x1b_ref)
        tb_ref[...] = jnp.zeros_like(tb_ref)
        f_ref[...] = jnp.zeros_like(f_ref)

    tb = tb_ref[...]
    chunks = _hidden_chunks(hidden)

    def ffn_in(lo, hi):
        return _dot(tb, wi_ref[:, lo:hi]), _dot(tb, wi_ref[:, hidden + lo:hidden + hi])

    def act(gate_up, lo, hi):
        gate, up = gate_up
        act_ref[:, lo:hi] = (_silu(gate) * up).astype(BF16)

    def ffn_out(lo, hi):
        return _dot(act_ref[:, lo:hi], wf_ref[lo:hi, :])

    gu0 = ffn_in(*chunks[0])
    o_ref[...] = x1b_ref[...] + modc_ref[5:6, :] * _rms(f_ref[...], g_ref[3:4, :])
    gu1 = ffn_in(*chunks[1])
    hs = _get_chunks(hf_ref) + _get_chunks(hb_ref)
    hn = jnp.concatenate([_head_layer_norm_t(hs[h * HEAD_DIM:(h + 1) * HEAD_DIM]) for h in range(N_HEADS)], axis=0)
    bm_ref[...] = (hn * mg_ref[...] * jax.nn.sigmoid(_get_chunks(zo_ref).astype(F32))).astype(BF16)
    act(gu0, *chunks[0])
    gu2 = ffn_in(*chunks[2])
    act(gu1, *chunks[1])
    y = (_dot_tn(_get_chunks(a_ref), wo_ref[0:GROUP_W, :])
         + _dot_tn(bm_ref[...], wo_ref[GROUP_W:2 * GROUP_W, :])
         + _dot_tn(_get_chunks(c_ref), wo_ref[2 * GROUP_W:3 * GROUP_W, :])
         + _dot(d_ref[...], wo_ref[3 * GROUP_W:4 * GROUP_W, :]))
    rest = [ffn_in(*ch) for ch in chunks[3:]]
    act(gu2, *chunks[2])
    f = ffn_out(*chunks[0])
    x1b_ref[...] = x1_ref[...]
    x_tile = (_first_tile_from_ctx(ctx_ref, x_ref, jnp.minimum(pl.program_id(0), n_tiles - 1) % npb)
              if split_ctx else x_ref[...])
    x1 = x_tile + moda_ref[2:3, :] * _rms(y, g_ref[1:2, :])
    t = _rms(x1, g_ref[2:3, :] * (1.0 + moda_ref[4:5, :])) + moda_ref[3:4, :]
    x1_ref[...] = x1
    tb_ref[...] = t.astype(BF16)
    f = f + ffn_out(*chunks[1]) + ffn_out(*chunks[2])
    for gate_up, ch in zip(rest, chunks[3:]):
        act(gate_up, *ch)
        f = f + ffn_out(*ch)
    f_ref[...] = f


def _out_ffn(xs, a, hf, hb, vo, c, d, mods, norm_g, mnorm_g, w_out, w_ffn_in, w_ffn_out, n_batch_rows, drop_ctx):
    B, S, D = _stream_shape(xs)
    tm = ROW_TILE
    npb = S // tm
    n_tiles = B * npb
    hidden = w_ffn_out.shape[0]
    assert len(_hidden_chunks(hidden)) >= 3
    cur = lambda s: jnp.minimum(s, n_tiles - 1)
    prv = lambda s: jnp.maximum(s - 2, 0)
    row = lambda s: (cur(s) // npb, cur(s) % npb, 0)
    mod_row = lambda t: (jnp.where(t % npb == 0, n_batch_rows, t // npb), 0, 0)
    c2 = lambda s: (0, 0)
    cpt = tm // CHUNK
    fm = pl.BlockSpec((None, cpt, GROUP_W, CHUNK), lambda s: (cur(s) // npb, cur(s) % npb, 0, 0))
    if drop_ctx:
        out_spec = pl.BlockSpec((None, tm, D), lambda s: (prv(s) // npb, jnp.maximum(prv(s) % npb - 1, 0), 0))
        out_shape = jax.ShapeDtypeStruct((B, S - tm, D), F32)
    else:
        out_spec = pl.BlockSpec((None, tm, D), lambda s: (prv(s) // npb, prv(s) % npb, 0))
        out_shape = jax.ShapeDtypeStruct((B, S, D), F32)
    kern = functools.partial(_out_kernel, hidden=hidden, npb=npb, n_tiles=n_tiles, split_ctx=isinstance(xs, tuple))
    x_specs, x_args = _stream_specs(xs, tm, cur, npb, with_halo=False)
    return pl.pallas_call(
        kern,
        grid=(n_tiles + 2,),
        in_specs=x_specs + [
            fm, fm, fm,
            pl.BlockSpec((None, cpt, GROUP_W, CHUNK),
                         lambda s: (cur(s) // npb, cur(s) % npb, 1, 0)),
            fm,
            pl.BlockSpec((None, tm, GROUP_W), row),
            pl.BlockSpec((None, 6, D), lambda s: mod_row(cur(s))),
            pl.BlockSpec((None, 6, D), lambda s: mod_row(prv(s))),
            pl.BlockSpec((4, D), c2),
            pl.BlockSpec((GROUP_W, tm), c2),
            pl.BlockSpec((D, D), c2, pipeline_mode=pl.Buffered(1)),
            pl.BlockSpec((D, 2 * hidden), c2, pipeline_mode=pl.Buffered(1)),
            pl.BlockSpec((hidden, D), c2, pipeline_mode=pl.Buffered(1)),
        ],
        out_specs=out_spec,
        out_shape=out_shape,
        scratch_shapes=[pltpu.VMEM((tm, D), F32), pltpu.VMEM((tm, D), F32), pltpu.VMEM((tm, D), BF16),
                        pltpu.VMEM((tm, D), F32), pltpu.VMEM((tm, hidden), BF16), pltpu.VMEM((GROUP_W, tm), BF16)],
        compiler_params=pltpu.CompilerParams(dimension_semantics=("arbitrary",), vmem_limit_bytes=VMEM_LIMIT),
        name="out_ffn",
    )(*x_args, a, hf, hb, vo, c, d, mods, mods, norm_g, mnorm_g, w_out, w_ffn_in, w_ffn_out)


def _rope_tables(n_tokens, ctx_len):
    rows = n_tokens // GRID_W
    axis_freq = HEAD_DIM // 4
    row = jnp.broadcast_to(jnp.arange(rows)[:, None], (rows, GRID_W)).reshape(-1).astype(F32)
    col = jnp.broadcast_to(jnp.arange(GRID_W)[None, :], (rows, GRID_W)).reshape(-1).astype(F32)
    inv = jnp.power(ROPE_BASE, -jnp.arange(axis_freq, dtype=F32) * 2.0 / (2 * axis_freq))
    ar = row[:, None] * inv
    ac = col[:, None] * inv
    ang = jnp.concatenate([ar, ar, ac, ac], axis=-1)
    cos, sin = jnp.cos(ang), jnp.sin(ang)
    sign = jnp.where((jnp.arange(HEAD_DIM) % 32) < 16, -1.0, 1.0).astype(F32)
    cos = jnp.concatenate([jnp.ones((ctx_len, HEAD_DIM), F32), cos], axis=0)
    sin = jnp.concatenate([jnp.zeros((ctx_len, HEAD_DIM), F32), sin * sign], axis=0)
    inv_cnt = []
    for seg in (ctx_len, n_tokens):
        t = jnp.arange(seg)[:, None]
        half = jnp.asarray([w // 2 for w in POOL_WINDOWS])[None, :]
        cnt = jnp.minimum(t + half, seg) - jnp.maximum(t - half, 0)
        inv_cnt.append(jnp.repeat(1.0 / cnt.astype(F32), HEAD_DIM, axis=1))
    return (jnp.tile(cos, (1, KV_HEADS)), jnp.tile(sin, (1, KV_HEADS)), cos.T, sin.T,
            jnp.concatenate(inv_cnt, axis=0))


def _arrange_w_in(w_in):
    ml0 = 2 * GROUP_W
    gate0 = ml0 + 4 * GROUP_W
    att0 = gate0 + 4 * N_HEADS
    pool0 = att0 + GROUP_W + 2 * KV_W
    gates = w_in[:, gate0:att0].reshape(-1, 2, 2, N_HEADS)
    gates = gates.transpose(0, 2, 1, 3).reshape(-1, 4 * N_HEADS)
    feat = jnp.concatenate([
        w_in[:, 0:ml0],
        w_in[:, ml0:ml0 + 2 * GROUP_W],
        w_in[:, ml0 + 2 * GROUP_W:gate0],
        w_in[:, att0:att0 + GROUP_W],
        w_in[:, att0 + GROUP_W + KV_W:pool0],
        gates,
    ], axis=1)
    tok = jnp.concatenate([w_in[:, att0 + GROUP_W:att0 + GROUP_W + KV_W], w_in[:, pool0:pool0 + GROUP_W]], axis=1)
    return feat.T.astype(BF16), tok.astype(BF16)


def _block_diag(w):
    g = w.shape[0]
    eye = jnp.eye(g, dtype=w.dtype)
    return (eye[:, None, :, None] * w[:, :, None, :]).reshape(g * w.shape[1], g * w.shape[2])


def kernel(x, c, ctx, c_ctx, w_mod, b_mod, norm_g, w_in, w_out, sgu_w, sgu_b, mlstm_conv_w, mlstm_gate_b,
           mlstm_norm_g, attn_sink, pool_w, pool_scale, w_ffn_in, w_ffn_out):
    B, N, D = x.shape
    ctx_len = ctx.shape[1]
    depth = w_mod.shape[0]
    assert D == 4 * GROUP_W and N % ROW_TILE == 0 and ctx_len == ROW_TILE
    assert w_in.shape[2] == 2 * GROUP_W + 4 * GROUP_W + 4 * N_HEADS + GROUP_W + 2 * KV_W + GROUP_W
    n_ctx_chunks = ctx_len // CHUNK

    xs = (ctx, x)
    rows = -(-(B + 1) // SUBLANES) * SUBLANES
    cc = jnp.zeros((rows, D), F32).at[0:B].set(c).at[B].set(c_ctx)
    mods = _modulation(cc, w_mod, b_mod).reshape(depth, rows, 6, D)
    tables = _rope_tables(N, ctx_len)
    masks = _attention_masks(n_ctx_chunks)

    for l in range(depth):
        wt, wk = _arrange_w_in(w_in[l])
        conv_b = jnp.broadcast_to(mlstm_conv_w[l][:, :, None], (3, 2 * GROUP_W, ROW_TILE))
        gate_bias = jnp.broadcast_to(
            mlstm_gate_b[l].reshape(2, 2, N_HEADS).transpose(1, 0, 2).reshape(4 * N_HEADS, 1), (4 * N_HEADS, CHUNK))
        sink_rows = jnp.zeros((SUBLANES, (N_HEADS // KV_HEADS) * CHUNK), F32).at[0:KV_HEADS].set(
            jnp.repeat(attn_sink[l].reshape(KV_HEADS, N_HEADS // KV_HEADS), CHUNK, axis=1))
        mnorm_b = jnp.broadcast_to(mlstm_norm_g[l][:, None], (GROUP_W, ROW_TILE))

        zs, qk, vo, gt, aq, av, ak, dm = _in_proj(
            xs, mods[l], norm_g[l, 0:1], wt, wk, tables, conv_b, _block_diag(pool_w[l]).astype(BF16),
            pool_scale[l].reshape(1, GROUP_W), B)
        a, cm, hf, hb = _mixers(zs, sgu_w[l].transpose(0, 2, 1).astype(BF16), sgu_b[l], aq, ak, av, sink_rows, masks,
                                qk, vo, gt, gate_bias, n_ctx_chunks)
        xs = _out_ffn(xs, a, hf, hb, vo, cm, dm, mods[l], norm_g[l], mnorm_b,
                      w_out[l].astype(BF16), w_ffn_in[l].astype(BF16), w_ffn_out[l].astype(BF16), B,
                      drop_ctx=(l == depth - 1))
    return xs
```

```python
import functools

import jax
import jax.numpy as jnp
from jax import lax
from jax.experimental import pallas as pl
from jax.experimental.pallas import tpu as pltpu

F32 = jnp.float32
BF16 = jnp.bfloat16

GRID_W = 64
ROPE_BASE = 10000.0
EPS = 1e-6
LANES = 128
SUBLANES = 8
BF16_ROWS = 16
CHUNK = 128
HEAD_DIM = 64
N_HEADS = 4
KV_HEADS = 2
GROUP_W = N_HEADS * HEAD_DIM
KV_W = KV_HEADS * HEAD_DIM
POOL_WINDOWS = (2, 4, 8, 16)
HALO = 8
ROW_TILE = 256
NEG = -1e30
VMEM_LIMIT = 56 * 1024 * 1024

R_SGU = 0
R_QK = 512
R_VO = 1024
R_AQ = 1536
R_AV = 1792
R_GATE = 1920
R_END = 1936
T_AK = 0
T_POOL = KV_W
T_END = KV_W + GROUP_W

ST_ROWS = 80
LOG2E = 1.4426950408889634

HI = lax.Precision.HIGHEST


def _dot(a, b):
    return jnp.dot(a, b, preferred_element_type=F32)


def _dot_hi(a, b):
    return jnp.dot(a, b, preferred_element_type=F32, precision=HI)


def _dot_nt(a, b):
    return lax.dot_general(a, b, (((1,), (1,)), ((), ())), preferred_element_type=F32)


def _dot_tn(a, b):
    return lax.dot_general(a, b, (((0,), (0,)), ((), ())), preferred_element_type=F32)


def _rms(x, g):
    return x * lax.rsqrt(jnp.mean(x * x, axis=-1, keepdims=True) + EPS) * g


def _silu(x):
    return x * jax.nn.sigmoid(x)


def _put_chunks(ref, x):
    for i in range(ref.shape[0]):
        ref[i] = x[:, i * CHUNK:(i + 1) * CHUNK].astype(ref.dtype)


def _get_chunks(ref):
    return jnp.concatenate([ref[i] for i in range(ref.shape[0])], axis=1)


def _head_layer_norm_t(v):
    mu = jnp.mean(v, axis=0, keepdims=True)
    vc = v - mu
    return vc * lax.rsqrt(jnp.mean(vc * vc, axis=0, keepdims=True) + EPS)


def _mod_kernel(c_ref, w_ref, b_ref, o_ref):
    o_ref[...] = _dot(_silu(c_ref[...]).astype(BF16), w_ref[...].astype(BF16)) + b_ref[...]


def _modulation(cc, w_mod, b_mod):
    depth, d, n6 = w_mod.shape
    rows = cc.shape[0]
    tn = 1536
    return pl.pallas_call(
        _mod_kernel,
        grid=(depth, n6 // tn),
        in_specs=[
            pl.BlockSpec((rows, d), lambda l, j: (0, 0)),
            pl.BlockSpec((None, d, tn), lambda l, j: (l, 0, j)),
            pl.BlockSpec((None, 1, tn), lambda l, j: (l, 0, j)),
        ],
        out_specs=pl.BlockSpec((None, rows, tn), lambda l, j: (l, 0, j)),
        out_shape=jax.ShapeDtypeStruct((depth, rows, n6), F32),
        compiler_params=pltpu.CompilerParams(
            dimension_semantics=("arbitrary", "arbitrary"), vmem_limit_bytes=VMEM_LIMIT),
        name="modulation",
    )(cc, w_mod, b_mod.reshape(depth, 1, n6))


def _first_tile_from_ctx(ctx_ref, x_ref, tile):
    rows = lax.broadcasted_iota(jnp.int32, x_ref.shape, 0)
    return jnp.where(rows < jnp.where(tile == 0, x_ref.shape[0], 0), ctx_ref[...], x_ref[...])


def _in_kernel(*refs, npb, n_tiles, split_ctx):
    ctx_ref = refs[0] if split_ctx else None
    (x_ref, xp_ref, xn_ref, mod_ref, g_ref, wt_ref, wk_ref, cos_ref, sin_ref, cost_ref, sint_ref,
     conv_ref, pw_ref, ps_ref, invc_ref,
     zs_ref, qk_ref, vo_ref, gt_ref, aq_ref, av_ref, ak_ref, d_ref,
     hb_ref, hh_ref, z_ref, pcol_ref) = refs[1:] if split_ctx else refs
    tm = x_ref.shape[0]
    s = pl.program_id(0)

    @pl.when(s == 0)
    def _():
        hb_ref[...] = jnp.zeros_like(hb_ref)
        hh_ref[...] = jnp.zeros_like(hh_ref)
        z_ref[...] = jnp.zeros_like(z_ref)
        pcol_ref[...] = jnp.zeros_like(pcol_ref)

    def halo_flags(jj):
        return (jj >= 2).astype(F32), jnp.logical_and(jj >= 1, jj < npb - 1).astype(F32)

    has_prev, has_next = halo_flags(jnp.clip(s - 1, 0, n_tiles - 1) % npb)
    conv_prev, conv_next = halo_flags(jnp.maximum(s - 2, 0) % npb)
    hb = hb_ref[...]
    hh = hh_ref[...]

    zc = _dot_nt(wt_ref[R_QK:R_VO, :], hb)
    gain = g_ref[...] * (1.0 + mod_ref[1:2, :])
    x_tile = _first_tile_from_ctx(ctx_ref, x_ref, jnp.minimum(s, n_tiles - 1) % npb) if split_ctx else x_ref[...]
    hb_next = (_rms(x_tile, gain) + mod_ref[0:1, :]).astype(BF16)
    hh_next = (_rms(jnp.concatenate([xp_ref[...], xn_ref[...]], axis=0), gain) + mod_ref[0:1, :]).astype(BF16)

    _put_chunks(zs_ref, _dot_nt(wt_ref[R_SGU:R_QK, :], hb))

    z = z_ref[...]
    lane = lax.broadcasted_iota(jnp.int32, z.shape, 1)
    before = jnp.where(lane == 0, pcol_ref[:, 0:1] * conv_prev, pltpu.roll(z, 1, axis=1))
    after = jnp.where(lane == tm - 1, zc[:, 0:1] * conv_next, pltpu.roll(z, tm - 1, axis=1))
    qk = _silu(before * conv_ref[0] + z * conv_ref[1] + after * conv_ref[2])
    _put_chunks(qk_ref, jnp.concatenate([qk[0:GROUP_W], qk[GROUP_W:2 * GROUP_W] * (HEAD_DIM ** -0.5)], axis=0))
    pcol_ref[...] = jnp.broadcast_to(z[:, tm - 1:tm], pcol_ref.shape)
    z_ref[...] = zc

    zp = _dot(hb, wk_ref[:, T_POOL:T_END])
    zph = _dot(hh, wk_ref[:, T_POOL:T_END])
    _put_chunks(vo_ref, _dot_nt(wt_ref[R_VO:R_AQ, :], hb))
    rows = tm + 2 * HALO
    ext = jnp.concatenate([zph[0:HALO] * has_prev, zp, zph[HALO:2 * HALO] * has_next], axis=0)
    s2 = ext + pltpu.roll(ext, 1, axis=0)
    s4 = pltpu.roll(s2, 1, axis=0) + pltpu.roll(s2, rows - 1, axis=0)
    s8 = pltpu.roll(s4, 2, axis=0) + pltpu.roll(s4, rows - 2, axis=0)
    s16 = pltpu.roll(s8, 4, axis=0) + pltpu.roll(s8, rows - 4, axis=0)
    sums = [s[HALO:HALO + tm] for s in (s2, s4, s8, s16)]
    plane = lax.broadcasted_iota(jnp.int32, zp.shape, 1)
    pooled = sums[-1]
    for gi in range(len(POOL_WINDOWS) - 2, -1, -1):
        pooled = jnp.where(plane < (gi + 1) * HEAD_DIM, sums[gi], pooled)
    pooled = pooled * invc_ref[...]
    d_ref[...] = (_dot((pooled - zp).astype(BF16), pw_ref[...]) * ps_ref[...]).astype(BF16)

    q = _dot_nt(wt_ref[R_AQ:R_AV, :], hb)
    k = _dot(hb, wk_ref[:, T_AK:T_POOL])
    _put_chunks(av_ref, _dot_nt(wt_ref[R_AV:R_GATE, :], hb))
    _put_chunks(gt_ref, _dot_nt(wt_ref[R_GATE:R_END, :], hb))
    cos_t = jnp.concatenate([cost_ref[...]] * N_HEADS, axis=0)
    sin_t = jnp.concatenate([sint_ref[...]] * N_HEADS, axis=0)
    row = lax.broadcasted_iota(jnp.int32, q.shape, 0)
    rot_q = jnp.where((row & 31) < 16, pltpu.roll(q, GROUP_W - 16, axis=0), pltpu.roll(q, 16, axis=0))
    _put_chunks(aq_ref, (q * cos_t + rot_q * sin_t) * (HEAD_DIM ** -0.5 * LOG2E))
    klane = lax.broadcasted_iota(jnp.int32, k.shape, 1)
    rot_k = jnp.where((klane & 31) < 16, pltpu.roll(k, KV_W - 16, axis=1), pltpu.roll(k, 16, axis=1))
    ak_ref[...] = (k * cos_ref[...] + rot_k * sin_ref[...]).astype(BF16)

    hb_ref[...] = hb_next
    hh_ref[...] = hh_next


def _stream_specs(xs, tm, tile_of_step, npb, with_halo):
    per = tm // HALO
    bj = lambda s: (tile_of_step(s) // npb, tile_of_step(s) % npb)
    if isinstance(xs, tuple):
        ctx, x = xs
        D, last = x.shape[2], x.shape[1] // HALO - 1
        specs = [pl.BlockSpec((None, tm, D), lambda s: (bj(s)[0], 0, 0)),
                 pl.BlockSpec((None, tm, D), lambda s: (bj(s)[0], jnp.maximum(bj(s)[1] - 1, 0), 0))]
        halo = [pl.BlockSpec((None, HALO, D), lambda s: (bj(s)[0], jnp.clip((bj(s)[1] - 1) * per - 1, 0, last), 0)),
                pl.BlockSpec((None, HALO, D), lambda s: (bj(s)[0], jnp.clip(bj(s)[1] * per, 0, last), 0))]
        return specs + (halo if with_halo else []), [ctx, x] + ([x, x] if with_halo else [])
    D, last = xs.shape[2], xs.shape[1] // HALO - 1
    specs = [pl.BlockSpec((None, tm, D), lambda s: (bj(s)[0], bj(s)[1], 0))]
    halo = [pl.BlockSpec((None, HALO, D), lambda s: (bj(s)[0], jnp.maximum(bj(s)[1] * per - 1, 0), 0)),
            pl.BlockSpec((None, HALO, D), lambda s: (bj(s)[0], jnp.minimum((bj(s)[1] + 1) * per, last), 0))]
    return specs + (halo if with_halo else []), [xs] + ([xs, xs] if with_halo else [])


def _stream_shape(xs):
    if isinstance(xs, tuple):
        return xs[1].shape[0], xs[0].shape[1] + xs[1].shape[1], xs[1].shape[2]
    return xs.shape


def _in_proj(xs, mods, g0, wt, wk, tables, conv_b, pool_w_bd, pool_scale, n_batch_rows):
    B, S, D = _stream_shape(xs)
    tm = ROW_TILE
    npb = S // tm
    n_tiles = B * npb
    cos_k, sin_k, cos_t, sin_t, inv_cnt = tables
    cur = lambda s: jnp.minimum(s, n_tiles - 1)
    prv = lambda s: jnp.clip(s - 1, 0, n_tiles - 1)
    old = lambda s: jnp.maximum(s - 2, 0)
    row = lambda s: (prv(s) // npb, prv(s) % npb, 0)
    colm = lambda s: (prv(s) // npb, prv(s) % npb, 0, 0)
    c2 = lambda s: (0, 0)
    kern = functools.partial(_in_kernel, npb=npb, n_tiles=n_tiles, split_ctx=isinstance(xs, tuple))
    x_specs, x_args = _stream_specs(xs, tm, cur, npb, with_halo=True)
    cpt = tm // CHUNK
    fm = lambda rows, dt: (pl.BlockSpec((None, cpt, rows, CHUNK), colm),
                           jax.ShapeDtypeStruct((B, S // CHUNK, rows, CHUNK), dt))
    tk = lambda cols, dt: (pl.BlockSpec((None, tm, cols), row), jax.ShapeDtypeStruct((B, S, cols), dt))
    qk_out = (pl.BlockSpec((None, cpt, 2 * GROUP_W, CHUNK), lambda s: (old(s) // npb, old(s) % npb, 0, 0)),
              jax.ShapeDtypeStruct((B, S // CHUNK, 2 * GROUP_W, CHUNK), BF16))
    outs = [fm(2 * GROUP_W, BF16), qk_out, fm(2 * GROUP_W, BF16), fm(4 * N_HEADS, F32),
            fm(GROUP_W, BF16), fm(KV_W, BF16), tk(KV_W, BF16), tk(GROUP_W, BF16)]
    return pl.pallas_call(
        kern,
        grid=(n_tiles + 2,),
        in_specs=x_specs + [
            pl.BlockSpec((None, 6, D), lambda s: (jnp.where(cur(s) % npb == 0, n_batch_rows, cur(s) // npb), 0, 0)),
            pl.BlockSpec((1, D), c2),
            pl.BlockSpec((R_END, D), c2, pipeline_mode=pl.Buffered(1)),
            pl.BlockSpec((D, T_END), c2, pipeline_mode=pl.Buffered(1)),
            pl.BlockSpec((tm, KV_W), lambda s: (prv(s) % npb, 0)),
            pl.BlockSpec((tm, KV_W), lambda s: (prv(s) % npb, 0)),
            pl.BlockSpec((HEAD_DIM, tm), lambda s: (0, prv(s) % npb)),
            pl.BlockSpec((HEAD_DIM, tm), lambda s: (0, prv(s) % npb)),
            pl.BlockSpec((3, 2 * GROUP_W, tm), lambda s: (0, 0, 0)),
            pl.BlockSpec((GROUP_W, GROUP_W), c2),
            pl.BlockSpec((1, GROUP_W), c2),
            pl.BlockSpec((tm, GROUP_W), lambda s: (prv(s) % npb, 0)),
        ],
        out_specs=[o[0] for o in outs],
        out_shape=[o[1] for o in outs],
        scratch_shapes=[pltpu.VMEM((tm, D), BF16), pltpu.VMEM((2 * HALO, D), BF16),
                        pltpu.VMEM((2 * GROUP_W, tm), F32), pltpu.VMEM((2 * GROUP_W, LANES), F32)],
        compiler_params=pltpu.CompilerParams(dimension_semantics=("arbitrary",), vmem_limit_bytes=VMEM_LIMIT),
        name="in_proj",
    )(*x_args, mods, g0, wt, wk, cos_k, sin_k, cos_t, sin_t, conv_b, pool_w_bd, pool_scale, inv_cnt)


def _log_sigmoid(x):
    return jnp.minimum(x, 0.0) - jnp.log1p(jnp.exp(-jnp.abs(x)))


def _chunk_spec(n_b, rows, order=lambda c: c, row_block=0):
    return pl.BlockSpec((n_b, None, rows, CHUNK), lambda c: (0, order(c), row_block, 0))


def _mixers_kernel(zs_ref, sw_ref, sb_ref,
                   q_ref, kp_ref, kc_ref, kn_ref, kx_ref, vp_ref, vc_ref, vn_ref, vx_ref, sink_ref, mask_ref,
                   qkf_ref, vf_ref, gf_ref, qkb_ref, vb_ref, gb_ref, gbias_ref,
                   a_ref, c_ref, hf_ref, hb_ref, st_ref, m_ref, *, n_ctx_chunks):
    n_b = zs_ref.shape[0]

    @pl.when(pl.program_id(0) == 0)
    def _():
        st_ref[...] = jnp.zeros_like(st_ref)
        m_ref[...] = jnp.zeros_like(m_ref)

    group = N_HEADS // KV_HEADS
    zeros = jnp.zeros((HEAD_DIM, group * CHUNK), BF16)
    pairs = [(b, kv) for b in range(n_b) for kv in range(KV_HEADS)]
    scores = []
    for b, kv in pairs:
        kcat = jnp.concatenate([kp_ref[b], kc_ref[b], kn_ref[b], kx_ref[b]], axis=0)
        q2 = jnp.concatenate([q_ref[b, (kv * group + g) * HEAD_DIM:(kv * group + g + 1) * HEAD_DIM, :]
                              for g in range(group)], axis=1)
        qm = jnp.concatenate([q2, zeros] if kv == 0 else [zeros, q2], axis=0)
        scores.append(_dot(kcat, qm))

    s_i = lax.broadcasted_iota(jnp.int32, (CHUNK, CHUNK), 0)
    t_i = lax.broadcasted_iota(jnp.int32, (CHUNK, CHUNK), 1)
    causal = {True: s_i <= t_i, False: s_i >= t_i}
    ones_row = jnp.where(lax.broadcasted_iota(jnp.int32, (ST_ROWS - HEAD_DIM, CHUNK), 0) == 0, 1.0, 0.0).astype(BF16)
    all_ones = jnp.ones((CHUNK, CHUNK), F32)
    chains = []
    for b in range(n_b):
        for forward, qk_ref, v_ref, g_ref, o_ref in ((True, qkf_ref, vf_ref, gf_ref, hf_ref),
                                                      (False, qkb_ref, vb_ref, gb_ref, hb_ref)):
            gates = g_ref[b] + gbias_ref[...]
            li8 = gates[0:2 * N_HEADS] * LOG2E
            lf8 = _log_sigmoid(gates[2 * N_HEADS:4 * N_HEADS]) * LOG2E
            cum8 = _dot_hi(lf8, jnp.where(causal[forward], 1.0, 0.0).astype(F32))
            tot8 = _dot_hi(lf8, all_ones)
            a8 = li8 - cum8
            for h in range(N_HEADS):
                r = (0 if forward else N_HEADS) + h
                chains.append(dict(
                    b=b, h=h, mask=causal[forward], o_ref=o_ref, slot=(b * 2 + (0 if forward else 1)) * N_HEADS + h,
                    a=a8[r:r + 1], bcum=cum8[r:r + 1], b_end=tot8[r:r + 1],
                    q=qk_ref[b, h * HEAD_DIM:(h + 1) * HEAD_DIM, :],
                    k=qk_ref[b, GROUP_W + h * HEAD_DIM:GROUP_W + (h + 1) * HEAD_DIM, :],
                    v=v_ref[b, h * HEAD_DIM:(h + 1) * HEAD_DIM, :]))
    for ch in chains:
        ch["st"] = st_ref[ch["slot"]]
        ch["m"] = m_ref[ch["slot"]:ch["slot"] + 1, :]
        ch["kq"] = _dot_tn(ch["k"], ch["q"])
        ch["cq"] = _dot(ch["st"][:, 0:HEAD_DIM].astype(BF16), ch["q"])

    for b in range(n_b):
        g = jax.nn.gelu(zs_ref[b].astype(F32))
        for h in range(N_HEADS):
            u = g[h * HEAD_DIM:(h + 1) * HEAD_DIM]
            vh = _head_layer_norm_t(g[GROUP_W + h * HEAD_DIM:GROUP_W + (h + 1) * HEAD_DIM]).astype(BF16)
            mixed = _dot(vh, sw_ref[h]) + sb_ref[h:h + 1, :]
            a_ref[b, h * HEAD_DIM:(h + 1) * HEAD_DIM, :] = (u * mixed).astype(BF16)

    probs = []
    for (b, kv), s in zip(pairs, scores):
        s = s + mask_ref[...]
        sink = sink_ref[kv:kv + 1, :]
        m = jnp.maximum(jnp.max(s, axis=0, keepdims=True), sink)
        probs.append((jnp.exp2(s - m).astype(BF16), jnp.exp2(sink - m)))

    for ch in chains:
        a, m, mask = ch["a"], ch["m"], ch["mask"]
        a_col = jnp.broadcast_to(a, (CHUNK, CHUNK)).T
        a_max = jnp.max(a, axis=1, keepdims=True)
        big_m = jnp.maximum(jnp.max(jnp.where(mask, a_col, NEG), axis=0, keepdims=True), m)
        ch["inter"] = jnp.exp2(m - big_m)
        ch["floor"] = jnp.exp2(-(ch["bcum"] + big_m))
        s = ch["kq"] * jnp.exp2(jnp.where(mask, a_col - big_m, NEG))
        ch["rowsum"] = jnp.sum(s, axis=0, keepdims=True)
        ch["s"] = s.astype(BF16)
        m_new = jnp.maximum(ch["b_end"] + m, ch["b_end"] + a_max)
        ch["decay"] = jnp.exp2(ch["b_end"] + m - m_new)
        ch["kw"] = (ch["k"].astype(F32) * jnp.exp2(ch["b_end"] + a - m_new)).astype(BF16)
        ch["m_new"] = m_new

    ones_keys = jnp.where(lax.broadcasted_iota(jnp.int32, (BF16_ROWS, (3 + n_ctx_chunks) * CHUNK), 0) == 0,
                          1.0, 0.0).astype(BF16)
    for (b, kv), (p, p_sink) in zip(pairs, probs):
        vcat = jnp.concatenate([vp_ref[b], vc_ref[b], vn_ref[b]] + [vx_ref[b, i] for i in range(n_ctx_chunks)],
                               axis=1)
        o_ext = _dot(jnp.concatenate([vcat[kv * HEAD_DIM:(kv + 1) * HEAD_DIM, :], ones_keys], axis=0), p)
        o = o_ext[0:HEAD_DIM] / (o_ext[HEAD_DIM:HEAD_DIM + 1] + p_sink)
        for g in range(group):
            h = kv * group + g
            c_ref[b, h * HEAD_DIM:(h + 1) * HEAD_DIM, :] = o[:, g * CHUNK:(g + 1) * CHUNK].astype(BF16)

    for ch in chains:
        ch["vs"] = _dot(ch["v"], ch["s"])
        v_ext = jnp.concatenate([ch["v"], ones_row], axis=0)
        ch["upd"] = _dot_nt(v_ext, ch["kw"])
    for ch in chains:
        h, slot, cq, inter = ch["h"], ch["slot"], ch["cq"], ch["inter"]
        num = cq[0:HEAD_DIM] * inter + ch["vs"]
        den = inter * cq[HEAD_DIM:HEAD_DIM + 1] + ch["rowsum"]
        ch["o_ref"][ch["b"], h * HEAD_DIM:(h + 1) * HEAD_DIM, :] = num / jnp.maximum(jnp.abs(den), ch["floor"])
        st_ref[slot, :, 0:HEAD_DIM] = ch["st"][:, 0:HEAD_DIM] * ch["decay"][:, 0:HEAD_DIM] + ch["upd"]
        m_ref[slot:slot + 1, :] = ch["m_new"]


def _attention_masks(n_ctx_chunks):
    group = N_HEADS // KV_HEADS
    n_keys = (3 + n_ctx_chunks) * CHUNK
    j = jnp.arange(n_keys)[:, None]
    i = (jnp.arange(group * CHUNK) & (CHUNK - 1))[None, :]
    masks = []
    for cur_ok, prev_ok, next_ok in ((False, False, False), (True, True, True), (True, False, True),
                                     (True, True, False), (True, False, False)):
        lo = (0 if prev_ok else CHUNK) if cur_ok else 3 * CHUNK
        hi = 3 * CHUNK if next_ok else 2 * CHUNK
        valid = ((j >= i) & (j <= i + 2 * CHUNK) & (j >= lo) & (j < hi)) | (j >= 3 * CHUNK)
        masks.append(jnp.where(valid, 0.0, NEG).astype(F32))
    return jnp.stack(masks)


def _mixers(zs, sgu_wt, sgu_b, aq, ak, av, sink_rows, masks, qk, vo, gt, gate_bias, n_ctx_chunks):
    B, n_chunks = zs.shape[:2]
    ctx_w = n_ctx_chunks * CHUNK
    group = N_HEADS // KV_HEADS
    n_keys = 3 * CHUNK + ctx_w
    kern = functools.partial(_mixers_kernel, n_ctx_chunks=n_ctx_chunks)
    ident = lambda c: c
    prev = lambda c: jnp.maximum(c - 1, 0)
    nxt = lambda c: jnp.minimum(c + 1, n_chunks - 1)
    bwd = lambda c: jnp.where(c < n_ctx_chunks, n_ctx_chunks - 1 - c, n_chunks - 1 + n_ctx_chunks - c)
    kspec = lambda f: pl.BlockSpec((B, CHUNK, KV_W), lambda c: (0, f(c), 0))
    vspec = lambda f: _chunk_spec(B, KV_W, f)

    def mask_kind(c):
        no_prev = (c == n_ctx_chunks).astype(jnp.int32)
        no_next = (c == n_chunks - 1).astype(jnp.int32)
        return jnp.where(c < n_ctx_chunks, 0, 1 + no_prev + 2 * no_next)

    def scan_specs(order):
        return [_chunk_spec(B, 2 * GROUP_W, order), _chunk_spec(B, GROUP_W, order),
                _chunk_spec(B, 4 * N_HEADS, order)]

    n_slots = B * 2 * N_HEADS
    return pl.pallas_call(
        kern,
        grid=(n_chunks,),
        in_specs=[
            _chunk_spec(B, 2 * GROUP_W),
            pl.BlockSpec((N_HEADS, CHUNK, CHUNK), lambda c: (0, 0, 0)),
            pl.BlockSpec((N_HEADS, CHUNK), lambda c: (0, 0)),
            _chunk_spec(B, GROUP_W),
            kspec(prev), kspec(ident), kspec(nxt), pl.BlockSpec((B, ctx_w, KV_W), lambda c: (0, 0, 0)),
            vspec(prev), vspec(ident), vspec(nxt),
            pl.BlockSpec((B, n_ctx_chunks, KV_W, CHUNK), lambda c: (0, 0, 0, 0)),
            pl.BlockSpec((SUBLANES, group * CHUNK), lambda c: (0, 0)),
            pl.BlockSpec((None, n_keys, group * CHUNK), lambda c: (mask_kind(c), 0, 0)),
        ] + scan_specs(ident) + scan_specs(bwd) + [pl.BlockSpec((4 * N_HEADS, CHUNK), lambda c: (0, 0))],
        out_specs=[_chunk_spec(B, GROUP_W), _chunk_spec(B, GROUP_W),
                   _chunk_spec(B, GROUP_W, ident), _chunk_spec(B, GROUP_W, bwd)],
        out_shape=[jax.ShapeDtypeStruct((B, n_chunks, GROUP_W, CHUNK), BF16)] * 2
                  + [jax.ShapeDtypeStruct((B, n_chunks, GROUP_W, CHUNK), F32)] * 2,
        scratch_shapes=[
            pltpu.VMEM((n_slots, ST_ROWS, LANES), F32),
            pltpu.VMEM((n_slots, LANES), F32),
        ],
        compiler_params=pltpu.CompilerParams(dimension_semantics=("arbitrary",)),
        name="mixers",
    )(zs, sgu_wt, sgu_b, aq, ak, ak, ak, ak, av, av, av, av, sink_rows, masks, qk, vo, gt, qk, vo, gt, gate_bias)


def _hidden_chunks(hidden):
    step = 3 * 2 * LANES
    return [(lo, min(lo + step, hidden)) for lo in range(0, hidden, step)]


def _out_kernel(*refs, hidden, npb, n_tiles, split_ctx):
    ctx_ref = refs[0] if split_ctx else None
    (x_ref, a_ref, hf_ref, hb_ref, zo_ref, c_ref, d_ref, moda_ref, modc_ref, g_ref, mg_ref,
     wo_ref, wi_ref, wf_ref, o_ref, x1_ref, x1b_ref, tb_ref, f_ref, act_ref, bm_ref) = refs[1:] if split_ctx else refs
    @pl.when(pl.program_id(0) == 0)
    def _():
        x1_ref[...] = jnp.zeros_like(x1_ref)
        x1b_ref[...] = jnp.zeros_like(x1b_ref)
        tb_ref[...] = jnp.zeros_like(tb_ref)
        f_ref[...] = jnp.zeros_like(f_ref)

    tb = tb_ref[...]
    chunks = _hidden_chunks(hidden)

    def ffn_in(lo, hi):
        return _dot(tb, wi_ref[:, lo:hi]), _dot(tb, wi_ref[:, hidden + lo:hidden + hi])

    def act(gate_up, lo, hi):
        gate, up = gate_up
        act_ref[:, lo:hi] = (_silu(gate) * up).astype(BF16)

    def ffn_out(lo, hi):
        return _dot(act_ref[:, lo:hi], wf_ref[lo:hi, :])

    gu0 = ffn_in(*chunks[0])
    o_ref[...] = x1b_ref[...] + modc_ref[5:6, :] * _rms(f_ref[...], g_ref[3:4, :])
    gu1 = ffn_in(*chunks[1])
    hs = _get_chunks(hf_ref) + _get_chunks(hb_ref)
    hn = jnp.concatenate([_head_layer_norm_t(hs[h * HEAD_DIM:(h + 1) * HEAD_DIM]) for h in range(N_HEADS)], axis=0)
    bm_ref[...] = (hn * mg_ref[...] * jax.nn.sigmoid(_get_chunks(zo_ref).astype(F32))).astype(BF16)
    act(gu0, *chunks[0])
    gu2 = ffn_in(*chunks[2])
    act(gu1, *chunks[1])
    y = (_dot_tn(_get_chunks(a_ref), wo_ref[0:GROUP_W, :])
         + _dot_tn(bm_ref[...], wo_ref[GROUP_W:2 * GROUP_W, :])
         + _dot_tn(_get_chunks(c_ref), wo_ref[2 * GROUP_W:3 * GROUP_W, :])
         + _dot(d_ref[...], wo_ref[3 * GROUP_W:4 * GROUP_W, :]))
    rest = [ffn_in(*ch) for ch in chunks[3:]]
    act(gu2, *chunks[2])
    f = ffn_out(*chunks[0])
    x1b_ref[...] = x1_ref[...]
    x_tile = (_first_tile_from_ctx(ctx_ref, x_ref, jnp.minimum(pl.program_id(0), n_tiles - 1) % npb)
              if split_ctx else x_ref[...])
    x1 = x_tile + moda_ref[2:3, :] * _rms(y, g_ref[1:2, :])
    t = _rms(x1, g_ref[2:3, :] * (1.0 + moda_ref[4:5, :])) + moda_ref[3:4, :]
    x1_ref[...] = x1
    tb_ref[...] = t.astype(BF16)
    f = f + ffn_out(*chunks[1]) + ffn_out(*chunks[2])
    for gate_up, ch in zip(rest, chunks[3:]):
        act(gate_up, *ch)
        f = f + ffn_out(*ch)
    f_ref[...] = f


def _out_ffn(xs, a, hf, hb, vo, c, d, mods, norm_g, mnorm_g, w_out, w_ffn_in, w_ffn_out, n_batch_rows, drop_ctx):
    B, S, D = _stream_shape(xs)
    tm = ROW_TILE
    npb = S // tm
    n_tiles = B * npb
    hidden = w_ffn_out.shape[0]
    assert len(_hidden_chunks(hidden)) >= 3
    cur = lambda s: jnp.minimum(s, n_tiles - 1)
    prv = lambda s: jnp.maximum(s - 2, 0)
    row = lambda s: (cur(s) // npb, cur(s) % npb, 0)
    mod_row = lambda t: (jnp.where(t % npb == 0, n_batch_rows, t // npb), 0, 0)
    c2 = lambda s: (0, 0)
    cpt = tm // CHUNK
    fm = pl.BlockSpec((None, cpt, GROUP_W, CHUNK), lambda s: (cur(s) // npb, cur(s) % npb, 0, 0))
    if drop_ctx:
        out_spec = pl.BlockSpec((None, tm, D), lambda s: (prv(s) // npb, jnp.maximum(prv(s) % npb - 1, 0), 0))
        out_shape = jax.ShapeDtypeStruct((B, S - tm, D), F32)
    else:
        out_spec = pl.BlockSpec((None, tm, D), lambda s: (prv(s) // npb, prv(s) % npb, 0))
        out_shape = jax.ShapeDtypeStruct((B, S, D), F32)
    kern = functools.partial(_out_kernel, hidden=hidden, npb=npb, n_tiles=n_tiles, split_ctx=isinstance(xs, tuple))
    x_specs, x_args = _stream_specs(xs, tm, cur, npb, with_halo=False)
    return pl.pallas_call(
        kern,
        grid=(n_tiles + 2,),
        in_specs=x_specs + [
            fm, fm, fm,
            pl.BlockSpec((None, cpt, GROUP_W, CHUNK),
                         lambda s: (cur(s) // npb, cur(s) % npb, 1, 0)),
            fm,
            pl.BlockSpec((None, tm, GROUP_W), row),
            pl.BlockSpec((None, 6, D), lambda s: mod_row(cur(s))),
            pl.BlockSpec((None, 6, D), lambda s: mod_row(prv(s))),
            pl.BlockSpec((4, D), c2),
            pl.BlockSpec((GROUP_W, tm), c2),
            pl.BlockSpec((D, D), c2, pipeline_mode=pl.Buffered(1)),
            pl.BlockSpec((D, 2 * hidden), c2, pipeline_mode=pl.Buffered(1)),
            pl.BlockSpec((hidden, D), c2, pipeline_mode=pl.Buffered(1)),
        ],
        out_specs=out_spec,
        out_shape=out_shape,
        scratch_shapes=[pltpu.VMEM((tm, D), F32), pltpu.VMEM((tm, D), F32), pltpu.VMEM((tm, D), BF16),
                        pltpu.VMEM((tm, D), F32), pltpu.VMEM((tm, hidden), BF16), pltpu.VMEM((GROUP_W, tm), BF16)],
        compiler_params=pltpu.CompilerParams(dimension_semantics=("arbitrary",), vmem_limit_bytes=VMEM_LIMIT),
        name="out_ffn",
    )(*x_args, a, hf, hb, vo, c, d, mods, mods, norm_g, mnorm_g, w_out, w_ffn_in, w_ffn_out)


def _rope_tables(n_tokens, ctx_len):
    rows = n_tokens // GRID_W
    axis_freq = HEAD_DIM // 4
    inv = jnp.power(ROPE_BASE, -jnp.arange(axis_freq, dtype=F32) * 2.0 / (2 * axis_freq))
    ar = jnp.arange(rows, dtype=F32)[:, None] * inv
    ac = jnp.arange(GRID_W, dtype=F32)[:, None] * inv

    def per_token(fn):
        r = jnp.broadcast_to(fn(ar)[:, None, :], (rows, GRID_W, axis_freq))
        c = jnp.broadcast_to(fn(ac)[None, :, :], (rows, GRID_W, axis_freq))
        return jnp.concatenate([r, r, c, c], axis=-1).reshape(n_tokens, HEAD_DIM)

    cos, sin = per_token(jnp.cos), per_token(jnp.sin)
    sign = jnp.where((jnp.arange(HEAD_DIM) % 32) < 16, -1.0, 1.0).astype(F32)
    cos = jnp.concatenate([jnp.ones((ctx_len, HEAD_DIM), F32), cos], axis=0)
    sin = jnp.concatenate([jnp.zeros((ctx_len, HEAD_DIM), F32), sin * sign], axis=0)
    inv_cnt = []
    for seg in (ctx_len, n_tokens):
        t = jnp.arange(seg)[:, None]
        half = jnp.asarray([w // 2 for w in POOL_WINDOWS])[None, :]
        cnt = jnp.minimum(t + half, seg) - jnp.maximum(t - half, 0)
        inv_cnt.append(jnp.repeat(1.0 / cnt.astype(F32), HEAD_DIM, axis=1))
    return (jnp.tile(cos, (1, KV_HEADS)), jnp.tile(sin, (1, KV_HEADS)), cos.T, sin.T,
            jnp.concatenate(inv_cnt, axis=0))


def _arrange_w_in(w_in):
    ml0 = 2 * GROUP_W
    gate0 = ml0 + 4 * GROUP_W
    att0 = gate0 + 4 * N_HEADS
    pool0 = att0 + GROUP_W + 2 * KV_W
    gates = w_in[:, gate0:att0].reshape(-1, 2, 2, N_HEADS)
    gates = gates.transpose(0, 2, 1, 3).reshape(-1, 4 * N_HEADS)
    feat = jnp.concatenate([
        w_in[:, 0:ml0],
        w_in[:, ml0:ml0 + 2 * GROUP_W],
        w_in[:, ml0 + 2 * GROUP_W:gate0],
        w_in[:, att0:att0 + GROUP_W],
        w_in[:, att0 + GROUP_W + KV_W:pool0],
        gates,
    ], axis=1)
    tok = jnp.concatenate([w_in[:, att0 + GROUP_W:att0 + GROUP_W + KV_W], w_in[:, pool0:pool0 + GROUP_W]], axis=1)
    return feat.T.astype(BF16), tok.astype(BF16)


def _block_diag(w):
    g = w.shape[0]
    eye = jnp.eye(g, dtype=w.dtype)
    return (eye[:, None, :, None] * w[:, :, None, :]).reshape(g * w.shape[1], g * w.shape[2])


def kernel(x, c, ctx, c_ctx, w_mod, b_mod, norm_g, w_in, w_out, sgu_w, sgu_b, mlstm_conv_w, mlstm_gate_b,
           mlstm_norm_g, attn_sink, pool_w, pool_scale, w_ffn_in, w_ffn_out):
    B, N, D = x.shape
    ctx_len = ctx.shape[1]
    depth = w_mod.shape[0]
    assert D == 4 * GROUP_W and N % ROW_TILE == 0 and ctx_len == ROW_TILE
    assert w_in.shape[2] == 2 * GROUP_W + 4 * GROUP_W + 4 * N_HEADS + GROUP_W + 2 * KV_W + GROUP_W
    n_ctx_chunks = ctx_len // CHUNK

    xs = (ctx, x)
    rows = -(-(B + 1) // SUBLANES) * SUBLANES
    cc = jnp.zeros((rows, D), F32).at[0:B].set(c).at[B].set(c_ctx)
    mods = _modulation(cc, w_mod, b_mod).reshape(depth, rows, 6, D)
    tables = _rope_tables(N, ctx_len)
    masks = _attention_masks(n_ctx_chunks)

    for l in range(depth):
        wt, wk = _arrange_w_in(w_in[l])
        conv_b = jnp.broadcast_to(mlstm_conv_w[l][:, :, None], (3, 2 * GROUP_W, ROW_TILE))
        gate_bias = jnp.broadcast_to(
            mlstm_gate_b[l].reshape(2, 2, N_HEADS).transpose(1, 0, 2).reshape(4 * N_HEADS, 1), (4 * N_HEADS, CHUNK))
        sink_rows = jnp.zeros((SUBLANES, (N_HEADS // KV_HEADS) * CHUNK), F32).at[0:KV_HEADS].set(
            jnp.repeat(attn_sink[l].reshape(KV_HEADS, N_HEADS // KV_HEADS) * LOG2E, CHUNK, axis=1))
        mnorm_b = jnp.broadcast_to(mlstm_norm_g[l][:, None], (GROUP_W, ROW_TILE))

        zs, qk, vo, gt, aq, av, ak, dm = _in_proj(
            xs, mods[l], norm_g[l, 0:1], wt, wk, tables, conv_b, _block_diag(pool_w[l]).astype(BF16),
            pool_scale[l].reshape(1, GROUP_W), B)
        a, cm, hf, hb = _mixers(zs, sgu_w[l].transpose(0, 2, 1).astype(BF16), sgu_b[l], aq, ak, av, sink_rows, masks,
                                qk, vo, gt, gate_bias, n_ctx_chunks)
        xs = _out_ffn(xs, a, hf, hb, vo, cm, dm, mods[l], norm_g[l], mnorm_b,
                      w_out[l].astype(BF16), w_ffn_in[l].astype(BF16), w_ffn_out[l].astype(BF16), B,
                      drop_ctx=(l == depth - 1))
    return xs
```

```python
import functools

import jax
import jax.numpy as jnp
from jax import lax
from jax.experimental import pallas as pl
from jax.experimental.pallas import tpu as pltpu

F32 = jnp.float32
BF16 = jnp.bfloat16

GRID_W = 64
ROPE_BASE = 10000.0
EPS = 1e-6
LANES = 128
SUBLANES = 8
BF16_ROWS = 16
CHUNK = 128
HEAD_DIM = 64
N_HEADS = 4
KV_HEADS = 2
GROUP_W = N_HEADS * HEAD_DIM
KV_W = KV_HEADS * HEAD_DIM
POOL_WINDOWS = (2, 4, 8, 16)
HALO = 8
ROW_TILE = 256
NEG = -1e30
VMEM_LIMIT = 56 * 1024 * 1024

R_SGU = 0
R_QK = 512
R_VO = 1024
R_AQ = 1536
R_AV = 1792
R_GATE = 1920
R_END = 1936
T_AK = 0
T_POOL = KV_W
T_END = KV_W + GROUP_W

ST_ROWS = 80
LOG2E = 1.4426950408889634

HI = lax.Precision.HIGHEST


def _dot(a, b):
    return jnp.dot(a, b, preferred_element_type=F32)


def _dot_hi(a, b):
    return jnp.dot(a, b, preferred_element_type=F32, precision=HI)


def _dot_nt(a, b):
    return lax.dot_general(a, b, (((1,), (1,)), ((), ())), preferred_element_type=F32)


def _dot_tn(a, b):
    return lax.dot_general(a, b, (((0,), (0,)), ((), ())), preferred_element_type=F32)


def _rms(x, g):
    return x * lax.rsqrt(jnp.mean(x * x, axis=-1, keepdims=True) + EPS) * g


def _silu(x):
    return x * jax.nn.sigmoid(x)


def _put_chunks(ref, x):
    for i in range(ref.shape[0]):
        ref[i] = x[:, i * CHUNK:(i + 1) * CHUNK].astype(ref.dtype)


def _get_chunks(ref):
    return jnp.concatenate([ref[i] for i in range(ref.shape[0])], axis=1)


def _head_layer_norm_t(v):
    mu = jnp.mean(v, axis=0, keepdims=True)
    vc = v - mu
    return vc * lax.rsqrt(jnp.mean(vc * vc, axis=0, keepdims=True) + EPS)


def _zero_at_first_step(refs):
    @pl.when(pl.program_id(0) == 0)
    def _():
        for ref in refs:
            ref[...] = jnp.zeros_like(ref)


def _mod_kernel(c_ref, w_ref, b_ref, o_ref):
    o_ref[...] = _dot(_silu(c_ref[...]).astype(BF16), w_ref[...].astype(BF16)) + b_ref[...]


def _modulation(cc, w_mod, b_mod):
    depth, d, n6 = w_mod.shape
    rows = cc.shape[0]
    tn = 1536
    return pl.pallas_call(
        _mod_kernel,
        grid=(depth, n6 // tn),
        in_specs=[
            pl.BlockSpec((rows, d), lambda l, j: (0, 0)),
            pl.BlockSpec((None, d, tn), lambda l, j: (l, 0, j)),
            pl.BlockSpec((None, 1, tn), lambda l, j: (l, 0, j)),
        ],
        out_specs=pl.BlockSpec((None, rows, tn), lambda l, j: (l, 0, j)),
        out_shape=jax.ShapeDtypeStruct((depth, rows, n6), F32),
        compiler_params=pltpu.CompilerParams(
            dimension_semantics=("arbitrary", "arbitrary"), vmem_limit_bytes=VMEM_LIMIT),
        name="modulation",
    )(cc, w_mod, b_mod.reshape(depth, 1, n6))


N_IN_SCRATCH = 4


def _first_tile_from_ctx(ctx_ref, x_ref, tile):
    rows = lax.broadcasted_iota(jnp.int32, x_ref.shape, 0)
    return jnp.where(rows < jnp.where(tile == 0, x_ref.shape[0], 0), ctx_ref[...], x_ref[...])


def _project_tile(hb, has_prev, has_next, wt_ref, wk_ref, cos_ref, sin_ref, cost_ref, sint_ref,
                  conv_ref, pw_ref, ps_ref, invc_ref,
                  zs_ref, qk_ref, vo_ref, gt_ref, aq_ref, av_ref, ak_ref, d_ref, z_ref, pcol_ref, zp_ref, prow_ref):
    tm = hb.shape[0]
    zc = _dot_nt(wt_ref[R_QK:R_VO, :], hb)
    _put_chunks(zs_ref, _dot_nt(wt_ref[R_SGU:R_QK, :], hb))

    z = z_ref[...]
    lane = lax.broadcasted_iota(jnp.int32, z.shape, 1)
    before = jnp.where(lane == 0, pcol_ref[:, 0:1] * has_prev, pltpu.roll(z, 1, axis=1))
    after = jnp.where(lane == tm - 1, zc[:, 0:1] * has_next, pltpu.roll(z, tm - 1, axis=1))
    qk = _silu(before * conv_ref[0] + z * conv_ref[1] + after * conv_ref[2])
    _put_chunks(qk_ref, jnp.concatenate([qk[0:GROUP_W], qk[GROUP_W:2 * GROUP_W] * (HEAD_DIM ** -0.5)], axis=0))
    pcol_ref[...] = jnp.broadcast_to(z[:, tm - 1:tm], pcol_ref.shape)
    z_ref[...] = zc

    zp_new = _dot(hb, wk_ref[:, T_POOL:T_END])
    _put_chunks(vo_ref, _dot_nt(wt_ref[R_VO:R_AQ, :], hb))
    zp = zp_ref[...]
    rows = tm + 2 * HALO
    ext = jnp.concatenate([prow_ref[...] * has_prev, zp, zp_new[0:HALO] * has_next], axis=0)
    s2 = ext + pltpu.roll(ext, 1, axis=0)
    s4 = pltpu.roll(s2, 1, axis=0) + pltpu.roll(s2, rows - 1, axis=0)
    s8 = pltpu.roll(s4, 2, axis=0) + pltpu.roll(s4, rows - 2, axis=0)
    s16 = pltpu.roll(s8, 4, axis=0) + pltpu.roll(s8, rows - 4, axis=0)
    sums = [s[HALO:HALO + tm] for s in (s2, s4, s8, s16)]
    plane = lax.broadcasted_iota(jnp.int32, zp.shape, 1)
    pooled = sums[-1]
    for gi in range(len(POOL_WINDOWS) - 2, -1, -1):
        pooled = jnp.where(plane < (gi + 1) * HEAD_DIM, sums[gi], pooled)
    pooled = pooled * invc_ref[...]
    d_ref[...] = (_dot((pooled - zp).astype(BF16), pw_ref[...]) * ps_ref[...]).astype(BF16)
    prow_ref[...] = zp[tm - HALO:tm]
    zp_ref[...] = zp_new

    q = _dot_nt(wt_ref[R_AQ:R_AV, :], hb)
    k = _dot(hb, wk_ref[:, T_AK:T_POOL])
    _put_chunks(av_ref, _dot_nt(wt_ref[R_AV:R_GATE, :], hb))
    _put_chunks(gt_ref, _dot_nt(wt_ref[R_GATE:R_END, :], hb))
    cos_t = jnp.concatenate([cost_ref[...]] * N_HEADS, axis=0)
    sin_t = jnp.concatenate([sint_ref[...]] * N_HEADS, axis=0)
    row = lax.broadcasted_iota(jnp.int32, q.shape, 0)
    rot_q = jnp.where((row & 31) < 16, pltpu.roll(q, GROUP_W - 16, axis=0), pltpu.roll(q, 16, axis=0))
    _put_chunks(aq_ref, q * cos_t + rot_q * sin_t)
    klane = lax.broadcasted_iota(jnp.int32, k.shape, 1)
    rot_k = jnp.where((klane & 31) < 16, pltpu.roll(k, KV_W - 16, axis=1), pltpu.roll(k, 16, axis=1))
    ak_ref[...] = (k * cos_ref[...] + rot_k * sin_ref[...]).astype(BF16)


def _segment_flags(tile, npb):
    j = tile % npb
    return (j >= 2).astype(F32), jnp.logical_and(j >= 1, j < npb - 1).astype(F32)


def _in_kernel(*refs, npb, n_tiles, split_ctx):
    ctx_ref = refs[0] if split_ctx else None
    x_ref, mod_ref, g_ref = (refs[1:] if split_ctx else refs)[0:3]
    project_refs = (refs[1:] if split_ctx else refs)[3:-1]
    hb_ref = refs[-1]
    s = pl.program_id(0)

    _zero_at_first_step((hb_ref,) + tuple(project_refs[-N_IN_SCRATCH:]))
    has_prev, has_next = _segment_flags(jnp.maximum(s - 2, 0), npb)
    gain = g_ref[...] * (1.0 + mod_ref[1:2, :])
    x_tile = _first_tile_from_ctx(ctx_ref, x_ref, jnp.minimum(s, n_tiles - 1) % npb) if split_ctx else x_ref[...]
    hb_next = (_rms(x_tile, gain) + mod_ref[0:1, :]).astype(BF16)
    _project_tile(hb_ref[...], has_prev, has_next, *project_refs)
    hb_ref[...] = hb_next


def _stream_specs(xs, tm, tile_of_step, npb):
    bj = lambda s: (tile_of_step(s) // npb, tile_of_step(s) % npb)
    if isinstance(xs, tuple):
        ctx, x = xs
        D = x.shape[2]
        return [pl.BlockSpec((None, tm, D), lambda s: (bj(s)[0], 0, 0)),
                pl.BlockSpec((None, tm, D), lambda s: (bj(s)[0], jnp.maximum(bj(s)[1] - 1, 0), 0))], [ctx, x]
    return [pl.BlockSpec((None, tm, xs.shape[2]), lambda s: (bj(s)[0], bj(s)[1], 0))], [xs]


def _stream_shape(xs):
    if isinstance(xs, tuple):
        return xs[1].shape[0], xs[0].shape[1] + xs[1].shape[1], xs[1].shape[2]
    return xs.shape


def _project_call_parts(B, S, D, tm, npb, mid, old, proj):
    wt, wk, tables, conv_b, pool_w_bd, pool_scale = proj
    cos_k, sin_k, cos_t, sin_t, inv_cnt = tables
    c2 = lambda s: (0, 0)
    cpt = tm // CHUNK
    in_specs = [
        pl.BlockSpec((R_END, D), c2, pipeline_mode=pl.Buffered(1)),
        pl.BlockSpec((D, T_END), c2, pipeline_mode=pl.Buffered(1)),
        pl.BlockSpec((tm, KV_W), lambda s: (mid(s) % npb, 0)),
        pl.BlockSpec((tm, KV_W), lambda s: (mid(s) % npb, 0)),
        pl.BlockSpec((HEAD_DIM, tm), lambda s: (0, mid(s) % npb)),
        pl.BlockSpec((HEAD_DIM, tm), lambda s: (0, mid(s) % npb)),
        pl.BlockSpec((3, 2 * GROUP_W, tm), lambda s: (0, 0, 0)),
        pl.BlockSpec((GROUP_W, GROUP_W), c2),
        pl.BlockSpec((1, GROUP_W), c2),
        pl.BlockSpec((tm, GROUP_W), lambda s: (old(s) % npb, 0)),
    ]
    args = [wt, wk, cos_k, sin_k, cos_t, sin_t, conv_b, pool_w_bd, pool_scale, inv_cnt]

    def fm(rows, dt, tile):
        return (pl.BlockSpec((None, cpt, rows, CHUNK), lambda s: (tile(s) // npb, tile(s) % npb, 0, 0)),
                jax.ShapeDtypeStruct((B, S // CHUNK, rows, CHUNK), dt))

    def tk(cols, dt, tile):
        return (pl.BlockSpec((None, tm, cols), lambda s: (tile(s) // npb, tile(s) % npb, 0)),
                jax.ShapeDtypeStruct((B, S, cols), dt))

    outs = [fm(2 * GROUP_W, BF16, mid), fm(2 * GROUP_W, BF16, old), fm(2 * GROUP_W, BF16, mid),
            fm(4 * N_HEADS, F32, mid), fm(GROUP_W, BF16, mid), fm(KV_W, BF16, mid),
            tk(KV_W, BF16, mid), tk(GROUP_W, BF16, old)]
    scratch = [pltpu.VMEM((2 * GROUP_W, tm), F32), pltpu.VMEM((2 * GROUP_W, LANES), F32),
               pltpu.VMEM((tm, GROUP_W), F32), pltpu.VMEM((HALO, GROUP_W), F32)]
    return in_specs, args, [o[0] for o in outs], [o[1] for o in outs], scratch


def _in_proj(xs, mods, g0, proj, n_batch_rows):
    B, S, D = _stream_shape(xs)
    tm = ROW_TILE
    npb = S // tm
    n_tiles = B * npb
    cur = lambda s: jnp.minimum(s, n_tiles - 1)
    mid = lambda s: jnp.clip(s - 1, 0, n_tiles - 1)
    old = lambda s: jnp.maximum(s - 2, 0)
    kern = functools.partial(_in_kernel, npb=npb, n_tiles=n_tiles, split_ctx=isinstance(xs, tuple))
    x_specs, x_args = _stream_specs(xs, tm, cur, npb)
    p_specs, p_args, out_specs, out_shapes, p_scratch = _project_call_parts(B, S, D, tm, npb, mid, old, proj)
    return pl.pallas_call(
        kern,
        grid=(n_tiles + 2,),
        in_specs=x_specs + [
            pl.BlockSpec((None, 6, D), lambda s: (jnp.where(cur(s) % npb == 0, n_batch_rows, cur(s) // npb), 0, 0)),
            pl.BlockSpec((1, D), lambda s: (0, 0)),
        ] + p_specs,
        out_specs=out_specs,
        out_shape=out_shapes,
        scratch_shapes=p_scratch + [pltpu.VMEM((tm, D), BF16)],
        compiler_params=pltpu.CompilerParams(dimension_semantics=("arbitrary",), vmem_limit_bytes=VMEM_LIMIT),
        name="in_proj",
    )(*x_args, mods, g0, *p_args)


def _log_sigmoid(x):
    return jnp.minimum(x, 0.0) - jnp.log1p(jnp.exp(-jnp.abs(x)))


def _chunk_spec(n_b, rows, order=lambda c: c, row_block=0):
    return pl.BlockSpec((n_b, None, rows, CHUNK), lambda c: (0, order(c), row_block, 0))


def _mixers_kernel(zs_ref, sw_ref, sb_ref,
                   q_ref, kp_ref, kc_ref, kn_ref, kx_ref, vp_ref, vc_ref, vn_ref, vx_ref, sink_ref, mask_ref,
                   qkf_ref, vf_ref, gf_ref, qkb_ref, vb_ref, gb_ref, gbias_ref,
                   a_ref, c_ref, hf_ref, hb_ref, st_ref, m_ref, *, n_ctx_chunks):
    n_b = zs_ref.shape[0]
    _zero_at_first_step((st_ref, m_ref))

    group = N_HEADS // KV_HEADS
    zeros = jnp.zeros((HEAD_DIM, group * CHUNK), BF16)
    pairs = [(b, kv) for b in range(n_b) for kv in range(KV_HEADS)]
    scores = []
    for b, kv in pairs:
        kcat = jnp.concatenate([kp_ref[b], kc_ref[b], kn_ref[b], kx_ref[b]], axis=0)
        q2 = jnp.concatenate([q_ref[b, (kv * group + g) * HEAD_DIM:(kv * group + g + 1) * HEAD_DIM, :]
                              for g in range(group)], axis=1)
        qm = jnp.concatenate([q2, zeros] if kv == 0 else [zeros, q2], axis=0)
        scores.append(_dot(kcat, qm))

    s_i = lax.broadcasted_iota(jnp.int32, (CHUNK, CHUNK), 0)
    t_i = lax.broadcasted_iota(jnp.int32, (CHUNK, CHUNK), 1)
    causal = {True: s_i <= t_i, False: s_i >= t_i}
    ones_row = jnp.where(lax.broadcasted_iota(jnp.int32, (ST_ROWS - HEAD_DIM, CHUNK), 0) == 0, 1.0, 0.0).astype(BF16)
    all_ones = jnp.ones((CHUNK, CHUNK), F32)
    chains = []
    for b in range(n_b):
        for forward, qk_ref, v_ref, g_ref, o_ref in ((True, qkf_ref, vf_ref, gf_ref, hf_ref),
                                                      (False, qkb_ref, vb_ref, gb_ref, hb_ref)):
            gates = g_ref[b] + gbias_ref[...]
            li8 = gates[0:2 * N_HEADS] * LOG2E
            lf8 = _log_sigmoid(gates[2 * N_HEADS:4 * N_HEADS]) * LOG2E
            cum8 = _dot_hi(lf8, jnp.where(causal[forward], 1.0, 0.0).astype(F32))
            tot8 = _dot_hi(lf8, all_ones)
            a8 = li8 - cum8
            for h in range(N_HEADS):
                r = (0 if forward else N_HEADS) + h
                chains.append(dict(
                    b=b, h=h, mask=causal[forward], o_ref=o_ref, slot=(b * 2 + (0 if forward else 1)) * N_HEADS + h,
                    a=a8[r:r + 1], bcum=cum8[r:r + 1], b_end=tot8[r:r + 1],
                    q=qk_ref[b, h * HEAD_DIM:(h + 1) * HEAD_DIM, :],
                    k=qk_ref[b, GROUP_W + h * HEAD_DIM:GROUP_W + (h + 1) * HEAD_DIM, :],
                    v=v_ref[b, h * HEAD_DIM:(h + 1) * HEAD_DIM, :]))
    for ch in chains:
        ch["st"] = st_ref[ch["slot"]]
        ch["m"] = m_ref[ch["slot"]:ch["slot"] + 1, :]
        ch["kq"] = _dot_tn(ch["k"], ch["q"])
        ch["cq"] = _dot(ch["st"][:, 0:HEAD_DIM].astype(BF16), ch["q"])

    for b in range(n_b):
        g = jax.nn.gelu(zs_ref[b].astype(F32))
        for h in range(N_HEADS):
            u = g[h * HEAD_DIM:(h + 1) * HEAD_DIM]
            vh = _head_layer_norm_t(g[GROUP_W + h * HEAD_DIM:GROUP_W + (h + 1) * HEAD_DIM]).astype(BF16)
            mixed = _dot(vh, sw_ref[h]) + sb_ref[h:h + 1, :]
            a_ref[b, h * HEAD_DIM:(h + 1) * HEAD_DIM, :] = (u * mixed).astype(BF16)

    probs = []
    for (b, kv), s in zip(pairs, scores):
        s = s + mask_ref[...]
        sink = sink_ref[kv:kv + 1, :]
        m = jnp.maximum(jnp.max(s, axis=0, keepdims=True), sink)
        probs.append((jnp.exp2(s - m).astype(BF16), jnp.exp2(sink - m)))

    for ch in chains:
        a, m, mask = ch["a"], ch["m"], ch["mask"]
        a_col = jnp.broadcast_to(a, (CHUNK, CHUNK)).T
        a_max = jnp.max(a, axis=1, keepdims=True)
        big_m = jnp.maximum(jnp.max(jnp.where(mask, a_col, NEG), axis=0, keepdims=True), m)
        ch["inter"] = jnp.exp2(m - big_m)
        ch["floor"] = jnp.exp2(-(ch["bcum"] + big_m))
        s = ch["kq"] * jnp.exp2(jnp.where(mask, a_col - big_m, NEG))
        ch["rowsum"] = jnp.sum(s, axis=0, keepdims=True)
        ch["s"] = s.astype(BF16)
        m_new = jnp.maximum(ch["b_end"] + m, ch["b_end"] + a_max)
        ch["decay"] = jnp.exp2(ch["b_end"] + m - m_new)
        ch["kw"] = (ch["k"].astype(F32) * jnp.exp2(ch["b_end"] + a - m_new)).astype(BF16)
        ch["m_new"] = m_new

    ones_keys = jnp.where(lax.broadcasted_iota(jnp.int32, (BF16_ROWS, (3 + n_ctx_chunks) * CHUNK), 0) == 0,
                          1.0, 0.0).astype(BF16)
    for (b, kv), (p, p_sink) in zip(pairs, probs):
        vcat = jnp.concatenate([vp_ref[b], vc_ref[b], vn_ref[b]] + [vx_ref[b, i] for i in range(n_ctx_chunks)],
                               axis=1)
        o_ext = _dot(jnp.concatenate([vcat[kv * HEAD_DIM:(kv + 1) * HEAD_DIM, :], ones_keys], axis=0), p)
        o = o_ext[0:HEAD_DIM] / (o_ext[HEAD_DIM:HEAD_DIM + 1] + p_sink)
        for g in range(group):
            h = kv * group + g
            c_ref[b, h * HEAD_DIM:(h + 1) * HEAD_DIM, :] = o[:, g * CHUNK:(g + 1) * CHUNK].astype(BF16)

    for ch in chains:
        ch["vs"] = _dot(ch["v"], ch["s"])
        v_ext = jnp.concatenate([ch["v"], ones_row], axis=0)
        ch["upd"] = _dot_nt(v_ext, ch["kw"])
    for ch in chains:
        h, slot, cq, inter = ch["h"], ch["slot"], ch["cq"], ch["inter"]
        num = cq[0:HEAD_DIM] * inter + ch["vs"]
        den = inter * cq[HEAD_DIM:HEAD_DIM + 1] + ch["rowsum"]
        ch["o_ref"][ch["b"], h * HEAD_DIM:(h + 1) * HEAD_DIM, :] = num / jnp.maximum(jnp.abs(den), ch["floor"])
        st_ref[slot, :, 0:HEAD_DIM] = ch["st"][:, 0:HEAD_DIM] * ch["decay"][:, 0:HEAD_DIM] + ch["upd"]
        m_ref[slot:slot + 1, :] = ch["m_new"]


def _attention_masks(n_ctx_chunks):
    group = N_HEADS // KV_HEADS
    n_keys = (3 + n_ctx_chunks) * CHUNK
    j = jnp.arange(n_keys)[:, None]
    i = (jnp.arange(group * CHUNK) & (CHUNK - 1))[None, :]
    masks = []
    for cur_ok, prev_ok, next_ok in ((False, False, False), (True, True, True), (True, False, True),
                                     (True, True, False), (True, False, False)):
        lo = (0 if prev_ok else CHUNK) if cur_ok else 3 * CHUNK
        hi = 3 * CHUNK if next_ok else 2 * CHUNK
        valid = ((j >= i) & (j <= i + 2 * CHUNK) & (j >= lo) & (j < hi)) | (j >= 3 * CHUNK)
        masks.append(jnp.where(valid, 0.0, NEG).astype(F32))
    return jnp.stack(masks)


def _mixers(zs, sgu_wt, sgu_b, aq, ak, av, sink_rows, masks, qk, vo, gt, gate_bias, n_ctx_chunks):
    B, n_chunks = zs.shape[:2]
    ctx_w = n_ctx_chunks * CHUNK
    group = N_HEADS // KV_HEADS
    n_keys = 3 * CHUNK + ctx_w
    kern = functools.partial(_mixers_kernel, n_ctx_chunks=n_ctx_chunks)
    ident = lambda c: c
    prev = lambda c: jnp.maximum(c - 1, 0)
    nxt = lambda c: jnp.minimum(c + 1, n_chunks - 1)
    bwd = lambda c: jnp.where(c < n_ctx_chunks, n_ctx_chunks - 1 - c, n_chunks - 1 + n_ctx_chunks - c)
    kspec = lambda f: pl.BlockSpec((B, CHUNK, KV_W), lambda c: (0, f(c), 0))
    vspec = lambda f: _chunk_spec(B, KV_W, f)

    def mask_kind(c):
        no_prev = (c == n_ctx_chunks).astype(jnp.int32)
        no_next = (c == n_chunks - 1).astype(jnp.int32)
        return jnp.where(c < n_ctx_chunks, 0, 1 + no_prev + 2 * no_next)

    def scan_specs(order):
        return [_chunk_spec(B, 2 * GROUP_W, order), _chunk_spec(B, GROUP_W, order),
                _chunk_spec(B, 4 * N_HEADS, order)]

    n_slots = B * 2 * N_HEADS
    return pl.pallas_call(
        kern,
        grid=(n_chunks,),
        in_specs=[
            _chunk_spec(B, 2 * GROUP_W),
            pl.BlockSpec((N_HEADS, CHUNK, CHUNK), lambda c: (0, 0, 0)),
            pl.BlockSpec((N_HEADS, CHUNK), lambda c: (0, 0)),
            _chunk_spec(B, GROUP_W),
            kspec(prev), kspec(ident), kspec(nxt), pl.BlockSpec((B, ctx_w, KV_W), lambda c: (0, 0, 0)),
            vspec(prev), vspec(ident), vspec(nxt),
            pl.BlockSpec((B, n_ctx_chunks, KV_W, CHUNK), lambda c: (0, 0, 0, 0)),
            pl.BlockSpec((SUBLANES, group * CHUNK), lambda c: (0, 0)),
            pl.BlockSpec((None, n_keys, group * CHUNK), lambda c: (mask_kind(c), 0, 0)),
        ] + scan_specs(ident) + scan_specs(bwd) + [pl.BlockSpec((4 * N_HEADS, CHUNK), lambda c: (0, 0))],
        out_specs=[_chunk_spec(B, GROUP_W), _chunk_spec(B, GROUP_W),
                   _chunk_spec(B, GROUP_W, ident), _chunk_spec(B, GROUP_W, bwd)],
        out_shape=[jax.ShapeDtypeStruct((B, n_chunks, GROUP_W, CHUNK), BF16)] * 2
                  + [jax.ShapeDtypeStruct((B, n_chunks, GROUP_W, CHUNK), F32)] * 2,
        scratch_shapes=[
            pltpu.VMEM((n_slots, ST_ROWS, LANES), F32),
            pltpu.VMEM((n_slots, LANES), F32),
        ],
        compiler_params=pltpu.CompilerParams(dimension_semantics=("arbitrary",)),
        name="mixers",
    )(zs, sgu_wt, sgu_b, aq, ak, ak, ak, ak, av, av, av, av, sink_rows, masks, qk, vo, gt, qk, vo, gt, gate_bias)


N_OUT_SCRATCH = 6


def _hidden_chunks(hidden):
    step = 3 * 2 * LANES
    return [(lo, min(lo + step, hidden)) for lo in range(0, hidden, step)]


def _out_kernel(*refs, hidden, npb, n_tiles, split_ctx):
    _zero_at_first_step(refs[-N_OUT_SCRATCH:-2])
    ctx_ref = refs[0] if split_ctx else None
    (x_ref, a_ref, hf_ref, hb_ref, zo_ref, c_ref, d_ref, moda_ref, modc_ref, g_ref, mg_ref,
     wo_ref, wi_ref, wf_ref, o_ref, x1_ref, x1b_ref, tb_ref, f_ref, act_ref, bm_ref) = refs[1:] if split_ctx else refs
    tb = tb_ref[...]
    chunks = _hidden_chunks(hidden)

    def ffn_in(lo, hi):
        return _dot(tb, wi_ref[:, lo:hi]), _dot(tb, wi_ref[:, hidden + lo:hidden + hi])

    def act(gate_up, lo, hi):
        gate, up = gate_up
        act_ref[:, lo:hi] = (_silu(gate) * up).astype(BF16)

    def ffn_out(lo, hi):
        return _dot(act_ref[:, lo:hi], wf_ref[lo:hi, :])

    gu0 = ffn_in(*chunks[0])
    o_ref[...] = x1b_ref[...] + _rms(f_ref[...], modc_ref[5:6, :] * g_ref[3:4, :])
    gu1 = ffn_in(*chunks[1])
    hs = _get_chunks(hf_ref) + _get_chunks(hb_ref)
    hn = jnp.concatenate([_head_layer_norm_t(hs[h * HEAD_DIM:(h + 1) * HEAD_DIM]) for h in range(N_HEADS)], axis=0)
    bm_ref[...] = (hn * mg_ref[...] * jax.nn.sigmoid(_get_chunks(zo_ref).astype(F32))).astype(BF16)
    act(gu0, *chunks[0])
    gu2 = ffn_in(*chunks[2])
    act(gu1, *chunks[1])
    y = (_dot_tn(_get_chunks(a_ref), wo_ref[0:GROUP_W, :])
         + _dot_tn(bm_ref[...], wo_ref[GROUP_W:2 * GROUP_W, :])
         + _dot_tn(_get_chunks(c_ref), wo_ref[2 * GROUP_W:3 * GROUP_W, :])
         + _dot(d_ref[...], wo_ref[3 * GROUP_W:4 * GROUP_W, :]))
    rest = [ffn_in(*ch) for ch in chunks[3:]]
    act(gu2, *chunks[2])
    f = ffn_out(*chunks[0])
    x1b_ref[...] = x1_ref[...]
    x_tile = (_first_tile_from_ctx(ctx_ref, x_ref, jnp.minimum(pl.program_id(0), n_tiles - 1) % npb)
              if split_ctx else x_ref[...])
    x1 = x_tile + _rms(y, moda_ref[2:3, :] * g_ref[1:2, :])
    t = _rms(x1, g_ref[2:3, :] * (1.0 + moda_ref[4:5, :])) + moda_ref[3:4, :]
    x1_ref[...] = x1
    tb_ref[...] = t.astype(BF16)
    f = f + ffn_out(*chunks[1]) + ffn_out(*chunks[2])
    for gate_up, ch in zip(rest, chunks[3:]):
        act(gate_up, *ch)
        f = f + ffn_out(*ch)
    f_ref[...] = f


def _out_ffn(xs, a, hf, hb, vo, c, d, mods, norm_g, mnorm_g, w_out, w_ffn_in, w_ffn_out, n_batch_rows, drop_ctx):
    B, S, D = _stream_shape(xs)
    tm = ROW_TILE
    npb = S // tm
    n_tiles = B * npb
    hidden = w_ffn_out.shape[0]
    assert len(_hidden_chunks(hidden)) >= 3
    cur = lambda s: jnp.minimum(s, n_tiles - 1)
    prv = lambda s: jnp.clip(s - 2, 0, n_tiles - 1)
    row = lambda s: (cur(s) // npb, cur(s) % npb, 0)
    mod_row = lambda t: (jnp.where(t % npb == 0, n_batch_rows, t // npb), 0, 0)
    c2 = lambda s: (0, 0)
    cpt = tm // CHUNK
    fm = pl.BlockSpec((None, cpt, GROUP_W, CHUNK), lambda s: (cur(s) // npb, cur(s) % npb, 0, 0))
    split_ctx = isinstance(xs, tuple)
    x_specs, x_args = _stream_specs(xs, tm, cur, npb)
    in_specs = x_specs + [
        fm, fm, fm,
        pl.BlockSpec((None, cpt, GROUP_W, CHUNK), lambda s: (cur(s) // npb, cur(s) % npb, 1, 0)),
        fm,
        pl.BlockSpec((None, tm, GROUP_W), row),
        pl.BlockSpec((None, 6, D), lambda s: mod_row(cur(s))),
        pl.BlockSpec((None, 6, D), lambda s: mod_row(prv(s))),
        pl.BlockSpec((4, D), c2),
        pl.BlockSpec((GROUP_W, tm), c2),
        pl.BlockSpec((D, D), c2, pipeline_mode=pl.Buffered(1)),
        pl.BlockSpec((D, 2 * hidden), c2, pipeline_mode=pl.Buffered(1)),
        pl.BlockSpec((hidden, D), c2, pipeline_mode=pl.Buffered(1)),
    ]
    args = x_args + [a, hf, hb, vo, c, d, mods, mods, norm_g, mnorm_g, w_out, w_ffn_in, w_ffn_out]
    scratch = [pltpu.VMEM((tm, D), F32), pltpu.VMEM((tm, D), F32), pltpu.VMEM((tm, D), BF16),
               pltpu.VMEM((tm, D), F32), pltpu.VMEM((tm, hidden), BF16), pltpu.VMEM((GROUP_W, tm), BF16)]
    params = pltpu.CompilerParams(dimension_semantics=("arbitrary",), vmem_limit_bytes=VMEM_LIMIT)
    static = dict(hidden=hidden, npb=npb, n_tiles=n_tiles, split_ctx=split_ctx)
    if drop_ctx:
        out_spec = pl.BlockSpec((None, tm, D), lambda s: (prv(s) // npb, jnp.maximum(prv(s) % npb - 1, 0), 0))
        out_shape = jax.ShapeDtypeStruct((B, S - tm, D), F32)
    else:
        out_spec = pl.BlockSpec((None, tm, D), lambda s: (prv(s) // npb, prv(s) % npb, 0))
        out_shape = jax.ShapeDtypeStruct((B, S, D), F32)
    return pl.pallas_call(
        functools.partial(_out_kernel, **static),
        grid=(n_tiles + 2,),
        in_specs=in_specs,
        out_specs=out_spec,
        out_shape=out_shape,
        scratch_shapes=scratch, compiler_params=params, name="out_ffn",
    )(*args)


def _rope_tables(n_tokens, ctx_len):
    rows = n_tokens // GRID_W
    axis_freq = HEAD_DIM // 4
    inv = jnp.power(ROPE_BASE, -jnp.arange(axis_freq, dtype=F32) * 2.0 / (2 * axis_freq))
    ar = jnp.arange(rows, dtype=F32)[:, None] * inv
    ac = jnp.arange(GRID_W, dtype=F32)[:, None] * inv

    def per_token(fn):
        r = jnp.broadcast_to(fn(ar)[:, None, :], (rows, GRID_W, axis_freq))
        c = jnp.broadcast_to(fn(ac)[None, :, :], (rows, GRID_W, axis_freq))
        return jnp.concatenate([r, r, c, c], axis=-1).reshape(n_tokens, HEAD_DIM)

    cos, sin = per_token(jnp.cos), per_token(jnp.sin)
    sign = jnp.where((jnp.arange(HEAD_DIM) % 32) < 16, -1.0, 1.0).astype(F32)
    cos = jnp.concatenate([jnp.ones((ctx_len, HEAD_DIM), F32), cos], axis=0)
    sin = jnp.concatenate([jnp.zeros((ctx_len, HEAD_DIM), F32), sin * sign], axis=0)
    inv_cnt = []
    for seg in (ctx_len, n_tokens):
        t = jnp.arange(seg)[:, None]
        half = jnp.asarray([w // 2 for w in POOL_WINDOWS])[None, :]
        cnt = jnp.minimum(t + half, seg) - jnp.maximum(t - half, 0)
        inv_cnt.append(jnp.repeat(1.0 / cnt.astype(F32), HEAD_DIM, axis=1))
    q_scale = HEAD_DIM ** -0.5 * LOG2E
    return (jnp.tile(cos, (1, KV_HEADS)), jnp.tile(sin, (1, KV_HEADS)), cos.T * q_scale, sin.T * q_scale,
            jnp.concatenate(inv_cnt, axis=0))


def _arrange_w_in(w_in):
    ml0 = 2 * GROUP_W
    gate0 = ml0 + 4 * GROUP_W
    att0 = gate0 + 4 * N_HEADS
    pool0 = att0 + GROUP_W + 2 * KV_W
    gates = w_in[:, gate0:att0].reshape(-1, 2, 2, N_HEADS)
    gates = gates.transpose(0, 2, 1, 3).reshape(-1, 4 * N_HEADS)
    feat = jnp.concatenate([
        w_in[:, 0:ml0],
        w_in[:, ml0:ml0 + 2 * GROUP_W],
        w_in[:, ml0 + 2 * GROUP_W:gate0],
        w_in[:, att0:att0 + GROUP_W],
        w_in[:, att0 + GROUP_W + KV_W:pool0],
        gates,
    ], axis=1)
    tok = jnp.concatenate([w_in[:, att0 + GROUP_W:att0 + GROUP_W + KV_W], w_in[:, pool0:pool0 + GROUP_W]], axis=1)
    return feat.T.astype(BF16), tok.astype(BF16)


def _block_diag(w):
    g = w.shape[0]
    eye = jnp.eye(g, dtype=w.dtype)
    return (eye[:, None, :, None] * w[:, :, None, :]).reshape(g * w.shape[1], g * w.shape[2])


def kernel(x, c, ctx, c_ctx, w_mod, b_mod, norm_g, w_in, w_out, sgu_w, sgu_b, mlstm_conv_w, mlstm_gate_b,
           mlstm_norm_g, attn_sink, pool_w, pool_scale, w_ffn_in, w_ffn_out):
    B, N, D = x.shape
    ctx_len = ctx.shape[1]
    depth = w_mod.shape[0]
    assert D == 4 * GROUP_W and N % ROW_TILE == 0 and ctx_len == ROW_TILE
    assert w_in.shape[2] == 2 * GROUP_W + 4 * GROUP_W + 4 * N_HEADS + GROUP_W + 2 * KV_W + GROUP_W
    n_ctx_chunks = ctx_len // CHUNK

    xs = (ctx, x)
    rows = -(-(B + 1) // SUBLANES) * SUBLANES
    cc = jnp.zeros((rows, D), F32).at[0:B].set(c).at[B].set(c_ctx)
    mods = _modulation(cc, w_mod, b_mod).reshape(depth, rows, 6, D)
    tables = _rope_tables(N, ctx_len)
    masks = _attention_masks(n_ctx_chunks)

    def layer_proj(l):
        wt, wk = _arrange_w_in(w_in[l])
        conv_b = jnp.broadcast_to(mlstm_conv_w[l][:, :, None], (3, 2 * GROUP_W, ROW_TILE))
        return (wt, wk, tables, conv_b, _block_diag(pool_w[l]).astype(BF16), pool_scale[l].reshape(1, GROUP_W))

    for l in range(depth):
        zs, qk, vo, gt, aq, av, ak, dm = _in_proj(xs, mods[l], norm_g[l, 0:1], layer_proj(l), B)
        gate_bias = jnp.broadcast_to(
            mlstm_gate_b[l].reshape(2, 2, N_HEADS).transpose(1, 0, 2).reshape(4 * N_HEADS, 1), (4 * N_HEADS, CHUNK))
        sink_rows = jnp.zeros((SUBLANES, (N_HEADS // KV_HEADS) * CHUNK), F32).at[0:KV_HEADS].set(
            jnp.repeat(attn_sink[l].reshape(KV_HEADS, N_HEADS // KV_HEADS) * LOG2E, CHUNK, axis=1))
        mnorm_b = jnp.broadcast_to(mlstm_norm_g[l][:, None], (GROUP_W, ROW_TILE))
        a, cm, hf, hb = _mixers(zs, sgu_w[l].transpose(0, 2, 1).astype(BF16), sgu_b[l], aq, ak, av, sink_rows, masks,
                                qk, vo, gt, gate_bias, n_ctx_chunks)
        xs = _out_ffn(xs, a, hf, hb, vo, cm, dm, mods[l], norm_g[l], mnorm_b,
                      w_out[l].astype(BF16), w_ffn_in[l].astype(BF16), w_ffn_out[l].astype(BF16), B,
                      drop_ctx=(l == depth - 1))
    return xs
```

```python
import functools

import jax
import jax.numpy as jnp
from jax import lax
from jax.experimental import pallas as pl
from jax.experimental.pallas import tpu as pltpu

F32 = jnp.float32
BF16 = jnp.bfloat16

GRID_W = 64
ROPE_BASE = 10000.0
EPS = 1e-6
LANES = 128
SUBLANES = 8
BF16_ROWS = 16
CHUNK = 128
HEAD_DIM = 64
N_HEADS = 4
KV_HEADS = 2
GROUP_W = N_HEADS * HEAD_DIM
KV_W = KV_HEADS * HEAD_DIM
POOL_WINDOWS = (2, 4, 8, 16)
HALO = 8
ROW_TILE = 256
NEG = -1e30
VMEM_LIMIT = 56 * 1024 * 1024

R_SGU = 0
R_QK = 512
R_VO = 1024
R_AQ = 1536
R_AV = 1792
R_GATE = 1920
R_END = 1936
T_AK = 0
T_POOL = KV_W
T_END = KV_W + GROUP_W

ST_ROWS = 80
LOG2E = 1.4426950408889634

HI = lax.Precision.HIGHEST


def _dot(a, b):
    return jnp.dot(a, b, preferred_element_type=F32)


def _dot_hi(a, b):
    return jnp.dot(a, b, preferred_element_type=F32, precision=HI)


def _dot_nt(a, b):
    return lax.dot_general(a, b, (((1,), (1,)), ((), ())), preferred_element_type=F32)


def _dot_tn(a, b):
    return lax.dot_general(a, b, (((0,), (0,)), ((), ())), preferred_element_type=F32)


def _rms(x, g):
    return x * lax.rsqrt(jnp.mean(x * x, axis=-1, keepdims=True) + EPS) * g


def _silu(x):
    return x * jax.nn.sigmoid(x)


def _put_chunks(ref, x):
    for i in range(ref.shape[0]):
        ref[i] = x[:, i * CHUNK:(i + 1) * CHUNK].astype(ref.dtype)


def _get_chunks(ref):
    return jnp.concatenate([ref[i] for i in range(ref.shape[0])], axis=1)


def _head_layer_norm_t(v):
    mu = jnp.mean(v, axis=0, keepdims=True)
    vc = v - mu
    return vc * lax.rsqrt(jnp.mean(vc * vc, axis=0, keepdims=True) + EPS)


def _zero_at_first_step(refs):
    @pl.when(pl.program_id(0) == 0)
    def _():
        for ref in refs:
            ref[...] = jnp.zeros_like(ref)


def _mod_kernel(c_ref, w_ref, b_ref, o_ref):
    o_ref[...] = _dot(_silu(c_ref[...]).astype(BF16), w_ref[...].astype(BF16)) + b_ref[...]


def _modulation(cc, w_mod, b_mod):
    depth, d, n6 = w_mod.shape
    rows = cc.shape[0]
    tn = 1536
    return pl.pallas_call(
        _mod_kernel,
        grid=(depth, n6 // tn),
        in_specs=[
            pl.BlockSpec((rows, d), lambda l, j: (0, 0)),
            pl.BlockSpec((None, d, tn), lambda l, j: (l, 0, j)),
            pl.BlockSpec((None, 1, tn), lambda l, j: (l, 0, j)),
        ],
        out_specs=pl.BlockSpec((None, rows, tn), lambda l, j: (l, 0, j)),
        out_shape=jax.ShapeDtypeStruct((depth, rows, n6), F32),
        compiler_params=pltpu.CompilerParams(
            dimension_semantics=("arbitrary", "arbitrary"), vmem_limit_bytes=VMEM_LIMIT),
        name="modulation",
    )(cc, w_mod, b_mod.reshape(depth, 1, n6))


N_IN_SCRATCH = 4


def _first_tile_from_ctx(ctx_ref, x_ref, tile):
    rows = lax.broadcasted_iota(jnp.int32, x_ref.shape, 0)
    return jnp.where(rows < jnp.where(tile == 0, x_ref.shape[0], 0), ctx_ref[...], x_ref[...])


def _project_tile(hb, has_prev, has_next, wt_ref, wk_ref, cos_ref, sin_ref, cost_ref, sint_ref,
                  conv_ref, pw_ref, ps_ref, invc_ref,
                  zs_ref, qk_ref, vo_ref, gt_ref, aq_ref, av_ref, ak_ref, d_ref, z_ref, pcol_ref, zp_ref, prow_ref):
    tm = hb.shape[0]
    zc = _dot_nt(wt_ref[R_QK:R_VO, :], hb)
    _put_chunks(zs_ref, _dot_nt(wt_ref[R_SGU:R_QK, :], hb))

    z = z_ref[...]
    lane = lax.broadcasted_iota(jnp.int32, z.shape, 1)
    before = jnp.where(lane == 0, pcol_ref[:, 0:1] * has_prev, pltpu.roll(z, 1, axis=1))
    after = jnp.where(lane == tm - 1, zc[:, 0:1] * has_next, pltpu.roll(z, tm - 1, axis=1))
    qk = _silu(before * conv_ref[0] + z * conv_ref[1] + after * conv_ref[2])
    _put_chunks(qk_ref, jnp.concatenate([qk[0:GROUP_W], qk[GROUP_W:2 * GROUP_W] * (HEAD_DIM ** -0.5)], axis=0))
    pcol_ref[...] = jnp.broadcast_to(z[:, tm - 1:tm], pcol_ref.shape)
    z_ref[...] = zc

    zp_new = _dot(hb, wk_ref[:, T_POOL:T_END])
    _put_chunks(vo_ref, _dot_nt(wt_ref[R_VO:R_AQ, :], hb))
    zp = zp_ref[...]
    rows = tm + 2 * HALO
    ext = jnp.concatenate([prow_ref[...] * has_prev, zp, zp_new[0:HALO] * has_next], axis=0)
    s2 = ext + pltpu.roll(ext, 1, axis=0)
    s4 = pltpu.roll(s2, 1, axis=0) + pltpu.roll(s2, rows - 1, axis=0)
    s8 = pltpu.roll(s4, 2, axis=0) + pltpu.roll(s4, rows - 2, axis=0)
    s16 = pltpu.roll(s8, 4, axis=0) + pltpu.roll(s8, rows - 4, axis=0)
    sums = [s[HALO:HALO + tm] for s in (s2, s4, s8, s16)]
    plane = lax.broadcasted_iota(jnp.int32, zp.shape, 1)
    pooled = sums[-1]
    for gi in range(len(POOL_WINDOWS) - 2, -1, -1):
        pooled = jnp.where(plane < (gi + 1) * HEAD_DIM, sums[gi], pooled)
    pooled = pooled * invc_ref[...]
    d_ref[...] = (_dot((pooled - zp).astype(BF16), pw_ref[...]) * ps_ref[...]).astype(BF16)
    prow_ref[...] = zp[tm - HALO:tm]
    zp_ref[...] = zp_new

    q = _dot_nt(wt_ref[R_AQ:R_AV, :], hb)
    k = _dot(hb, wk_ref[:, T_AK:T_POOL])
    _put_chunks(av_ref, _dot_nt(wt_ref[R_AV:R_GATE, :], hb))
    _put_chunks(gt_ref, _dot_nt(wt_ref[R_GATE:R_END, :], hb))
    cos_t = jnp.concatenate([cost_ref[...]] * N_HEADS, axis=0)
    sin_t = jnp.concatenate([sint_ref[...]] * N_HEADS, axis=0)
    row = lax.broadcasted_iota(jnp.int32, q.shape, 0)
    rot_q = jnp.where((row & 31) < 16, pltpu.roll(q, GROUP_W - 16, axis=0), pltpu.roll(q, 16, axis=0))
    _put_chunks(aq_ref, q * cos_t + rot_q * sin_t)
    klane = lax.broadcasted_iota(jnp.int32, k.shape, 1)
    rot_k = jnp.where((klane & 31) < 16, pltpu.roll(k, KV_W - 16, axis=1), pltpu.roll(k, 16, axis=1))
    ak_ref[...] = (k * cos_ref[...] + rot_k * sin_ref[...]).astype(BF16)


def _segment_flags(tile, npb):
    j = tile % npb
    return (j >= 2).astype(F32), jnp.logical_and(j >= 1, j < npb - 1).astype(F32)


def _in_kernel(*refs, npb, n_tiles, split_ctx):
    ctx_ref = refs[0] if split_ctx else None
    x_ref, mod_ref, g_ref = (refs[1:] if split_ctx else refs)[0:3]
    project_refs = (refs[1:] if split_ctx else refs)[3:-1]
    hb_ref = refs[-1]
    s = pl.program_id(0)

    _zero_at_first_step((hb_ref,) + tuple(project_refs[-N_IN_SCRATCH:]))
    has_prev, has_next = _segment_flags(jnp.maximum(s - 2, 0), npb)
    gain = g_ref[...] * (1.0 + mod_ref[1:2, :])
    x_tile = _first_tile_from_ctx(ctx_ref, x_ref, jnp.minimum(s, n_tiles - 1) % npb) if split_ctx else x_ref[...]
    hb_next = (_rms(x_tile, gain) + mod_ref[0:1, :]).astype(BF16)
    _project_tile(hb_ref[...], has_prev, has_next, *project_refs)
    hb_ref[...] = hb_next


def _stream_specs(xs, tm, tile_of_step, npb):
    bj = lambda s: (tile_of_step(s) // npb, tile_of_step(s) % npb)
    if isinstance(xs, tuple):
        ctx, x = xs
        D = x.shape[2]
        return [pl.BlockSpec((None, tm, D), lambda s: (bj(s)[0], 0, 0)),
                pl.BlockSpec((None, tm, D), lambda s: (bj(s)[0], jnp.maximum(bj(s)[1] - 1, 0), 0))], [ctx, x]
    return [pl.BlockSpec((None, tm, xs.shape[2]), lambda s: (bj(s)[0], bj(s)[1], 0))], [xs]


def _stream_shape(xs):
    if isinstance(xs, tuple):
        return xs[1].shape[0], xs[0].shape[1] + xs[1].shape[1], xs[1].shape[2]
    return xs.shape


def _project_call_parts(B, S, D, tm, npb, mid, old, proj):
    wt, wk, tables, conv_b, pool_w_bd, pool_scale = proj
    cos_k, sin_k, cos_t, sin_t, inv_cnt = tables
    c2 = lambda s: (0, 0)
    cpt = tm // CHUNK
    in_specs = [
        pl.BlockSpec((R_END, D), c2, pipeline_mode=pl.Buffered(1)),
        pl.BlockSpec((D, T_END), c2, pipeline_mode=pl.Buffered(1)),
        pl.BlockSpec((tm, KV_W), lambda s: (mid(s) % npb, 0)),
        pl.BlockSpec((tm, KV_W), lambda s: (mid(s) % npb, 0)),
        pl.BlockSpec((HEAD_DIM, tm), lambda s: (0, mid(s) % npb)),
        pl.BlockSpec((HEAD_DIM, tm), lambda s: (0, mid(s) % npb)),
        pl.BlockSpec((3, 2 * GROUP_W, tm), lambda s: (0, 0, 0)),
        pl.BlockSpec((GROUP_W, GROUP_W), c2),
        pl.BlockSpec((1, GROUP_W), c2),
        pl.BlockSpec((tm, GROUP_W), lambda s: (old(s) % npb, 0)),
    ]
    args = [wt, wk, cos_k, sin_k, cos_t, sin_t, conv_b, pool_w_bd, pool_scale, inv_cnt]

    def fm(rows, dt, tile):
        return (pl.BlockSpec((None, cpt, rows, CHUNK), lambda s: (tile(s) // npb, tile(s) % npb, 0, 0)),
                jax.ShapeDtypeStruct((B, S // CHUNK, rows, CHUNK), dt))

    def tk(cols, dt, tile):
        return (pl.BlockSpec((None, tm, cols), lambda s: (tile(s) // npb, tile(s) % npb, 0)),
                jax.ShapeDtypeStruct((B, S, cols), dt))

    outs = [fm(2 * GROUP_W, BF16, mid), fm(2 * GROUP_W, BF16, old), fm(2 * GROUP_W, BF16, mid),
            fm(4 * N_HEADS, F32, mid), fm(GROUP_W, BF16, mid), fm(KV_W, BF16, mid),
            tk(KV_W, BF16, mid), tk(GROUP_W, BF16, old)]
    scratch = [pltpu.VMEM((2 * GROUP_W, tm), F32), pltpu.VMEM((2 * GROUP_W, LANES), F32),
               pltpu.VMEM((tm, GROUP_W), F32), pltpu.VMEM((HALO, GROUP_W), F32)]
    return in_specs, args, [o[0] for o in outs], [o[1] for o in outs], scratch


def _in_proj(xs, mods, g0, proj, n_batch_rows):
    B, S, D = _stream_shape(xs)
    tm = ROW_TILE
    npb = S // tm
    n_tiles = B * npb
    cur = lambda s: jnp.minimum(s, n_tiles - 1)
    mid = lambda s: jnp.clip(s - 1, 0, n_tiles - 1)
    old = lambda s: jnp.maximum(s - 2, 0)
    kern = functools.partial(_in_kernel, npb=npb, n_tiles=n_tiles, split_ctx=isinstance(xs, tuple))
    x_specs, x_args = _stream_specs(xs, tm, cur, npb)
    p_specs, p_args, out_specs, out_shapes, p_scratch = _project_call_parts(B, S, D, tm, npb, mid, old, proj)
    return pl.pallas_call(
        kern,
        grid=(n_tiles + 2,),
        in_specs=x_specs + [
            pl.BlockSpec((None, 6, D), lambda s: (jnp.where(cur(s) % npb == 0, n_batch_rows, cur(s) // npb), 0, 0)),
            pl.BlockSpec((1, D), lambda s: (0, 0)),
        ] + p_specs,
        out_specs=out_specs,
        out_shape=out_shapes,
        scratch_shapes=p_scratch + [pltpu.VMEM((tm, D), BF16)],
        compiler_params=pltpu.CompilerParams(dimension_semantics=("arbitrary",), vmem_limit_bytes=VMEM_LIMIT),
        name="in_proj",
    )(*x_args, mods, g0, *p_args)


def _log_sigmoid(x):
    return jnp.minimum(x, 0.0) - jnp.log1p(jnp.exp(-jnp.abs(x)))


CPS = 2


def _pair_spec(n_b, rows, pair=lambda p: p, row_block=0):
    return pl.BlockSpec((n_b, CPS, rows, CHUNK), lambda p: (0, pair(p), row_block, 0))


def _mixers_kernel(zs_ref, sw_ref, sb_ref,
                   q_ref, kp_ref, kc_ref, kn_ref, kx_ref, vp_ref, vc_ref, vn_ref, vx_ref, sink_ref, mask0_ref, mask1_ref,
                   qkf_ref, vf_ref, gf_ref, qkb_ref, vb_ref, gb_ref, gbias_ref,
                   a_ref, c_ref, hf_ref, hb_ref, st_ref, m_ref, *, n_ctx_chunks):
    n_b = zs_ref.shape[0]
    _zero_at_first_step((st_ref, m_ref))

    group = N_HEADS // KV_HEADS
    zeros = jnp.zeros((HEAD_DIM, group * CHUNK), BF16)
    pairs = [(u, b, kv) for u in range(CPS) for b in range(n_b) for kv in range(KV_HEADS)]
    mask_refs = (mask0_ref, mask1_ref)

    def keys(u, b):
        local = ([kp_ref[b, CHUNK:2 * CHUNK, :], kc_ref[b]] if u == 0 else [kc_ref[b], kn_ref[b, 0:CHUNK, :]])
        return jnp.concatenate(local + [kx_ref[b]], axis=0)

    def values(u, b):
        local = ([vp_ref[b, 1], vc_ref[b, 0], vc_ref[b, 1]] if u == 0 else [vc_ref[b, 0], vc_ref[b, 1], vn_ref[b, 0]])
        return jnp.concatenate(local + [vx_ref[b, i] for i in range(n_ctx_chunks)], axis=1)

    scores = []
    for u, b, kv in pairs:
        q2 = jnp.concatenate([q_ref[b, u, (kv * group + g) * HEAD_DIM:(kv * group + g + 1) * HEAD_DIM, :]
                              for g in range(group)], axis=1)
        qm = jnp.concatenate([q2, zeros] if kv == 0 else [zeros, q2], axis=0)
        scores.append(_dot(keys(u, b), qm))

    s_i = lax.broadcasted_iota(jnp.int32, (CHUNK, CHUNK), 0)
    t_i = lax.broadcasted_iota(jnp.int32, (CHUNK, CHUNK), 1)
    causal = {True: s_i <= t_i, False: s_i >= t_i}
    ones_row = jnp.where(lax.broadcasted_iota(jnp.int32, (ST_ROWS - HEAD_DIM, CHUNK), 0) == 0, 1.0, 0.0).astype(BF16)
    all_ones = jnp.ones((CHUNK, CHUNK), F32)

    def start_chains(t):
        chains = []
        for b in range(n_b):
            for forward, qk_ref, v_ref, g_ref, o_ref in ((True, qkf_ref, vf_ref, gf_ref, hf_ref),
                                                          (False, qkb_ref, vb_ref, gb_ref, hb_ref)):
                u = t if forward else CPS - 1 - t
                gates = g_ref[b, u] + gbias_ref[...]
                li8 = gates[0:2 * N_HEADS] * LOG2E
                lf8 = _log_sigmoid(gates[2 * N_HEADS:4 * N_HEADS]) * LOG2E
                cum8 = _dot_hi(lf8, jnp.where(causal[forward], 1.0, 0.0).astype(F32))
                tot8 = _dot_hi(lf8, all_ones)
                a8 = li8 - cum8
                for h in range(N_HEADS):
                    r = (0 if forward else N_HEADS) + h
                    ch = dict(b=b, u=u, h=h, mask=causal[forward], o_ref=o_ref,
                              slot=(b * 2 + (0 if forward else 1)) * N_HEADS + h,
                              a=a8[r:r + 1], bcum=cum8[r:r + 1], b_end=tot8[r:r + 1],
                              q=qk_ref[b, u, h * HEAD_DIM:(h + 1) * HEAD_DIM, :],
                              k=qk_ref[b, u, GROUP_W + h * HEAD_DIM:GROUP_W + (h + 1) * HEAD_DIM, :],
                              v=v_ref[b, u, h * HEAD_DIM:(h + 1) * HEAD_DIM, :])
                    ch["kq"] = _dot_tn(ch["k"], ch["q"])
                    chains.append(ch)
        return chains

    def read_state(chains, prev=None):
        for i, ch in enumerate(chains):
            ch["st"] = st_ref[ch["slot"]][:, 0:HEAD_DIM] if prev is None else prev[i]["st_new"]
            ch["m"] = m_ref[ch["slot"]:ch["slot"] + 1, :] if prev is None else prev[i]["m_new"]
            ch["cq"] = _dot(ch["st"].astype(BF16), ch["q"])

    def gate_math(chains):
        for ch in chains:
            a, m, mask = ch["a"], ch["m"], ch["mask"]
            a_col = jnp.broadcast_to(a, (CHUNK, CHUNK)).T
            a_max = jnp.max(a, axis=1, keepdims=True)
            big_m = jnp.maximum(jnp.max(jnp.where(mask, a_col, NEG), axis=0, keepdims=True), m)
            ch["inter"] = jnp.exp2(m - big_m)
            ch["floor"] = jnp.exp2(-(ch["bcum"] + big_m))
            s = ch["kq"] * jnp.exp2(jnp.where(mask, a_col - big_m, NEG))
            ch["rowsum"] = jnp.sum(s, axis=0, keepdims=True)
            ch["s"] = s.astype(BF16)
            m_new = jnp.maximum(ch["b_end"] + m, ch["b_end"] + a_max)
            ch["decay"] = jnp.exp2(ch["b_end"] + m - m_new)
            ch["kw"] = (ch["k"].astype(F32) * jnp.exp2(ch["b_end"] + a - m_new)).astype(BF16)
            ch["m_new"] = m_new

    def finish(chains):
        for ch in chains:
            ch["vs"] = _dot(ch["v"], ch["s"])
            v_ext = jnp.concatenate([ch["v"], ones_row], axis=0)
            ch["upd"] = _dot_nt(v_ext, ch["kw"])
        for ch in chains:
            h, cq, inter = ch["h"], ch["cq"], ch["inter"]
            num = cq[0:HEAD_DIM] * inter + ch["vs"]
            den = inter * cq[HEAD_DIM:HEAD_DIM + 1] + ch["rowsum"]
            ch["o_ref"][ch["b"], ch["u"], h * HEAD_DIM:(h + 1) * HEAD_DIM, :] = (
                num / jnp.maximum(jnp.abs(den), ch["floor"]))
            ch["st_new"] = ch["st"] * ch["decay"][:, 0:HEAD_DIM] + ch["upd"]

    first = start_chains(0)
    read_state(first)
    second = start_chains(1)

    for u in range(CPS):
        for b in range(n_b):
            g = jax.nn.gelu(zs_ref[b, u].astype(F32))
            for h in range(N_HEADS):
                gu = g[h * HEAD_DIM:(h + 1) * HEAD_DIM]
                vh = _head_layer_norm_t(g[GROUP_W + h * HEAD_DIM:GROUP_W + (h + 1) * HEAD_DIM]).astype(BF16)
                mixed = _dot(vh, sw_ref[h]) + sb_ref[h:h + 1, :]
                a_ref[b, u, h * HEAD_DIM:(h + 1) * HEAD_DIM, :] = (gu * mixed).astype(BF16)

    probs = []
    for (u, b, kv), s in zip(pairs, scores):
        s = s + mask_refs[u][...]
        sink = sink_ref[kv:kv + 1, :]
        m = jnp.maximum(jnp.max(s, axis=0, keepdims=True), sink)
        probs.append((jnp.exp2(s - m).astype(BF16), jnp.exp2(sink - m)))

    gate_math(first)
    finish(first)

    ones_keys = jnp.where(lax.broadcasted_iota(jnp.int32, (BF16_ROWS, (3 + n_ctx_chunks) * CHUNK), 0) == 0,
                          1.0, 0.0).astype(BF16)
    for (u, b, kv), (p, p_sink) in zip(pairs, probs):
        o_ext = _dot(jnp.concatenate([values(u, b)[kv * HEAD_DIM:(kv + 1) * HEAD_DIM, :], ones_keys], axis=0), p)
        o = o_ext[0:HEAD_DIM] / (o_ext[HEAD_DIM:HEAD_DIM + 1] + p_sink)
        for g in range(group):
            h = kv * group + g
            c_ref[b, u, h * HEAD_DIM:(h + 1) * HEAD_DIM, :] = o[:, g * CHUNK:(g + 1) * CHUNK].astype(BF16)

    read_state(second, prev=first)
    gate_math(second)
    finish(second)
    for ch in second:
        st_ref[ch["slot"], :, 0:HEAD_DIM] = ch["st_new"]
        m_ref[ch["slot"]:ch["slot"] + 1, :] = ch["m_new"]


def _attention_masks(n_ctx_chunks):
    group = N_HEADS // KV_HEADS
    n_keys = (3 + n_ctx_chunks) * CHUNK
    j = jnp.arange(n_keys)[:, None]
    i = (jnp.arange(group * CHUNK) & (CHUNK - 1))[None, :]
    masks = []
    for cur_ok, prev_ok, next_ok in ((False, False, False), (True, True, True), (True, False, True),
                                     (True, True, False), (True, False, False)):
        lo = (0 if prev_ok else CHUNK) if cur_ok else 3 * CHUNK
        hi = 3 * CHUNK if next_ok else 2 * CHUNK
        valid = ((j >= i) & (j <= i + 2 * CHUNK) & (j >= lo) & (j < hi)) | (j >= 3 * CHUNK)
        masks.append(jnp.where(valid, 0.0, NEG).astype(F32))
    return jnp.stack(masks)


def _mixers(zs, sgu_wt, sgu_b, aq, ak, av, sink_rows, masks, qk, vo, gt, gate_bias, n_ctx_chunks):
    B, n_chunks = zs.shape[:2]
    assert n_chunks % CPS == 0 and n_ctx_chunks == CPS
    n_pairs = n_chunks // CPS
    ctx_w = n_ctx_chunks * CHUNK
    group = N_HEADS // KV_HEADS
    n_keys = 3 * CHUNK + ctx_w
    kern = functools.partial(_mixers_kernel, n_ctx_chunks=n_ctx_chunks)
    ident = lambda p: p
    prev = lambda p: jnp.maximum(p - 1, 0)
    nxt = lambda p: jnp.minimum(p + 1, n_pairs - 1)
    bwd = lambda p: jnp.where(p == 0, 0, n_pairs - p)
    kspec = lambda f: pl.BlockSpec((B, CPS * CHUNK, KV_W), lambda p: (0, f(p), 0))
    vspec = lambda f: _pair_spec(B, KV_W, f)

    def mask_spec(u):
        def kind(p):
            c = CPS * p + u
            no_prev = (c == n_ctx_chunks).astype(jnp.int32)
            no_next = (c == n_chunks - 1).astype(jnp.int32)
            return jnp.where(c < n_ctx_chunks, 0, 1 + no_prev + 2 * no_next)
        return pl.BlockSpec((None, n_keys, group * CHUNK), lambda p: (kind(p), 0, 0))

    def scan_specs(order):
        return [_pair_spec(B, 2 * GROUP_W, order), _pair_spec(B, GROUP_W, order),
                _pair_spec(B, 4 * N_HEADS, order)]

    n_slots = B * 2 * N_HEADS
    return pl.pallas_call(
        kern,
        grid=(n_pairs,),
        in_specs=[
            _pair_spec(B, 2 * GROUP_W),
            pl.BlockSpec((N_HEADS, CHUNK, CHUNK), lambda p: (0, 0, 0)),
            pl.BlockSpec((N_HEADS, CHUNK), lambda p: (0, 0)),
            _pair_spec(B, GROUP_W),
            kspec(prev), kspec(ident), kspec(nxt), pl.BlockSpec((B, ctx_w, KV_W), lambda p: (0, 0, 0)),
            vspec(prev), vspec(ident), vspec(nxt),
            pl.BlockSpec((B, n_ctx_chunks, KV_W, CHUNK), lambda p: (0, 0, 0, 0)),
            pl.BlockSpec((SUBLANES, group * CHUNK), lambda p: (0, 0)),
            mask_spec(0), mask_spec(1),
        ] + scan_specs(ident) + scan_specs(bwd) + [pl.BlockSpec((4 * N_HEADS, CHUNK), lambda p: (0, 0))],
        out_specs=[_pair_spec(B, GROUP_W), _pair_spec(B, GROUP_W),
                   _pair_spec(B, GROUP_W, ident), _pair_spec(B, GROUP_W, bwd)],
        out_shape=[jax.ShapeDtypeStruct((B, n_chunks, GROUP_W, CHUNK), BF16)] * 2
                  + [jax.ShapeDtypeStruct((B, n_chunks, GROUP_W, CHUNK), F32)] * 2,
        scratch_shapes=[
            pltpu.VMEM((n_slots, ST_ROWS, LANES), F32),
            pltpu.VMEM((n_slots, LANES), F32),
        ],
        compiler_params=pltpu.CompilerParams(dimension_semantics=("arbitrary",)),
        name="mixers",
    )(zs, sgu_wt, sgu_b, aq, ak, ak, ak, ak, av, av, av, av, sink_rows, masks, masks, qk, vo, gt, qk, vo, gt,
      gate_bias)


N_OUT_SCRATCH = 6


def _hidden_chunks(hidden):
    step = 3 * 2 * LANES
    return [(lo, min(lo + step, hidden)) for lo in range(0, hidden, step)]


def _out_kernel(*refs, hidden, npb, n_tiles, split_ctx):
    _zero_at_first_step(refs[-N_OUT_SCRATCH:-2])
    ctx_ref = refs[0] if split_ctx else None
    (x_ref, a_ref, hf_ref, hb_ref, zo_ref, c_ref, d_ref, moda_ref, modc_ref, g_ref, mg_ref,
     wo_ref, wi_ref, wf_ref, o_ref, x1_ref, x1b_ref, tb_ref, f_ref, act_ref, bm_ref) = refs[1:] if split_ctx else refs
    tb = tb_ref[...]
    chunks = _hidden_chunks(hidden)

    def ffn_in(lo, hi):
        return _dot(tb, wi_ref[:, lo:hi]), _dot(tb, wi_ref[:, hidden + lo:hidden + hi])

    def act(gate_up, lo, hi):
        gate, up = gate_up
        act_ref[:, lo:hi] = (_silu(gate) * up).astype(BF16)

    def ffn_out(lo, hi):
        return _dot(act_ref[:, lo:hi], wf_ref[lo:hi, :])

    gu0 = ffn_in(*chunks[0])
    o_ref[...] = x1b_ref[...] + _rms(f_ref[...], modc_ref[5:6, :] * g_ref[3:4, :])
    gu1 = ffn_in(*chunks[1])
    hs = _get_chunks(hf_ref) + _get_chunks(hb_ref)
    hn = jnp.concatenate([_head_layer_norm_t(hs[h * HEAD_DIM:(h + 1) * HEAD_DIM]) for h in range(N_HEADS)], axis=0)
    bm_ref[...] = (hn * mg_ref[...] * jax.nn.sigmoid(_get_chunks(zo_ref).astype(F32))).astype(BF16)
    act(gu0, *chunks[0])
    gu2 = ffn_in(*chunks[2])
    act(gu1, *chunks[1])
    y = (_dot_tn(_get_chunks(a_ref), wo_ref[0:GROUP_W, :])
         + _dot_tn(bm_ref[...], wo_ref[GROUP_W:2 * GROUP_W, :])
         + _dot_tn(_get_chunks(c_ref), wo_ref[2 * GROUP_W:3 * GROUP_W, :])
         + _dot(d_ref[...], wo_ref[3 * GROUP_W:4 * GROUP_W, :]))
    rest = [ffn_in(*ch) for ch in chunks[3:]]
    act(gu2, *chunks[2])
    f = ffn_out(*chunks[0])
    x1b_ref[...] = x1_ref[...]
    x_tile = (_first_tile_from_ctx(ctx_ref, x_ref, jnp.minimum(pl.program_id(0), n_tiles - 1) % npb)
              if split_ctx else x_ref[...])
    x1 = x_tile + _rms(y, moda_ref[2:3, :] * g_ref[1:2, :])
    t = _rms(x1, g_ref[2:3, :] * (1.0 + moda_ref[4:5, :])) + moda_ref[3:4, :]
    x1_ref[...] = x1
    tb_ref[...] = t.astype(BF16)
    f = f + ffn_out(*chunks[1]) + ffn_out(*chunks[2])
    for gate_up, ch in zip(rest, chunks[3:]):
        act(gate_up, *ch)
        f = f + ffn_out(*ch)
    f_ref[...] = f


def _out_ffn(xs, a, hf, hb, vo, c, d, mods, norm_g, mnorm_g, w_out, w_ffn_in, w_ffn_out, n_batch_rows, drop_ctx):
    B, S, D = _stream_shape(xs)
    tm = ROW_TILE
    npb = S // tm
    n_tiles = B * npb
    hidden = w_ffn_out.shape[0]
    assert len(_hidden_chunks(hidden)) >= 3
    cur = lambda s: jnp.minimum(s, n_tiles - 1)
    prv = lambda s: jnp.clip(s - 2, 0, n_tiles - 1)
    row = lambda s: (cur(s) // npb, cur(s) % npb, 0)
    mod_row = lambda t: (jnp.where(t % npb == 0, n_batch_rows, t // npb), 0, 0)
    c2 = lambda s: (0, 0)
    cpt = tm // CHUNK
    fm = pl.BlockSpec((None, cpt, GROUP_W, CHUNK), lambda s: (cur(s) // npb, cur(s) % npb, 0, 0))
    split_ctx = isinstance(xs, tuple)
    x_specs, x_args = _stream_specs(xs, tm, cur, npb)
    in_specs = x_specs + [
        fm, fm, fm,
        pl.BlockSpec((None, cpt, GROUP_W, CHUNK), lambda s: (cur(s) // npb, cur(s) % npb, 1, 0)),
        fm,
        pl.BlockSpec((None, tm, GROUP_W), row),
        pl.BlockSpec((None, 6, D), lambda s: mod_row(cur(s))),
        pl.BlockSpec((None, 6, D), lambda s: mod_row(prv(s))),
        pl.BlockSpec((4, D), c2),
        pl.BlockSpec((GROUP_W, tm), c2),
        pl.BlockSpec((D, D), c2, pipeline_mode=pl.Buffered(1)),
        pl.BlockSpec((D, 2 * hidden), c2, pipeline_mode=pl.Buffered(1)),
        pl.BlockSpec((hidden, D), c2, pipeline_mode=pl.Buffered(1)),
    ]
    args = x_args + [a, hf, hb, vo, c, d, mods, mods, norm_g, mnorm_g, w_out, w_ffn_in, w_ffn_out]
    scratch = [pltpu.VMEM((tm, D), F32), pltpu.VMEM((tm, D), F32), pltpu.VMEM((tm, D), BF16),
               pltpu.VMEM((tm, D), F32), pltpu.VMEM((tm, hidden), BF16), pltpu.VMEM((GROUP_W, tm), BF16)]
    params = pltpu.CompilerParams(dimension_semantics=("arbitrary",), vmem_limit_bytes=VMEM_LIMIT)
    static = dict(hidden=hidden, npb=npb, n_tiles=n_tiles, split_ctx=split_ctx)
    if drop_ctx:
        out_spec = pl.BlockSpec((None, tm, D), lambda s: (prv(s) // npb, jnp.maximum(prv(s) % npb - 1, 0), 0))
        out_shape = jax.ShapeDtypeStruct((B, S - tm, D), F32)
    else:
        out_spec = pl.BlockSpec((None, tm, D), lambda s: (prv(s) // npb, prv(s) % npb, 0))
        out_shape = jax.ShapeDtypeStruct((B, S, D), F32)
    return pl.pallas_call(
        functools.partial(_out_kernel, **static),
        grid=(n_tiles + 2,),
        in_specs=in_specs,
        out_specs=out_spec,
        out_shape=out_shape,
        scratch_shapes=scratch, compiler_params=params, name="out_ffn",
    )(*args)


def _rope_tables(n_tokens, ctx_len):
    rows = n_tokens // GRID_W
    axis_freq = HEAD_DIM // 4
    inv = jnp.power(ROPE_BASE, -jnp.arange(axis_freq, dtype=F32) * 2.0 / (2 * axis_freq))
    ar = jnp.arange(rows, dtype=F32)[:, None] * inv
    ac = jnp.arange(GRID_W, dtype=F32)[:, None] * inv

    def per_token(fn):
        r = jnp.broadcast_to(fn(ar)[:, None, :], (rows, GRID_W, axis_freq))
        c = jnp.broadcast_to(fn(ac)[None, :, :], (rows, GRID_W, axis_freq))
        return jnp.concatenate([r, r, c, c], axis=-1).reshape(n_tokens, HEAD_DIM)

    cos, sin = per_token(jnp.cos), per_token(jnp.sin)
    sign = jnp.where((jnp.arange(HEAD_DIM) % 32) < 16, -1.0, 1.0).astype(F32)
    cos = jnp.concatenate([jnp.ones((ctx_len, HEAD_DIM), F32), cos], axis=0)
    sin = jnp.concatenate([jnp.zeros((ctx_len, HEAD_DIM), F32), sin * sign], axis=0)
    inv_cnt = []
    for seg in (ctx_len, n_tokens):
        t = jnp.arange(seg)[:, None]
        half = jnp.asarray([w // 2 for w in POOL_WINDOWS])[None, :]
        cnt = jnp.minimum(t + half, seg) - jnp.maximum(t - half, 0)
        inv_cnt.append(jnp.repeat(1.0 / cnt.astype(F32), HEAD_DIM, axis=1))
    q_scale = HEAD_DIM ** -0.5 * LOG2E
    return (jnp.tile(cos, (1, KV_HEADS)), jnp.tile(sin, (1, KV_HEADS)), cos.T * q_scale, sin.T * q_scale,
            jnp.concatenate(inv_cnt, axis=0))


def _arrange_w_in(w_in):
    ml0 = 2 * GROUP_W
    gate0 = ml0 + 4 * GROUP_W
    att0 = gate0 + 4 * N_HEADS
    pool0 = att0 + GROUP_W + 2 * KV_W
    gates = w_in[:, gate0:att0].reshape(-1, 2, 2, N_HEADS)
    gates = gates.transpose(0, 2, 1, 3).reshape(-1, 4 * N_HEADS)
    feat = jnp.concatenate([
        w_in[:, 0:ml0],
        w_in[:, ml0:ml0 + 2 * GROUP_W],
        w_in[:, ml0 + 2 * GROUP_W:gate0],
        w_in[:, att0:att0 + GROUP_W],
        w_in[:, att0 + GROUP_W + KV_W:pool0],
        gates,
    ], axis=1)
    tok = jnp.concatenate([w_in[:, att0 + GROUP_W:att0 + GROUP_W + KV_W], w_in[:, pool0:pool0 + GROUP_W]], axis=1)
    return feat.T.astype(BF16), tok.astype(BF16)


def _block_diag(w):
    g = w.shape[0]
    eye = jnp.eye(g, dtype=w.dtype)
    return (eye[:, None, :, None] * w[:, :, None, :]).reshape(g * w.shape[1], g * w.shape[2])


def kernel(x, c, ctx, c_ctx, w_mod, b_mod, norm_g, w_in, w_out, sgu_w, sgu_b, mlstm_conv_w, mlstm_gate_b,
           mlstm_norm_g, attn_sink, pool_w, pool_scale, w_ffn_in, w_ffn_out):
    B, N, D = x.shape
    ctx_len = ctx.shape[1]
    depth = w_mod.shape[0]
    assert D == 4 * GROUP_W and N % ROW_TILE == 0 and ctx_len == ROW_TILE
    assert w_in.shape[2] == 2 * GROUP_W + 4 * GROUP_W + 4 * N_HEADS + GROUP_W + 2 * KV_W + GROUP_W
    n_ctx_chunks = ctx_len // CHUNK

    xs = (ctx, x)
    rows = -(-(B + 1) // SUBLANES) * SUBLANES
    cc = jnp.zeros((rows, D), F32).at[0:B].set(c).at[B].set(c_ctx)
    mods = _modulation(cc, w_mod, b_mod).reshape(depth, rows, 6, D)
    tables = _rope_tables(N, ctx_len)
    masks = _attention_masks(n_ctx_chunks)

    def layer_proj(l):
        wt, wk = _arrange_w_in(w_in[l])
        conv_b = jnp.broadcast_to(mlstm_conv_w[l][:, :, None], (3, 2 * GROUP_W, ROW_TILE))
        return (wt, wk, tables, conv_b, _block_diag(pool_w[l]).astype(BF16), pool_scale[l].reshape(1, GROUP_W))

    for l in range(depth):
        zs, qk, vo, gt, aq, av, ak, dm = _in_proj(xs, mods[l], norm_g[l, 0:1], layer_proj(l), B)
        gate_bias = jnp.broadcast_to(
            mlstm_gate_b[l].reshape(2, 2, N_HEADS).transpose(1, 0, 2).reshape(4 * N_HEADS, 1), (4 * N_HEADS, CHUNK))
        sink_rows = jnp.zeros((SUBLANES, (N_HEADS // KV_HEADS) * CHUNK), F32).at[0:KV_HEADS].set(
            jnp.repeat(attn_sink[l].reshape(KV_HEADS, N_HEADS // KV_HEADS) * LOG2E, CHUNK, axis=1))
        mnorm_b = jnp.broadcast_to(mlstm_norm_g[l][:, None], (GROUP_W, ROW_TILE))
        a, cm, hf, hb = _mixers(zs, sgu_w[l].transpose(0, 2, 1).astype(BF16), sgu_b[l], aq, ak, av, sink_rows, masks,
                                qk, vo, gt, gate_bias, n_ctx_chunks)
        xs = _out_ffn(xs, a, hf, hb, vo, cm, dm, mods[l], norm_g[l], mnorm_b,
                      w_out[l].astype(BF16), w_ffn_in[l].astype(BF16), w_ffn_out[l].astype(BF16), B,
                      drop_ctx=(l == depth - 1))
    return xs
```

```python
import functools

import jax
import jax.numpy as jnp
from jax import lax
from jax.experimental import pallas as pl
from jax.experimental.pallas import tpu as pltpu

F32 = jnp.float32
BF16 = jnp.bfloat16

GRID_W = 64
ROPE_BASE = 10000.0
EPS = 1e-6
LANES = 128
SUBLANES = 8
BF16_ROWS = 16
CHUNK = 128
HEAD_DIM = 64
N_HEADS = 4
KV_HEADS = 2
GROUP_W = N_HEADS * HEAD_DIM
KV_W = KV_HEADS * HEAD_DIM
POOL_WINDOWS = (2, 4, 8, 16)
HALO = 8
ROW_TILE = 256
NEG = -1e30
VMEM_LIMIT = 56 * 1024 * 1024

R_SGU = 0
R_QK = 512
R_VO = 1024
R_AQ = 1536
R_AV = 1792
R_GATE = 1920
R_END = 1936
T_AK = 0
T_POOL = KV_W
T_END = KV_W + GROUP_W

ST_ROWS = 80
LOG2E = 1.4426950408889634

HI = lax.Precision.HIGHEST


def _dot(a, b):
    return jnp.dot(a, b, preferred_element_type=F32)


def _dot_hi(a, b):
    return jnp.dot(a, b, preferred_element_type=F32, precision=HI)


def _dot_nt(a, b):
    return lax.dot_general(a, b, (((1,), (1,)), ((), ())), preferred_element_type=F32)


def _dot_tn(a, b):
    return lax.dot_general(a, b, (((0,), (0,)), ((), ())), preferred_element_type=F32)


def _rms(x, g):
    return x * lax.rsqrt(jnp.mean(x * x, axis=-1, keepdims=True) + EPS) * g


def _silu(x):
    return x * jax.nn.sigmoid(x)


def _put_chunks(ref, x):
    for i in range(ref.shape[0]):
        ref[i] = x[:, i * CHUNK:(i + 1) * CHUNK].astype(ref.dtype)


def _get_chunks(ref):
    return jnp.concatenate([ref[i] for i in range(ref.shape[0])], axis=1)


def _head_layer_norm_t(v):
    mu = jnp.mean(v, axis=0, keepdims=True)
    vc = v - mu
    return vc * lax.rsqrt(jnp.mean(vc * vc, axis=0, keepdims=True) + EPS)


def _zero_at_first_step(refs):
    @pl.when(pl.program_id(0) == 0)
    def _():
        for ref in refs:
            ref[...] = jnp.zeros_like(ref)


def _mod_kernel(c_ref, w_ref, b_ref, o_ref):
    o_ref[...] = _dot(_silu(c_ref[...]).astype(BF16), w_ref[...].astype(BF16)) + b_ref[...]


def _modulation(cc, w_mod, b_mod):
    depth, d, n6 = w_mod.shape
    rows = cc.shape[0]
    tn = 1536
    return pl.pallas_call(
        _mod_kernel,
        grid=(depth, n6 // tn),
        in_specs=[
            pl.BlockSpec((rows, d), lambda l, j: (0, 0)),
            pl.BlockSpec((None, d, tn), lambda l, j: (l, 0, j)),
            pl.BlockSpec((None, 1, tn), lambda l, j: (l, 0, j)),
        ],
        out_specs=pl.BlockSpec((None, rows, tn), lambda l, j: (l, 0, j)),
        out_shape=jax.ShapeDtypeStruct((depth, rows, n6), F32),
        compiler_params=pltpu.CompilerParams(
            dimension_semantics=("arbitrary", "arbitrary"), vmem_limit_bytes=VMEM_LIMIT),
        name="modulation",
    )(cc, w_mod, b_mod.reshape(depth, 1, n6))


N_IN_SCRATCH = 4


def _first_tile_from_ctx(ctx_ref, x_ref, tile):
    rows = lax.broadcasted_iota(jnp.int32, x_ref.shape, 0)
    return jnp.where(rows < jnp.where(tile == 0, x_ref.shape[0], 0), ctx_ref[...], x_ref[...])


def _project_tile(hb, has_prev, has_next, wt_ref, wk_ref, cos_ref, sin_ref, cost_ref, sint_ref,
                  conv_ref, pw_ref, ps_ref, invc_ref,
                  zs_ref, qk_ref, vo_ref, gt_ref, aq_ref, av_ref, ak_ref, d_ref, z_ref, pcol_ref, zp_ref, prow_ref):
    tm = hb.shape[0]
    zc = _dot_nt(wt_ref[R_QK:R_VO, :], hb)
    _put_chunks(zs_ref, _dot_nt(wt_ref[R_SGU:R_QK, :], hb))

    z = z_ref[...]
    lane = lax.broadcasted_iota(jnp.int32, z.shape, 1)
    before = jnp.where(lane == 0, pcol_ref[:, 0:1] * has_prev, pltpu.roll(z, 1, axis=1))
    after = jnp.where(lane == tm - 1, zc[:, 0:1] * has_next, pltpu.roll(z, tm - 1, axis=1))
    qk = _silu(before * conv_ref[0] + z * conv_ref[1] + after * conv_ref[2])
    _put_chunks(qk_ref, jnp.concatenate([qk[0:GROUP_W], qk[GROUP_W:2 * GROUP_W] * (HEAD_DIM ** -0.5)], axis=0))
    pcol_ref[...] = jnp.broadcast_to(z[:, tm - 1:tm], pcol_ref.shape)
    z_ref[...] = zc

    zp_new = _dot(hb, wk_ref[:, T_POOL:T_END])
    _put_chunks(vo_ref, _dot_nt(wt_ref[R_VO:R_AQ, :], hb))
    zp = zp_ref[...]
    rows = tm + 2 * HALO
    ext = jnp.concatenate([prow_ref[...] * has_prev, zp, zp_new[0:HALO] * has_next], axis=0)
    s2 = ext + pltpu.roll(ext, 1, axis=0)
    s4 = pltpu.roll(s2, 1, axis=0) + pltpu.roll(s2, rows - 1, axis=0)
    s8 = pltpu.roll(s4, 2, axis=0) + pltpu.roll(s4, rows - 2, axis=0)
    s16 = pltpu.roll(s8, 4, axis=0) + pltpu.roll(s8, rows - 4, axis=0)
    sums = [s[HALO:HALO + tm] for s in (s2, s4, s8, s16)]
    plane = lax.broadcasted_iota(jnp.int32, zp.shape, 1)
    pooled = sums[-1]
    for gi in range(len(POOL_WINDOWS) - 2, -1, -1):
        pooled = jnp.where(plane < (gi + 1) * HEAD_DIM, sums[gi], pooled)
    pooled = pooled * invc_ref[...]
    d_ref[...] = (_dot((pooled - zp).astype(BF16), pw_ref[...]) * ps_ref[...]).astype(BF16)
    prow_ref[...] = zp[tm - HALO:tm]
    zp_ref[...] = zp_new

    q = _dot_nt(wt_ref[R_AQ:R_AV, :], hb)
    k = _dot(hb, wk_ref[:, T_AK:T_POOL])
    _put_chunks(av_ref, _dot_nt(wt_ref[R_AV:R_GATE, :], hb))
    _put_chunks(gt_ref, _dot_nt(wt_ref[R_GATE:R_END, :], hb))
    cos_t = jnp.concatenate([cost_ref[...]] * N_HEADS, axis=0)
    sin_t = jnp.concatenate([sint_ref[...]] * N_HEADS, axis=0)
    row = lax.broadcasted_iota(jnp.int32, q.shape, 0)
    rot_q = jnp.where((row & 31) < 16, pltpu.roll(q, GROUP_W - 16, axis=0), pltpu.roll(q, 16, axis=0))
    _put_chunks(aq_ref, q * cos_t + rot_q * sin_t)
    klane = lax.broadcasted_iota(jnp.int32, k.shape, 1)
    rot_k = jnp.where((klane & 31) < 16, pltpu.roll(k, KV_W - 16, axis=1), pltpu.roll(k, 16, axis=1))
    ak_ref[...] = (k * cos_ref[...] + rot_k * sin_ref[...]).astype(BF16)


def _segment_flags(tile, npb):
    j = tile % npb
    return (j >= 2).astype(F32), jnp.logical_and(j >= 1, j < npb - 1).astype(F32)


def _in_kernel(*refs, npb, n_tiles, split_ctx):
    ctx_ref = refs[0] if split_ctx else None
    x_ref, mod_ref, g_ref = (refs[1:] if split_ctx else refs)[0:3]
    project_refs = (refs[1:] if split_ctx else refs)[3:-1]
    hb_ref = refs[-1]
    s = pl.program_id(0)

    _zero_at_first_step((hb_ref,) + tuple(project_refs[-N_IN_SCRATCH:]))
    has_prev, has_next = _segment_flags(jnp.maximum(s - 2, 0), npb)
    gain = g_ref[...] * (1.0 + mod_ref[1:2, :])
    x_tile = _first_tile_from_ctx(ctx_ref, x_ref, jnp.minimum(s, n_tiles - 1) % npb) if split_ctx else x_ref[...]
    hb_next = (_rms(x_tile, gain) + mod_ref[0:1, :]).astype(BF16)
    _project_tile(hb_ref[...], has_prev, has_next, *project_refs)
    hb_ref[...] = hb_next


def _stream_specs(xs, tm, tile_of_step, npb):
    bj = lambda s: (tile_of_step(s) // npb, tile_of_step(s) % npb)
    if isinstance(xs, tuple):
        ctx, x = xs
        D = x.shape[2]
        return [pl.BlockSpec((None, tm, D), lambda s: (bj(s)[0], 0, 0)),
                pl.BlockSpec((None, tm, D), lambda s: (bj(s)[0], jnp.maximum(bj(s)[1] - 1, 0), 0))], [ctx, x]
    return [pl.BlockSpec((None, tm, xs.shape[2]), lambda s: (bj(s)[0], bj(s)[1], 0))], [xs]


def _stream_shape(xs):
    if isinstance(xs, tuple):
        return xs[1].shape[0], xs[0].shape[1] + xs[1].shape[1], xs[1].shape[2]
    return xs.shape


def _project_call_parts(B, S, D, tm, npb, mid, old, proj, layer):
    wt, wk, tables, conv_b, pool_w_bd, pool_scale = proj
    cos_k, sin_k, cos_t, sin_t, inv_cnt = tables
    c2 = lambda s: (0, 0)
    cpt = tm // CHUNK
    in_specs = [
        pl.BlockSpec((None, R_END, D), lambda s: (layer, 0, 0), pipeline_mode=pl.Buffered(1)),
        pl.BlockSpec((None, D, T_END), lambda s: (layer, 0, 0), pipeline_mode=pl.Buffered(1)),
        pl.BlockSpec((tm, KV_W), lambda s: (mid(s) % npb, 0)),
        pl.BlockSpec((tm, KV_W), lambda s: (mid(s) % npb, 0)),
        pl.BlockSpec((HEAD_DIM, tm), lambda s: (0, mid(s) % npb)),
        pl.BlockSpec((HEAD_DIM, tm), lambda s: (0, mid(s) % npb)),
        pl.BlockSpec((3, 2 * GROUP_W, tm), lambda s: (0, 0, 0)),
        pl.BlockSpec((GROUP_W, GROUP_W), c2),
        pl.BlockSpec((1, GROUP_W), c2),
        pl.BlockSpec((tm, GROUP_W), lambda s: (old(s) % npb, 0)),
    ]
    args = [wt, wk, cos_k, sin_k, cos_t, sin_t, conv_b, pool_w_bd, pool_scale, inv_cnt]

    def fm(rows, dt, tile):
        return (pl.BlockSpec((None, cpt, rows, CHUNK), lambda s: (tile(s) // npb, tile(s) % npb, 0, 0)),
                jax.ShapeDtypeStruct((B, S // CHUNK, rows, CHUNK), dt))

    def tk(cols, dt, tile):
        return (pl.BlockSpec((None, tm, cols), lambda s: (tile(s) // npb, tile(s) % npb, 0)),
                jax.ShapeDtypeStruct((B, S, cols), dt))

    outs = [fm(2 * GROUP_W, BF16, mid), fm(2 * GROUP_W, BF16, old), fm(2 * GROUP_W, BF16, mid),
            fm(4 * N_HEADS, F32, mid), fm(GROUP_W, BF16, mid), fm(KV_W, BF16, mid),
            tk(KV_W, BF16, mid), tk(GROUP_W, BF16, old)]
    scratch = [pltpu.VMEM((2 * GROUP_W, tm), F32), pltpu.VMEM((2 * GROUP_W, LANES), F32),
               pltpu.VMEM((tm, GROUP_W), F32), pltpu.VMEM((HALO, GROUP_W), F32)]
    return in_specs, args, [o[0] for o in outs], [o[1] for o in outs], scratch


def _in_proj(xs, mods, g0, proj, layer, n_batch_rows):
    B, S, D = _stream_shape(xs)
    tm = ROW_TILE
    npb = S // tm
    n_tiles = B * npb
    cur = lambda s: jnp.minimum(s, n_tiles - 1)
    mid = lambda s: jnp.clip(s - 1, 0, n_tiles - 1)
    old = lambda s: jnp.maximum(s - 2, 0)
    kern = functools.partial(_in_kernel, npb=npb, n_tiles=n_tiles, split_ctx=isinstance(xs, tuple))
    x_specs, x_args = _stream_specs(xs, tm, cur, npb)
    p_specs, p_args, out_specs, out_shapes, p_scratch = _project_call_parts(B, S, D, tm, npb, mid, old, proj, layer)
    return pl.pallas_call(
        kern,
        grid=(n_tiles + 2,),
        in_specs=x_specs + [
            pl.BlockSpec((None, None, 6, D),
                         lambda s: (layer, jnp.where(cur(s) % npb == 0, n_batch_rows, cur(s) // npb), 0, 0)),
            pl.BlockSpec((1, D), lambda s: (0, 0)),
        ] + p_specs,
        out_specs=out_specs,
        out_shape=out_shapes,
        scratch_shapes=p_scratch + [pltpu.VMEM((tm, D), BF16)],
        compiler_params=pltpu.CompilerParams(dimension_semantics=("arbitrary",), vmem_limit_bytes=VMEM_LIMIT),
        name="in_proj",
    )(*x_args, mods, g0, *p_args)


def _log_sigmoid(x):
    return jnp.minimum(x, 0.0) - jnp.log1p(jnp.exp(-jnp.abs(x)))


CPS = 2


def _pair_spec(n_b, rows, pair=lambda p: p, row_block=0):
    return pl.BlockSpec((n_b, CPS, rows, CHUNK), lambda p: (0, pair(p), row_block, 0))


def _mixers_kernel(zs_ref, sw_ref, sb_ref,
                   q_ref, kp_ref, kc_ref, kn_ref, kx_ref, vp_ref, vc_ref, vn_ref, vx_ref, sink_ref, mask0_ref, mask1_ref,
                   qkf_ref, vf_ref, gf_ref, qkb_ref, vb_ref, gb_ref, gbias_ref,
                   a_ref, c_ref, hf_ref, hb_ref, st_ref, m_ref, *, n_ctx_chunks):
    n_b = zs_ref.shape[0]
    _zero_at_first_step((st_ref, m_ref))

    group = N_HEADS // KV_HEADS
    zeros = jnp.zeros((HEAD_DIM, group * CHUNK), BF16)
    pairs = [(u, b, kv) for u in range(CPS) for b in range(n_b) for kv in range(KV_HEADS)]
    mask_refs = (mask0_ref, mask1_ref)

    def keys(u, b):
        local = ([kp_ref[b, CHUNK:2 * CHUNK, :], kc_ref[b]] if u == 0 else [kc_ref[b], kn_ref[b, 0:CHUNK, :]])
        return jnp.concatenate(local + [kx_ref[b]], axis=0)

    def values(u, b):
        local = ([vp_ref[b, 1], vc_ref[b, 0], vc_ref[b, 1]] if u == 0 else [vc_ref[b, 0], vc_ref[b, 1], vn_ref[b, 0]])
        return jnp.concatenate(local + [vx_ref[b, i] for i in range(n_ctx_chunks)], axis=1)

    scores = []
    for u, b, kv in pairs:
        q2 = jnp.concatenate([q_ref[b, u, (kv * group + g) * HEAD_DIM:(kv * group + g + 1) * HEAD_DIM, :]
                              for g in range(group)], axis=1)
        qm = jnp.concatenate([q2, zeros] if kv == 0 else [zeros, q2], axis=0)
        scores.append(_dot(keys(u, b), qm))

    s_i = lax.broadcasted_iota(jnp.int32, (CHUNK, CHUNK), 0)
    t_i = lax.broadcasted_iota(jnp.int32, (CHUNK, CHUNK), 1)
    causal = {True: s_i <= t_i, False: s_i >= t_i}
    ones_row = jnp.where(lax.broadcasted_iota(jnp.int32, (ST_ROWS - HEAD_DIM, CHUNK), 0) == 0, 1.0, 0.0).astype(BF16)
    all_ones = jnp.ones((CHUNK, CHUNK), F32)

    def start_chains(t):
        chains = []
        for b in range(n_b):
            for forward, qk_ref, v_ref, g_ref, o_ref in ((True, qkf_ref, vf_ref, gf_ref, hf_ref),
                                                          (False, qkb_ref, vb_ref, gb_ref, hb_ref)):
                u = t if forward else CPS - 1 - t
                gates = g_ref[b, u] + gbias_ref[...]
                li8 = gates[0:2 * N_HEADS] * LOG2E
                lf8 = _log_sigmoid(gates[2 * N_HEADS:4 * N_HEADS]) * LOG2E
                cum8 = _dot_hi(lf8, jnp.where(causal[forward], 1.0, 0.0).astype(F32))
                tot8 = _dot_hi(lf8, all_ones)
                a8 = li8 - cum8
                for h in range(N_HEADS):
                    r = (0 if forward else N_HEADS) + h
                    ch = dict(b=b, u=u, h=h, mask=causal[forward], o_ref=o_ref,
                              slot=(b * 2 + (0 if forward else 1)) * N_HEADS + h,
                              a=a8[r:r + 1], bcum=cum8[r:r + 1], b_end=tot8[r:r + 1],
                              q=qk_ref[b, u, h * HEAD_DIM:(h + 1) * HEAD_DIM, :],
                              k=qk_ref[b, u, GROUP_W + h * HEAD_DIM:GROUP_W + (h + 1) * HEAD_DIM, :],
                              v=v_ref[b, u, h * HEAD_DIM:(h + 1) * HEAD_DIM, :])
                    ch["kq"] = _dot_tn(ch["k"], ch["q"])
                    chains.append(ch)
        return chains

    def read_state(chains, prev=None):
        for i, ch in enumerate(chains):
            ch["st"] = st_ref[ch["slot"]][:, 0:HEAD_DIM] if prev is None else prev[i]["st_new"]
            ch["m"] = m_ref[ch["slot"]:ch["slot"] + 1, :] if prev is None else prev[i]["m_new"]
            ch["cq"] = _dot(ch["st"].astype(BF16), ch["q"])

    def gate_math(chains):
        for ch in chains:
            a, m, mask = ch["a"], ch["m"], ch["mask"]
            a_col = jnp.broadcast_to(a, (CHUNK, CHUNK)).T
            a_max = jnp.max(a, axis=1, keepdims=True)
            big_m = jnp.maximum(jnp.max(jnp.where(mask, a_col, NEG), axis=0, keepdims=True), m)
            ch["inter"] = jnp.exp2(m - big_m)
            ch["floor"] = jnp.exp2(-(ch["bcum"] + big_m))
            s = ch["kq"] * jnp.exp2(jnp.where(mask, a_col - big_m, NEG))
            ch["rowsum"] = jnp.sum(s, axis=0, keepdims=True)
            ch["s"] = s.astype(BF16)
            m_new = jnp.maximum(ch["b_end"] + m, ch["b_end"] + a_max)
            ch["decay"] = jnp.exp2(ch["b_end"] + m - m_new)
            ch["kw"] = (ch["k"].astype(F32) * jnp.exp2(ch["b_end"] + a - m_new)).astype(BF16)
            ch["m_new"] = m_new

    def finish(chains):
        for ch in chains:
            ch["vs"] = _dot(ch["v"], ch["s"])
            v_ext = jnp.concatenate([ch["v"], ones_row], axis=0)
            ch["upd"] = _dot_nt(v_ext, ch["kw"])
        for ch in chains:
            h, cq, inter = ch["h"], ch["cq"], ch["inter"]
            num = cq[0:HEAD_DIM] * inter + ch["vs"]
            den = inter * cq[HEAD_DIM:HEAD_DIM + 1] + ch["rowsum"]
            ch["o_ref"][ch["b"], ch["u"], h * HEAD_DIM:(h + 1) * HEAD_DIM, :] = (
                num / jnp.maximum(jnp.abs(den), ch["floor"]))
            ch["st_new"] = ch["st"] * ch["decay"][:, 0:HEAD_DIM] + ch["upd"]

    first = start_chains(0)
    read_state(first)
    second = start_chains(1)

    for u in range(CPS):
        for b in range(n_b):
            g = jax.nn.gelu(zs_ref[b, u].astype(F32))
            for h in range(N_HEADS):
                gu = g[h * HEAD_DIM:(h + 1) * HEAD_DIM]
                vh = _head_layer_norm_t(g[GROUP_W + h * HEAD_DIM:GROUP_W + (h + 1) * HEAD_DIM]).astype(BF16)
                mixed = _dot(vh, sw_ref[h]) + sb_ref[h:h + 1, :]
                a_ref[b, u, h * HEAD_DIM:(h + 1) * HEAD_DIM, :] = (gu * mixed).astype(BF16)

    probs = []
    for (u, b, kv), s in zip(pairs, scores):
        s = s + mask_refs[u][...]
        sink = sink_ref[kv:kv + 1, :]
        m = jnp.maximum(jnp.max(s, axis=0, keepdims=True), sink)
        probs.append((jnp.exp2(s - m).astype(BF16), jnp.exp2(sink - m)))

    gate_math(first)
    finish(first)

    ones_keys = jnp.where(lax.broadcasted_iota(jnp.int32, (BF16_ROWS, (3 + n_ctx_chunks) * CHUNK), 0) == 0,
                          1.0, 0.0).astype(BF16)
    for (u, b, kv), (p, p_sink) in zip(pairs, probs):
        o_ext = _dot(jnp.concatenate([values(u, b)[kv * HEAD_DIM:(kv + 1) * HEAD_DIM, :], ones_keys], axis=0), p)
        o = o_ext[0:HEAD_DIM] / (o_ext[HEAD_DIM:HEAD_DIM + 1] + p_sink)
        for g in range(group):
            h = kv * group + g
            c_ref[b, u, h * HEAD_DIM:(h + 1) * HEAD_DIM, :] = o[:, g * CHUNK:(g + 1) * CHUNK].astype(BF16)

    read_state(second, prev=first)
    gate_math(second)
    finish(second)
    for ch in second:
        st_ref[ch["slot"], :, 0:HEAD_DIM] = ch["st_new"]
        m_ref[ch["slot"]:ch["slot"] + 1, :] = ch["m_new"]


def _attention_masks(n_ctx_chunks):
    group = N_HEADS // KV_HEADS
    n_keys = (3 + n_ctx_chunks) * CHUNK
    j = jnp.arange(n_keys)[:, None]
    i = (jnp.arange(group * CHUNK) & (CHUNK - 1))[None, :]
    masks = []
    for cur_ok, prev_ok, next_ok in ((False, False, False), (True, True, True), (True, False, True),
                                     (True, True, False), (True, False, False)):
        lo = (0 if prev_ok else CHUNK) if cur_ok else 3 * CHUNK
        hi = 3 * CHUNK if next_ok else 2 * CHUNK
        valid = ((j >= i) & (j <= i + 2 * CHUNK) & (j >= lo) & (j < hi)) | (j >= 3 * CHUNK)
        masks.append(jnp.where(valid, 0.0, NEG).astype(F32))
    return jnp.stack(masks)


def _mixers(zs, sgu_wt, sgu_b, aq, ak, av, sink_rows, masks, qk, vo, gt, gate_bias, n_ctx_chunks):
    B, n_chunks = zs.shape[:2]
    assert n_chunks % CPS == 0 and n_ctx_chunks == CPS
    n_pairs = n_chunks // CPS
    ctx_w = n_ctx_chunks * CHUNK
    group = N_HEADS // KV_HEADS
    n_keys = 3 * CHUNK + ctx_w
    kern = functools.partial(_mixers_kernel, n_ctx_chunks=n_ctx_chunks)
    ident = lambda p: p
    prev = lambda p: jnp.maximum(p - 1, 0)
    nxt = lambda p: jnp.minimum(p + 1, n_pairs - 1)
    bwd = lambda p: jnp.where(p == 0, 0, n_pairs - p)
    kspec = lambda f: pl.BlockSpec((B, CPS * CHUNK, KV_W), lambda p: (0, f(p), 0))
    vspec = lambda f: _pair_spec(B, KV_W, f)

    def mask_spec(u):
        def kind(p):
            c = CPS * p + u
            no_prev = (c == n_ctx_chunks).astype(jnp.int32)
            no_next = (c == n_chunks - 1).astype(jnp.int32)
            return jnp.where(c < n_ctx_chunks, 0, 1 + no_prev + 2 * no_next)
        return pl.BlockSpec((None, n_keys, group * CHUNK), lambda p: (kind(p), 0, 0))

    def scan_specs(order):
        return [_pair_spec(B, 2 * GROUP_W, order), _pair_spec(B, GROUP_W, order),
                _pair_spec(B, 4 * N_HEADS, order)]

    n_slots = B * 2 * N_HEADS
    return pl.pallas_call(
        kern,
        grid=(n_pairs,),
        in_specs=[
            _pair_spec(B, 2 * GROUP_W),
            pl.BlockSpec((N_HEADS, CHUNK, CHUNK), lambda p: (0, 0, 0)),
            pl.BlockSpec((N_HEADS, CHUNK), lambda p: (0, 0)),
            _pair_spec(B, GROUP_W),
            kspec(prev), kspec(ident), kspec(nxt), pl.BlockSpec((B, ctx_w, KV_W), lambda p: (0, 0, 0)),
            vspec(prev), vspec(ident), vspec(nxt),
            pl.BlockSpec((B, n_ctx_chunks, KV_W, CHUNK), lambda p: (0, 0, 0, 0)),
            pl.BlockSpec((SUBLANES, group * CHUNK), lambda p: (0, 0)),
            mask_spec(0), mask_spec(1),
        ] + scan_specs(ident) + scan_specs(bwd) + [pl.BlockSpec((4 * N_HEADS, CHUNK), lambda p: (0, 0))],
        out_specs=[_pair_spec(B, GROUP_W), _pair_spec(B, GROUP_W),
                   _pair_spec(B, GROUP_W, ident), _pair_spec(B, GROUP_W, bwd)],
        out_shape=[jax.ShapeDtypeStruct((B, n_chunks, GROUP_W, CHUNK), BF16)] * 2
                  + [jax.ShapeDtypeStruct((B, n_chunks, GROUP_W, CHUNK), F32)] * 2,
        scratch_shapes=[
            pltpu.VMEM((n_slots, ST_ROWS, LANES), F32),
            pltpu.VMEM((n_slots, LANES), F32),
        ],
        compiler_params=pltpu.CompilerParams(dimension_semantics=("arbitrary",)),
        name="mixers",
    )(zs, sgu_wt, sgu_b, aq, ak, ak, ak, ak, av, av, av, av, sink_rows, masks, masks, qk, vo, gt, qk, vo, gt,
      gate_bias)


N_OUT_SCRATCH = 6


def _hidden_chunks(hidden):
    step = 3 * 2 * LANES
    return [(lo, min(lo + step, hidden)) for lo in range(0, hidden, step)]


def _out_kernel(*refs, hidden, npb, n_tiles, split_ctx):
    _zero_at_first_step(refs[-N_OUT_SCRATCH:-1])
    ctx_ref = refs[0] if split_ctx else None
    (x_ref, a_ref, hf_ref, hb_ref, zo_ref, c_ref, d_ref, moda_ref, modc_ref, g_ref, mg_ref,
     wo_ref, wi_ref, wf_ref, o_ref, x1_ref, y_ref, f_ref, tb_ref, act_ref, bm_ref) = refs[1:] if split_ctx else refs
    s = pl.program_id(0)
    f_ref[s % 2] = _dot(act_ref[...], wf_ref[...])

    x_tile = (_first_tile_from_ctx(ctx_ref, x_ref, jnp.clip(s - 1, 0, n_tiles - 1) % npb)
              if split_ctx else x_ref[...])
    x1 = x_tile + _rms(y_ref[...], moda_ref[2:3, :] * g_ref[1:2, :])
    x1_ref[s % 3] = x1
    tb_ref[...] = (_rms(x1, g_ref[2:3, :] * (1.0 + moda_ref[4:5, :])) + moda_ref[3:4, :]).astype(BF16)

    o_ref[...] = x1_ref[(s + 1) % 3] + _rms(f_ref[(s + 1) % 2], modc_ref[5:6, :] * g_ref[3:4, :])

    for lo, hi in _hidden_chunks(hidden):
        gate = _dot(tb_ref[...], wi_ref[:, lo:hi])
        up = _dot(tb_ref[...], wi_ref[:, hidden + lo:hidden + hi])
        act_ref[:, lo:hi] = (_silu(gate) * up).astype(BF16)

    hs = _get_chunks(hf_ref) + _get_chunks(hb_ref)
    hn = jnp.concatenate([_head_layer_norm_t(hs[h * HEAD_DIM:(h + 1) * HEAD_DIM]) for h in range(N_HEADS)], axis=0)
    bm_ref[...] = (hn * mg_ref[...] * jax.nn.sigmoid(_get_chunks(zo_ref).astype(F32))).astype(BF16)
    y_ref[...] = (_dot_tn(_get_chunks(a_ref), wo_ref[0:GROUP_W, :])
                  + _dot_tn(bm_ref[...], wo_ref[GROUP_W:2 * GROUP_W, :])
                  + _dot_tn(_get_chunks(c_ref), wo_ref[2 * GROUP_W:3 * GROUP_W, :])
                  + _dot(d_ref[...], wo_ref[3 * GROUP_W:4 * GROUP_W, :]))


def _out_ffn(xs, a, hf, hb, vo, c, d, mods, norm_g, mnorm_g, w_out, w_ffn_in, w_ffn_out, layer, n_batch_rows,
             drop_ctx):
    B, S, D = _stream_shape(xs)
    tm = ROW_TILE
    npb = S // tm
    n_tiles = B * npb
    hidden = w_ffn_out.shape[1]
    cur = lambda s: jnp.minimum(s, n_tiles - 1)
    mid = lambda s: jnp.clip(s - 1, 0, n_tiles - 1)
    prv = lambda s: jnp.clip(s - 3, 0, n_tiles - 1)
    row = lambda s: (cur(s) // npb, cur(s) % npb, 0)
    mod_row = lambda t: (layer, jnp.where(t % npb == 0, n_batch_rows, t // npb), 0, 0)
    c2 = lambda s: (0, 0)
    this_layer = lambda s: (layer, 0, 0)
    cpt = tm // CHUNK
    fm = pl.BlockSpec((None, cpt, GROUP_W, CHUNK), lambda s: (cur(s) // npb, cur(s) % npb, 0, 0))
    split_ctx = isinstance(xs, tuple)
    x_specs, x_args = _stream_specs(xs, tm, mid, npb)
    in_specs = x_specs + [
        fm, fm, fm,
        pl.BlockSpec((None, cpt, GROUP_W, CHUNK), lambda s: (cur(s) // npb, cur(s) % npb, 1, 0)),
        fm,
        pl.BlockSpec((None, tm, GROUP_W), row),
        pl.BlockSpec((None, None, 6, D), lambda s: mod_row(mid(s))),
        pl.BlockSpec((None, None, 6, D), lambda s: mod_row(prv(s))),
        pl.BlockSpec((None, 4, D), this_layer),
        pl.BlockSpec((GROUP_W, tm), c2),
        pl.BlockSpec((None, D, D), this_layer, pipeline_mode=pl.Buffered(1)),
        pl.BlockSpec((None, D, 2 * hidden), this_layer, pipeline_mode=pl.Buffered(1)),
        pl.BlockSpec((None, hidden, D), this_layer, pipeline_mode=pl.Buffered(1)),
    ]
    args = x_args + [a, hf, hb, vo, c, d, mods, mods, norm_g, mnorm_g, w_out, w_ffn_in, w_ffn_out]
    scratch = [pltpu.VMEM((3, tm, D), F32), pltpu.VMEM((tm, D), F32), pltpu.VMEM((2, tm, D), F32),
               pltpu.VMEM((tm, D), BF16), pltpu.VMEM((tm, hidden), BF16), pltpu.VMEM((GROUP_W, tm), BF16)]
    params = pltpu.CompilerParams(dimension_semantics=("arbitrary",), vmem_limit_bytes=VMEM_LIMIT)
    static = dict(hidden=hidden, npb=npb, n_tiles=n_tiles, split_ctx=split_ctx)
    if drop_ctx:
        out_spec = pl.BlockSpec((None, tm, D), lambda s: (prv(s) // npb, jnp.maximum(prv(s) % npb - 1, 0), 0))
        out_shape = jax.ShapeDtypeStruct((B, S - tm, D), F32)
    else:
        out_spec = pl.BlockSpec((None, tm, D), lambda s: (prv(s) // npb, prv(s) % npb, 0))
        out_shape = jax.ShapeDtypeStruct((B, S, D), F32)
    return pl.pallas_call(
        functools.partial(_out_kernel, **static),
        grid=(n_tiles + 3,),
        in_specs=in_specs,
        out_specs=out_spec,
        out_shape=out_shape,
        scratch_shapes=scratch, compiler_params=params, name="out_ffn",
    )(*args)


def _rope_tables(n_tokens, ctx_len):
    rows = n_tokens // GRID_W
    axis_freq = HEAD_DIM // 4
    inv = jnp.power(ROPE_BASE, -jnp.arange(axis_freq, dtype=F32) * 2.0 / (2 * axis_freq))
    ar = jnp.arange(rows, dtype=F32)[:, None] * inv
    ac = jnp.arange(GRID_W, dtype=F32)[:, None] * inv

    def per_token(fn):
        r = jnp.broadcast_to(fn(ar)[:, None, :], (rows, GRID_W, axis_freq))
        c = jnp.broadcast_to(fn(ac)[None, :, :], (rows, GRID_W, axis_freq))
        return jnp.concatenate([r, r, c, c], axis=-1).reshape(n_tokens, HEAD_DIM)

    cos, sin = per_token(jnp.cos), per_token(jnp.sin)
    sign = jnp.where((jnp.arange(HEAD_DIM) % 32) < 16, -1.0, 1.0).astype(F32)
    cos = jnp.concatenate([jnp.ones((ctx_len, HEAD_DIM), F32), cos], axis=0)
    sin = jnp.concatenate([jnp.zeros((ctx_len, HEAD_DIM), F32), sin * sign], axis=0)
    inv_cnt = []
    for seg in (ctx_len, n_tokens):
        t = jnp.arange(seg)[:, None]
        half = jnp.asarray([w // 2 for w in POOL_WINDOWS])[None, :]
        cnt = jnp.minimum(t + half, seg) - jnp.maximum(t - half, 0)
        inv_cnt.append(jnp.repeat(1.0 / cnt.astype(F32), HEAD_DIM, axis=1))
    q_scale = HEAD_DIM ** -0.5 * LOG2E
    return (jnp.tile(cos, (1, KV_HEADS)), jnp.tile(sin, (1, KV_HEADS)), cos.T * q_scale, sin.T * q_scale,
            jnp.concatenate(inv_cnt, axis=0))


def _arrange_w_in(w_in):
    ml0 = 2 * GROUP_W
    gate0 = ml0 + 4 * GROUP_W
    att0 = gate0 + 4 * N_HEADS
    pool0 = att0 + GROUP_W + 2 * KV_W
    gates = w_in[:, gate0:att0].reshape(-1, 2, 2, N_HEADS)
    gates = gates.transpose(0, 2, 1, 3).reshape(-1, 4 * N_HEADS)
    feat = jnp.concatenate([
        w_in[:, 0:ml0],
        w_in[:, ml0:ml0 + 2 * GROUP_W],
        w_in[:, ml0 + 2 * GROUP_W:gate0],
        w_in[:, att0:att0 + GROUP_W],
        w_in[:, att0 + GROUP_W + KV_W:pool0],
        gates,
    ], axis=1)
    tok = jnp.concatenate([w_in[:, att0 + GROUP_W:att0 + GROUP_W + KV_W], w_in[:, pool0:pool0 + GROUP_W]], axis=1)
    return feat.T.astype(BF16), tok.astype(BF16)


def _block_diag(w):
    g = w.shape[0]
    eye = jnp.eye(g, dtype=w.dtype)
    return (eye[:, None, :, None] * w[:, :, None, :]).reshape(g * w.shape[1], g * w.shape[2])


def kernel(x, c, ctx, c_ctx, w_mod, b_mod, norm_g, w_in, w_out, sgu_w, sgu_b, mlstm_conv_w, mlstm_gate_b,
           mlstm_norm_g, attn_sink, pool_w, pool_scale, w_ffn_in, w_ffn_out):
    B, N, D = x.shape
    ctx_len = ctx.shape[1]
    depth = w_mod.shape[0]
    assert D == 4 * GROUP_W and N % ROW_TILE == 0 and ctx_len == ROW_TILE
    assert w_in.shape[2] == 2 * GROUP_W + 4 * GROUP_W + 4 * N_HEADS + GROUP_W + 2 * KV_W + GROUP_W
    n_ctx_chunks = ctx_len // CHUNK

    xs = (ctx, x)
    rows = -(-(B + 1) // SUBLANES) * SUBLANES
    cc = jnp.zeros((rows, D), F32).at[0:B].set(c).at[B].set(c_ctx)
    mods = _modulation(cc, w_mod, b_mod).reshape(depth, rows, 6, D)
    tables = _rope_tables(N, ctx_len)
    masks = _attention_masks(n_ctx_chunks)

    wt, wk = jax.vmap(_arrange_w_in)(w_in)
    w_out_b, w_ffn_in_b, w_ffn_out_b = w_out.astype(BF16), w_ffn_in.astype(BF16), w_ffn_out.astype(BF16)

    def layer_proj(l):
        conv_b = jnp.broadcast_to(mlstm_conv_w[l][:, :, None], (3, 2 * GROUP_W, ROW_TILE))
        return (wt, wk, tables, conv_b, _block_diag(pool_w[l]).astype(BF16), pool_scale[l].reshape(1, GROUP_W))

    for l in range(depth):
        zs, qk, vo, gt, aq, av, ak, dm = _in_proj(xs, mods, norm_g[l, 0:1], layer_proj(l), l, B)
        gate_bias = jnp.broadcast_to(
            mlstm_gate_b[l].reshape(2, 2, N_HEADS).transpose(1, 0, 2).reshape(4 * N_HEADS, 1), (4 * N_HEADS, CHUNK))
        sink_rows = jnp.zeros((SUBLANES, (N_HEADS // KV_HEADS) * CHUNK), F32).at[0:KV_HEADS].set(
            jnp.repeat(attn_sink[l].reshape(KV_HEADS, N_HEADS // KV_HEADS) * LOG2E, CHUNK, axis=1))
        mnorm_b = jnp.broadcast_to(mlstm_norm_g[l][:, None], (GROUP_W, ROW_TILE))
        a, cm, hf, hb = _mixers(zs, sgu_w[l].transpose(0, 2, 1).astype(BF16), sgu_b[l], aq, ak, av, sink_rows, masks,
                                qk, vo, gt, gate_bias, n_ctx_chunks)
        xs = _out_ffn(xs, a, hf, hb, vo, cm, dm, mods, norm_g, mnorm_b, w_out_b, w_ffn_in_b, w_ffn_out_b, l, B,
                      drop_ctx=(l == depth - 1))
    return xs
```

```python
import functools

import jax
import jax.numpy as jnp
from jax import lax
from jax.experimental import pallas as pl
from jax.experimental.pallas import tpu as pltpu

F32 = jnp.float32
BF16 = jnp.bfloat16

GRID_W = 64
ROPE_BASE = 10000.0
EPS = 1e-6
LANES = 128
SUBLANES = 8
BF16_ROWS = 16
CHUNK = 128
HEAD_DIM = 64
N_HEADS = 4
KV_HEADS = 2
GROUP_W = N_HEADS * HEAD_DIM
KV_W = KV_HEADS * HEAD_DIM
POOL_WINDOWS = (2, 4, 8, 16)
HALO = 8
ROW_TILE = 256
NEG = -1e30
VMEM_LIMIT = 56 * 1024 * 1024

R_SGU = 0
R_QK = 512
R_VO = 1024
R_AQ = 1536
R_AV = 1792
R_GATE = 1920
R_END = 1936
T_AK = 0
T_POOL = KV_W
T_END = KV_W + GROUP_W

ST_ROWS = 80
LOG2E = 1.4426950408889634

HI = lax.Precision.HIGHEST


def _dot(a, b):
    return jnp.dot(a, b, preferred_element_type=F32)


def _dot_hi(a, b):
    return jnp.dot(a, b, preferred_element_type=F32, precision=HI)


def _dot_nt(a, b):
    return lax.dot_general(a, b, (((1,), (1,)), ((), ())), preferred_element_type=F32)


def _dot_tn(a, b):
    return lax.dot_general(a, b, (((0,), (0,)), ((), ())), preferred_element_type=F32)


def _rms(x, g):
    return x * lax.rsqrt(jnp.mean(x * x, axis=-1, keepdims=True) + EPS) * g


def _silu(x):
    return x * jax.nn.sigmoid(x)


def _put_chunks(ref, x):
    for i in range(ref.shape[0]):
        ref[i] = x[:, i * CHUNK:(i + 1) * CHUNK].astype(ref.dtype)


def _get_chunks(ref):
    return jnp.concatenate([ref[i] for i in range(ref.shape[0])], axis=1)


def _head_layer_norm_t(v):
    mu = jnp.mean(v, axis=0, keepdims=True)
    vc = v - mu
    return vc * lax.rsqrt(jnp.mean(vc * vc, axis=0, keepdims=True) + EPS)


def _zero_at_first_step(refs):
    @pl.when(pl.program_id(0) == 0)
    def _():
        for ref in refs:
            ref[...] = jnp.zeros_like(ref)


def _mod_kernel(c_ref, w_ref, b_ref, o_ref):
    o_ref[...] = _dot(_silu(c_ref[...]).astype(BF16), w_ref[...].astype(BF16)) + b_ref[...]


def _modulation(cc, w_mod, b_mod):
    depth, d, n6 = w_mod.shape
    rows = cc.shape[0]
    tn = 1536
    return pl.pallas_call(
        _mod_kernel,
        grid=(depth, n6 // tn),
        in_specs=[
            pl.BlockSpec((rows, d), lambda l, j: (0, 0)),
            pl.BlockSpec((None, d, tn), lambda l, j: (l, 0, j)),
            pl.BlockSpec((None, 1, tn), lambda l, j: (l, 0, j)),
        ],
        out_specs=pl.BlockSpec((None, rows, tn), lambda l, j: (l, 0, j)),
        out_shape=jax.ShapeDtypeStruct((depth, rows, n6), F32),
        compiler_params=pltpu.CompilerParams(
            dimension_semantics=("arbitrary", "arbitrary"), vmem_limit_bytes=VMEM_LIMIT),
        name="modulation",
    )(cc, w_mod, b_mod.reshape(depth, 1, n6))


N_IN_SCRATCH = 4


def _first_tile_from_ctx(ctx_ref, x_ref, tile):
    rows = lax.broadcasted_iota(jnp.int32, x_ref.shape, 0)
    return jnp.where(rows < jnp.where(tile == 0, x_ref.shape[0], 0), ctx_ref[...], x_ref[...])


def _project_tile(hb, has_prev, has_next, wt_ref, wk_ref, cos_ref, sin_ref, cost_ref, sint_ref,
                  conv_ref, pw_ref, ps_ref, invc_ref,
                  zs_ref, qk_ref, vo_ref, gt_ref, aq_ref, av_ref, ak_ref, d_ref, z_ref, pcol_ref, zp_ref, prow_ref):
    tm = hb.shape[0]
    zc = _dot_nt(wt_ref[R_QK:R_VO, :], hb)
    _put_chunks(zs_ref, _dot_nt(wt_ref[R_SGU:R_QK, :], hb))

    z = z_ref[...]
    lane = lax.broadcasted_iota(jnp.int32, z.shape, 1)
    before = jnp.where(lane == 0, pcol_ref[:, 0:1] * has_prev, pltpu.roll(z, 1, axis=1))
    after = jnp.where(lane == tm - 1, zc[:, 0:1] * has_next, pltpu.roll(z, tm - 1, axis=1))
    qk = _silu(before * conv_ref[0] + z * conv_ref[1] + after * conv_ref[2])
    _put_chunks(qk_ref, jnp.concatenate([qk[0:GROUP_W], qk[GROUP_W:2 * GROUP_W] * (HEAD_DIM ** -0.5)], axis=0))
    pcol_ref[...] = jnp.broadcast_to(z[:, tm - 1:tm], pcol_ref.shape)
    z_ref[...] = zc

    zp_new = _dot(hb, wk_ref[:, T_POOL:T_END])
    _put_chunks(vo_ref, _dot_nt(wt_ref[R_VO:R_AQ, :], hb))
    zp = zp_ref[...]
    rows = tm + 2 * HALO
    ext = jnp.concatenate([prow_ref[...] * has_prev, zp, zp_new[0:HALO] * has_next], axis=0)
    s2 = ext + pltpu.roll(ext, 1, axis=0)
    s4 = pltpu.roll(s2, 1, axis=0) + pltpu.roll(s2, rows - 1, axis=0)
    s8 = pltpu.roll(s4, 2, axis=0) + pltpu.roll(s4, rows - 2, axis=0)
    s16 = pltpu.roll(s8, 4, axis=0) + pltpu.roll(s8, rows - 4, axis=0)
    sums = [s[HALO:HALO + tm] for s in (s2, s4, s8, s16)]
    plane = lax.broadcasted_iota(jnp.int32, zp.shape, 1)
    pooled = sums[-1]
    for gi in range(len(POOL_WINDOWS) - 2, -1, -1):
        pooled = jnp.where(plane < (gi + 1) * HEAD_DIM, sums[gi], pooled)
    pooled = pooled * invc_ref[...]
    d_ref[...] = (_dot((pooled - zp).astype(BF16), pw_ref[...]) * ps_ref[...]).astype(BF16)
    prow_ref[...] = zp[tm - HALO:tm]
    zp_ref[...] = zp_new

    q = _dot_nt(wt_ref[R_AQ:R_AV, :], hb)
    k = _dot(hb, wk_ref[:, T_AK:T_POOL])
    _put_chunks(av_ref, _dot_nt(wt_ref[R_AV:R_GATE, :], hb))
    _put_chunks(gt_ref, _dot_nt(wt_ref[R_GATE:R_END, :], hb))
    cos_t = jnp.concatenate([cost_ref[...]] * N_HEADS, axis=0)
    sin_t = jnp.concatenate([sint_ref[...]] * N_HEADS, axis=0)
    row = lax.broadcasted_iota(jnp.int32, q.shape, 0)
    rot_q = jnp.where((row & 31) < 16, pltpu.roll(q, GROUP_W - 16, axis=0), pltpu.roll(q, 16, axis=0))
    _put_chunks(aq_ref, q * cos_t + rot_q * sin_t)
    klane = lax.broadcasted_iota(jnp.int32, k.shape, 1)
    rot_k = jnp.where((klane & 31) < 16, pltpu.roll(k, KV_W - 16, axis=1), pltpu.roll(k, 16, axis=1))
    ak_ref[...] = (k * cos_ref[...] + rot_k * sin_ref[...]).astype(BF16)


def _segment_flags(tile, npb):
    j = tile % npb
    return (j >= 2).astype(F32), jnp.logical_and(j >= 1, j < npb - 1).astype(F32)


def _in_kernel(*refs, npb, n_tiles, split_ctx):
    ctx_ref = refs[0] if split_ctx else None
    x_ref, mod_ref, g_ref = (refs[1:] if split_ctx else refs)[0:3]
    project_refs = (refs[1:] if split_ctx else refs)[3:-1]
    hb_ref = refs[-1]
    s = pl.program_id(0)

    _zero_at_first_step((hb_ref,) + tuple(project_refs[-N_IN_SCRATCH:]))
    has_prev, has_next = _segment_flags(jnp.maximum(s - 2, 0), npb)
    gain = g_ref[...] * (1.0 + mod_ref[1:2, :])
    x_tile = _first_tile_from_ctx(ctx_ref, x_ref, jnp.minimum(s, n_tiles - 1) % npb) if split_ctx else x_ref[...]
    hb_next = (_rms(x_tile, gain) + mod_ref[0:1, :]).astype(BF16)
    _project_tile(hb_ref[...], has_prev, has_next, *project_refs)
    hb_ref[...] = hb_next


def _stream_specs(xs, tm, tile_of_step, npb):
    bj = lambda s: (tile_of_step(s) // npb, tile_of_step(s) % npb)
    if isinstance(xs, tuple):
        ctx, x = xs
        D = x.shape[2]
        return [pl.BlockSpec((None, tm, D), lambda s: (bj(s)[0], 0, 0)),
                pl.BlockSpec((None, tm, D), lambda s: (bj(s)[0], jnp.maximum(bj(s)[1] - 1, 0), 0))], [ctx, x]
    return [pl.BlockSpec((None, tm, xs.shape[2]), lambda s: (bj(s)[0], bj(s)[1], 0))], [xs]


def _stream_shape(xs):
    if isinstance(xs, tuple):
        return xs[1].shape[0], xs[0].shape[1] + xs[1].shape[1], xs[1].shape[2]
    return xs.shape


def _project_call_parts(B, S, D, tm, npb, mid, old, proj, layer):
    wt, wk, tables, conv_b, pool_w_bd, pool_scale = proj
    cos_k, sin_k, cos_t, sin_t, inv_cnt = tables
    c2 = lambda s: (0, 0)
    cpt = tm // CHUNK
    in_specs = [
        pl.BlockSpec((None, R_END, D), lambda s: (layer, 0, 0), pipeline_mode=pl.Buffered(1)),
        pl.BlockSpec((None, D, T_END), lambda s: (layer, 0, 0), pipeline_mode=pl.Buffered(1)),
        pl.BlockSpec((tm, KV_W), lambda s: (mid(s) % npb, 0)),
        pl.BlockSpec((tm, KV_W), lambda s: (mid(s) % npb, 0)),
        pl.BlockSpec((HEAD_DIM, tm), lambda s: (0, mid(s) % npb)),
        pl.BlockSpec((HEAD_DIM, tm), lambda s: (0, mid(s) % npb)),
        pl.BlockSpec((3, 2 * GROUP_W, tm), lambda s: (0, 0, 0)),
        pl.BlockSpec((GROUP_W, GROUP_W), c2),
        pl.BlockSpec((1, GROUP_W), c2),
        pl.BlockSpec((tm, GROUP_W), lambda s: (old(s) % npb, 0)),
    ]
    args = [wt, wk, cos_k, sin_k, cos_t, sin_t, conv_b, pool_w_bd, pool_scale, inv_cnt]

    def fm(rows, dt, tile):
        return (pl.BlockSpec((None, cpt, rows, CHUNK), lambda s: (tile(s) // npb, tile(s) % npb, 0, 0)),
                jax.ShapeDtypeStruct((B, S // CHUNK, rows, CHUNK), dt))

    def tk(cols, dt, tile):
        return (pl.BlockSpec((None, tm, cols), lambda s: (tile(s) // npb, tile(s) % npb, 0)),
                jax.ShapeDtypeStruct((B, S, cols), dt))

    outs = [fm(2 * GROUP_W, BF16, mid), fm(2 * GROUP_W, BF16, old), fm(2 * GROUP_W, BF16, mid),
            fm(4 * N_HEADS, F32, mid), fm(GROUP_W, BF16, mid), fm(KV_W, BF16, mid),
            tk(KV_W, BF16, mid), tk(GROUP_W, BF16, old)]
    scratch = [pltpu.VMEM((2 * GROUP_W, tm), F32), pltpu.VMEM((2 * GROUP_W, LANES), F32),
               pltpu.VMEM((tm, GROUP_W), F32), pltpu.VMEM((HALO, GROUP_W), F32)]
    return in_specs, args, [o[0] for o in outs], [o[1] for o in outs], scratch


def _in_proj(xs, mods, g0, proj, layer, n_batch_rows):
    B, S, D = _stream_shape(xs)
    tm = ROW_TILE
    npb = S // tm
    n_tiles = B * npb
    cur = lambda s: jnp.minimum(s, n_tiles - 1)
    mid = lambda s: jnp.clip(s - 1, 0, n_tiles - 1)
    old = lambda s: jnp.maximum(s - 2, 0)
    kern = functools.partial(_in_kernel, npb=npb, n_tiles=n_tiles, split_ctx=isinstance(xs, tuple))
    x_specs, x_args = _stream_specs(xs, tm, cur, npb)
    p_specs, p_args, out_specs, out_shapes, p_scratch = _project_call_parts(B, S, D, tm, npb, mid, old, proj, layer)
    return pl.pallas_call(
        kern,
        grid=(n_tiles + 2,),
        in_specs=x_specs + [
            pl.BlockSpec((None, None, 6, D),
                         lambda s: (layer, jnp.where(cur(s) % npb == 0, n_batch_rows, cur(s) // npb), 0, 0)),
            pl.BlockSpec((1, D), lambda s: (0, 0)),
        ] + p_specs,
        out_specs=out_specs,
        out_shape=out_shapes,
        scratch_shapes=p_scratch + [pltpu.VMEM((tm, D), BF16)],
        compiler_params=pltpu.CompilerParams(dimension_semantics=("arbitrary",), vmem_limit_bytes=VMEM_LIMIT),
        name="in_proj",
    )(*x_args, mods, g0, *p_args)


def _log_sigmoid(x):
    return jnp.minimum(x, 0.0) - jnp.log1p(jnp.exp(-jnp.abs(x)))


CPS = 2


def _pair_spec(n_b, rows, pair=lambda p: p, row_block=0):
    return pl.BlockSpec((n_b, CPS, rows, CHUNK), lambda p: (0, pair(p), row_block, 0))


def _mixers_kernel(zs_ref, sw_ref, sb_ref,
                   q_ref, kp_ref, kc_ref, kn_ref, kx_ref, vp_ref, vc_ref, vn_ref, vx_ref, sink_ref, mask0_ref, mask1_ref,
                   qkf_ref, vf_ref, gf_ref, qkb_ref, vb_ref, gb_ref, gbias_ref,
                   a_ref, c_ref, hf_ref, hb_ref, st_ref, m_ref, sc_ref, p_ref, *, n_ctx_chunks):
    n_b = zs_ref.shape[0]
    _zero_at_first_step((st_ref, m_ref))

    group = N_HEADS // KV_HEADS
    zeros = jnp.zeros((HEAD_DIM, group * CHUNK), BF16)
    pairs = [(u, b, kv) for u in range(CPS) for b in range(n_b) for kv in range(KV_HEADS)]
    mask_refs = (mask0_ref, mask1_ref)

    def keys(u, b):
        local = ([kp_ref[b, CHUNK:2 * CHUNK, :], kc_ref[b]] if u == 0 else [kc_ref[b], kn_ref[b, 0:CHUNK, :]])
        return jnp.concatenate(local + [kx_ref[b]], axis=0)

    def values(u, b):
        local = ([vp_ref[b, 1], vc_ref[b, 0], vc_ref[b, 1]] if u == 0 else [vc_ref[b, 0], vc_ref[b, 1], vn_ref[b, 0]])
        return jnp.concatenate(local + [vx_ref[b, i] for i in range(n_ctx_chunks)], axis=1)

    for i, (u, b, kv) in enumerate(pairs):
        q2 = jnp.concatenate([q_ref[b, u, (kv * group + g) * HEAD_DIM:(kv * group + g + 1) * HEAD_DIM, :]
                              for g in range(group)], axis=1)
        qm = jnp.concatenate([q2, zeros] if kv == 0 else [zeros, q2], axis=0)
        sc_ref[i] = _dot(keys(u, b), qm)

    s_i = lax.broadcasted_iota(jnp.int32, (CHUNK, CHUNK), 0)
    t_i = lax.broadcasted_iota(jnp.int32, (CHUNK, CHUNK), 1)
    causal = {True: s_i <= t_i, False: s_i >= t_i}
    ones_row = jnp.where(lax.broadcasted_iota(jnp.int32, (ST_ROWS - HEAD_DIM, CHUNK), 0) == 0, 1.0, 0.0).astype(BF16)
    all_ones = jnp.ones((CHUNK, CHUNK), F32)

    def start_chains(t):
        chains = []
        for b in range(n_b):
            for forward, qk_ref, v_ref, g_ref, o_ref in ((True, qkf_ref, vf_ref, gf_ref, hf_ref),
                                                          (False, qkb_ref, vb_ref, gb_ref, hb_ref)):
                u = t if forward else CPS - 1 - t
                gates = g_ref[b, u] + gbias_ref[...]
                li8 = gates[0:2 * N_HEADS] * LOG2E
                lf8 = _log_sigmoid(gates[2 * N_HEADS:4 * N_HEADS]) * LOG2E
                cum8 = _dot_hi(lf8, jnp.where(causal[forward], 1.0, 0.0).astype(F32))
                tot8 = _dot_hi(lf8, all_ones)
                a8 = li8 - cum8
                for h in range(N_HEADS):
                    r = (0 if forward else N_HEADS) + h
                    ch = dict(b=b, u=u, h=h, mask=causal[forward], o_ref=o_ref,
                              slot=(b * 2 + (0 if forward else 1)) * N_HEADS + h,
                              a=a8[r:r + 1], bcum=cum8[r:r + 1], b_end=tot8[r:r + 1],
                              q=qk_ref[b, u, h * HEAD_DIM:(h + 1) * HEAD_DIM, :],
                              k=qk_ref[b, u, GROUP_W + h * HEAD_DIM:GROUP_W + (h + 1) * HEAD_DIM, :],
                              v=v_ref[b, u, h * HEAD_DIM:(h + 1) * HEAD_DIM, :])
                    ch["kq"] = _dot_tn(ch["k"], ch["q"])
                    chains.append(ch)
        return chains

    def read_state(chains, prev=None):
        for i, ch in enumerate(chains):
            ch["st"] = st_ref[ch["slot"]][:, 0:HEAD_DIM] if prev is None else prev[i]["st_new"]
            ch["m"] = m_ref[ch["slot"]:ch["slot"] + 1, :] if prev is None else prev[i]["m_new"]
            ch["cq"] = _dot(ch["st"].astype(BF16), ch["q"])

    def gate_math(chains):
        for ch in chains:
            a, m, mask = ch["a"], ch["m"], ch["mask"]
            a_col = jnp.broadcast_to(a, (CHUNK, CHUNK)).T
            a_max = jnp.max(a, axis=1, keepdims=True)
            big_m = jnp.maximum(jnp.max(jnp.where(mask, a_col, NEG), axis=0, keepdims=True), m)
            ch["inter"] = jnp.exp2(m - big_m)
            ch["floor"] = jnp.exp2(-(ch["bcum"] + big_m))
            s = ch["kq"] * jnp.exp2(jnp.where(mask, a_col - big_m, NEG))
            ch["rowsum"] = jnp.sum(s, axis=0, keepdims=True)
            ch["s"] = s.astype(BF16)
            m_new = jnp.maximum(ch["b_end"] + m, ch["b_end"] + a_max)
            ch["decay"] = jnp.exp2(ch["b_end"] + m - m_new)
            ch["kw"] = (ch["k"].astype(F32) * jnp.exp2(ch["b_end"] + a - m_new)).astype(BF16)
            ch["m_new"] = m_new

    def finish(chains):
        for ch in chains:
            ch["vs"] = _dot(ch["v"], ch["s"])
            v_ext = jnp.concatenate([ch["v"], ones_row], axis=0)
            ch["upd"] = _dot_nt(v_ext, ch["kw"])
        for ch in chains:
            h, cq, inter = ch["h"], ch["cq"], ch["inter"]
            num = cq[0:HEAD_DIM] * inter + ch["vs"]
            den = inter * cq[HEAD_DIM:HEAD_DIM + 1] + ch["rowsum"]
            ch["o_ref"][ch["b"], ch["u"], h * HEAD_DIM:(h + 1) * HEAD_DIM, :] = (
                num / jnp.maximum(jnp.abs(den), ch["floor"]))
            ch["st_new"] = ch["st"] * ch["decay"][:, 0:HEAD_DIM] + ch["upd"]

    first = start_chains(0)
    read_state(first)
    second = start_chains(1)

    for u in range(CPS):
        for b in range(n_b):
            g = jax.nn.gelu(zs_ref[b, u].astype(F32))
            for h in range(N_HEADS):
                gu = g[h * HEAD_DIM:(h + 1) * HEAD_DIM]
                vh = _head_layer_norm_t(g[GROUP_W + h * HEAD_DIM:GROUP_W + (h + 1) * HEAD_DIM]).astype(BF16)
                mixed = _dot(vh, sw_ref[h]) + sb_ref[h:h + 1, :]
                a_ref[b, u, h * HEAD_DIM:(h + 1) * HEAD_DIM, :] = (gu * mixed).astype(BF16)

    p_sinks = []
    for i, (u, b, kv) in enumerate(pairs):
        sink = sink_ref[kv:kv + 1, :]
        m = jnp.maximum(jnp.max(sc_ref[i] + mask_refs[u][...], axis=0, keepdims=True), sink)
        for r in range(0, sc_ref.shape[1], CHUNK):
            rows = slice(r, r + CHUNK)
            p_ref[i, rows, :] = jnp.exp2(sc_ref[i, rows, :] + mask_refs[u][rows, :] - m).astype(BF16)
        p_sinks.append(jnp.exp2(sink - m))

    gate_math(first)
    finish(first)

    ones_keys = jnp.where(lax.broadcasted_iota(jnp.int32, (BF16_ROWS, (3 + n_ctx_chunks) * CHUNK), 0) == 0,
                          1.0, 0.0).astype(BF16)
    for i, ((u, b, kv), p_sink) in enumerate(zip(pairs, p_sinks)):
        o_ext = _dot(jnp.concatenate([values(u, b)[kv * HEAD_DIM:(kv + 1) * HEAD_DIM, :], ones_keys], axis=0),
                     p_ref[i])
        o = o_ext[0:HEAD_DIM] / (o_ext[HEAD_DIM:HEAD_DIM + 1] + p_sink)
        for g in range(group):
            h = kv * group + g
            c_ref[b, u, h * HEAD_DIM:(h + 1) * HEAD_DIM, :] = o[:, g * CHUNK:(g + 1) * CHUNK].astype(BF16)

    read_state(second, prev=first)
    gate_math(second)
    finish(second)
    for ch in second:
        st_ref[ch["slot"], :, 0:HEAD_DIM] = ch["st_new"]
        m_ref[ch["slot"]:ch["slot"] + 1, :] = ch["m_new"]


def _attention_masks(n_ctx_chunks):
    group = N_HEADS // KV_HEADS
    n_keys = (3 + n_ctx_chunks) * CHUNK
    j = jnp.arange(n_keys)[:, None]
    i = (jnp.arange(group * CHUNK) & (CHUNK - 1))[None, :]
    masks = []
    for cur_ok, prev_ok, next_ok in ((False, False, False), (True, True, True), (True, False, True),
                                     (True, True, False), (True, False, False)):
        lo = (0 if prev_ok else CHUNK) if cur_ok else 3 * CHUNK
        hi = 3 * CHUNK if next_ok else 2 * CHUNK
        valid = ((j >= i) & (j <= i + 2 * CHUNK) & (j >= lo) & (j < hi)) | (j >= 3 * CHUNK)
        masks.append(jnp.where(valid, 0.0, NEG).astype(F32))
    return jnp.stack(masks)


def _mixers(zs, sgu_wt, sgu_b, aq, ak, av, sink_rows, masks, qk, vo, gt, gate_bias, n_ctx_chunks):
    B, n_chunks = zs.shape[:2]
    assert n_chunks % CPS == 0 and n_ctx_chunks == CPS
    n_pairs = n_chunks // CPS
    ctx_w = n_ctx_chunks * CHUNK
    group = N_HEADS // KV_HEADS
    n_keys = 3 * CHUNK + ctx_w
    kern = functools.partial(_mixers_kernel, n_ctx_chunks=n_ctx_chunks)
    ident = lambda p: p
    prev = lambda p: jnp.maximum(p - 1, 0)
    nxt = lambda p: jnp.minimum(p + 1, n_pairs - 1)
    bwd = lambda p: jnp.where(p == 0, 0, n_pairs - p)
    kspec = lambda f: pl.BlockSpec((B, CPS * CHUNK, KV_W), lambda p: (0, f(p), 0))
    vspec = lambda f: _pair_spec(B, KV_W, f)

    def mask_spec(u):
        def kind(p):
            c = CPS * p + u
            no_prev = (c == n_ctx_chunks).astype(jnp.int32)
            no_next = (c == n_chunks - 1).astype(jnp.int32)
            return jnp.where(c < n_ctx_chunks, 0, 1 + no_prev + 2 * no_next)
        return pl.BlockSpec((None, n_keys, group * CHUNK), lambda p: (kind(p), 0, 0))

    def scan_specs(order):
        return [_pair_spec(B, 2 * GROUP_W, order), _pair_spec(B, GROUP_W, order),
                _pair_spec(B, 4 * N_HEADS, order)]

    n_slots = B * 2 * N_HEADS
    return pl.pallas_call(
        kern,
        grid=(n_pairs,),
        in_specs=[
            _pair_spec(B, 2 * GROUP_W),
            pl.BlockSpec((N_HEADS, CHUNK, CHUNK), lambda p: (0, 0, 0)),
            pl.BlockSpec((N_HEADS, CHUNK), lambda p: (0, 0)),
            _pair_spec(B, GROUP_W),
            kspec(prev), kspec(ident), kspec(nxt), pl.BlockSpec((B, ctx_w, KV_W), lambda p: (0, 0, 0)),
            vspec(prev), vspec(ident), vspec(nxt),
            pl.BlockSpec((B, n_ctx_chunks, KV_W, CHUNK), lambda p: (0, 0, 0, 0)),
            pl.BlockSpec((SUBLANES, group * CHUNK), lambda p: (0, 0)),
            mask_spec(0), mask_spec(1),
        ] + scan_specs(ident) + scan_specs(bwd) + [pl.BlockSpec((4 * N_HEADS, CHUNK), lambda p: (0, 0))],
        out_specs=[_pair_spec(B, GROUP_W), _pair_spec(B, GROUP_W),
                   _pair_spec(B, GROUP_W, ident), _pair_spec(B, GROUP_W, bwd)],
        out_shape=[jax.ShapeDtypeStruct((B, n_chunks, GROUP_W, CHUNK), BF16)] * 2
                  + [jax.ShapeDtypeStruct((B, n_chunks, GROUP_W, CHUNK), F32)] * 2,
        scratch_shapes=[
            pltpu.VMEM((n_slots, ST_ROWS, LANES), F32),
            pltpu.VMEM((n_slots, LANES), F32),
            pltpu.VMEM((CPS * B * KV_HEADS, n_keys, group * CHUNK), F32),
            pltpu.VMEM((CPS * B * KV_HEADS, n_keys, group * CHUNK), BF16),
        ],
        compiler_params=pltpu.CompilerParams(dimension_semantics=("arbitrary",)),
        name="mixers",
    )(zs, sgu_wt, sgu_b, aq, ak, ak, ak, ak, av, av, av, av, sink_rows, masks, masks, qk, vo, gt, qk, vo, gt,
      gate_bias)


N_OUT_SCRATCH = 6


def _hidden_chunks(hidden):
    step = 3 * 2 * LANES
    return [(lo, min(lo + step, hidden)) for lo in range(0, hidden, step)]


def _out_kernel(*refs, hidden, npb, n_tiles, split_ctx):
    _zero_at_first_step(refs[-N_OUT_SCRATCH:-1])
    ctx_ref = refs[0] if split_ctx else None
    (x_ref, a_ref, hf_ref, hb_ref, zo_ref, c_ref, d_ref, moda_ref, modc_ref, g_ref, mg_ref,
     wo_ref, wi_ref, wf_ref, o_ref, x1_ref, y_ref, f_ref, tb_ref, act_ref, bm_ref) = refs[1:] if split_ctx else refs
    s = pl.program_id(0)
    f_ref[s % 2] = _dot(act_ref[...], wf_ref[...])

    x_tile = (_first_tile_from_ctx(ctx_ref, x_ref, jnp.clip(s - 1, 0, n_tiles - 1) % npb)
              if split_ctx else x_ref[...])
    x1 = x_tile + _rms(y_ref[...], moda_ref[2:3, :] * g_ref[1:2, :])
    x1_ref[s % 3] = x1
    tb_ref[...] = (_rms(x1, g_ref[2:3, :] * (1.0 + moda_ref[4:5, :])) + moda_ref[3:4, :]).astype(BF16)

    o_ref[...] = x1_ref[(s + 1) % 3] + _rms(f_ref[(s + 1) % 2], modc_ref[5:6, :] * g_ref[3:4, :])

    for lo, hi in _hidden_chunks(hidden):
        gate = _dot(tb_ref[...], wi_ref[:, lo:hi])
        up = _dot(tb_ref[...], wi_ref[:, hidden + lo:hidden + hi])
        act_ref[:, lo:hi] = (_silu(gate) * up).astype(BF16)

    hs = _get_chunks(hf_ref) + _get_chunks(hb_ref)
    hn = jnp.concatenate([_head_layer_norm_t(hs[h * HEAD_DIM:(h + 1) * HEAD_DIM]) for h in range(N_HEADS)], axis=0)
    bm_ref[...] = (hn * mg_ref[...] * jax.nn.sigmoid(_get_chunks(zo_ref).astype(F32))).astype(BF16)
    y_ref[...] = (_dot_tn(_get_chunks(a_ref), wo_ref[0:GROUP_W, :])
                  + _dot_tn(bm_ref[...], wo_ref[GROUP_W:2 * GROUP_W, :])
                  + _dot_tn(_get_chunks(c_ref), wo_ref[2 * GROUP_W:3 * GROUP_W, :])
                  + _dot(d_ref[...], wo_ref[3 * GROUP_W:4 * GROUP_W, :]))


def _out_ffn(xs, a, hf, hb, vo, c, d, mods, norm_g, mnorm_g, w_out, w_ffn_in, w_ffn_out, layer, n_batch_rows,
             drop_ctx):
    B, S, D = _stream_shape(xs)
    tm = ROW_TILE
    npb = S // tm
    n_tiles = B * npb
    hidden = w_ffn_out.shape[1]
    cur = lambda s: jnp.minimum(s, n_tiles - 1)
    mid = lambda s: jnp.clip(s - 1, 0, n_tiles - 1)
    prv = lambda s: jnp.clip(s - 3, 0, n_tiles - 1)
    row = lambda s: (cur(s) // npb, cur(s) % npb, 0)
    mod_row = lambda t: (layer, jnp.where(t % npb == 0, n_batch_rows, t // npb), 0, 0)
    c2 = lambda s: (0, 0)
    this_layer = lambda s: (layer, 0, 0)
    cpt = tm // CHUNK
    fm = pl.BlockSpec((None, cpt, GROUP_W, CHUNK), lambda s: (cur(s) // npb, cur(s) % npb, 0, 0))
    split_ctx = isinstance(xs, tuple)
    x_specs, x_args = _stream_specs(xs, tm, mid, npb)
    in_specs = x_specs + [
        fm, fm, fm,
        pl.BlockSpec((None, cpt, GROUP_W, CHUNK), lambda s: (cur(s) // npb, cur(s) % npb, 1, 0)),
        fm,
        pl.BlockSpec((None, tm, GROUP_W), row),
        pl.BlockSpec((None, None, 6, D), lambda s: mod_row(mid(s))),
        pl.BlockSpec((None, None, 6, D), lambda s: mod_row(prv(s))),
        pl.BlockSpec((None, 4, D), this_layer),
        pl.BlockSpec((GROUP_W, tm), c2),
        pl.BlockSpec((None, D, D), this_layer, pipeline_mode=pl.Buffered(1)),
        pl.BlockSpec((None, D, 2 * hidden), this_layer, pipeline_mode=pl.Buffered(1)),
        pl.BlockSpec((None, hidden, D), this_layer, pipeline_mode=pl.Buffered(1)),
    ]
    args = x_args + [a, hf, hb, vo, c, d, mods, mods, norm_g, mnorm_g, w_out, w_ffn_in, w_ffn_out]
    scratch = [pltpu.VMEM((3, tm, D), F32), pltpu.VMEM((tm, D), F32), pltpu.VMEM((2, tm, D), F32),
               pltpu.VMEM((tm, D), BF16), pltpu.VMEM((tm, hidden), BF16), pltpu.VMEM((GROUP_W, tm), BF16)]
    params = pltpu.CompilerParams(dimension_semantics=("arbitrary",), vmem_limit_bytes=VMEM_LIMIT)
    static = dict(hidden=hidden, npb=npb, n_tiles=n_tiles, split_ctx=split_ctx)
    if drop_ctx:
        out_spec = pl.BlockSpec((None, tm, D), lambda s: (prv(s) // npb, jnp.maximum(prv(s) % npb - 1, 0), 0))
        out_shape = jax.ShapeDtypeStruct((B, S - tm, D), F32)
    else:
        out_spec = pl.BlockSpec((None, tm, D), lambda s: (prv(s) // npb, prv(s) % npb, 0))
        out_shape = jax.ShapeDtypeStruct((B, S, D), F32)
    return pl.pallas_call(
        functools.partial(_out_kernel, **static),
        grid=(n_tiles + 3,),
        in_specs=in_specs,
        out_specs=out_spec,
        out_shape=out_shape,
        scratch_shapes=scratch, compiler_params=params, name="out_ffn",
    )(*args)


def _rope_tables(n_tokens, ctx_len):
    rows = n_tokens // GRID_W
    axis_freq = HEAD_DIM // 4
    inv = jnp.power(ROPE_BASE, -jnp.arange(axis_freq, dtype=F32) * 2.0 / (2 * axis_freq))
    ar = jnp.arange(rows, dtype=F32)[:, None] * inv
    ac = jnp.arange(GRID_W, dtype=F32)[:, None] * inv

    def per_token(fn):
        r = jnp.broadcast_to(fn(ar)[:, None, :], (rows, GRID_W, axis_freq))
        c = jnp.broadcast_to(fn(ac)[None, :, :], (rows, GRID_W, axis_freq))
        return jnp.concatenate([r, r, c, c], axis=-1).reshape(n_tokens, HEAD_DIM)

    cos, sin = per_token(jnp.cos), per_token(jnp.sin)
    sign = jnp.where((jnp.arange(HEAD_DIM) % 32) < 16, -1.0, 1.0).astype(F32)
    cos = jnp.concatenate([jnp.ones((ctx_len, HEAD_DIM), F32), cos], axis=0)
    sin = jnp.concatenate([jnp.zeros((ctx_len, HEAD_DIM), F32), sin * sign], axis=0)
    inv_cnt = []
    for seg in (ctx_len, n_tokens):
        t = jnp.arange(seg)[:, None]
        half = jnp.asarray([w // 2 for w in POOL_WINDOWS])[None, :]
        cnt = jnp.minimum(t + half, seg) - jnp.maximum(t - half, 0)
        inv_cnt.append(jnp.repeat(1.0 / cnt.astype(F32), HEAD_DIM, axis=1))
    q_scale = HEAD_DIM ** -0.5 * LOG2E
    return (jnp.tile(cos, (1, KV_HEADS)), jnp.tile(sin, (1, KV_HEADS)), cos.T * q_scale, sin.T * q_scale,
            jnp.concatenate(inv_cnt, axis=0))


def _arrange_w_in(w_in):
    ml0 = 2 * GROUP_W
    gate0 = ml0 + 4 * GROUP_W
    att0 = gate0 + 4 * N_HEADS
    pool0 = att0 + GROUP_W + 2 * KV_W
    gates = w_in[:, gate0:att0].reshape(-1, 2, 2, N_HEADS)
    gates = gates.transpose(0, 2, 1, 3).reshape(-1, 4 * N_HEADS)
    feat = jnp.concatenate([
        w_in[:, 0:ml0],
        w_in[:, ml0:ml0 + 2 * GROUP_W],
        w_in[:, ml0 + 2 * GROUP_W:gate0],
        w_in[:, att0:att0 + GROUP_W],
        w_in[:, att0 + GROUP_W + KV_W:pool0],
        gates,
    ], axis=1)
    tok = jnp.concatenate([w_in[:, att0 + GROUP_W:att0 + GROUP_W + KV_W], w_in[:, pool0:pool0 + GROUP_W]], axis=1)
    return feat.T.astype(BF16), tok.astype(BF16)


def _block_diag(w):
    g = w.shape[0]
    eye = jnp.eye(g, dtype=w.dtype)
    return (eye[:, None, :, None] * w[:, :, None, :]).reshape(g * w.shape[1], g * w.shape[2])


def kernel(x, c, ctx, c_ctx, w_mod, b_mod, norm_g, w_in, w_out, sgu_w, sgu_b, mlstm_conv_w, mlstm_gate_b,
           mlstm_norm_g, attn_sink, pool_w, pool_scale, w_ffn_in, w_ffn_out):
    B, N, D = x.shape
    ctx_len = ctx.shape[1]
    depth = w_mod.shape[0]
    assert D == 4 * GROUP_W and N % ROW_TILE == 0 and ctx_len == ROW_TILE
    assert w_in.shape[2] == 2 * GROUP_W + 4 * GROUP_W + 4 * N_HEADS + GROUP_W + 2 * KV_W + GROUP_W
    n_ctx_chunks = ctx_len // CHUNK

    xs = (ctx, x)
    rows = -(-(B + 1) // SUBLANES) * SUBLANES
    cc = jnp.zeros((rows, D), F32).at[0:B].set(c).at[B].set(c_ctx)
    mods = _modulation(cc, w_mod, b_mod).reshape(depth, rows, 6, D)
    tables = _rope_tables(N, ctx_len)
    masks = _attention_masks(n_ctx_chunks)

    wt, wk = jax.vmap(_arrange_w_in)(w_in)
    w_out_b, w_ffn_in_b, w_ffn_out_b = w_out.astype(BF16), w_ffn_in.astype(BF16), w_ffn_out.astype(BF16)

    def layer_proj(l):
        conv_b = jnp.broadcast_to(mlstm_conv_w[l][:, :, None], (3, 2 * GROUP_W, ROW_TILE))
        return (wt, wk, tables, conv_b, _block_diag(pool_w[l]).astype(BF16), pool_scale[l].reshape(1, GROUP_W))

    for l in range(depth):
        zs, qk, vo, gt, aq, av, ak, dm = _in_proj(xs, mods, norm_g[l, 0:1], layer_proj(l), l, B)
        gate_bias = jnp.broadcast_to(
            mlstm_gate_b[l].reshape(2, 2, N_HEADS).transpose(1, 0, 2).reshape(4 * N_HEADS, 1), (4 * N_HEADS, CHUNK))
        sink_rows = jnp.zeros((SUBLANES, (N_HEADS // KV_HEADS) * CHUNK), F32).at[0:KV_HEADS].set(
            jnp.repeat(attn_sink[l].reshape(KV_HEADS, N_HEADS // KV_HEADS) * LOG2E, CHUNK, axis=1))
        mnorm_b = jnp.broadcast_to(mlstm_norm_g[l][:, None], (GROUP_W, ROW_TILE))
        a, cm, hf, hb = _mixers(zs, sgu_w[l].transpose(0, 2, 1).astype(BF16), sgu_b[l], aq, ak, av, sink_rows, masks,
                                qk, vo, gt, gate_bias, n_ctx_chunks)
        xs = _out_ffn(xs, a, hf, hb, vo, cm, dm, mods, norm_g, mnorm_b, w_out_b, w_ffn_in_b, w_ffn_out_b, l, B,
                      drop_ctx=(l == depth - 1))
    return xs
```

```python
import functools

import jax
import jax.numpy as jnp
from jax import lax
from jax.experimental import pallas as pl
from jax.experimental.pallas import tpu as pltpu

F32 = jnp.float32
BF16 = jnp.bfloat16

GRID_W = 64
ROPE_BASE = 10000.0
EPS = 1e-6
LANES = 128
SUBLANES = 8
BF16_ROWS = 16
CHUNK = 128
HEAD_DIM = 64
N_HEADS = 4
KV_HEADS = 2
GROUP_W = N_HEADS * HEAD_DIM
KV_W = KV_HEADS * HEAD_DIM
POOL_WINDOWS = (2, 4, 8, 16)
HALO = 8
ROW_TILE = 256
NEG = -1e30
VMEM_LIMIT = 56 * 1024 * 1024

R_SGU = 0
R_QK = 512
R_VO = 1024
R_AQ = 1536
R_AV = 1792
R_GATE = 1920
R_END = 1936
T_AK = 0
T_POOL = KV_W
T_END = KV_W + GROUP_W

ST_ROWS = 80
LOG2E = 1.4426950408889634

HI = lax.Precision.HIGHEST


def _dot(a, b):
    return jnp.dot(a, b, preferred_element_type=F32)


def _dot_hi(a, b):
    return jnp.dot(a, b, preferred_element_type=F32, precision=HI)


def _dot_nt(a, b):
    return lax.dot_general(a, b, (((1,), (1,)), ((), ())), preferred_element_type=F32)


def _dot_tn(a, b):
    return lax.dot_general(a, b, (((0,), (0,)), ((), ())), preferred_element_type=F32)


def _rms(x, g):
    return x * lax.rsqrt(jnp.mean(x * x, axis=-1, keepdims=True) + EPS) * g


def _silu(x):
    return x * jax.nn.sigmoid(x)


def _put_chunks(ref, x):
    for i in range(ref.shape[0]):
        ref[i] = x[:, i * CHUNK:(i + 1) * CHUNK].astype(ref.dtype)


def _get_chunks(ref):
    return jnp.concatenate([ref[i] for i in range(ref.shape[0])], axis=1)


def _head_layer_norm_t(v):
    mu = jnp.mean(v, axis=0, keepdims=True)
    vc = v - mu
    return vc * lax.rsqrt(jnp.mean(vc * vc, axis=0, keepdims=True) + EPS)


def _zero_at_first_step(refs):
    @pl.when(pl.program_id(0) == 0)
    def _():
        for ref in refs:
            ref[...] = jnp.zeros_like(ref)


def _mod_kernel(c_ref, w_ref, b_ref, o_ref):
    o_ref[...] = _dot(_silu(c_ref[...]).astype(BF16), w_ref[...].astype(BF16)) + b_ref[...]


def _modulation(cc, w_mod, b_mod):
    depth, d, n6 = w_mod.shape
    rows = cc.shape[0]
    tn = 1536
    return pl.pallas_call(
        _mod_kernel,
        grid=(depth, n6 // tn),
        in_specs=[
            pl.BlockSpec((rows, d), lambda l, j: (0, 0)),
            pl.BlockSpec((None, d, tn), lambda l, j: (l, 0, j)),
            pl.BlockSpec((None, 1, tn), lambda l, j: (l, 0, j)),
        ],
        out_specs=pl.BlockSpec((None, rows, tn), lambda l, j: (l, 0, j)),
        out_shape=jax.ShapeDtypeStruct((depth, rows, n6), F32),
        compiler_params=pltpu.CompilerParams(
            dimension_semantics=("arbitrary", "arbitrary"), vmem_limit_bytes=VMEM_LIMIT),
        name="modulation",
    )(cc, w_mod, b_mod.reshape(depth, 1, n6))


N_IN_SCRATCH = 4


def _first_tile_from_ctx(ctx_ref, x_ref, tile):
    rows = lax.broadcasted_iota(jnp.int32, x_ref.shape, 0)
    return jnp.where(rows < jnp.where(tile == 0, x_ref.shape[0], 0), ctx_ref[...], x_ref[...])


def _project_tile(hb, has_prev, has_next, wt_ref, wk_ref, cos_ref, sin_ref, cost_ref, sint_ref,
                  conv_ref, pw_ref, ps_ref, invc_ref,
                  zs_ref, qk_ref, vo_ref, gt_ref, aq_ref, av_ref, ak_ref, d_ref, z_ref, pcol_ref, zp_ref, prow_ref):
    tm = hb.shape[0]
    zc = _dot_nt(wt_ref[R_QK:R_VO, :], hb)
    _put_chunks(zs_ref, _dot_nt(wt_ref[R_SGU:R_QK, :], hb))

    z = z_ref[...]
    lane = lax.broadcasted_iota(jnp.int32, z.shape, 1)
    before = jnp.where(lane == 0, pcol_ref[:, 0:1] * has_prev, pltpu.roll(z, 1, axis=1))
    after = jnp.where(lane == tm - 1, zc[:, 0:1] * has_next, pltpu.roll(z, tm - 1, axis=1))
    qk = _silu(before * conv_ref[0] + z * conv_ref[1] + after * conv_ref[2])
    _put_chunks(qk_ref, jnp.concatenate([qk[0:GROUP_W], qk[GROUP_W:2 * GROUP_W] * (HEAD_DIM ** -0.5)], axis=0))
    pcol_ref[...] = jnp.broadcast_to(z[:, tm - 1:tm], pcol_ref.shape)
    z_ref[...] = zc

    zp_new = _dot(hb, wk_ref[:, T_POOL:T_END])
    _put_chunks(vo_ref, _dot_nt(wt_ref[R_VO:R_AQ, :], hb))
    zp = zp_ref[...]
    rows = tm + 2 * HALO
    ext = jnp.concatenate([prow_ref[...] * has_prev, zp, zp_new[0:HALO] * has_next], axis=0)
    s2 = ext + pltpu.roll(ext, 1, axis=0)
    s4 = pltpu.roll(s2, 1, axis=0) + pltpu.roll(s2, rows - 1, axis=0)
    s8 = pltpu.roll(s4, 2, axis=0) + pltpu.roll(s4, rows - 2, axis=0)
    s16 = pltpu.roll(s8, 4, axis=0) + pltpu.roll(s8, rows - 4, axis=0)
    sums = [s[HALO:HALO + tm] for s in (s2, s4, s8, s16)]
    plane = lax.broadcasted_iota(jnp.int32, zp.shape, 1)
    pooled = sums[-1]
    for gi in range(len(POOL_WINDOWS) - 2, -1, -1):
        pooled = jnp.where(plane < (gi + 1) * HEAD_DIM, sums[gi], pooled)
    pooled = pooled * invc_ref[...]
    d_ref[...] = (_dot((pooled - zp).astype(BF16), pw_ref[...]) * ps_ref[...]).astype(BF16)
    prow_ref[...] = zp[tm - HALO:tm]
    zp_ref[...] = zp_new

    q = _dot_nt(wt_ref[R_AQ:R_AV, :], hb)
    k = _dot(hb, wk_ref[:, T_AK:T_POOL])
    _put_chunks(av_ref, _dot_nt(wt_ref[R_AV:R_GATE, :], hb))
    _put_chunks(gt_ref, _dot_nt(wt_ref[R_GATE:R_END, :], hb))
    cos_t = jnp.concatenate([cost_ref[...]] * N_HEADS, axis=0)
    sin_t = jnp.concatenate([sint_ref[...]] * N_HEADS, axis=0)
    row = lax.broadcasted_iota(jnp.int32, q.shape, 0)
    rot_q = jnp.where((row & 31) < 16, pltpu.roll(q, GROUP_W - 16, axis=0), pltpu.roll(q, 16, axis=0))
    _put_chunks(aq_ref, q * cos_t + rot_q * sin_t)
    klane = lax.broadcasted_iota(jnp.int32, k.shape, 1)
    rot_k = jnp.where((klane & 31) < 16, pltpu.roll(k, KV_W - 16, axis=1), pltpu.roll(k, 16, axis=1))
    ak_ref[...] = (k * cos_ref[...] + rot_k * sin_ref[...]).astype(BF16)


def _segment_flags(tile, npb):
    j = tile % npb
    return (j >= 2).astype(F32), jnp.logical_and(j >= 1, j < npb - 1).astype(F32)


def _in_kernel(*refs, npb, n_tiles, split_ctx):
    ctx_ref = refs[0] if split_ctx else None
    x_ref, mod_ref, g_ref = (refs[1:] if split_ctx else refs)[0:3]
    project_refs = (refs[1:] if split_ctx else refs)[3:-1]
    hb_ref = refs[-1]
    s = pl.program_id(0)

    _zero_at_first_step((hb_ref,) + tuple(project_refs[-N_IN_SCRATCH:]))
    has_prev, has_next = _segment_flags(jnp.maximum(s - 2, 0), npb)
    gain = g_ref[...] * (1.0 + mod_ref[1:2, :])
    x_tile = _first_tile_from_ctx(ctx_ref, x_ref, jnp.minimum(s, n_tiles - 1) % npb) if split_ctx else x_ref[...]
    hb_next = (_rms(x_tile, gain) + mod_ref[0:1, :]).astype(BF16)
    _project_tile(hb_ref[...], has_prev, has_next, *project_refs)
    hb_ref[...] = hb_next


def _stream_specs(xs, tm, tile_of_step, npb):
    bj = lambda s: (tile_of_step(s) // npb, tile_of_step(s) % npb)
    if isinstance(xs, tuple):
        ctx, x = xs
        D = x.shape[2]
        return [pl.BlockSpec((None, tm, D), lambda s: (bj(s)[0], 0, 0)),
                pl.BlockSpec((None, tm, D), lambda s: (bj(s)[0], jnp.maximum(bj(s)[1] - 1, 0), 0))], [ctx, x]
    return [pl.BlockSpec((None, tm, xs.shape[2]), lambda s: (bj(s)[0], bj(s)[1], 0))], [xs]


def _stream_shape(xs):
    if isinstance(xs, tuple):
        return xs[1].shape[0], xs[0].shape[1] + xs[1].shape[1], xs[1].shape[2]
    return xs.shape


def _project_call_parts(B, S, D, tm, npb, mid, old, proj, layer):
    wt, wk, tables, conv_b, pool_w_bd, pool_scale = proj
    cos_k, sin_k, cos_t, sin_t, inv_cnt = tables
    c2 = lambda s: (0, 0)
    cpt = tm // CHUNK
    pool_kind = lambda j: (j <= 1).astype(jnp.int32) + 2 * ((j == 0) | (j == npb - 1)).astype(jnp.int32)
    in_specs = [
        pl.BlockSpec((None, R_END, D), lambda s: (layer, 0, 0), pipeline_mode=pl.Buffered(1)),
        pl.BlockSpec((None, D, T_END), lambda s: (layer, 0, 0), pipeline_mode=pl.Buffered(1)),
        pl.BlockSpec((tm, KV_W), lambda s: (mid(s) % npb, 0)),
        pl.BlockSpec((tm, KV_W), lambda s: (mid(s) % npb, 0)),
        pl.BlockSpec((HEAD_DIM, tm), lambda s: (0, mid(s) % npb)),
        pl.BlockSpec((HEAD_DIM, tm), lambda s: (0, mid(s) % npb)),
        pl.BlockSpec((3, 2 * GROUP_W, tm), lambda s: (0, 0, 0)),
        pl.BlockSpec((GROUP_W, GROUP_W), c2),
        pl.BlockSpec((1, GROUP_W), c2),
        pl.BlockSpec((None, tm, GROUP_W), lambda s: (pool_kind(old(s) % npb), 0, 0)),
    ]
    args = [wt, wk, cos_k, sin_k, cos_t, sin_t, conv_b, pool_w_bd, pool_scale, inv_cnt]

    def fm(rows, dt, tile):
        return (pl.BlockSpec((None, cpt, rows, CHUNK), lambda s: (tile(s) // npb, tile(s) % npb, 0, 0)),
                jax.ShapeDtypeStruct((B, S // CHUNK, rows, CHUNK), dt))

    def tk(cols, dt, tile):
        return (pl.BlockSpec((None, tm, cols), lambda s: (tile(s) // npb, tile(s) % npb, 0)),
                jax.ShapeDtypeStruct((B, S, cols), dt))

    outs = [fm(2 * GROUP_W, BF16, mid), fm(2 * GROUP_W, BF16, old), fm(2 * GROUP_W, BF16, mid),
            fm(4 * N_HEADS, F32, mid), fm(GROUP_W, BF16, mid), fm(KV_W, BF16, mid),
            tk(KV_W, BF16, mid), tk(GROUP_W, BF16, old)]
    scratch = [pltpu.VMEM((2 * GROUP_W, tm), F32), pltpu.VMEM((2 * GROUP_W, LANES), F32),
               pltpu.VMEM((tm, GROUP_W), F32), pltpu.VMEM((HALO, GROUP_W), F32)]
    return in_specs, args, [o[0] for o in outs], [o[1] for o in outs], scratch


def _in_proj(xs, mods, g0, proj, layer, n_batch_rows):
    B, S, D = _stream_shape(xs)
    tm = ROW_TILE
    npb = S // tm
    n_tiles = B * npb
    cur = lambda s: jnp.minimum(s, n_tiles - 1)
    mid = lambda s: jnp.clip(s - 1, 0, n_tiles - 1)
    old = lambda s: jnp.maximum(s - 2, 0)
    kern = functools.partial(_in_kernel, npb=npb, n_tiles=n_tiles, split_ctx=isinstance(xs, tuple))
    x_specs, x_args = _stream_specs(xs, tm, cur, npb)
    p_specs, p_args, out_specs, out_shapes, p_scratch = _project_call_parts(B, S, D, tm, npb, mid, old, proj, layer)
    return pl.pallas_call(
        kern,
        grid=(n_tiles + 2,),
        in_specs=x_specs + [
            pl.BlockSpec((None, None, 6, D),
                         lambda s: (layer, jnp.where(cur(s) % npb == 0, n_batch_rows, cur(s) // npb), 0, 0)),
            pl.BlockSpec((1, D), lambda s: (0, 0)),
        ] + p_specs,
        out_specs=out_specs,
        out_shape=out_shapes,
        scratch_shapes=p_scratch + [pltpu.VMEM((tm, D), BF16)],
        compiler_params=pltpu.CompilerParams(dimension_semantics=("arbitrary",), vmem_limit_bytes=VMEM_LIMIT),
        name="in_proj",
    )(*x_args, mods, g0, *p_args)


def _log_sigmoid(x):
    return jnp.minimum(x, 0.0) - jnp.log1p(jnp.exp(-jnp.abs(x)))


CPS = 2


def _pair_spec(n_b, rows, pair=lambda p: p, row_block=0):
    return pl.BlockSpec((n_b, CPS, rows, CHUNK), lambda p: (0, pair(p), row_block, 0))


def _mixers_kernel(zs_ref, sw_ref, sb_ref,
                   q_ref, kp_ref, kc_ref, kn_ref, kx_ref, vp_ref, vc_ref, vn_ref, vx_ref, sink_ref, mask0_ref, mask1_ref,
                   qkf_ref, vf_ref, gf_ref, qkb_ref, vb_ref, gb_ref, gbias_ref,
                   a_ref, c_ref, hf_ref, hb_ref, st_ref, m_ref, sc_ref, p_ref, *, n_ctx_chunks):
    n_b = zs_ref.shape[0]
    _zero_at_first_step((st_ref, m_ref))

    group = N_HEADS // KV_HEADS
    zeros = jnp.zeros((HEAD_DIM, group * CHUNK), BF16)
    pairs = [(u, b, kv) for u in range(CPS) for b in range(n_b) for kv in range(KV_HEADS)]
    mask_refs = (mask0_ref, mask1_ref)

    def keys(u, b):
        local = ([kp_ref[b, CHUNK:2 * CHUNK, :], kc_ref[b]] if u == 0 else [kc_ref[b], kn_ref[b, 0:CHUNK, :]])
        return jnp.concatenate(local + [kx_ref[b]], axis=0)

    def values(u, b):
        local = ([vp_ref[b, 1], vc_ref[b, 0], vc_ref[b, 1]] if u == 0 else [vc_ref[b, 0], vc_ref[b, 1], vn_ref[b, 0]])
        return jnp.concatenate(local + [vx_ref[b, i] for i in range(n_ctx_chunks)], axis=1)

    for i, (u, b, kv) in enumerate(pairs):
        q2 = jnp.concatenate([q_ref[b, u, (kv * group + g) * HEAD_DIM:(kv * group + g + 1) * HEAD_DIM, :]
                              for g in range(group)], axis=1)
        qm = jnp.concatenate([q2, zeros] if kv == 0 else [zeros, q2], axis=0)
        sc_ref[i] = _dot(keys(u, b), qm)

    s_i = lax.broadcasted_iota(jnp.int32, (CHUNK, CHUNK), 0)
    t_i = lax.broadcasted_iota(jnp.int32, (CHUNK, CHUNK), 1)
    causal = {True: s_i <= t_i, False: s_i >= t_i}
    ones_row = jnp.where(lax.broadcasted_iota(jnp.int32, (ST_ROWS - HEAD_DIM, CHUNK), 0) == 0, 1.0, 0.0).astype(BF16)
    all_ones = jnp.ones((CHUNK, CHUNK), F32)

    def start_chains(t):
        chains = []
        for b in range(n_b):
            for forward, qk_ref, v_ref, g_ref, o_ref in ((True, qkf_ref, vf_ref, gf_ref, hf_ref),
                                                          (False, qkb_ref, vb_ref, gb_ref, hb_ref)):
                u = t if forward else CPS - 1 - t
                gates = g_ref[b, u] + gbias_ref[...]
                li8 = gates[0:2 * N_HEADS] * LOG2E
                lf8 = _log_sigmoid(gates[2 * N_HEADS:4 * N_HEADS]) * LOG2E
                cum8 = _dot_hi(lf8, jnp.where(causal[forward], 1.0, 0.0).astype(F32))
                tot8 = _dot_hi(lf8, all_ones)
                a8 = li8 - cum8
                for h in range(N_HEADS):
                    r = (0 if forward else N_HEADS) + h
                    ch = dict(b=b, u=u, h=h, mask=causal[forward], o_ref=o_ref,
                              slot=(b * 2 + (0 if forward else 1)) * N_HEADS + h,
                              a=a8[r:r + 1], bcum=cum8[r:r + 1], b_end=tot8[r:r + 1],
                              q=qk_ref[b, u, h * HEAD_DIM:(h + 1) * HEAD_DIM, :],
                              k=qk_ref[b, u, GROUP_W + h * HEAD_DIM:GROUP_W + (h + 1) * HEAD_DIM, :],
                              v=v_ref[b, u, h * HEAD_DIM:(h + 1) * HEAD_DIM, :])
                    ch["kq"] = _dot_tn(ch["k"], ch["q"])
                    chains.append(ch)
        return chains

    def read_state(chains, prev=None):
        for i, ch in enumerate(chains):
            ch["st"] = st_ref[ch["slot"]][:, 0:HEAD_DIM] if prev is None else prev[i]["st_new"]
            ch["m"] = m_ref[ch["slot"]:ch["slot"] + 1, :] if prev is None else prev[i]["m_new"]
            ch["cq"] = _dot(ch["st"].astype(BF16), ch["q"])

    def gate_math(chains):
        for ch in chains:
            a, m, mask = ch["a"], ch["m"], ch["mask"]
            a_col = jnp.broadcast_to(a, (CHUNK, CHUNK)).T
            a_max = jnp.max(a, axis=1, keepdims=True)
            big_m = jnp.maximum(jnp.max(jnp.where(mask, a_col, NEG), axis=0, keepdims=True), m)
            ch["inter"] = jnp.exp2(m - big_m)
            ch["floor"] = jnp.exp2(-(ch["bcum"] + big_m))
            s = ch["kq"] * jnp.exp2(jnp.where(mask, a_col - big_m, NEG))
            ch["rowsum"] = jnp.sum(s, axis=0, keepdims=True)
            ch["s"] = s.astype(BF16)
            m_new = jnp.maximum(ch["b_end"] + m, ch["b_end"] + a_max)
            ch["decay"] = jnp.exp2(ch["b_end"] + m - m_new)
            ch["kw"] = (ch["k"].astype(F32) * jnp.exp2(ch["b_end"] + a - m_new)).astype(BF16)
            ch["m_new"] = m_new

    def finish(chains):
        for ch in chains:
            ch["vs"] = _dot(ch["v"], ch["s"])
            v_ext = jnp.concatenate([ch["v"], ones_row], axis=0)
            ch["upd"] = _dot_nt(v_ext, ch["kw"])
        for ch in chains:
            h, cq, inter = ch["h"], ch["cq"], ch["inter"]
            num = cq[0:HEAD_DIM] * inter + ch["vs"]
            den = inter * cq[HEAD_DIM:HEAD_DIM + 1] + ch["rowsum"]
            ch["o_ref"][ch["b"], ch["u"], h * HEAD_DIM:(h + 1) * HEAD_DIM, :] = (
                num / jnp.maximum(jnp.abs(den), ch["floor"]))
            ch["st_new"] = ch["st"] * ch["decay"][:, 0:HEAD_DIM] + ch["upd"]

    first = start_chains(0)
    read_state(first)
    second = start_chains(1)

    for u in range(CPS):
        for b in range(n_b):
            g = jax.nn.gelu(zs_ref[b, u].astype(F32))
            for h in range(N_HEADS):
                gu = g[h * HEAD_DIM:(h + 1) * HEAD_DIM]
                vh = _head_layer_norm_t(g[GROUP_W + h * HEAD_DIM:GROUP_W + (h + 1) * HEAD_DIM]).astype(BF16)
                mixed = _dot(vh, sw_ref[h]) + sb_ref[h:h + 1, :]
                a_ref[b, u, h * HEAD_DIM:(h + 1) * HEAD_DIM, :] = (gu * mixed).astype(BF16)

    p_sinks = []
    for i, (u, b, kv) in enumerate(pairs):
        sink = sink_ref[kv:kv + 1, :]
        blocks = [sc_ref[i, r:r + CHUNK, :] + mask_refs[u][r:r + CHUNK, :] for r in range(0, sc_ref.shape[1], CHUNK)]
        m = jnp.maximum(jnp.max(functools.reduce(jnp.maximum, blocks), axis=0, keepdims=True), sink)
        for r in range(0, sc_ref.shape[1], CHUNK // 2):
            rows = slice(r, r + CHUNK // 2)
            p_ref[i, rows, :] = jnp.exp2(sc_ref[i, rows, :] + mask_refs[u][rows, :] - m).astype(BF16)
        p_sinks.append(jnp.exp2(sink - m))

    gate_math(first)
    finish(first)

    ones_keys = jnp.where(lax.broadcasted_iota(jnp.int32, (BF16_ROWS, (3 + n_ctx_chunks) * CHUNK), 0) == 0,
                          1.0, 0.0).astype(BF16)
    for i, ((u, b, kv), p_sink) in enumerate(zip(pairs, p_sinks)):
        o_ext = _dot(jnp.concatenate([values(u, b)[kv * HEAD_DIM:(kv + 1) * HEAD_DIM, :], ones_keys], axis=0),
                     p_ref[i])
        o = o_ext[0:HEAD_DIM] / (o_ext[HEAD_DIM:HEAD_DIM + 1] + p_sink)
        for g in range(group):
            h = kv * group + g
            c_ref[b, u, h * HEAD_DIM:(h + 1) * HEAD_DIM, :] = o[:, g * CHUNK:(g + 1) * CHUNK].astype(BF16)

    read_state(second, prev=first)
    gate_math(second)
    finish(second)
    for ch in second:
        st_ref[ch["slot"], :, 0:HEAD_DIM] = ch["st_new"]
        m_ref[ch["slot"]:ch["slot"] + 1, :] = ch["m_new"]


def _attention_masks(n_ctx_chunks):
    group = N_HEADS // KV_HEADS
    n_keys = (3 + n_ctx_chunks) * CHUNK
    j = jnp.arange(n_keys)[:, None]
    i = (jnp.arange(group * CHUNK) & (CHUNK - 1))[None, :]
    masks = []
    for cur_ok, prev_ok, next_ok in ((False, False, False), (True, True, True), (True, False, True),
                                     (True, True, False), (True, False, False)):
        lo = (0 if prev_ok else CHUNK) if cur_ok else 3 * CHUNK
        hi = 3 * CHUNK if next_ok else 2 * CHUNK
        valid = ((j >= i) & (j <= i + 2 * CHUNK) & (j >= lo) & (j < hi)) | (j >= 3 * CHUNK)
        masks.append(jnp.where(valid, 0.0, NEG).astype(F32))
    return jnp.stack(masks)


def _mixers(zs, sgu_wt, sgu_b, aq, ak, av, sink_rows, masks, qk, vo, gt, gate_bias, n_ctx_chunks):
    B, n_chunks = zs.shape[:2]
    assert n_chunks % CPS == 0 and n_ctx_chunks == CPS
    n_pairs = n_chunks // CPS
    ctx_w = n_ctx_chunks * CHUNK
    group = N_HEADS // KV_HEADS
    n_keys = 3 * CHUNK + ctx_w
    kern = functools.partial(_mixers_kernel, n_ctx_chunks=n_ctx_chunks)
    ident = lambda p: p
    prev = lambda p: jnp.maximum(p - 1, 0)
    nxt = lambda p: jnp.minimum(p + 1, n_pairs - 1)
    bwd = lambda p: jnp.where(p == 0, 0, n_pairs - p)
    kspec = lambda f: pl.BlockSpec((B, CPS * CHUNK, KV_W), lambda p: (0, f(p), 0))
    vspec = lambda f: _pair_spec(B, KV_W, f)

    def mask_spec(u):
        def kind(p):
            c = CPS * p + u
            no_prev = (c == n_ctx_chunks).astype(jnp.int32)
            no_next = (c == n_chunks - 1).astype(jnp.int32)
            return jnp.where(c < n_ctx_chunks, 0, 1 + no_prev + 2 * no_next)
        return pl.BlockSpec((None, n_keys, group * CHUNK), lambda p: (kind(p), 0, 0))

    def scan_specs(order):
        return [_pair_spec(B, 2 * GROUP_W, order), _pair_spec(B, GROUP_W, order),
                _pair_spec(B, 4 * N_HEADS, order)]

    n_slots = B * 2 * N_HEADS
    return pl.pallas_call(
        kern,
        grid=(n_pairs,),
        in_specs=[
            _pair_spec(B, 2 * GROUP_W),
            pl.BlockSpec((N_HEADS, CHUNK, CHUNK), lambda p: (0, 0, 0)),
            pl.BlockSpec((N_HEADS, CHUNK), lambda p: (0, 0)),
            _pair_spec(B, GROUP_W),
            kspec(prev), kspec(ident), kspec(nxt), pl.BlockSpec((B, ctx_w, KV_W), lambda p: (0, 0, 0)),
            vspec(prev), vspec(ident), vspec(nxt),
            pl.BlockSpec((B, n_ctx_chunks, KV_W, CHUNK), lambda p: (0, 0, 0, 0)),
            pl.BlockSpec((SUBLANES, group * CHUNK), lambda p: (0, 0)),
            mask_spec(0), mask_spec(1),
        ] + scan_specs(ident) + scan_specs(bwd) + [pl.BlockSpec((4 * N_HEADS, CHUNK), lambda p: (0, 0))],
        out_specs=[_pair_spec(B, GROUP_W), _pair_spec(B, GROUP_W),
                   _pair_spec(B, GROUP_W, ident), _pair_spec(B, GROUP_W, bwd)],
        out_shape=[jax.ShapeDtypeStruct((B, n_chunks, GROUP_W, CHUNK), BF16)] * 2
                  + [jax.ShapeDtypeStruct((B, n_chunks, GROUP_W, CHUNK), F32)] * 2,
        scratch_shapes=[
            pltpu.VMEM((n_slots, ST_ROWS, LANES), F32),
            pltpu.VMEM((n_slots, LANES), F32),
            pltpu.VMEM((CPS * B * KV_HEADS, n_keys, group * CHUNK), F32),
            pltpu.VMEM((CPS * B * KV_HEADS, n_keys, group * CHUNK), BF16),
        ],
        compiler_params=pltpu.CompilerParams(dimension_semantics=("arbitrary",)),
        name="mixers",
    )(zs, sgu_wt, sgu_b, aq, ak, ak, ak, ak, av, av, av, av, sink_rows, masks, masks, qk, vo, gt, qk, vo, gt,
      gate_bias)


N_OUT_SCRATCH = 6


def _hidden_chunks(hidden):
    step = 3 * 2 * LANES
    return [(lo, min(lo + step, hidden)) for lo in range(0, hidden, step)]


def _out_kernel(*refs, hidden, npb, n_tiles, split_ctx):
    _zero_at_first_step(refs[-N_OUT_SCRATCH:-1])
    ctx_ref = refs[0] if split_ctx else None
    (x_ref, a_ref, hf_ref, hb_ref, zo_ref, c_ref, d_ref, moda_ref, modc_ref, g_ref, mg_ref,
     wo_ref, wi_ref, wf_ref, o_ref, x1_ref, y_ref, f_ref, tb_ref, act_ref, bm_ref) = refs[1:] if split_ctx else refs
    s = pl.program_id(0)
    f_ref[s % 2] = _dot(act_ref[...], wf_ref[...])

    x_tile = (_first_tile_from_ctx(ctx_ref, x_ref, jnp.clip(s - 1, 0, n_tiles - 1) % npb)
              if split_ctx else x_ref[...])
    x1 = x_tile + _rms(y_ref[...], moda_ref[2:3, :] * g_ref[1:2, :])
    x1_ref[s % 3] = x1
    tb_ref[...] = (_rms(x1, g_ref[2:3, :] * (1.0 + moda_ref[4:5, :])) + moda_ref[3:4, :]).astype(BF16)

    o_ref[...] = x1_ref[(s + 1) % 3] + _rms(f_ref[(s + 1) % 2], modc_ref[5:6, :] * g_ref[3:4, :])

    for lo, hi in _hidden_chunks(hidden):
        gate = _dot(tb_ref[...], wi_ref[:, lo:hi])
        up = _dot(tb_ref[...], wi_ref[:, hidden + lo:hidden + hi])
        act_ref[:, lo:hi] = (_silu(gate) * up).astype(BF16)

    hs = _get_chunks(hf_ref) + _get_chunks(hb_ref)
    hn = jnp.concatenate([_head_layer_norm_t(hs[h * HEAD_DIM:(h + 1) * HEAD_DIM]) for h in range(N_HEADS)], axis=0)
    bm_ref[...] = (hn * mg_ref[...] * jax.nn.sigmoid(_get_chunks(zo_ref).astype(F32))).astype(BF16)
    y_ref[...] = (_dot_tn(_get_chunks(a_ref), wo_ref[0:GROUP_W, :])
                  + _dot_tn(bm_ref[...], wo_ref[GROUP_W:2 * GROUP_W, :])
                  + _dot_tn(_get_chunks(c_ref), wo_ref[2 * GROUP_W:3 * GROUP_W, :])
                  + _dot(d_ref[...], wo_ref[3 * GROUP_W:4 * GROUP_W, :]))


def _out_ffn(xs, a, hf, hb, vo, c, d, mods, norm_g, mnorm_g, w_out, w_ffn_in, w_ffn_out, layer, n_batch_rows,
             drop_ctx):
    B, S, D = _stream_shape(xs)
    tm = ROW_TILE
    npb = S // tm
    n_tiles = B * npb
    hidden = w_ffn_out.shape[1]
    cur = lambda s: jnp.minimum(s, n_tiles - 1)
    mid = lambda s: jnp.clip(s - 1, 0, n_tiles - 1)
    prv = lambda s: jnp.clip(s - 3, 0, n_tiles - 1)
    row = lambda s: (cur(s) // npb, cur(s) % npb, 0)
    mod_row = lambda t: (layer, jnp.where(t % npb == 0, n_batch_rows, t // npb), 0, 0)
    c2 = lambda s: (0, 0)
    this_layer = lambda s: (layer, 0, 0)
    cpt = tm // CHUNK
    fm = pl.BlockSpec((None, cpt, GROUP_W, CHUNK), lambda s: (cur(s) // npb, cur(s) % npb, 0, 0))
    split_ctx = isinstance(xs, tuple)
    x_specs, x_args = _stream_specs(xs, tm, mid, npb)
    in_specs = x_specs + [
        fm, fm, fm,
        pl.BlockSpec((None, cpt, GROUP_W, CHUNK), lambda s: (cur(s) // npb, cur(s) % npb, 1, 0)),
        fm,
        pl.BlockSpec((None, tm, GROUP_W), row),
        pl.BlockSpec((None, None, 6, D), lambda s: mod_row(mid(s))),
        pl.BlockSpec((None, None, 6, D), lambda s: mod_row(prv(s))),
        pl.BlockSpec((None, 4, D), this_layer),
        pl.BlockSpec((GROUP_W, tm), c2),
        pl.BlockSpec((None, D, D), this_layer, pipeline_mode=pl.Buffered(1)),
        pl.BlockSpec((None, D, 2 * hidden), this_layer, pipeline_mode=pl.Buffered(1)),
        pl.BlockSpec((None, hidden, D), this_layer, pipeline_mode=pl.Buffered(1)),
    ]
    args = x_args + [a, hf, hb, vo, c, d, mods, mods, norm_g, mnorm_g, w_out, w_ffn_in, w_ffn_out]
    scratch = [pltpu.VMEM((3, tm, D), F32), pltpu.VMEM((tm, D), F32), pltpu.VMEM((2, tm, D), F32),
               pltpu.VMEM((tm, D), BF16), pltpu.VMEM((tm, hidden), BF16), pltpu.VMEM((GROUP_W, tm), BF16)]
    params = pltpu.CompilerParams(dimension_semantics=("arbitrary",), vmem_limit_bytes=VMEM_LIMIT)
    static = dict(hidden=hidden, npb=npb, n_tiles=n_tiles, split_ctx=split_ctx)
    if drop_ctx:
        out_spec = pl.BlockSpec((None, tm, D), lambda s: (prv(s) // npb, jnp.maximum(prv(s) % npb - 1, 0), 0))
        out_shape = jax.ShapeDtypeStruct((B, S - tm, D), F32)
    else:
        out_spec = pl.BlockSpec((None, tm, D), lambda s: (prv(s) // npb, prv(s) % npb, 0))
        out_shape = jax.ShapeDtypeStruct((B, S, D), F32)
    return pl.pallas_call(
        functools.partial(_out_kernel, **static),
        grid=(n_tiles + 3,),
        in_specs=in_specs,
        out_specs=out_spec,
        out_shape=out_shape,
        scratch_shapes=scratch, compiler_params=params, name="out_ffn",
    )(*args)


def _rope_tables(n_tokens, ctx_len, tm):
    axis_freq = HEAD_DIM // 4

    def rotation(shape, token_axis):
        tok = lax.broadcasted_iota(jnp.int32, shape, token_axis) - ctx_len
        f = lax.broadcasted_iota(jnp.int32, shape, 1 - token_axis) % HEAD_DIM
        pos = jnp.where(f < 2 * axis_freq, tok // GRID_W, tok % GRID_W).astype(F32)
        inv = jnp.power(ROPE_BASE, -(f % axis_freq).astype(F32) * 2.0 / (2 * axis_freq))
        ang = pos * inv
        sign = jnp.where(f % (2 * axis_freq) < axis_freq, -1.0, 1.0).astype(F32)
        return jnp.where(tok < 0, 1.0, jnp.cos(ang)), jnp.where(tok < 0, 0.0, jnp.sin(ang) * sign)

    cos_k, sin_k = rotation((ctx_len + n_tokens, KV_HEADS * HEAD_DIM), 0)
    cos_t, sin_t = rotation((HEAD_DIM, ctx_len + n_tokens), 1)
    t = jnp.arange(tm)[:, None]
    half = jnp.asarray([w // 2 for w in POOL_WINDOWS])[None, :]
    inv_cnt = []
    for last in (False, True):
        for first in (False, True):
            lo = jnp.maximum(t - half, 0) if first else t - half
            hi = jnp.minimum(t + half, tm) if last else t + half
            inv_cnt.append(jnp.repeat(1.0 / (hi - lo).astype(F32), HEAD_DIM, axis=1))
    q_scale = HEAD_DIM ** -0.5 * LOG2E
    return cos_k, sin_k, cos_t * q_scale, sin_t * q_scale, jnp.stack(inv_cnt)


W_BLOCK = 512


def _transpose_kernel(w_ref, tail_ref, o_ref):
    head_blocks = pl.num_programs(1) - 1

    @pl.when(pl.program_id(1) < head_blocks)
    def _():
        o_ref[...] = w_ref[...].T.astype(BF16)

    @pl.when(pl.program_id(1) == head_blocks)
    def _():
        o_ref[...] = tail_ref[...].T.astype(BF16)


def _arrange_w_in(w_in):
    depth, D, _ = w_in.shape
    ml0 = 2 * GROUP_W
    gate0 = ml0 + 4 * GROUP_W
    att0 = gate0 + 4 * N_HEADS
    pool0 = att0 + GROUP_W + 2 * KV_W
    assert gate0 == R_AQ and R_AQ % W_BLOCK == 0 and R_END - R_AQ <= W_BLOCK
    gates = w_in[:, :, gate0:att0].reshape(depth, D, 2, 2, N_HEADS)
    gates = gates.transpose(0, 1, 3, 2, 4).reshape(depth, D, 4 * N_HEADS)
    tail = jnp.concatenate([
        w_in[:, :, att0:att0 + GROUP_W],
        w_in[:, :, att0 + GROUP_W + KV_W:pool0],
        gates,
        jnp.zeros((depth, D, W_BLOCK - (R_END - R_AQ)), w_in.dtype),
    ], axis=2)
    head_blocks = R_AQ // W_BLOCK
    wt = pl.pallas_call(
        _transpose_kernel,
        grid=(depth, head_blocks + 1),
        in_specs=[pl.BlockSpec((None, D, W_BLOCK), lambda l, j: (l, 0, jnp.minimum(j, head_blocks - 1))),
                  pl.BlockSpec((None, D, W_BLOCK), lambda l, j: (l, 0, 0))],
        out_specs=pl.BlockSpec((None, W_BLOCK, D), lambda l, j: (l, j, 0)),
        out_shape=jax.ShapeDtypeStruct((depth, R_END, D), BF16),
        compiler_params=pltpu.CompilerParams(dimension_semantics=("arbitrary", "arbitrary")),
        name="arrange_w_in",
    )(w_in, tail)
    tok = jnp.concatenate([w_in[:, :, att0 + GROUP_W:att0 + GROUP_W + KV_W], w_in[:, :, pool0:pool0 + GROUP_W]],
                          axis=2)
    return wt, tok.astype(BF16)


def _block_diag(w):
    g = w.shape[0]
    eye = jnp.eye(g, dtype=w.dtype)
    return (eye[:, None, :, None] * w[:, :, None, :]).reshape(g * w.shape[1], g * w.shape[2])


def kernel(x, c, ctx, c_ctx, w_mod, b_mod, norm_g, w_in, w_out, sgu_w, sgu_b, mlstm_conv_w, mlstm_gate_b,
           mlstm_norm_g, attn_sink, pool_w, pool_scale, w_ffn_in, w_ffn_out):
    B, N, D = x.shape
    ctx_len = ctx.shape[1]
    depth = w_mod.shape[0]
    assert D == 4 * GROUP_W and N % ROW_TILE == 0 and ctx_len == ROW_TILE
    assert w_in.shape[2] == 2 * GROUP_W + 4 * GROUP_W + 4 * N_HEADS + GROUP_W + 2 * KV_W + GROUP_W
    n_ctx_chunks = ctx_len // CHUNK

    xs = (ctx, x)
    rows = -(-(B + 1) // SUBLANES) * SUBLANES
    cc = jnp.zeros((rows, D), F32).at[0:B].set(c).at[B].set(c_ctx)
    mods = _modulation(cc, w_mod, b_mod).reshape(depth, rows, 6, D)
    tables = _rope_tables(N, ctx_len, ROW_TILE)
    masks = _attention_masks(n_ctx_chunks)

    wt, wk = _arrange_w_in(w_in)
    w_out_b, w_ffn_in_b, w_ffn_out_b = w_out.astype(BF16), w_ffn_in.astype(BF16), w_ffn_out.astype(BF16)

    def layer_proj(l):
        conv_b = jnp.broadcast_to(mlstm_conv_w[l][:, :, None], (3, 2 * GROUP_W, ROW_TILE))
        return (wt, wk, tables, conv_b, _block_diag(pool_w[l]).astype(BF16), pool_scale[l].reshape(1, GROUP_W))

    for l in range(depth):
        zs, qk, vo, gt, aq, av, ak, dm = _in_proj(xs, mods, norm_g[l, 0:1], layer_proj(l), l, B)
        gate_bias = jnp.broadcast_to(
            mlstm_gate_b[l].reshape(2, 2, N_HEADS).transpose(1, 0, 2).reshape(4 * N_HEADS, 1), (4 * N_HEADS, CHUNK))
        sink_rows = jnp.zeros((SUBLANES, (N_HEADS // KV_HEADS) * CHUNK), F32).at[0:KV_HEADS].set(
            jnp.repeat(attn_sink[l].reshape(KV_HEADS, N_HEADS // KV_HEADS) * LOG2E, CHUNK, axis=1))
        mnorm_b = jnp.broadcast_to(mlstm_norm_g[l][:, None], (GROUP_W, ROW_TILE))
        a, cm, hf, hb = _mixers(zs, sgu_w[l].transpose(0, 2, 1).astype(BF16), sgu_b[l], aq, ak, av, sink_rows, masks,
                                qk, vo, gt, gate_bias, n_ctx_chunks)
        xs = _out_ffn(xs, a, hf, hb, vo, cm, dm, mods, norm_g, mnorm_b, w_out_b, w_ffn_in_b, w_ffn_out_b, l, B,
                      drop_ctx=(l == depth - 1))
    return xs
```

```python
import functools

import jax
import jax.numpy as jnp
from jax import lax
from jax.experimental import pallas as pl
from jax.experimental.pallas import tpu as pltpu

F32 = jnp.float32
BF16 = jnp.bfloat16

GRID_W = 64
ROPE_BASE = 10000.0
EPS = 1e-6
LANES = 128
SUBLANES = 8
BF16_ROWS = 16
CHUNK = 128
HEAD_DIM = 64
N_HEADS = 4
KV_HEADS = 2
GROUP_W = N_HEADS * HEAD_DIM
KV_W = KV_HEADS * HEAD_DIM
POOL_WINDOWS = (2, 4, 8, 16)
HALO = 8
ROW_TILE = 256
NEG = -1e30
VMEM_LIMIT = 56 * 1024 * 1024

R_SGU = 0
R_QK = 512
R_VO = 1024
R_AQ = 1536
R_AV = 1792
R_GATE = 1920
R_END = 1936
T_AK = 0
T_POOL = KV_W
T_END = KV_W + GROUP_W

ST_ROWS = 80
LOG2E = 1.4426950408889634

HI = lax.Precision.HIGHEST


def _dot(a, b):
    return jnp.dot(a, b, preferred_element_type=F32)


def _dot_hi(a, b):
    return jnp.dot(a, b, preferred_element_type=F32, precision=HI)


def _dot_nt(a, b):
    return lax.dot_general(a, b, (((1,), (1,)), ((), ())), preferred_element_type=F32)


def _dot_tn(a, b):
    return lax.dot_general(a, b, (((0,), (0,)), ((), ())), preferred_element_type=F32)


def _rms(x, g):
    return x * lax.rsqrt(jnp.mean(x * x, axis=-1, keepdims=True) + EPS) * g


def _silu(x):
    return x * jax.nn.sigmoid(x)


def _put_chunks(ref, x):
    for i in range(ref.shape[0]):
        ref[i] = x[:, i * CHUNK:(i + 1) * CHUNK].astype(ref.dtype)


def _get_chunks(ref):
    return jnp.concatenate([ref[i] for i in range(ref.shape[0])], axis=1)


def _head_layer_norm_t(v):
    mu = jnp.mean(v, axis=0, keepdims=True)
    vc = v - mu
    return vc * lax.rsqrt(jnp.mean(vc * vc, axis=0, keepdims=True) + EPS)


def _zero_at_first_step(refs):
    @pl.when(pl.program_id(0) == 0)
    def _():
        for ref in refs:
            ref[...] = jnp.zeros_like(ref)


def _mod_kernel(c_ref, w_ref, b_ref, o_ref):
    o_ref[...] = _dot(_silu(c_ref[...]).astype(BF16), w_ref[...].astype(BF16)) + b_ref[...]


def _modulation(cc, w_mod, b_mod):
    depth, d, n6 = w_mod.shape
    rows = cc.shape[0]
    tn = 1536
    return pl.pallas_call(
        _mod_kernel,
        grid=(depth, n6 // tn),
        in_specs=[
            pl.BlockSpec((rows, d), lambda l, j: (0, 0)),
            pl.BlockSpec((None, d, tn), lambda l, j: (l, 0, j)),
            pl.BlockSpec((None, 1, tn), lambda l, j: (l, 0, j)),
        ],
        out_specs=pl.BlockSpec((None, rows, tn), lambda l, j: (l, 0, j)),
        out_shape=jax.ShapeDtypeStruct((depth, rows, n6), F32),
        compiler_params=pltpu.CompilerParams(
            dimension_semantics=("arbitrary", "arbitrary"), vmem_limit_bytes=VMEM_LIMIT),
        name="modulation",
    )(cc, w_mod, b_mod.reshape(depth, 1, n6))


N_IN_SCRATCH = 4


def _first_tile_from_ctx(ctx_ref, x_ref, tile):
    rows = lax.broadcasted_iota(jnp.int32, x_ref.shape, 0)
    return jnp.where(rows < jnp.where(tile == 0, x_ref.shape[0], 0), ctx_ref[...], x_ref[...])


def _project_tile(hb, has_prev, has_next, wt_ref, wk_ref, cos_ref, sin_ref, cost_ref, sint_ref,
                  conv_ref, pw_ref, ps_ref, invc_ref,
                  zs_ref, qk_ref, vo_ref, gt_ref, aq_ref, av_ref, ak_ref, d_ref, z_ref, pcol_ref, zp_ref, prow_ref):
    tm = hb.shape[0]
    zc = _dot_nt(wt_ref[R_QK:R_VO, :], hb)
    _put_chunks(zs_ref, _dot_nt(wt_ref[R_SGU:R_QK, :], hb))

    z = z_ref[...]
    lane = lax.broadcasted_iota(jnp.int32, z.shape, 1)
    before = jnp.where(lane == 0, pcol_ref[:, 0:1] * has_prev, pltpu.roll(z, 1, axis=1))
    after = jnp.where(lane == tm - 1, zc[:, 0:1] * has_next, pltpu.roll(z, tm - 1, axis=1))
    qk = _silu(before * conv_ref[0] + z * conv_ref[1] + after * conv_ref[2])
    _put_chunks(qk_ref, jnp.concatenate([qk[0:GROUP_W], qk[GROUP_W:2 * GROUP_W] * (HEAD_DIM ** -0.5)], axis=0))
    pcol_ref[...] = jnp.broadcast_to(z[:, tm - 1:tm], pcol_ref.shape)
    z_ref[...] = zc

    zp_new = _dot(hb, wk_ref[:, T_POOL:T_END])
    _put_chunks(vo_ref, _dot_nt(wt_ref[R_VO:R_AQ, :], hb))
    zp = zp_ref[...]
    rows = tm + 2 * HALO
    ext = jnp.concatenate([prow_ref[...] * has_prev, zp, zp_new[0:HALO] * has_next], axis=0)
    s2 = ext + pltpu.roll(ext, 1, axis=0)
    s4 = pltpu.roll(s2, 1, axis=0) + pltpu.roll(s2, rows - 1, axis=0)
    s8 = pltpu.roll(s4, 2, axis=0) + pltpu.roll(s4, rows - 2, axis=0)
    s16 = pltpu.roll(s8, 4, axis=0) + pltpu.roll(s8, rows - 4, axis=0)
    sums = [s[HALO:HALO + tm] for s in (s2, s4, s8, s16)]
    plane = lax.broadcasted_iota(jnp.int32, zp.shape, 1)
    pooled = sums[-1]
    for gi in range(len(POOL_WINDOWS) - 2, -1, -1):
        pooled = jnp.where(plane < (gi + 1) * HEAD_DIM, sums[gi], pooled)
    pooled = pooled * invc_ref[...]
    d_ref[...] = (_dot((pooled - zp).astype(BF16), pw_ref[...]) * ps_ref[...]).astype(BF16)
    prow_ref[...] = zp[tm - HALO:tm]
    zp_ref[...] = zp_new

    q = _dot_nt(wt_ref[R_AQ:R_AV, :], hb)
    k = _dot(hb, wk_ref[:, T_AK:T_POOL])
    _put_chunks(av_ref, _dot_nt(wt_ref[R_AV:R_GATE, :], hb))
    _put_chunks(gt_ref, _dot_nt(wt_ref[R_GATE:R_END, :], hb))
    cos_t = jnp.concatenate([cost_ref[...]] * N_HEADS, axis=0)
    sin_t = jnp.concatenate([sint_ref[...]] * N_HEADS, axis=0)
    row = lax.broadcasted_iota(jnp.int32, q.shape, 0)
    rot_q = jnp.where((row & 31) < 16, pltpu.roll(q, GROUP_W - 16, axis=0), pltpu.roll(q, 16, axis=0))
    _put_chunks(aq_ref, q * cos_t + rot_q * sin_t)
    klane = lax.broadcasted_iota(jnp.int32, k.shape, 1)
    rot_k = jnp.where((klane & 31) < 16, pltpu.roll(k, KV_W - 16, axis=1), pltpu.roll(k, 16, axis=1))
    ak_ref[...] = (k * cos_ref[...] + rot_k * sin_ref[...]).astype(BF16)


def _segment_flags(tile, npb):
    j = tile % npb
    return (j >= 2).astype(F32), jnp.logical_and(j >= 1, j < npb - 1).astype(F32)


def _in_kernel(*refs, npb, n_tiles, split_ctx):
    ctx_ref = refs[0] if split_ctx else None
    x_ref, mod_ref, g_ref = (refs[1:] if split_ctx else refs)[0:3]
    project_refs = (refs[1:] if split_ctx else refs)[3:-1]
    hb_ref = refs[-1]
    s = pl.program_id(0)

    _zero_at_first_step((hb_ref,) + tuple(project_refs[-N_IN_SCRATCH:]))
    has_prev, has_next = _segment_flags(jnp.maximum(s - 2, 0), npb)
    gain = g_ref[...] * (1.0 + mod_ref[1:2, :])
    x_tile = _first_tile_from_ctx(ctx_ref, x_ref, jnp.minimum(s, n_tiles - 1) % npb) if split_ctx else x_ref[...]
    hb_next = (_rms(x_tile, gain) + mod_ref[0:1, :]).astype(BF16)
    _project_tile(hb_ref[...], has_prev, has_next, *project_refs)
    hb_ref[...] = hb_next


def _stream_specs(xs, tm, tile_of_step, npb):
    bj = lambda s: (tile_of_step(s) // npb, tile_of_step(s) % npb)
    if isinstance(xs, tuple):
        ctx, x = xs
        D = x.shape[2]
        return [pl.BlockSpec((None, tm, D), lambda s: (bj(s)[0], 0, 0)),
                pl.BlockSpec((None, tm, D), lambda s: (bj(s)[0], jnp.maximum(bj(s)[1] - 1, 0), 0))], [ctx, x]
    return [pl.BlockSpec((None, tm, xs.shape[2]), lambda s: (bj(s)[0], bj(s)[1], 0))], [xs]


def _stream_shape(xs):
    if isinstance(xs, tuple):
        return xs[1].shape[0], xs[0].shape[1] + xs[1].shape[1], xs[1].shape[2]
    return xs.shape


def _project_call_parts(B, S, D, tm, npb, mid, old, proj, layer):
    wt, wk, tables, conv_b, pool_w_bd, pool_scale = proj
    cos_k, sin_k, cos_t, sin_t, inv_cnt = tables
    c2 = lambda s: (0, 0)
    cpt = tm // CHUNK
    pool_kind = lambda j: (j <= 1).astype(jnp.int32) + 2 * ((j == 0) | (j == npb - 1)).astype(jnp.int32)
    in_specs = [
        pl.BlockSpec((None, R_END, D), lambda s: (layer, 0, 0), pipeline_mode=pl.Buffered(1)),
        pl.BlockSpec((None, D, T_END), lambda s: (layer, 0, 0), pipeline_mode=pl.Buffered(1)),
        pl.BlockSpec((tm, KV_W), lambda s: (mid(s) % npb, 0)),
        pl.BlockSpec((tm, KV_W), lambda s: (mid(s) % npb, 0)),
        pl.BlockSpec((HEAD_DIM, tm), lambda s: (0, mid(s) % npb)),
        pl.BlockSpec((HEAD_DIM, tm), lambda s: (0, mid(s) % npb)),
        pl.BlockSpec((3, 2 * GROUP_W, tm), lambda s: (0, 0, 0)),
        pl.BlockSpec((GROUP_W, GROUP_W), c2),
        pl.BlockSpec((1, GROUP_W), c2),
        pl.BlockSpec((None, tm, GROUP_W), lambda s: (pool_kind(old(s) % npb), 0, 0)),
    ]
    args = [wt, wk, cos_k, sin_k, cos_t, sin_t, conv_b, pool_w_bd, pool_scale, inv_cnt]

    def fm(rows, dt, tile):
        return (pl.BlockSpec((None, cpt, rows, CHUNK), lambda s: (tile(s) // npb, tile(s) % npb, 0, 0)),
                jax.ShapeDtypeStruct((B, S // CHUNK, rows, CHUNK), dt))

    def tk(cols, dt, tile):
        return (pl.BlockSpec((None, tm, cols), lambda s: (tile(s) // npb, tile(s) % npb, 0)),
                jax.ShapeDtypeStruct((B, S, cols), dt))

    outs = [fm(2 * GROUP_W, BF16, mid), fm(2 * GROUP_W, BF16, old), fm(2 * GROUP_W, BF16, mid),
            fm(4 * N_HEADS, F32, mid), fm(GROUP_W, BF16, mid), fm(KV_W, BF16, mid),
            tk(KV_W, BF16, mid), tk(GROUP_W, BF16, old)]
    scratch = [pltpu.VMEM((2 * GROUP_W, tm), F32), pltpu.VMEM((2 * GROUP_W, LANES), F32),
               pltpu.VMEM((tm, GROUP_W), F32), pltpu.VMEM((HALO, GROUP_W), F32)]
    return in_specs, args, [o[0] for o in outs], [o[1] for o in outs], scratch


def _in_proj(xs, mods, g0, proj, layer, n_batch_rows):
    B, S, D = _stream_shape(xs)
    tm = ROW_TILE
    npb = S // tm
    n_tiles = B * npb
    cur = lambda s: jnp.minimum(s, n_tiles - 1)
    mid = lambda s: jnp.clip(s - 1, 0, n_tiles - 1)
    old = lambda s: jnp.maximum(s - 2, 0)
    kern = functools.partial(_in_kernel, npb=npb, n_tiles=n_tiles, split_ctx=isinstance(xs, tuple))
    x_specs, x_args = _stream_specs(xs, tm, cur, npb)
    p_specs, p_args, out_specs, out_shapes, p_scratch = _project_call_parts(B, S, D, tm, npb, mid, old, proj, layer)
    return pl.pallas_call(
        kern,
        grid=(n_tiles + 2,),
        in_specs=x_specs + [
            pl.BlockSpec((None, None, 6, D),
                         lambda s: (layer, jnp.where(cur(s) % npb == 0, n_batch_rows, cur(s) // npb), 0, 0)),
            pl.BlockSpec((1, D), lambda s: (0, 0)),
        ] + p_specs,
        out_specs=out_specs,
        out_shape=out_shapes,
        scratch_shapes=p_scratch + [pltpu.VMEM((tm, D), BF16)],
        compiler_params=pltpu.CompilerParams(dimension_semantics=("arbitrary",), vmem_limit_bytes=VMEM_LIMIT),
        name="in_proj",
    )(*x_args, mods, g0, *p_args)


def _log_sigmoid(x):
    return jnp.minimum(x, 0.0) - jnp.log1p(jnp.exp(-jnp.abs(x)))


CPS = 2


def _pair_spec(n_b, rows, pair=lambda p: p, row_block=0):
    return pl.BlockSpec((n_b, CPS, rows, CHUNK), lambda p: (0, pair(p), row_block, 0))


def _mixers_kernel(zs_ref, sw_ref, sb_ref,
                   q_ref, kp_ref, kc_ref, kn_ref, kx_ref, vp_ref, vc_ref, vn_ref, vx_ref, sink_ref, mask0_ref, mask1_ref,
                   qkf_ref, vf_ref, gf_ref, qkb_ref, vb_ref, gb_ref, gbias_ref,
                   a_ref, c_ref, hf_ref, hb_ref, st_ref, m_ref, sc_ref, p_ref, *, n_ctx_chunks):
    n_b = zs_ref.shape[0]
    _zero_at_first_step((st_ref, m_ref))

    group = N_HEADS // KV_HEADS
    zeros = jnp.zeros((HEAD_DIM, group * CHUNK), BF16)
    pairs = [(u, b, kv) for u in range(CPS) for b in range(n_b) for kv in range(KV_HEADS)]
    mask_refs = (mask0_ref, mask1_ref)

    def keys(u, b):
        local = ([kp_ref[b, CHUNK:2 * CHUNK, :], kc_ref[b]] if u == 0 else [kc_ref[b], kn_ref[b, 0:CHUNK, :]])
        return jnp.concatenate(local + [kx_ref[b]], axis=0)

    def values(u, b):
        local = ([vp_ref[b, 1], vc_ref[b, 0], vc_ref[b, 1]] if u == 0 else [vc_ref[b, 0], vc_ref[b, 1], vn_ref[b, 0]])
        return jnp.concatenate(local + [vx_ref[b, i] for i in range(n_ctx_chunks)], axis=1)

    for i, (u, b, kv) in enumerate(pairs):
        q2 = jnp.concatenate([q_ref[b, u, (kv * group + g) * HEAD_DIM:(kv * group + g + 1) * HEAD_DIM, :]
                              for g in range(group)], axis=1)
        qm = jnp.concatenate([q2, zeros] if kv == 0 else [zeros, q2], axis=0)
        sc_ref[i] = _dot(keys(u, b), qm)

    s_i = lax.broadcasted_iota(jnp.int32, (CHUNK, CHUNK), 0)
    t_i = lax.broadcasted_iota(jnp.int32, (CHUNK, CHUNK), 1)
    causal = {True: s_i <= t_i, False: s_i >= t_i}
    ones_row = jnp.where(lax.broadcasted_iota(jnp.int32, (ST_ROWS - HEAD_DIM, CHUNK), 0) == 0, 1.0, 0.0).astype(BF16)
    all_ones = jnp.ones((CHUNK, CHUNK), F32)

    def start_chains(t):
        chains = []
        for b in range(n_b):
            for forward, qk_ref, v_ref, g_ref, o_ref in ((True, qkf_ref, vf_ref, gf_ref, hf_ref),
                                                          (False, qkb_ref, vb_ref, gb_ref, hb_ref)):
                u = t if forward else CPS - 1 - t
                gates = g_ref[b, u] + gbias_ref[...]
                li8 = gates[0:2 * N_HEADS] * LOG2E
                lf8 = _log_sigmoid(gates[2 * N_HEADS:4 * N_HEADS]) * LOG2E
                cum8 = _dot_hi(lf8, jnp.where(causal[forward], 1.0, 0.0).astype(F32))
                tot8 = _dot_hi(lf8, all_ones)
                a8 = li8 - cum8
                for h in range(N_HEADS):
                    r = (0 if forward else N_HEADS) + h
                    ch = dict(b=b, u=u, h=h, mask=causal[forward], o_ref=o_ref,
                              slot=(b * 2 + (0 if forward else 1)) * N_HEADS + h,
                              a=a8[r:r + 1], bcum=cum8[r:r + 1], b_end=tot8[r:r + 1],
                              q=qk_ref[b, u, h * HEAD_DIM:(h + 1) * HEAD_DIM, :],
                              k=qk_ref[b, u, GROUP_W + h * HEAD_DIM:GROUP_W + (h + 1) * HEAD_DIM, :],
                              v=v_ref[b, u, h * HEAD_DIM:(h + 1) * HEAD_DIM, :])
                    ch["kq"] = _dot_tn(ch["k"], ch["q"])
                    chains.append(ch)
        return chains

    def read_state(chains, prev=None):
        for i, ch in enumerate(chains):
            ch["st"] = st_ref[ch["slot"]][:, 0:HEAD_DIM] if prev is None else prev[i]["st_new"]
            ch["m"] = m_ref[ch["slot"]:ch["slot"] + 1, :] if prev is None else prev[i]["m_new"]
            ch["cq"] = _dot(ch["st"].astype(BF16), ch["q"])

    def gate_math(chains):
        for ch in chains:
            a, m, mask = ch["a"], ch["m"], ch["mask"]
            a_col = jnp.broadcast_to(a, (CHUNK, CHUNK)).T
            a_max = jnp.max(a, axis=1, keepdims=True)
            big_m = jnp.maximum(jnp.max(jnp.where(mask, a_col, NEG), axis=0, keepdims=True), m)
            ch["inter"] = jnp.exp2(m - big_m)
            ch["floor"] = jnp.exp2(-(ch["bcum"] + big_m))
            s = ch["kq"] * jnp.exp2(jnp.where(mask, a_col - big_m, NEG))
            ch["rowsum"] = jnp.sum(s, axis=0, keepdims=True)
            ch["s"] = s.astype(BF16)
            m_new = jnp.maximum(ch["b_end"] + m, ch["b_end"] + a_max)
            ch["decay"] = jnp.exp2(ch["b_end"] + m - m_new)
            ch["kw"] = (ch["k"].astype(F32) * jnp.exp2(ch["b_end"] + a - m_new)).astype(BF16)
            ch["m_new"] = m_new

    def finish(chains):
        for ch in chains:
            ch["vs"] = _dot(ch["v"], ch["s"])
            v_ext = jnp.concatenate([ch["v"], ones_row], axis=0)
            ch["upd"] = _dot_nt(v_ext, ch["kw"])
        for ch in chains:
            h, cq, inter = ch["h"], ch["cq"], ch["inter"]
            num = cq[0:HEAD_DIM] * inter + ch["vs"]
            den = inter * cq[HEAD_DIM:HEAD_DIM + 1] + ch["rowsum"]
            ch["o_ref"][ch["b"], ch["u"], h * HEAD_DIM:(h + 1) * HEAD_DIM, :] = (
                num / jnp.maximum(jnp.abs(den), ch["floor"]))
            ch["st_new"] = ch["st"] * ch["decay"][:, 0:HEAD_DIM] + ch["upd"]

    first = start_chains(0)
    read_state(first)
    second = start_chains(1)

    for u in range(CPS):
        for b in range(n_b):
            g = jax.nn.gelu(zs_ref[b, u].astype(F32))
            for h in range(N_HEADS):
                gu = g[h * HEAD_DIM:(h + 1) * HEAD_DIM]
                vh = _head_layer_norm_t(g[GROUP_W + h * HEAD_DIM:GROUP_W + (h + 1) * HEAD_DIM]).astype(BF16)
                mixed = _dot(vh, sw_ref[h]) + sb_ref[h:h + 1, :]
                a_ref[b, u, h * HEAD_DIM:(h + 1) * HEAD_DIM, :] = (gu * mixed).astype(BF16)

    p_sinks = []
    for i, (u, b, kv) in enumerate(pairs):
        sink = sink_ref[kv:kv + 1, :]
        blocks = [sc_ref[i, r:r + CHUNK, :] + mask_refs[u][r:r + CHUNK, :] for r in range(0, sc_ref.shape[1], CHUNK)]
        m = jnp.maximum(jnp.max(functools.reduce(jnp.maximum, blocks), axis=0, keepdims=True), sink)
        for r in range(0, sc_ref.shape[1], CHUNK // 2):
            rows = slice(r, r + CHUNK // 2)
            p_ref[i, rows, :] = jnp.exp2(sc_ref[i, rows, :] + mask_refs[u][rows, :] - m).astype(BF16)
        p_sinks.append(jnp.exp2(sink - m))

    gate_math(first)
    finish(first)

    ones_keys = jnp.where(lax.broadcasted_iota(jnp.int32, (BF16_ROWS, (3 + n_ctx_chunks) * CHUNK), 0) == 0,
                          1.0, 0.0).astype(BF16)
    for i, ((u, b, kv), p_sink) in enumerate(zip(pairs, p_sinks)):
        o_ext = _dot(jnp.concatenate([values(u, b)[kv * HEAD_DIM:(kv + 1) * HEAD_DIM, :], ones_keys], axis=0),
                     p_ref[i])
        o = o_ext[0:HEAD_DIM] / (o_ext[HEAD_DIM:HEAD_DIM + 1] + p_sink)
        for g in range(group):
            h = kv * group + g
            c_ref[b, u, h * HEAD_DIM:(h + 1) * HEAD_DIM, :] = o[:, g * CHUNK:(g + 1) * CHUNK].astype(BF16)

    read_state(second, prev=first)
    gate_math(second)
    finish(second)
    for ch in second:
        st_ref[ch["slot"], :, 0:HEAD_DIM] = ch["st_new"]
        m_ref[ch["slot"]:ch["slot"] + 1, :] = ch["m_new"]


def _attention_masks(n_ctx_chunks):
    group = N_HEADS // KV_HEADS
    n_keys = (3 + n_ctx_chunks) * CHUNK
    j = jnp.arange(n_keys)[:, None]
    i = (jnp.arange(group * CHUNK) & (CHUNK - 1))[None, :]
    masks = []
    for cur_ok, prev_ok, next_ok in ((False, False, False), (True, True, True), (True, False, True),
                                     (True, True, False), (True, False, False)):
        lo = (0 if prev_ok else CHUNK) if cur_ok else 3 * CHUNK
        hi = 3 * CHUNK if next_ok else 2 * CHUNK
        valid = ((j >= i) & (j <= i + 2 * CHUNK) & (j >= lo) & (j < hi)) | (j >= 3 * CHUNK)
        masks.append(jnp.where(valid, 0.0, NEG).astype(F32))
    return jnp.stack(masks)


def _mixers(zs, sgu_wt, sgu_b, aq, ak, av, sink_rows, masks, qk, vo, gt, gate_bias, n_ctx_chunks):
    B, n_chunks = zs.shape[:2]
    assert n_chunks % CPS == 0 and n_ctx_chunks == CPS
    n_pairs = n_chunks // CPS
    ctx_w = n_ctx_chunks * CHUNK
    group = N_HEADS // KV_HEADS
    n_keys = 3 * CHUNK + ctx_w
    kern = functools.partial(_mixers_kernel, n_ctx_chunks=n_ctx_chunks)
    ident = lambda p: p
    prev = lambda p: jnp.maximum(p - 1, 0)
    nxt = lambda p: jnp.minimum(p + 1, n_pairs - 1)
    bwd = lambda p: jnp.where(p == 0, 0, n_pairs - p)
    kspec = lambda f: pl.BlockSpec((B, CPS * CHUNK, KV_W), lambda p: (0, f(p), 0))
    vspec = lambda f: _pair_spec(B, KV_W, f)

    def mask_spec(u):
        def kind(p):
            c = CPS * p + u
            no_prev = (c == n_ctx_chunks).astype(jnp.int32)
            no_next = (c == n_chunks - 1).astype(jnp.int32)
            return jnp.where(c < n_ctx_chunks, 0, 1 + no_prev + 2 * no_next)
        return pl.BlockSpec((None, n_keys, group * CHUNK), lambda p: (kind(p), 0, 0))

    def scan_specs(order):
        return [_pair_spec(B, 2 * GROUP_W, order), _pair_spec(B, GROUP_W, order),
                _pair_spec(B, 4 * N_HEADS, order)]

    n_slots = B * 2 * N_HEADS
    return pl.pallas_call(
        kern,
        grid=(n_pairs,),
        in_specs=[
            _pair_spec(B, 2 * GROUP_W),
            pl.BlockSpec((N_HEADS, CHUNK, CHUNK), lambda p: (0, 0, 0)),
            pl.BlockSpec((N_HEADS, CHUNK), lambda p: (0, 0)),
            _pair_spec(B, GROUP_W),
            kspec(prev), kspec(ident), kspec(nxt), pl.BlockSpec((B, ctx_w, KV_W), lambda p: (0, 0, 0)),
            vspec(prev), vspec(ident), vspec(nxt),
            pl.BlockSpec((B, n_ctx_chunks, KV_W, CHUNK), lambda p: (0, 0, 0, 0)),
            pl.BlockSpec((SUBLANES, group * CHUNK), lambda p: (0, 0)),
            mask_spec(0), mask_spec(1),
        ] + scan_specs(ident) + scan_specs(bwd) + [pl.BlockSpec((4 * N_HEADS, CHUNK), lambda p: (0, 0))],
        out_specs=[_pair_spec(B, GROUP_W), _pair_spec(B, GROUP_W),
                   _pair_spec(B, GROUP_W, ident), _pair_spec(B, GROUP_W, bwd)],
        out_shape=[jax.ShapeDtypeStruct((B, n_chunks, GROUP_W, CHUNK), BF16)] * 2
                  + [jax.ShapeDtypeStruct((B, n_chunks, GROUP_W, CHUNK), F32)] * 2,
        scratch_shapes=[
            pltpu.VMEM((n_slots, ST_ROWS, LANES), F32),
            pltpu.VMEM((n_slots, LANES), F32),
            pltpu.VMEM((CPS * B * KV_HEADS, n_keys, group * CHUNK), F32),
            pltpu.VMEM((CPS * B * KV_HEADS, n_keys, group * CHUNK), BF16),
        ],
        compiler_params=pltpu.CompilerParams(dimension_semantics=("arbitrary",)),
        name="mixers",
    )(zs, sgu_wt, sgu_b, aq, ak, ak, ak, ak, av, av, av, av, sink_rows, masks, masks, qk, vo, gt, qk, vo, gt,
      gate_bias)


N_OUT_SCRATCH = 6


def _hidden_chunks(hidden):
    step = 3 * 2 * LANES
    return [(lo, min(lo + step, hidden)) for lo in range(0, hidden, step)]


def _out_kernel(*refs, hidden, npb, n_tiles, split_ctx):
    _zero_at_first_step(refs[-N_OUT_SCRATCH:-1])
    ctx_ref = refs[0] if split_ctx else None
    (x_ref, a_ref, hf_ref, hb_ref, zo_ref, c_ref, d_ref, moda_ref, modc_ref, g_ref, mg_ref,
     wo_ref, wi_ref, wf_ref, o_ref, x1_ref, y_ref, f_ref, tb_ref, act_ref, bm_ref) = refs[1:] if split_ctx else refs
    s = pl.program_id(0)
    f_ref[s % 2] = _dot(act_ref[...], wf_ref[...])

    x_tile = (_first_tile_from_ctx(ctx_ref, x_ref, jnp.clip(s - 1, 0, n_tiles - 1) % npb)
              if split_ctx else x_ref[...])
    x1 = x_tile + _rms(y_ref[...], moda_ref[2:3, :] * g_ref[1:2, :])
    x1_ref[s % 3] = x1
    tb_ref[...] = (_rms(x1, g_ref[2:3, :] * (1.0 + moda_ref[4:5, :])) + moda_ref[3:4, :]).astype(BF16)

    o_ref[...] = x1_ref[(s + 1) % 3] + _rms(f_ref[(s + 1) % 2], modc_ref[5:6, :] * g_ref[3:4, :])

    for lo, hi in _hidden_chunks(hidden):
        gate = _dot(tb_ref[...], wi_ref[:, lo:hi])
        up = _dot(tb_ref[...], wi_ref[:, hidden + lo:hidden + hi])
        act_ref[:, lo:hi] = (_silu(gate) * up).astype(BF16)

    hs = _get_chunks(hf_ref) + _get_chunks(hb_ref)
    hn = jnp.concatenate([_head_layer_norm_t(hs[h * HEAD_DIM:(h + 1) * HEAD_DIM]) for h in range(N_HEADS)], axis=0)
    bm_ref[...] = (hn * mg_ref[...] * jax.nn.sigmoid(_get_chunks(zo_ref).astype(F32))).astype(BF16)
    y_ref[...] = (_dot_tn(_get_chunks(a_ref), wo_ref[0:GROUP_W, :])
                  + _dot_tn(bm_ref[...], wo_ref[GROUP_W:2 * GROUP_W, :])
                  + _dot_tn(_get_chunks(c_ref), wo_ref[2 * GROUP_W:3 * GROUP_W, :])
                  + _dot(d_ref[...], wo_ref[3 * GROUP_W:4 * GROUP_W, :]))


def _out_ffn(xs, a, hf, hb, vo, c, d, mods, norm_g, mnorm_g, w_out, w_ffn_in, w_ffn_out, layer, n_batch_rows,
             drop_ctx):
    B, S, D = _stream_shape(xs)
    tm = ROW_TILE
    npb = S // tm
    n_tiles = B * npb
    hidden = w_ffn_out.shape[1]
    cur = lambda s: jnp.minimum(s, n_tiles - 1)
    mid = lambda s: jnp.clip(s - 1, 0, n_tiles - 1)
    prv = lambda s: jnp.clip(s - 3, 0, n_tiles - 1)
    row = lambda s: (cur(s) // npb, cur(s) % npb, 0)
    mod_row = lambda t: (layer, jnp.where(t % npb == 0, n_batch_rows, t // npb), 0, 0)
    c2 = lambda s: (0, 0)
    this_layer = lambda s: (layer, 0, 0)
    cpt = tm // CHUNK
    fm = pl.BlockSpec((None, cpt, GROUP_W, CHUNK), lambda s: (cur(s) // npb, cur(s) % npb, 0, 0))
    split_ctx = isinstance(xs, tuple)
    x_specs, x_args = _stream_specs(xs, tm, mid, npb)
    in_specs = x_specs + [
        fm, fm, fm,
        pl.BlockSpec((None, cpt, GROUP_W, CHUNK), lambda s: (cur(s) // npb, cur(s) % npb, 1, 0)),
        fm,
        pl.BlockSpec((None, tm, GROUP_W), row),
        pl.BlockSpec((None, None, 6, D), lambda s: mod_row(mid(s))),
        pl.BlockSpec((None, None, 6, D), lambda s: mod_row(prv(s))),
        pl.BlockSpec((None, 4, D), this_layer),
        pl.BlockSpec((GROUP_W, tm), c2),
        pl.BlockSpec((None, D, D), this_layer, pipeline_mode=pl.Buffered(1)),
        pl.BlockSpec((None, D, 2 * hidden), this_layer, pipeline_mode=pl.Buffered(1)),
        pl.BlockSpec((None, hidden, D), this_layer, pipeline_mode=pl.Buffered(1)),
    ]
    args = x_args + [a, hf, hb, vo, c, d, mods, mods, norm_g, mnorm_g, w_out, w_ffn_in, w_ffn_out]
    scratch = [pltpu.VMEM((3, tm, D), F32), pltpu.VMEM((tm, D), F32), pltpu.VMEM((2, tm, D), F32),
               pltpu.VMEM((tm, D), BF16), pltpu.VMEM((tm, hidden), BF16), pltpu.VMEM((GROUP_W, tm), BF16)]
    params = pltpu.CompilerParams(dimension_semantics=("arbitrary",), vmem_limit_bytes=VMEM_LIMIT)
    static = dict(hidden=hidden, npb=npb, n_tiles=n_tiles, split_ctx=split_ctx)
    if drop_ctx:
        out_spec = pl.BlockSpec((None, tm, D), lambda s: (prv(s) // npb, jnp.maximum(prv(s) % npb - 1, 0), 0))
        out_shape = jax.ShapeDtypeStruct((B, S - tm, D), F32)
    else:
        out_spec = pl.BlockSpec((None, tm, D), lambda s: (prv(s) // npb, prv(s) % npb, 0))
        out_shape = jax.ShapeDtypeStruct((B, S, D), F32)
    return pl.pallas_call(
        functools.partial(_out_kernel, **static),
        grid=(n_tiles + 3,),
        in_specs=in_specs,
        out_specs=out_spec,
        out_shape=out_shape,
        scratch_shapes=scratch, compiler_params=params, name="out_ffn",
    )(*args)


def _rope_tables(n_tokens, ctx_len, tm):
    rows = n_tokens // GRID_W
    axis_freq = HEAD_DIM // 4
    inv = jnp.power(ROPE_BASE, -jnp.arange(axis_freq, dtype=F32) * 2.0 / (2 * axis_freq))
    ar = jnp.arange(rows, dtype=F32)[:, None] * inv
    ac = jnp.arange(GRID_W, dtype=F32)[:, None] * inv

    def per_token(fn):
        r = jnp.broadcast_to(fn(ar)[:, None, :], (rows, GRID_W, axis_freq))
        c = jnp.broadcast_to(fn(ac)[None, :, :], (rows, GRID_W, axis_freq))
        return jnp.concatenate([r, r, c, c], axis=-1).reshape(n_tokens, HEAD_DIM)

    cos, sin = per_token(jnp.cos), per_token(jnp.sin)
    sign = jnp.where((jnp.arange(HEAD_DIM) % 32) < 16, -1.0, 1.0).astype(F32)
    cos = jnp.concatenate([jnp.ones((ctx_len, HEAD_DIM), F32), cos], axis=0)
    sin = jnp.concatenate([jnp.zeros((ctx_len, HEAD_DIM), F32), sin * sign], axis=0)
    t = jnp.arange(tm)[:, None]
    half = jnp.asarray([w // 2 for w in POOL_WINDOWS])[None, :]
    inv_cnt = []
    for last in (False, True):
        for first in (False, True):
            lo = jnp.maximum(t - half, 0) if first else t - half
            hi = jnp.minimum(t + half, tm) if last else t + half
            inv_cnt.append(jnp.repeat(1.0 / (hi - lo).astype(F32), HEAD_DIM, axis=1))
    q_scale = HEAD_DIM ** -0.5 * LOG2E
    return (jnp.tile(cos, (1, KV_HEADS)), jnp.tile(sin, (1, KV_HEADS)), cos.T * q_scale, sin.T * q_scale,
            jnp.stack(inv_cnt))


def _arrange_w_in(w_in):
    depth = w_in.shape[0]
    ml0 = 2 * GROUP_W
    gate0 = ml0 + 4 * GROUP_W
    att0 = gate0 + 4 * N_HEADS
    pool0 = att0 + GROUP_W + 2 * KV_W
    w_t = jnp.swapaxes(w_in, 1, 2)
    gates = w_t[:, gate0:att0].reshape(depth, 2, 2, N_HEADS, -1)
    gates = gates.transpose(0, 2, 1, 3, 4).reshape(depth, 4 * N_HEADS, -1)
    feat = jnp.concatenate([
        w_t[:, 0:gate0],
        w_t[:, att0:att0 + GROUP_W],
        w_t[:, att0 + GROUP_W + KV_W:pool0],
        gates,
    ], axis=1)
    tok = jnp.concatenate([w_t[:, att0 + GROUP_W:att0 + GROUP_W + KV_W], w_t[:, pool0:pool0 + GROUP_W]], axis=1)
    return feat.astype(BF16), jnp.swapaxes(tok.astype(BF16), 1, 2)


def _block_diag(w):
    g = w.shape[0]
    eye = jnp.eye(g, dtype=w.dtype)
    return (eye[:, None, :, None] * w[:, :, None, :]).reshape(g * w.shape[1], g * w.shape[2])


def kernel(x, c, ctx, c_ctx, w_mod, b_mod, norm_g, w_in, w_out, sgu_w, sgu_b, mlstm_conv_w, mlstm_gate_b,
           mlstm_norm_g, attn_sink, pool_w, pool_scale, w_ffn_in, w_ffn_out):
    B, N, D = x.shape
    ctx_len = ctx.shape[1]
    depth = w_mod.shape[0]
    assert D == 4 * GROUP_W and N % ROW_TILE == 0 and ctx_len == ROW_TILE
    assert w_in.shape[2] == 2 * GROUP_W + 4 * GROUP_W + 4 * N_HEADS + GROUP_W + 2 * KV_W + GROUP_W
    n_ctx_chunks = ctx_len // CHUNK

    xs = (ctx, x)
    rows = -(-(B + 1) // SUBLANES) * SUBLANES
    cc = jnp.zeros((rows, D), F32).at[0:B].set(c).at[B].set(c_ctx)
    mods = _modulation(cc, w_mod, b_mod).reshape(depth, rows, 6, D)
    tables = _rope_tables(N, ctx_len, ROW_TILE)
    masks = _attention_masks(n_ctx_chunks)

    wt, wk = _arrange_w_in(w_in)
    w_out_b, w_ffn_in_b, w_ffn_out_b = w_out.astype(BF16), w_ffn_in.astype(BF16), w_ffn_out.astype(BF16)

    def layer_proj(l):
        conv_b = jnp.broadcast_to(mlstm_conv_w[l][:, :, None], (3, 2 * GROUP_W, ROW_TILE))
        return (wt, wk, tables, conv_b, _block_diag(pool_w[l]).astype(BF16), pool_scale[l].reshape(1, GROUP_W))

    for l in range(depth):
        zs, qk, vo, gt, aq, av, ak, dm = _in_proj(xs, mods, norm_g[l, 0:1], layer_proj(l), l, B)
        gate_bias = jnp.broadcast_to(
            mlstm_gate_b[l].reshape(2, 2, N_HEADS).transpose(1, 0, 2).reshape(4 * N_HEADS, 1), (4 * N_HEADS, CHUNK))
        sink_rows = jnp.zeros((SUBLANES, (N_HEADS // KV_HEADS) * CHUNK), F32).at[0:KV_HEADS].set(
            jnp.repeat(attn_sink[l].reshape(KV_HEADS, N_HEADS // KV_HEADS) * LOG2E, CHUNK, axis=1))
        mnorm_b = jnp.broadcast_to(mlstm_norm_g[l][:, None], (GROUP_W, ROW_TILE))
        a, cm, hf, hb = _mixers(zs, sgu_w[l].transpose(0, 2, 1).astype(BF16), sgu_b[l], aq, ak, av, sink_rows, masks,
                                qk, vo, gt, gate_bias, n_ctx_chunks)
        xs = _out_ffn(xs, a, hf, hb, vo, cm, dm, mods, norm_g, mnorm_b, w_out_b, w_ffn_in_b, w_ffn_out_b, l, B,
                      drop_ctx=(l == depth - 1))
    return xs
```

```python
import functools

import jax
import jax.numpy as jnp
from jax import lax
from jax.experimental import pallas as pl
from jax.experimental.pallas import tpu as pltpu

F32 = jnp.float32
BF16 = jnp.bfloat16

GRID_W = 64
ROPE_BASE = 10000.0
EPS = 1e-6
LANES = 128
SUBLANES = 8
BF16_ROWS = 16
CHUNK = 128
HEAD_DIM = 64
N_HEADS = 4
KV_HEADS = 2
GROUP_W = N_HEADS * HEAD_DIM
KV_W = KV_HEADS * HEAD_DIM
POOL_WINDOWS = (2, 4, 8, 16)
HALO = 8
ROW_TILE = 256
NEG = -1e30
VMEM_LIMIT = 56 * 1024 * 1024

R_SGU = 0
R_QK = 512
R_VO = 1024
R_AQ = 1536
R_AV = 1792
R_GATE = 1920
R_END = 1936
T_AK = 0
T_POOL = KV_W
T_END = KV_W + GROUP_W

ST_ROWS = 80
LOG2E = 1.4426950408889634

HI = lax.Precision.HIGHEST


def _dot(a, b):
    return jnp.dot(a, b, preferred_element_type=F32)


def _dot_hi(a, b):
    return jnp.dot(a, b, preferred_element_type=F32, precision=HI)


def _dot_nt(a, b):
    return lax.dot_general(a, b, (((1,), (1,)), ((), ())), preferred_element_type=F32)


def _dot_tn(a, b):
    return lax.dot_general(a, b, (((0,), (0,)), ((), ())), preferred_element_type=F32)


def _rms(x, g):
    return x * lax.rsqrt(jnp.mean(x * x, axis=-1, keepdims=True) + EPS) * g


def _silu(x):
    return x * jax.nn.sigmoid(x)


def _put_chunks(ref, x):
    for i in range(ref.shape[0]):
        ref[i] = x[:, i * CHUNK:(i + 1) * CHUNK].astype(ref.dtype)


def _get_chunks(ref):
    return jnp.concatenate([ref[i] for i in range(ref.shape[0])], axis=1)


def _head_layer_norm_t(v):
    mu = jnp.mean(v, axis=0, keepdims=True)
    vc = v - mu
    return vc * lax.rsqrt(jnp.mean(vc * vc, axis=0, keepdims=True) + EPS)


def _zero_at_first_step(refs):
    @pl.when(pl.program_id(0) == 0)
    def _():
        for ref in refs:
            ref[...] = jnp.zeros_like(ref)


def _mod_kernel(c_ref, w_ref, b_ref, o_ref):
    o_ref[...] = _dot(_silu(c_ref[...]).astype(BF16), w_ref[...].astype(BF16)) + b_ref[...]


def _modulation(cc, w_mod, b_mod):
    depth, d, n6 = w_mod.shape
    rows = cc.shape[0]
    tn = 1536
    return pl.pallas_call(
        _mod_kernel,
        grid=(depth, n6 // tn),
        in_specs=[
            pl.BlockSpec((rows, d), lambda l, j: (0, 0)),
            pl.BlockSpec((None, d, tn), lambda l, j: (l, 0, j)),
            pl.BlockSpec((None, 1, tn), lambda l, j: (l, 0, j)),
        ],
        out_specs=pl.BlockSpec((None, rows, tn), lambda l, j: (l, 0, j)),
        out_shape=jax.ShapeDtypeStruct((depth, rows, n6), F32),
        compiler_params=pltpu.CompilerParams(
            dimension_semantics=("arbitrary", "arbitrary"), vmem_limit_bytes=VMEM_LIMIT),
        name="modulation",
    )(cc, w_mod, b_mod.reshape(depth, 1, n6))


N_IN_SCRATCH = 4


def _first_tile_from_ctx(ctx_ref, x_ref, tile):
    rows = lax.broadcasted_iota(jnp.int32, x_ref.shape, 0)
    return jnp.where(rows < jnp.where(tile == 0, x_ref.shape[0], 0), ctx_ref[...], x_ref[...])


def _project_tile(hb, has_prev, has_next, wt_ref, wk_ref, cos_ref, sin_ref, cost_ref, sint_ref,
                  conv_ref, pw_ref, ps_ref, invc_ref,
                  zs_ref, qk_ref, vo_ref, gt_ref, aq_ref, av_ref, ak_ref, d_ref, z_ref, pcol_ref, zp_ref, prow_ref):
    tm = hb.shape[0]
    zc = _dot_nt(wt_ref[R_QK:R_VO, :], hb)
    _put_chunks(zs_ref, _dot_nt(wt_ref[R_SGU:R_QK, :], hb))

    z = z_ref[...]
    lane = lax.broadcasted_iota(jnp.int32, z.shape, 1)
    before = jnp.where(lane == 0, pcol_ref[:, 0:1] * has_prev, pltpu.roll(z, 1, axis=1))
    after = jnp.where(lane == tm - 1, zc[:, 0:1] * has_next, pltpu.roll(z, tm - 1, axis=1))
    qk = _silu(before * conv_ref[0] + z * conv_ref[1] + after * conv_ref[2])
    _put_chunks(qk_ref, jnp.concatenate([qk[0:GROUP_W], qk[GROUP_W:2 * GROUP_W] * (HEAD_DIM ** -0.5)], axis=0))
    pcol_ref[...] = jnp.broadcast_to(z[:, tm - 1:tm], pcol_ref.shape)
    z_ref[...] = zc

    zp_new = _dot(hb, wk_ref[:, T_POOL:T_END])
    _put_chunks(vo_ref, _dot_nt(wt_ref[R_VO:R_AQ, :], hb))
    zp = zp_ref[...]
    rows = tm + 2 * HALO
    ext = jnp.concatenate([prow_ref[...] * has_prev, zp, zp_new[0:HALO] * has_next], axis=0)
    s2 = ext + pltpu.roll(ext, 1, axis=0)
    s4 = pltpu.roll(s2, 1, axis=0) + pltpu.roll(s2, rows - 1, axis=0)
    s8 = pltpu.roll(s4, 2, axis=0) + pltpu.roll(s4, rows - 2, axis=0)
    s16 = pltpu.roll(s8, 4, axis=0) + pltpu.roll(s8, rows - 4, axis=0)
    sums = [s[HALO:HALO + tm] for s in (s2, s4, s8, s16)]
    plane = lax.broadcasted_iota(jnp.int32, zp.shape, 1)
    pooled = sums[-1]
    for gi in range(len(POOL_WINDOWS) - 2, -1, -1):
        pooled = jnp.where(plane < (gi + 1) * HEAD_DIM, sums[gi], pooled)
    pooled = pooled * invc_ref[...]
    d_ref[...] = (_dot((pooled - zp).astype(BF16), pw_ref[...]) * ps_ref[...]).astype(BF16)
    prow_ref[...] = zp[tm - HALO:tm]
    zp_ref[...] = zp_new

    q = _dot_nt(wt_ref[R_AQ:R_AV, :], hb)
    k = _dot(hb, wk_ref[:, T_AK:T_POOL])
    _put_chunks(av_ref, _dot_nt(wt_ref[R_AV:R_GATE, :], hb))
    _put_chunks(gt_ref, _dot_nt(wt_ref[R_GATE:R_END, :], hb))
    cos_t = jnp.concatenate([cost_ref[...]] * N_HEADS, axis=0)
    sin_t = jnp.concatenate([sint_ref[...]] * N_HEADS, axis=0)
    row = lax.broadcasted_iota(jnp.int32, q.shape, 0)
    rot_q = jnp.where((row & 31) < 16, pltpu.roll(q, GROUP_W - 16, axis=0), pltpu.roll(q, 16, axis=0))
    _put_chunks(aq_ref, q * cos_t + rot_q * sin_t)
    klane = lax.broadcasted_iota(jnp.int32, k.shape, 1)
    rot_k = jnp.where((klane & 31) < 16, pltpu.roll(k, KV_W - 16, axis=1), pltpu.roll(k, 16, axis=1))
    ak_ref[...] = (k * cos_ref[...] + rot_k * sin_ref[...]).astype(BF16)


def _segment_flags(tile, npb):
    j = tile % npb
    return (j >= 2).astype(F32), jnp.logical_and(j >= 1, j < npb - 1).astype(F32)


def _in_kernel(*refs, npb, n_tiles, split_ctx):
    ctx_ref = refs[0] if split_ctx else None
    x_ref, mod_ref, g_ref = (refs[1:] if split_ctx else refs)[0:3]
    project_refs = (refs[1:] if split_ctx else refs)[3:-1]
    hb_ref = refs[-1]
    s = pl.program_id(0)

    _zero_at_first_step((hb_ref,) + tuple(project_refs[-N_IN_SCRATCH:]))
    has_prev, has_next = _segment_flags(jnp.maximum(s - 2, 0), npb)
    gain = g_ref[...] * (1.0 + mod_ref[1:2, :])
    x_tile = _first_tile_from_ctx(ctx_ref, x_ref, jnp.minimum(s, n_tiles - 1) % npb) if split_ctx else x_ref[...]
    hb_next = (_rms(x_tile, gain) + mod_ref[0:1, :]).astype(BF16)
    _project_tile(hb_ref[...], has_prev, has_next, *project_refs)
    hb_ref[...] = hb_next


def _stream_specs(xs, tm, tile_of_step, npb):
    bj = lambda s: (tile_of_step(s) // npb, tile_of_step(s) % npb)
    if isinstance(xs, tuple):
        ctx, x = xs
        D = x.shape[2]
        return [pl.BlockSpec((None, tm, D), lambda s: (bj(s)[0], 0, 0)),
                pl.BlockSpec((None, tm, D), lambda s: (bj(s)[0], jnp.maximum(bj(s)[1] - 1, 0), 0))], [ctx, x]
    return [pl.BlockSpec((None, tm, xs.shape[2]), lambda s: (bj(s)[0], bj(s)[1], 0))], [xs]


def _stream_shape(xs):
    if isinstance(xs, tuple):
        return xs[1].shape[0], xs[0].shape[1] + xs[1].shape[1], xs[1].shape[2]
    return xs.shape


def _project_call_parts(B, S, D, tm, npb, mid, old, proj, layer):
    wt, wk, tables, conv_b, pool_w_bd, pool_scale = proj
    cos_k, sin_k, cos_t, sin_t, inv_cnt = tables
    c2 = lambda s: (0, 0)
    cpt = tm // CHUNK
    pool_kind = lambda j: (j <= 1).astype(jnp.int32) + 2 * ((j == 0) | (j == npb - 1)).astype(jnp.int32)
    in_specs = [
        pl.BlockSpec((None, R_END, D), lambda s: (layer, 0, 0), pipeline_mode=pl.Buffered(1)),
        pl.BlockSpec((None, D, T_END), lambda s: (layer, 0, 0), pipeline_mode=pl.Buffered(1)),
        pl.BlockSpec((tm, KV_W), lambda s: (mid(s) % npb, 0)),
        pl.BlockSpec((tm, KV_W), lambda s: (mid(s) % npb, 0)),
        pl.BlockSpec((HEAD_DIM, tm), lambda s: (0, mid(s) % npb)),
        pl.BlockSpec((HEAD_DIM, tm), lambda s: (0, mid(s) % npb)),
        pl.BlockSpec((3, 2 * GROUP_W, tm), lambda s: (0, 0, 0)),
        pl.BlockSpec((GROUP_W, GROUP_W), c2),
        pl.BlockSpec((1, GROUP_W), c2),
        pl.BlockSpec((None, tm, GROUP_W), lambda s: (pool_kind(old(s) % npb), 0, 0)),
    ]
    args = [wt, wk, cos_k, sin_k, cos_t, sin_t, conv_b, pool_w_bd, pool_scale, inv_cnt]

    def fm(rows, dt, tile):
        return (pl.BlockSpec((None, cpt, rows, CHUNK), lambda s: (tile(s) // npb, tile(s) % npb, 0, 0)),
                jax.ShapeDtypeStruct((B, S // CHUNK, rows, CHUNK), dt))

    def tk(cols, dt, tile):
        return (pl.BlockSpec((None, tm, cols), lambda s: (tile(s) // npb, tile(s) % npb, 0)),
                jax.ShapeDtypeStruct((B, S, cols), dt))

    outs = [fm(2 * GROUP_W, BF16, mid), fm(2 * GROUP_W, BF16, old), fm(2 * GROUP_W, BF16, mid),
            fm(4 * N_HEADS, F32, mid), fm(GROUP_W, BF16, mid), fm(KV_W, BF16, mid),
            tk(KV_W, BF16, mid), tk(GROUP_W, BF16, old)]
    scratch = [pltpu.VMEM((2 * GROUP_W, tm), F32), pltpu.VMEM((2 * GROUP_W, LANES), F32),
               pltpu.VMEM((tm, GROUP_W), F32), pltpu.VMEM((HALO, GROUP_W), F32)]
    return in_specs, args, [o[0] for o in outs], [o[1] for o in outs], scratch


def _in_proj(xs, mods, g0, proj, layer, n_batch_rows):
    B, S, D = _stream_shape(xs)
    tm = ROW_TILE
    npb = S // tm
    n_tiles = B * npb
    cur = lambda s: jnp.minimum(s, n_tiles - 1)
    mid = lambda s: jnp.clip(s - 1, 0, n_tiles - 1)
    old = lambda s: jnp.maximum(s - 2, 0)
    kern = functools.partial(_in_kernel, npb=npb, n_tiles=n_tiles, split_ctx=isinstance(xs, tuple))
    x_specs, x_args = _stream_specs(xs, tm, cur, npb)
    p_specs, p_args, out_specs, out_shapes, p_scratch = _project_call_parts(B, S, D, tm, npb, mid, old, proj, layer)
    return pl.pallas_call(
        kern,
        grid=(n_tiles + 2,),
        in_specs=x_specs + [
            pl.BlockSpec((None, None, 6, D),
                         lambda s: (layer, jnp.where(cur(s) % npb == 0, n_batch_rows, cur(s) // npb), 0, 0)),
            pl.BlockSpec((1, D), lambda s: (0, 0)),
        ] + p_specs,
        out_specs=out_specs,
        out_shape=out_shapes,
        scratch_shapes=p_scratch + [pltpu.VMEM((tm, D), BF16)],
        compiler_params=pltpu.CompilerParams(dimension_semantics=("arbitrary",), vmem_limit_bytes=VMEM_LIMIT),
        name="in_proj",
    )(*x_args, mods, g0, *p_args)


def _log_sigmoid(x):
    return jnp.minimum(x, 0.0) - jnp.log1p(jnp.exp(-jnp.abs(x)))


CPS = 2


def _pair_spec(n_b, rows, pair=lambda p: p, row_block=0):
    return pl.BlockSpec((n_b, CPS, rows, CHUNK), lambda p: (0, pair(p), row_block, 0))


def _mixers_kernel(zs_ref, sw_ref, sb_ref,
                   q_ref, kp_ref, kc_ref, kn_ref, kx_ref, vp_ref, vc_ref, vn_ref, vx_ref, sink_ref, mask0_ref, mask1_ref,
                   qkf_ref, vf_ref, gf_ref, qkb_ref, vb_ref, gb_ref, gbias_ref,
                   a_ref, c_ref, hf_ref, hb_ref, st_ref, m_ref, sc_ref, p_ref, *, n_ctx_chunks):
    n_b = zs_ref.shape[0]
    _zero_at_first_step((st_ref, m_ref))

    group = N_HEADS // KV_HEADS
    zeros = jnp.zeros((HEAD_DIM, group * CHUNK), BF16)
    pairs = [(u, b, kv) for u in range(CPS) for b in range(n_b) for kv in range(KV_HEADS)]
    mask_refs = (mask0_ref, mask1_ref)

    def keys(u, b):
        local = ([kp_ref[b, CHUNK:2 * CHUNK, :], kc_ref[b]] if u == 0 else [kc_ref[b], kn_ref[b, 0:CHUNK, :]])
        return jnp.concatenate(local + [kx_ref[b]], axis=0)

    def values(u, b):
        local = ([vp_ref[b, 1], vc_ref[b, 0], vc_ref[b, 1]] if u == 0 else [vc_ref[b, 0], vc_ref[b, 1], vn_ref[b, 0]])
        return jnp.concatenate(local + [vx_ref[b, i] for i in range(n_ctx_chunks)], axis=1)

    for i, (u, b, kv) in enumerate(pairs):
        q2 = jnp.concatenate([q_ref[b, u, (kv * group + g) * HEAD_DIM:(kv * group + g + 1) * HEAD_DIM, :]
                              for g in range(group)], axis=1)
        qm = jnp.concatenate([q2, zeros] if kv == 0 else [zeros, q2], axis=0)
        sc_ref[i] = _dot(keys(u, b), qm)

    s_i = lax.broadcasted_iota(jnp.int32, (CHUNK, CHUNK), 0)
    t_i = lax.broadcasted_iota(jnp.int32, (CHUNK, CHUNK), 1)
    causal = {True: s_i <= t_i, False: s_i >= t_i}
    ones_row = jnp.where(lax.broadcasted_iota(jnp.int32, (ST_ROWS - HEAD_DIM, CHUNK), 0) == 0, 1.0, 0.0).astype(BF16)
    all_ones = jnp.ones((CHUNK, CHUNK), F32)

    def start_chains(t):
        chains = []
        for b in range(n_b):
            for forward, qk_ref, v_ref, g_ref, o_ref in ((True, qkf_ref, vf_ref, gf_ref, hf_ref),
                                                          (False, qkb_ref, vb_ref, gb_ref, hb_ref)):
                u = t if forward else CPS - 1 - t
                gates = g_ref[b, u] + gbias_ref[...]
                li8 = gates[0:2 * N_HEADS] * LOG2E
                lf8 = _log_sigmoid(gates[2 * N_HEADS:4 * N_HEADS]) * LOG2E
                cum8 = _dot_hi(lf8, jnp.where(causal[forward], 1.0, 0.0).astype(F32))
                tot8 = _dot_hi(lf8, all_ones)
                a8 = li8 - cum8
                for h in range(N_HEADS):
                    r = (0 if forward else N_HEADS) + h
                    ch = dict(b=b, u=u, h=h, mask=causal[forward], o_ref=o_ref,
                              slot=(b * 2 + (0 if forward else 1)) * N_HEADS + h,
                              a=a8[r:r + 1], bcum=cum8[r:r + 1], b_end=tot8[r:r + 1],
                              q=qk_ref[b, u, h * HEAD_DIM:(h + 1) * HEAD_DIM, :],
                              k=qk_ref[b, u, GROUP_W + h * HEAD_DIM:GROUP_W + (h + 1) * HEAD_DIM, :],
                              v=v_ref[b, u, h * HEAD_DIM:(h + 1) * HEAD_DIM, :])
                    ch["kq"] = _dot_tn(ch["k"], ch["q"])
                    chains.append(ch)
        return chains

    def read_state(chains, prev=None):
        for i, ch in enumerate(chains):
            ch["st"] = st_ref[ch["slot"]][:, 0:HEAD_DIM] if prev is None else prev[i]["st_new"]
            ch["m"] = m_ref[ch["slot"]:ch["slot"] + 1, :] if prev is None else prev[i]["m_new"]
            ch["cq"] = _dot(ch["st"].astype(BF16), ch["q"])

    def gate_math(chains):
        for ch in chains:
            a, m, mask = ch["a"], ch["m"], ch["mask"]
            a_col = jnp.broadcast_to(a, (CHUNK, CHUNK)).T
            a_max = jnp.max(a, axis=1, keepdims=True)
            big_m = jnp.maximum(jnp.max(jnp.where(mask, a_col, NEG), axis=0, keepdims=True), m)
            ch["inter"] = jnp.exp2(m - big_m)
            ch["floor"] = jnp.exp2(-(ch["bcum"] + big_m))
            s = ch["kq"] * jnp.exp2(jnp.where(mask, a_col - big_m, NEG))
            ch["rowsum"] = jnp.sum(s, axis=0, keepdims=True)
            ch["s"] = s.astype(BF16)
            m_new = jnp.maximum(ch["b_end"] + m, ch["b_end"] + a_max)
            ch["decay"] = jnp.exp2(ch["b_end"] + m - m_new)
            ch["kw"] = (ch["k"].astype(F32) * jnp.exp2(ch["b_end"] + a - m_new)).astype(BF16)
            ch["m_new"] = m_new

    def finish(chains):
        for ch in chains:
            ch["vs"] = _dot(ch["v"], ch["s"])
            v_ext = jnp.concatenate([ch["v"], ones_row], axis=0)
            ch["upd"] = _dot_nt(v_ext, ch["kw"])
        for ch in chains:
            h, cq, inter = ch["h"], ch["cq"], ch["inter"]
            num = cq[0:HEAD_DIM] * inter + ch["vs"]
            den = inter * cq[HEAD_DIM:HEAD_DIM + 1] + ch["rowsum"]
            ch["o_ref"][ch["b"], ch["u"], h * HEAD_DIM:(h + 1) * HEAD_DIM, :] = (
                num / jnp.maximum(jnp.abs(den), ch["floor"]))
            ch["st_new"] = ch["st"] * ch["decay"][:, 0:HEAD_DIM] + ch["upd"]

    first = start_chains(0)
    read_state(first)
    second = start_chains(1)

    for u in range(CPS):
        for b in range(n_b):
            g = jax.nn.gelu(zs_ref[b, u].astype(F32))
            for h in range(N_HEADS):
                gu = g[h * HEAD_DIM:(h + 1) * HEAD_DIM]
                vh = _head_layer_norm_t(g[GROUP_W + h * HEAD_DIM:GROUP_W + (h + 1) * HEAD_DIM]).astype(BF16)
                mixed = _dot(vh, sw_ref[h]) + sb_ref[h:h + 1, :]
                a_ref[b, u, h * HEAD_DIM:(h + 1) * HEAD_DIM, :] = (gu * mixed).astype(BF16)

    p_sinks = []
    for i, (u, b, kv) in enumerate(pairs):
        sink = sink_ref[kv:kv + 1, :]
        blocks = [sc_ref[i, r:r + CHUNK, :] + mask_refs[u][r:r + CHUNK, :] for r in range(0, sc_ref.shape[1], CHUNK)]
        m = jnp.maximum(jnp.max(functools.reduce(jnp.maximum, blocks), axis=0, keepdims=True), sink)
        for r in range(0, sc_ref.shape[1], CHUNK // 2):
            rows = slice(r, r + CHUNK // 2)
            p_ref[i, rows, :] = jnp.exp2(sc_ref[i, rows, :] + mask_refs[u][rows, :] - m).astype(BF16)
        p_sinks.append(jnp.exp2(sink - m))

    gate_math(first)
    finish(first)

    ones_keys = jnp.where(lax.broadcasted_iota(jnp.int32, (BF16_ROWS, (3 + n_ctx_chunks) * CHUNK), 0) == 0,
                          1.0, 0.0).astype(BF16)
    for i, ((u, b, kv), p_sink) in enumerate(zip(pairs, p_sinks)):
        o_ext = _dot(jnp.concatenate([values(u, b)[kv * HEAD_DIM:(kv + 1) * HEAD_DIM, :], ones_keys], axis=0),
                     p_ref[i])
        o = o_ext[0:HEAD_DIM] / (o_ext[HEAD_DIM:HEAD_DIM + 1] + p_sink)
        for g in range(group):
            h = kv * group + g
            c_ref[b, u, h * HEAD_DIM:(h + 1) * HEAD_DIM, :] = o[:, g * CHUNK:(g + 1) * CHUNK].astype(BF16)

    read_state(second, prev=first)
    gate_math(second)
    finish(second)
    for ch in second:
        st_ref[ch["slot"], :, 0:HEAD_DIM] = ch["st_new"]
        m_ref[ch["slot"]:ch["slot"] + 1, :] = ch["m_new"]


def _attention_masks(n_ctx_chunks):
    group = N_HEADS // KV_HEADS
    n_keys = (3 + n_ctx_chunks) * CHUNK
    j = jnp.arange(n_keys)[:, None]
    i = (jnp.arange(group * CHUNK) & (CHUNK - 1))[None, :]
    masks = []
    for cur_ok, prev_ok, next_ok in ((False, False, False), (True, True, True), (True, False, True),
                                     (True, True, False), (True, False, False)):
        lo = (0 if prev_ok else CHUNK) if cur_ok else 3 * CHUNK
        hi = 3 * CHUNK if next_ok else 2 * CHUNK
        valid = ((j >= i) & (j <= i + 2 * CHUNK) & (j >= lo) & (j < hi)) | (j >= 3 * CHUNK)
        masks.append(jnp.where(valid, 0.0, NEG).astype(F32))
    return jnp.stack(masks)


def _mixers(zs, sgu_wt, sgu_b, aq, ak, av, sink_rows, masks, qk, vo, gt, gate_bias, n_ctx_chunks):
    B, n_chunks = zs.shape[:2]
    assert n_chunks % CPS == 0 and n_ctx_chunks == CPS
    n_pairs = n_chunks // CPS
    ctx_w = n_ctx_chunks * CHUNK
    group = N_HEADS // KV_HEADS
    n_keys = 3 * CHUNK + ctx_w
    kern = functools.partial(_mixers_kernel, n_ctx_chunks=n_ctx_chunks)
    ident = lambda p: p
    prev = lambda p: jnp.maximum(p - 1, 0)
    nxt = lambda p: jnp.minimum(p + 1, n_pairs - 1)
    bwd = lambda p: jnp.where(p == 0, 0, n_pairs - p)
    kspec = lambda f: pl.BlockSpec((B, CPS * CHUNK, KV_W), lambda p: (0, f(p), 0))
    vspec = lambda f: _pair_spec(B, KV_W, f)

    def mask_spec(u):
        def kind(p):
            c = CPS * p + u
            no_prev = (c == n_ctx_chunks).astype(jnp.int32)
            no_next = (c == n_chunks - 1).astype(jnp.int32)
            return jnp.where(c < n_ctx_chunks, 0, 1 + no_prev + 2 * no_next)
        return pl.BlockSpec((None, n_keys, group * CHUNK), lambda p: (kind(p), 0, 0))

    def scan_specs(order):
        return [_pair_spec(B, 2 * GROUP_W, order), _pair_spec(B, GROUP_W, order),
                _pair_spec(B, 4 * N_HEADS, order)]

    n_slots = B * 2 * N_HEADS
    return pl.pallas_call(
        kern,
        grid=(n_pairs,),
        in_specs=[
            _pair_spec(B, 2 * GROUP_W),
            pl.BlockSpec((N_HEADS, CHUNK, CHUNK), lambda p: (0, 0, 0)),
            pl.BlockSpec((N_HEADS, CHUNK), lambda p: (0, 0)),
            _pair_spec(B, GROUP_W),
            kspec(prev), kspec(ident), kspec(nxt), pl.BlockSpec((B, ctx_w, KV_W), lambda p: (0, 0, 0)),
            vspec(prev), vspec(ident), vspec(nxt),
            pl.BlockSpec((B, n_ctx_chunks, KV_W, CHUNK), lambda p: (0, 0, 0, 0)),
            pl.BlockSpec((SUBLANES, group * CHUNK), lambda p: (0, 0)),
            mask_spec(0), mask_spec(1),
        ] + scan_specs(ident) + scan_specs(bwd) + [pl.BlockSpec((4 * N_HEADS, CHUNK), lambda p: (0, 0))],
        out_specs=[_pair_spec(B, GROUP_W), _pair_spec(B, GROUP_W),
                   _pair_spec(B, GROUP_W, ident), _pair_spec(B, GROUP_W, bwd)],
        out_shape=[jax.ShapeDtypeStruct((B, n_chunks, GROUP_W, CHUNK), BF16)] * 2
                  + [jax.ShapeDtypeStruct((B, n_chunks, GROUP_W, CHUNK), F32)] * 2,
        scratch_shapes=[
            pltpu.VMEM((n_slots, ST_ROWS, LANES), F32),
            pltpu.VMEM((n_slots, LANES), F32),
            pltpu.VMEM((CPS * B * KV_HEADS, n_keys, group * CHUNK), F32),
            pltpu.VMEM((CPS * B * KV_HEADS, n_keys, group * CHUNK), BF16),
        ],
        compiler_params=pltpu.CompilerParams(dimension_semantics=("arbitrary",)),
        name="mixers",
    )(zs, sgu_wt, sgu_b, aq, ak, ak, ak, ak, av, av, av, av, sink_rows, masks, masks, qk, vo, gt, qk, vo, gt,
      gate_bias)


def _hidden_chunks(hidden):
    step = 3 * 2 * LANES
    return [(lo, min(lo + step, hidden)) for lo in range(0, hidden, step)]


def _out_kernel(*refs, hidden, npb, n_tiles, split_ctx):
    ctx_ref = refs[0] if split_ctx else None
    (x_ref, a_ref, hf_ref, hb_ref, zo_ref, c_ref, d_ref, moda_ref, modc_ref, g_ref, mg_ref,
     wo_ref, wi_ref, wf_ref, o_ref, x1_ref, y_ref, f_ref, tb_ref, act_ref, bm_ref) = refs[1:] if split_ctx else refs
    s = pl.program_id(0)
    _zero_at_first_step((x1_ref, f_ref))

    def ffn_out():
        f_ref[s % 2] = _dot(act_ref[...], wf_ref[...])

    def ffn_in_norm():
        x_tile = (_first_tile_from_ctx(ctx_ref, x_ref, jnp.clip(s - 1, 0, n_tiles - 1) % npb)
                  if split_ctx else x_ref[...])
        x1 = x_tile + _rms(y_ref[...], moda_ref[2:3, :] * g_ref[1:2, :])
        x1_ref[s % 3] = x1
        tb_ref[...] = (_rms(x1, g_ref[2:3, :] * (1.0 + moda_ref[4:5, :])) + moda_ref[3:4, :]).astype(BF16)

    def finish():
        o_ref[...] = x1_ref[(s + 1) % 3] + _rms(f_ref[(s + 1) % 2], modc_ref[5:6, :] * g_ref[3:4, :])

    def ffn_in():
        for lo, hi in _hidden_chunks(hidden):
            gate = _dot(tb_ref[...], wi_ref[:, lo:hi])
            up = _dot(tb_ref[...], wi_ref[:, hidden + lo:hidden + hi])
            act_ref[:, lo:hi] = (_silu(gate) * up).astype(BF16)

    def out_proj():
        hs = _get_chunks(hf_ref) + _get_chunks(hb_ref)
        hn = jnp.concatenate([_head_layer_norm_t(hs[h * HEAD_DIM:(h + 1) * HEAD_DIM]) for h in range(N_HEADS)],
                             axis=0)
        bm_ref[...] = (hn * mg_ref[...] * jax.nn.sigmoid(_get_chunks(zo_ref).astype(F32))).astype(BF16)
        y_ref[...] = (_dot_tn(_get_chunks(a_ref), wo_ref[0:GROUP_W, :])
                      + _dot_tn(bm_ref[...], wo_ref[GROUP_W:2 * GROUP_W, :])
                      + _dot_tn(_get_chunks(c_ref), wo_ref[2 * GROUP_W:3 * GROUP_W, :])
                      + _dot(d_ref[...], wo_ref[3 * GROUP_W:4 * GROUP_W, :]))

    pl.when(s == 0)(out_proj)

    @pl.when(s == 1)
    def _():
        ffn_in_norm()
        ffn_in()
        out_proj()

    @pl.when(jnp.logical_and(s >= 2, s <= n_tiles))
    def _():
        ffn_out()
        ffn_in_norm()
        finish()
        ffn_in()
        out_proj()

    @pl.when(s == n_tiles + 1)
    def _():
        ffn_out()
        finish()

    pl.when(s == n_tiles + 2)(finish)


def _out_ffn(xs, a, hf, hb, vo, c, d, mods, norm_g, mnorm_g, w_out, w_ffn_in, w_ffn_out, layer, n_batch_rows,
             drop_ctx):
    B, S, D = _stream_shape(xs)
    tm = ROW_TILE
    npb = S // tm
    n_tiles = B * npb
    hidden = w_ffn_out.shape[1]
    cur = lambda s: jnp.minimum(s, n_tiles - 1)
    mid = lambda s: jnp.clip(s - 1, 0, n_tiles - 1)
    prv = lambda s: jnp.clip(s - 3, 0, n_tiles - 1)
    row = lambda s: (cur(s) // npb, cur(s) % npb, 0)
    mod_row = lambda t: (layer, jnp.where(t % npb == 0, n_batch_rows, t // npb), 0, 0)
    c2 = lambda s: (0, 0)
    this_layer = lambda s: (layer, 0, 0)
    cpt = tm // CHUNK
    fm = pl.BlockSpec((None, cpt, GROUP_W, CHUNK), lambda s: (cur(s) // npb, cur(s) % npb, 0, 0))
    split_ctx = isinstance(xs, tuple)
    x_specs, x_args = _stream_specs(xs, tm, mid, npb)
    in_specs = x_specs + [
        fm, fm, fm,
        pl.BlockSpec((None, cpt, GROUP_W, CHUNK), lambda s: (cur(s) // npb, cur(s) % npb, 1, 0)),
        fm,
        pl.BlockSpec((None, tm, GROUP_W), row),
        pl.BlockSpec((None, None, 6, D), lambda s: mod_row(mid(s))),
        pl.BlockSpec((None, None, 6, D), lambda s: mod_row(prv(s))),
        pl.BlockSpec((None, 4, D), this_layer),
        pl.BlockSpec((GROUP_W, tm), c2),
        pl.BlockSpec((None, D, D), this_layer, pipeline_mode=pl.Buffered(1)),
        pl.BlockSpec((None, D, 2 * hidden), this_layer, pipeline_mode=pl.Buffered(1)),
        pl.BlockSpec((None, hidden, D), this_layer, pipeline_mode=pl.Buffered(1)),
    ]
    args = x_args + [a, hf, hb, vo, c, d, mods, mods, norm_g, mnorm_g, w_out, w_ffn_in, w_ffn_out]
    scratch = [pltpu.VMEM((3, tm, D), F32), pltpu.VMEM((tm, D), F32), pltpu.VMEM((2, tm, D), F32),
               pltpu.VMEM((tm, D), BF16), pltpu.VMEM((tm, hidden), BF16), pltpu.VMEM((GROUP_W, tm), BF16)]
    params = pltpu.CompilerParams(dimension_semantics=("arbitrary",), vmem_limit_bytes=VMEM_LIMIT)
    static = dict(hidden=hidden, npb=npb, n_tiles=n_tiles, split_ctx=split_ctx)
    if drop_ctx:
        out_spec = pl.BlockSpec((None, tm, D), lambda s: (prv(s) // npb, jnp.maximum(prv(s) % npb - 1, 0), 0))
        out_shape = jax.ShapeDtypeStruct((B, S - tm, D), F32)
    else:
        out_spec = pl.BlockSpec((None, tm, D), lambda s: (prv(s) // npb, prv(s) % npb, 0))
        out_shape = jax.ShapeDtypeStruct((B, S, D), F32)
    return pl.pallas_call(
        functools.partial(_out_kernel, **static),
        grid=(n_tiles + 3,),
        in_specs=in_specs,
        out_specs=out_spec,
        out_shape=out_shape,
        scratch_shapes=scratch, compiler_params=params, name="out_ffn",
    )(*args)


def _rope_tables(n_tokens, ctx_len, tm):
    rows = n_tokens // GRID_W
    axis_freq = HEAD_DIM // 4
    inv = jnp.power(ROPE_BASE, -jnp.arange(axis_freq, dtype=F32) * 2.0 / (2 * axis_freq))
    ar = jnp.arange(rows, dtype=F32)[:, None] * inv
    ac = jnp.arange(GRID_W, dtype=F32)[:, None] * inv

    def per_token(fn):
        r = jnp.broadcast_to(fn(ar)[:, None, :], (rows, GRID_W, axis_freq))
        c = jnp.broadcast_to(fn(ac)[None, :, :], (rows, GRID_W, axis_freq))
        return jnp.concatenate([r, r, c, c], axis=-1).reshape(n_tokens, HEAD_DIM)

    cos, sin = per_token(jnp.cos), per_token(jnp.sin)
    sign = jnp.where((jnp.arange(HEAD_DIM) % 32) < 16, -1.0, 1.0).astype(F32)
    cos = jnp.concatenate([jnp.ones((ctx_len, HEAD_DIM), F32), cos], axis=0)
    sin = jnp.concatenate([jnp.zeros((ctx_len, HEAD_DIM), F32), sin * sign], axis=0)
    t = jnp.arange(tm)[:, None]
    half = jnp.asarray([w // 2 for w in POOL_WINDOWS])[None, :]
    inv_cnt = []
    for last in (False, True):
        for first in (False, True):
            lo = jnp.maximum(t - half, 0) if first else t - half
            hi = jnp.minimum(t + half, tm) if last else t + half
            inv_cnt.append(jnp.repeat(1.0 / (hi - lo).astype(F32), HEAD_DIM, axis=1))
    q_scale = HEAD_DIM ** -0.5 * LOG2E
    return (jnp.tile(cos, (1, KV_HEADS)), jnp.tile(sin, (1, KV_HEADS)), cos.T * q_scale, sin.T * q_scale,
            jnp.stack(inv_cnt))


def _arrange_w_in(w_in):
    depth = w_in.shape[0]
    ml0 = 2 * GROUP_W
    gate0 = ml0 + 4 * GROUP_W
    att0 = gate0 + 4 * N_HEADS
    pool0 = att0 + GROUP_W + 2 * KV_W
    w_t = jnp.swapaxes(w_in, 1, 2)
    gates = w_t[:, gate0:att0].reshape(depth, 2, 2, N_HEADS, -1)
    gates = gates.transpose(0, 2, 1, 3, 4).reshape(depth, 4 * N_HEADS, -1)
    feat = jnp.concatenate([
        w_t[:, 0:gate0],
        w_t[:, att0:att0 + GROUP_W],
        w_t[:, att0 + GROUP_W + KV_W:pool0],
        gates,
    ], axis=1)
    tok = jnp.concatenate([w_t[:, att0 + GROUP_W:att0 + GROUP_W + KV_W], w_t[:, pool0:pool0 + GROUP_W]], axis=1)
    return feat.astype(BF16), jnp.swapaxes(tok.astype(BF16), 1, 2)


def _block_diag(w):
    g = w.shape[0]
    eye = jnp.eye(g, dtype=w.dtype)
    return (eye[:, None, :, None] * w[:, :, None, :]).reshape(g * w.shape[1], g * w.shape[2])


def kernel(x, c, ctx, c_ctx, w_mod, b_mod, norm_g, w_in, w_out, sgu_w, sgu_b, mlstm_conv_w, mlstm_gate_b,
           mlstm_norm_g, attn_sink, pool_w, pool_scale, w_ffn_in, w_ffn_out):
    B, N, D = x.shape
    ctx_len = ctx.shape[1]
    depth = w_mod.shape[0]
    assert D == 4 * GROUP_W and N % ROW_TILE == 0 and ctx_len == ROW_TILE
    assert w_in.shape[2] == 2 * GROUP_W + 4 * GROUP_W + 4 * N_HEADS + GROUP_W + 2 * KV_W + GROUP_W
    n_ctx_chunks = ctx_len // CHUNK

    xs = (ctx, x)
    rows = -(-(B + 1) // SUBLANES) * SUBLANES
    cc = jnp.zeros((rows, D), F32).at[0:B].set(c).at[B].set(c_ctx)
    mods = _modulation(cc, w_mod, b_mod).reshape(depth, rows, 6, D)
    tables = _rope_tables(N, ctx_len, ROW_TILE)
    masks = _attention_masks(n_ctx_chunks)

    wt, wk = _arrange_w_in(w_in)
    w_out_b, w_ffn_in_b, w_ffn_out_b = w_out.astype(BF16), w_ffn_in.astype(BF16), w_ffn_out.astype(BF16)

    def layer_proj(l):
        conv_b = jnp.broadcast_to(mlstm_conv_w[l][:, :, None], (3, 2 * GROUP_W, ROW_TILE))
        return (wt, wk, tables, conv_b, _block_diag(pool_w[l]).astype(BF16), pool_scale[l].reshape(1, GROUP_W))

    for l in range(depth):
        zs, qk, vo, gt, aq, av, ak, dm = _in_proj(xs, mods, norm_g[l, 0:1], layer_proj(l), l, B)
        gate_bias = jnp.broadcast_to(
            mlstm_gate_b[l].reshape(2, 2, N_HEADS).transpose(1, 0, 2).reshape(4 * N_HEADS, 1), (4 * N_HEADS, CHUNK))
        sink_rows = jnp.zeros((SUBLANES, (N_HEADS // KV_HEADS) * CHUNK), F32).at[0:KV_HEADS].set(
            jnp.repeat(attn_sink[l].reshape(KV_HEADS, N_HEADS // KV_HEADS) * LOG2E, CHUNK, axis=1))
        mnorm_b = jnp.broadcast_to(mlstm_norm_g[l][:, None], (GROUP_W, ROW_TILE))
        a, cm, hf, hb = _mixers(zs, sgu_w[l].transpose(0, 2, 1).astype(BF16), sgu_b[l], aq, ak, av, sink_rows, masks,
                                qk, vo, gt, gate_bias, n_ctx_chunks)
        xs = _out_ffn(xs, a, hf, hb, vo, cm, dm, mods, norm_g, mnorm_b, w_out_b, w_ffn_in_b, w_ffn_out_b, l, B,
                      drop_ctx=(l == depth - 1))
    return xs
```

```python
import functools

import jax
import jax.numpy as jnp
from jax import lax
from jax.experimental import pallas as pl
from jax.experimental.pallas import tpu as pltpu

F32 = jnp.float32
BF16 = jnp.bfloat16

GRID_W = 64
ROPE_BASE = 10000.0
EPS = 1e-6
LANES = 128
SUBLANES = 8
BF16_ROWS = 16
CHUNK = 128
HEAD_DIM = 64
N_HEADS = 4
KV_HEADS = 2
GROUP_W = N_HEADS * HEAD_DIM
KV_W = KV_HEADS * HEAD_DIM
POOL_WINDOWS = (2, 4, 8, 16)
HALO = 8
ROW_TILE = 256
NEG = -1e30
VMEM_LIMIT = 56 * 1024 * 1024

R_SGU = 0
R_QK = 512
R_VO = 1024
R_AQ = 1536
R_AV = 1792
R_GATE = 1920
R_END = 1936
T_AK = 0
T_POOL = KV_W
T_END = KV_W + GROUP_W

ST_ROWS = 80
LOG2E = 1.4426950408889634

HI = lax.Precision.HIGHEST


def _dot(a, b):
    return jnp.dot(a, b, preferred_element_type=F32)


def _dot_hi(a, b):
    return jnp.dot(a, b, preferred_element_type=F32, precision=HI)


def _dot_nt(a, b):
    return lax.dot_general(a, b, (((1,), (1,)), ((), ())), preferred_element_type=F32)


def _dot_tn(a, b):
    return lax.dot_general(a, b, (((0,), (0,)), ((), ())), preferred_element_type=F32)


def _rms(x, g):
    return x * lax.rsqrt(jnp.mean(x * x, axis=-1, keepdims=True) + EPS) * g


def _silu(x):
    return x * jax.nn.sigmoid(x)


def _put_chunks(ref, x):
    for i in range(ref.shape[0]):
        ref[i] = x[:, i * CHUNK:(i + 1) * CHUNK].astype(ref.dtype)


def _get_chunks(ref):
    return jnp.concatenate([ref[i] for i in range(ref.shape[0])], axis=1)


def _head_layer_norm_t(v):
    mu = jnp.mean(v, axis=0, keepdims=True)
    vc = v - mu
    return vc * lax.rsqrt(jnp.mean(vc * vc, axis=0, keepdims=True) + EPS)


def _zero_at_first_step(refs):
    @pl.when(pl.program_id(0) == 0)
    def _():
        for ref in refs:
            ref[...] = jnp.zeros_like(ref)


def _mod_kernel(c_ref, w_ref, b_ref, o_ref):
    o_ref[...] = _dot(_silu(c_ref[...]).astype(BF16), w_ref[...].astype(BF16)) + b_ref[...]


def _modulation(cc, w_mod, b_mod):
    depth, d, n6 = w_mod.shape
    rows = cc.shape[0]
    tn = 1536
    return pl.pallas_call(
        _mod_kernel,
        grid=(depth, n6 // tn),
        in_specs=[
            pl.BlockSpec((rows, d), lambda l, j: (0, 0)),
            pl.BlockSpec((None, d, tn), lambda l, j: (l, 0, j)),
            pl.BlockSpec((None, 1, tn), lambda l, j: (l, 0, j)),
        ],
        out_specs=pl.BlockSpec((None, rows, tn), lambda l, j: (l, 0, j)),
        out_shape=jax.ShapeDtypeStruct((depth, rows, n6), F32),
        compiler_params=pltpu.CompilerParams(
            dimension_semantics=("arbitrary", "arbitrary"), vmem_limit_bytes=VMEM_LIMIT),
        name="modulation",
    )(cc, w_mod, b_mod.reshape(depth, 1, n6))


N_IN_SCRATCH = 4


def _first_tile_from_ctx(ctx_ref, x_ref, tile):
    rows = lax.broadcasted_iota(jnp.int32, x_ref.shape, 0)
    return jnp.where(rows < jnp.where(tile == 0, x_ref.shape[0], 0), ctx_ref[...], x_ref[...])


def _project_tile(hb, has_prev, has_next, wt_ref, wk_ref, cos_ref, sin_ref, cost_ref, sint_ref,
                  conv_ref, pw_ref, ps_ref, invc_ref,
                  zs_ref, qk_ref, vo_ref, gt_ref, aq_ref, av_ref, ak_ref, d_ref, z_ref, pcol_ref, zp_ref, prow_ref):
    tm = hb.shape[0]
    zc = _dot_nt(wt_ref[R_QK:R_VO, :], hb)
    _put_chunks(zs_ref, _dot_nt(wt_ref[R_SGU:R_QK, :], hb))

    z = z_ref[...]
    lane = lax.broadcasted_iota(jnp.int32, z.shape, 1)
    before = jnp.where(lane == 0, pcol_ref[:, 0:1] * has_prev, pltpu.roll(z, 1, axis=1))
    after = jnp.where(lane == tm - 1, zc[:, 0:1] * has_next, pltpu.roll(z, tm - 1, axis=1))
    qk = _silu(before * conv_ref[0] + z * conv_ref[1] + after * conv_ref[2])
    _put_chunks(qk_ref, jnp.concatenate([qk[0:GROUP_W], qk[GROUP_W:2 * GROUP_W] * (HEAD_DIM ** -0.5)], axis=0))
    pcol_ref[...] = jnp.broadcast_to(z[:, tm - 1:tm], pcol_ref.shape)
    z_ref[...] = zc

    zp_new = _dot(hb, wk_ref[:, T_POOL:T_END])
    _put_chunks(vo_ref, _dot_nt(wt_ref[R_VO:R_AQ, :], hb))
    zp = zp_ref[...]
    rows = tm + 2 * HALO
    ext = jnp.concatenate([prow_ref[...] * has_prev, zp, zp_new[0:HALO] * has_next], axis=0)
    s2 = ext + pltpu.roll(ext, 1, axis=0)
    s4 = pltpu.roll(s2, 1, axis=0) + pltpu.roll(s2, rows - 1, axis=0)
    s8 = pltpu.roll(s4, 2, axis=0) + pltpu.roll(s4, rows - 2, axis=0)
    s16 = pltpu.roll(s8, 4, axis=0) + pltpu.roll(s8, rows - 4, axis=0)
    sums = [s[HALO:HALO + tm] for s in (s2, s4, s8, s16)]
    plane = lax.broadcasted_iota(jnp.int32, zp.shape, 1)
    pooled = sums[-1]
    for gi in range(len(POOL_WINDOWS) - 2, -1, -1):
        pooled = jnp.where(plane < (gi + 1) * HEAD_DIM, sums[gi], pooled)
    pooled = pooled * invc_ref[...]
    d_ref[...] = (_dot((pooled - zp).astype(BF16), pw_ref[...]) * ps_ref[...]).astype(BF16)
    prow_ref[...] = zp[tm - HALO:tm]
    zp_ref[...] = zp_new

    q = _dot_nt(wt_ref[R_AQ:R_AV, :], hb)
    k = _dot(hb, wk_ref[:, T_AK:T_POOL])
    _put_chunks(av_ref, _dot_nt(wt_ref[R_AV:R_GATE, :], hb))
    _put_chunks(gt_ref, _dot_nt(wt_ref[R_GATE:R_END, :], hb))
    cos_t = jnp.concatenate([cost_ref[...]] * N_HEADS, axis=0)
    sin_t = jnp.concatenate([sint_ref[...]] * N_HEADS, axis=0)
    row = lax.broadcasted_iota(jnp.int32, q.shape, 0)
    rot_q = jnp.where((row & 31) < 16, pltpu.roll(q, GROUP_W - 16, axis=0), pltpu.roll(q, 16, axis=0))
    _put_chunks(aq_ref, q * cos_t + rot_q * sin_t)
    klane = lax.broadcasted_iota(jnp.int32, k.shape, 1)
    rot_k = jnp.where((klane & 31) < 16, pltpu.roll(k, KV_W - 16, axis=1), pltpu.roll(k, 16, axis=1))
    ak_ref[...] = (k * cos_ref[...] + rot_k * sin_ref[...]).astype(BF16)


def _segment_flags(tile, npb):
    j = tile % npb
    return (j >= 2).astype(F32), jnp.logical_and(j >= 1, j < npb - 1).astype(F32)


def _in_kernel(*refs, npb, n_tiles, split_ctx):
    ctx_ref = refs[0] if split_ctx else None
    x_ref, mod_ref, g_ref = (refs[1:] if split_ctx else refs)[0:3]
    project_refs = (refs[1:] if split_ctx else refs)[3:-1]
    hb_ref = refs[-1]
    s = pl.program_id(0)

    _zero_at_first_step((hb_ref,) + tuple(project_refs[-N_IN_SCRATCH:]))
    has_prev, has_next = _segment_flags(jnp.maximum(s - 2, 0), npb)
    gain = g_ref[...] * (1.0 + mod_ref[1:2, :])
    x_tile = _first_tile_from_ctx(ctx_ref, x_ref, jnp.minimum(s, n_tiles - 1) % npb) if split_ctx else x_ref[...]
    hb_next = (_rms(x_tile, gain) + mod_ref[0:1, :]).astype(BF16)
    _project_tile(hb_ref[...], has_prev, has_next, *project_refs)
    hb_ref[...] = hb_next


def _stream_specs(xs, tm, tile_of_step, npb):
    bj = lambda s: (tile_of_step(s) // npb, tile_of_step(s) % npb)
    if isinstance(xs, tuple):
        ctx, x = xs
        D = x.shape[2]
        return [pl.BlockSpec((None, tm, D), lambda s: (bj(s)[0], 0, 0)),
                pl.BlockSpec((None, tm, D), lambda s: (bj(s)[0], jnp.maximum(bj(s)[1] - 1, 0), 0))], [ctx, x]
    return [pl.BlockSpec((None, tm, xs.shape[2]), lambda s: (bj(s)[0], bj(s)[1], 0))], [xs]


def _stream_shape(xs):
    if isinstance(xs, tuple):
        return xs[1].shape[0], xs[0].shape[1] + xs[1].shape[1], xs[1].shape[2]
    return xs.shape


def _project_call_parts(B, S, D, tm, npb, mid, old, proj, layer):
    wt, wk, tables, conv_b, pool_w_bd, pool_scale = proj
    cos_k, sin_k, cos_t, sin_t, inv_cnt = tables
    c2 = lambda s: (0, 0)
    cpt = tm // CHUNK
    pool_kind = lambda j: (j <= 1).astype(jnp.int32) + 2 * ((j == 0) | (j == npb - 1)).astype(jnp.int32)
    in_specs = [
        pl.BlockSpec((None, R_END, D), lambda s: (layer, 0, 0), pipeline_mode=pl.Buffered(1)),
        pl.BlockSpec((None, D, T_END), lambda s: (layer, 0, 0), pipeline_mode=pl.Buffered(1)),
        pl.BlockSpec((tm, KV_W), lambda s: (mid(s) % npb, 0)),
        pl.BlockSpec((tm, KV_W), lambda s: (mid(s) % npb, 0)),
        pl.BlockSpec((HEAD_DIM, tm), lambda s: (0, mid(s) % npb)),
        pl.BlockSpec((HEAD_DIM, tm), lambda s: (0, mid(s) % npb)),
        pl.BlockSpec((None, 3, 2 * GROUP_W, tm), lambda s: (layer, 0, 0, 0)),
        pl.BlockSpec((None, GROUP_W, GROUP_W), lambda s: (layer, 0, 0)),
        pl.BlockSpec((None, 1, GROUP_W), lambda s: (layer, 0, 0)),
        pl.BlockSpec((None, tm, GROUP_W), lambda s: (pool_kind(old(s) % npb), 0, 0)),
    ]
    args = [wt, wk, cos_k, sin_k, cos_t, sin_t, conv_b, pool_w_bd, pool_scale, inv_cnt]

    def fm(rows, dt, tile):
        return (pl.BlockSpec((None, cpt, rows, CHUNK), lambda s: (tile(s) // npb, tile(s) % npb, 0, 0)),
                jax.ShapeDtypeStruct((B, S // CHUNK, rows, CHUNK), dt))

    def tk(cols, dt, tile):
        return (pl.BlockSpec((None, tm, cols), lambda s: (tile(s) // npb, tile(s) % npb, 0)),
                jax.ShapeDtypeStruct((B, S, cols), dt))

    outs = [fm(2 * GROUP_W, BF16, mid), fm(2 * GROUP_W, BF16, old), fm(2 * GROUP_W, BF16, mid),
            fm(4 * N_HEADS, F32, mid), fm(GROUP_W, BF16, mid), fm(KV_W, BF16, mid),
            tk(KV_W, BF16, mid), tk(GROUP_W, BF16, old)]
    scratch = [pltpu.VMEM((2 * GROUP_W, tm), F32), pltpu.VMEM((2 * GROUP_W, LANES), F32),
               pltpu.VMEM((tm, GROUP_W), F32), pltpu.VMEM((HALO, GROUP_W), F32)]
    return in_specs, args, [o[0] for o in outs], [o[1] for o in outs], scratch


def _in_proj(xs, mods, g0, proj, layer, n_batch_rows):
    B, S, D = _stream_shape(xs)
    tm = ROW_TILE
    npb = S // tm
    n_tiles = B * npb
    cur = lambda s: jnp.minimum(s, n_tiles - 1)
    mid = lambda s: jnp.clip(s - 1, 0, n_tiles - 1)
    old = lambda s: jnp.maximum(s - 2, 0)
    kern = functools.partial(_in_kernel, npb=npb, n_tiles=n_tiles, split_ctx=isinstance(xs, tuple))
    x_specs, x_args = _stream_specs(xs, tm, cur, npb)
    p_specs, p_args, out_specs, out_shapes, p_scratch = _project_call_parts(B, S, D, tm, npb, mid, old, proj, layer)
    return pl.pallas_call(
        kern,
        grid=(n_tiles + 2,),
        in_specs=x_specs + [
            pl.BlockSpec((None, None, 6, D),
                         lambda s: (layer, jnp.where(cur(s) % npb == 0, n_batch_rows, cur(s) // npb), 0, 0)),
            pl.BlockSpec((None, 1, D), lambda s: (layer, 0, 0)),
        ] + p_specs,
        out_specs=out_specs,
        out_shape=out_shapes,
        scratch_shapes=p_scratch + [pltpu.VMEM((tm, D), BF16)],
        compiler_params=pltpu.CompilerParams(dimension_semantics=("arbitrary",), vmem_limit_bytes=VMEM_LIMIT),
        name="in_proj",
    )(*x_args, mods, g0, *p_args)


def _log_sigmoid(x):
    return jnp.minimum(x, 0.0) - jnp.log1p(jnp.exp(-jnp.abs(x)))


CPS = 2


def _pair_spec(n_b, rows, pair=lambda p: p, row_block=0):
    return pl.BlockSpec((n_b, CPS, rows, CHUNK), lambda p: (0, pair(p), row_block, 0))


def _mixers_kernel(zs_ref, sw_ref, sb_ref,
                   q_ref, kp_ref, kc_ref, kn_ref, kx_ref, vp_ref, vc_ref, vn_ref, vx_ref, sink_ref, mask0_ref, mask1_ref,
                   qkf_ref, vf_ref, gf_ref, qkb_ref, vb_ref, gb_ref, gbias_ref,
                   a_ref, c_ref, hf_ref, hb_ref, st_ref, m_ref, sc_ref, p_ref, *, n_ctx_chunks):
    n_b = zs_ref.shape[0]
    _zero_at_first_step((st_ref, m_ref))

    group = N_HEADS // KV_HEADS
    zeros = jnp.zeros((HEAD_DIM, group * CHUNK), BF16)
    pairs = [(u, b, kv) for u in range(CPS) for b in range(n_b) for kv in range(KV_HEADS)]
    mask_refs = (mask0_ref, mask1_ref)

    def keys(u, b):
        local = ([kp_ref[b, CHUNK:2 * CHUNK, :], kc_ref[b]] if u == 0 else [kc_ref[b], kn_ref[b, 0:CHUNK, :]])
        return jnp.concatenate(local + [kx_ref[b]], axis=0)

    def values(u, b):
        local = ([vp_ref[b, 1], vc_ref[b, 0], vc_ref[b, 1]] if u == 0 else [vc_ref[b, 0], vc_ref[b, 1], vn_ref[b, 0]])
        return jnp.concatenate(local + [vx_ref[b, i] for i in range(n_ctx_chunks)], axis=1)

    for i, (u, b, kv) in enumerate(pairs):
        q2 = jnp.concatenate([q_ref[b, u, (kv * group + g) * HEAD_DIM:(kv * group + g + 1) * HEAD_DIM, :]
                              for g in range(group)], axis=1)
        qm = jnp.concatenate([q2, zeros] if kv == 0 else [zeros, q2], axis=0)
        sc_ref[i] = _dot(keys(u, b), qm)

    s_i = lax.broadcasted_iota(jnp.int32, (CHUNK, CHUNK), 0)
    t_i = lax.broadcasted_iota(jnp.int32, (CHUNK, CHUNK), 1)
    causal = {True: s_i <= t_i, False: s_i >= t_i}
    ones_row = jnp.where(lax.broadcasted_iota(jnp.int32, (ST_ROWS - HEAD_DIM, CHUNK), 0) == 0, 1.0, 0.0).astype(BF16)
    all_ones = jnp.ones((CHUNK, CHUNK), F32)

    def start_chains(t):
        chains = []
        for b in range(n_b):
            for forward, qk_ref, v_ref, g_ref, o_ref in ((True, qkf_ref, vf_ref, gf_ref, hf_ref),
                                                          (False, qkb_ref, vb_ref, gb_ref, hb_ref)):
                u = t if forward else CPS - 1 - t
                gates = g_ref[b, u] + gbias_ref[...]
                li8 = gates[0:2 * N_HEADS] * LOG2E
                lf8 = _log_sigmoid(gates[2 * N_HEADS:4 * N_HEADS]) * LOG2E
                cum8 = _dot_hi(lf8, jnp.where(causal[forward], 1.0, 0.0).astype(F32))
                tot8 = _dot_hi(lf8, all_ones)
                a8 = li8 - cum8
                for h in range(N_HEADS):
                    r = (0 if forward else N_HEADS) + h
                    ch = dict(b=b, u=u, h=h, mask=causal[forward], o_ref=o_ref,
                              slot=(b * 2 + (0 if forward else 1)) * N_HEADS + h,
                              a=a8[r:r + 1], bcum=cum8[r:r + 1], b_end=tot8[r:r + 1],
                              q=qk_ref[b, u, h * HEAD_DIM:(h + 1) * HEAD_DIM, :],
                              k=qk_ref[b, u, GROUP_W + h * HEAD_DIM:GROUP_W + (h + 1) * HEAD_DIM, :],
                              v=v_ref[b, u, h * HEAD_DIM:(h + 1) * HEAD_DIM, :])
                    ch["kq"] = _dot_tn(ch["k"], ch["q"])
                    chains.append(ch)
        return chains

    def read_state(chains, prev=None):
        for i, ch in enumerate(chains):
            ch["st"] = st_ref[ch["slot"]][:, 0:HEAD_DIM] if prev is None else prev[i]["st_new"]
            ch["m"] = m_ref[ch["slot"]:ch["slot"] + 1, :] if prev is None else prev[i]["m_new"]
            ch["cq"] = _dot(ch["st"].astype(BF16), ch["q"])

    def gate_math(chains):
        for ch in chains:
            a, m, mask = ch["a"], ch["m"], ch["mask"]
            a_col = jnp.broadcast_to(a, (CHUNK, CHUNK)).T
            a_max = jnp.max(a, axis=1, keepdims=True)
            big_m = jnp.maximum(jnp.max(jnp.where(mask, a_col, NEG), axis=0, keepdims=True), m)
            ch["inter"] = jnp.exp2(m - big_m)
            ch["floor"] = jnp.exp2(-(ch["bcum"] + big_m))
            s = ch["kq"] * jnp.exp2(jnp.where(mask, a_col - big_m, NEG))
            ch["rowsum"] = jnp.sum(s, axis=0, keepdims=True)
            ch["s"] = s.astype(BF16)
            m_new = jnp.maximum(ch["b_end"] + m, ch["b_end"] + a_max)
            ch["decay"] = jnp.exp2(ch["b_end"] + m - m_new)
            ch["kw"] = (ch["k"].astype(F32) * jnp.exp2(ch["b_end"] + a - m_new)).astype(BF16)
            ch["m_new"] = m_new

    def finish(chains):
        for ch in chains:
            ch["vs"] = _dot(ch["v"], ch["s"])
            v_ext = jnp.concatenate([ch["v"], ones_row], axis=0)
            ch["upd"] = _dot_nt(v_ext, ch["kw"])
        for ch in chains:
            h, cq, inter = ch["h"], ch["cq"], ch["inter"]
            num = cq[0:HEAD_DIM] * inter + ch["vs"]
            den = inter * cq[HEAD_DIM:HEAD_DIM + 1] + ch["rowsum"]
            ch["o_ref"][ch["b"], ch["u"], h * HEAD_DIM:(h + 1) * HEAD_DIM, :] = (
                num / jnp.maximum(jnp.abs(den), ch["floor"]))
            ch["st_new"] = ch["st"] * ch["decay"][:, 0:HEAD_DIM] + ch["upd"]

    first = start_chains(0)
    read_state(first)
    second = start_chains(1)

    for u in range(CPS):
        for b in range(n_b):
            g = jax.nn.gelu(zs_ref[b, u].astype(F32))
            for h in range(N_HEADS):
                gu = g[h * HEAD_DIM:(h + 1) * HEAD_DIM]
                vh = _head_layer_norm_t(g[GROUP_W + h * HEAD_DIM:GROUP_W + (h + 1) * HEAD_DIM]).astype(BF16)
                mixed = _dot(vh, sw_ref[h]) + sb_ref[h:h + 1, :]
                a_ref[b, u, h * HEAD_DIM:(h + 1) * HEAD_DIM, :] = (gu * mixed).astype(BF16)

    p_sinks = []
    for i, (u, b, kv) in enumerate(pairs):
        sink = sink_ref[kv:kv + 1, :]
        blocks = [sc_ref[i, r:r + CHUNK, :] + mask_refs[u][r:r + CHUNK, :] for r in range(0, sc_ref.shape[1], CHUNK)]
        m = jnp.maximum(jnp.max(functools.reduce(jnp.maximum, blocks), axis=0, keepdims=True), sink)
        for r in range(0, sc_ref.shape[1], CHUNK // 2):
            rows = slice(r, r + CHUNK // 2)
            p_ref[i, rows, :] = jnp.exp2(sc_ref[i, rows, :] + mask_refs[u][rows, :] - m).astype(BF16)
        p_sinks.append(jnp.exp2(sink - m))

    gate_math(first)
    finish(first)

    ones_keys = jnp.where(lax.broadcasted_iota(jnp.int32, (BF16_ROWS, (3 + n_ctx_chunks) * CHUNK), 0) == 0,
                          1.0, 0.0).astype(BF16)
    for i, ((u, b, kv), p_sink) in enumerate(zip(pairs, p_sinks)):
        o_ext = _dot(jnp.concatenate([values(u, b)[kv * HEAD_DIM:(kv + 1) * HEAD_DIM, :], ones_keys], axis=0),
                     p_ref[i])
        o = o_ext[0:HEAD_DIM] / (o_ext[HEAD_DIM:HEAD_DIM + 1] + p_sink)
        for g in range(group):
            h = kv * group + g
            c_ref[b, u, h * HEAD_DIM:(h + 1) * HEAD_DIM, :] = o[:, g * CHUNK:(g + 1) * CHUNK].astype(BF16)

    read_state(second, prev=first)
    gate_math(second)
    finish(second)
    for ch in second:
        st_ref[ch["slot"], :, 0:HEAD_DIM] = ch["st_new"]
        m_ref[ch["slot"]:ch["slot"] + 1, :] = ch["m_new"]


def _attention_masks(n_ctx_chunks):
    group = N_HEADS // KV_HEADS
    n_keys = (3 + n_ctx_chunks) * CHUNK
    j = jnp.arange(n_keys)[:, None]
    i = (jnp.arange(group * CHUNK) & (CHUNK - 1))[None, :]
    masks = []
    for cur_ok, prev_ok, next_ok in ((False, False, False), (True, True, True), (True, False, True),
                                     (True, True, False), (True, False, False)):
        lo = (0 if prev_ok else CHUNK) if cur_ok else 3 * CHUNK
        hi = 3 * CHUNK if next_ok else 2 * CHUNK
        valid = ((j >= i) & (j <= i + 2 * CHUNK) & (j >= lo) & (j < hi)) | (j >= 3 * CHUNK)
        masks.append(jnp.where(valid, 0.0, NEG).astype(F32))
    return jnp.stack(masks)


def _mixers(zs, sgu_wt, sgu_b, aq, ak, av, sink_rows, masks, qk, vo, gt, gate_bias, layer, n_ctx_chunks):
    B, n_chunks = zs.shape[:2]
    assert n_chunks % CPS == 0 and n_ctx_chunks == CPS
    n_pairs = n_chunks // CPS
    ctx_w = n_ctx_chunks * CHUNK
    group = N_HEADS // KV_HEADS
    n_keys = 3 * CHUNK + ctx_w
    kern = functools.partial(_mixers_kernel, n_ctx_chunks=n_ctx_chunks)
    ident = lambda p: p
    prev = lambda p: jnp.maximum(p - 1, 0)
    nxt = lambda p: jnp.minimum(p + 1, n_pairs - 1)
    bwd = lambda p: jnp.where(p == 0, 0, n_pairs - p)
    kspec = lambda f: pl.BlockSpec((B, CPS * CHUNK, KV_W), lambda p: (0, f(p), 0))
    vspec = lambda f: _pair_spec(B, KV_W, f)

    def mask_spec(u):
        def kind(p):
            c = CPS * p + u
            no_prev = (c == n_ctx_chunks).astype(jnp.int32)
            no_next = (c == n_chunks - 1).astype(jnp.int32)
            return jnp.where(c < n_ctx_chunks, 0, 1 + no_prev + 2 * no_next)
        return pl.BlockSpec((None, n_keys, group * CHUNK), lambda p: (kind(p), 0, 0))

    def scan_specs(order):
        return [_pair_spec(B, 2 * GROUP_W, order), _pair_spec(B, GROUP_W, order),
                _pair_spec(B, 4 * N_HEADS, order)]

    n_slots = B * 2 * N_HEADS
    return pl.pallas_call(
        kern,
        grid=(n_pairs,),
        in_specs=[
            _pair_spec(B, 2 * GROUP_W),
            pl.BlockSpec((None, N_HEADS, CHUNK, CHUNK), lambda p: (layer, 0, 0, 0)),
            pl.BlockSpec((None, N_HEADS, CHUNK), lambda p: (layer, 0, 0)),
            _pair_spec(B, GROUP_W),
            kspec(prev), kspec(ident), kspec(nxt), pl.BlockSpec((B, ctx_w, KV_W), lambda p: (0, 0, 0)),
            vspec(prev), vspec(ident), vspec(nxt),
            pl.BlockSpec((B, n_ctx_chunks, KV_W, CHUNK), lambda p: (0, 0, 0, 0)),
            pl.BlockSpec((None, SUBLANES, group * CHUNK), lambda p: (layer, 0, 0)),
            mask_spec(0), mask_spec(1),
        ] + scan_specs(ident) + scan_specs(bwd) + [
            pl.BlockSpec((None, 4 * N_HEADS, CHUNK), lambda p: (layer, 0, 0))],
        out_specs=[_pair_spec(B, GROUP_W), _pair_spec(B, GROUP_W),
                   _pair_spec(B, GROUP_W, ident), _pair_spec(B, GROUP_W, bwd)],
        out_shape=[jax.ShapeDtypeStruct((B, n_chunks, GROUP_W, CHUNK), BF16)] * 2
                  + [jax.ShapeDtypeStruct((B, n_chunks, GROUP_W, CHUNK), F32)] * 2,
        scratch_shapes=[
            pltpu.VMEM((n_slots, ST_ROWS, LANES), F32),
            pltpu.VMEM((n_slots, LANES), F32),
            pltpu.VMEM((CPS * B * KV_HEADS, n_keys, group * CHUNK), F32),
            pltpu.VMEM((CPS * B * KV_HEADS, n_keys, group * CHUNK), BF16),
        ],
        compiler_params=pltpu.CompilerParams(dimension_semantics=("arbitrary",)),
        name="mixers",
    )(zs, sgu_wt, sgu_b, aq, ak, ak, ak, ak, av, av, av, av, sink_rows, masks, masks, qk, vo, gt, qk, vo, gt,
      gate_bias)


N_OUT_SCRATCH = 6


def _hidden_chunks(hidden):
    step = 3 * 2 * LANES
    return [(lo, min(lo + step, hidden)) for lo in range(0, hidden, step)]


def _out_kernel(*refs, hidden, npb, n_tiles, split_ctx):
    _zero_at_first_step(refs[-N_OUT_SCRATCH:-1])
    ctx_ref = refs[0] if split_ctx else None
    (x_ref, a_ref, hf_ref, hb_ref, zo_ref, c_ref, d_ref, moda_ref, modc_ref, g_ref, mg_ref,
     wo_ref, wi_ref, wf_ref, o_ref, x1_ref, y_ref, f_ref, tb_ref, act_ref, bm_ref) = refs[1:] if split_ctx else refs
    s = pl.program_id(0)
    f_ref[s % 2] = _dot(act_ref[...], wf_ref[...])

    x_tile = (_first_tile_from_ctx(ctx_ref, x_ref, jnp.clip(s - 1, 0, n_tiles - 1) % npb)
              if split_ctx else x_ref[...])
    x1 = x_tile + _rms(y_ref[...], moda_ref[2:3, :] * g_ref[1:2, :])
    x1_ref[s % 3] = x1
    tb_ref[...] = (_rms(x1, g_ref[2:3, :] * (1.0 + moda_ref[4:5, :])) + moda_ref[3:4, :]).astype(BF16)

    o_ref[...] = x1_ref[(s + 1) % 3] + _rms(f_ref[(s + 1) % 2], modc_ref[5:6, :] * g_ref[3:4, :])

    for lo, hi in _hidden_chunks(hidden):
        gate = _dot(tb_ref[...], wi_ref[:, lo:hi])
        up = _dot(tb_ref[...], wi_ref[:, hidden + lo:hidden + hi])
        act_ref[:, lo:hi] = (_silu(gate) * up).astype(BF16)

    hs = _get_chunks(hf_ref) + _get_chunks(hb_ref)
    hn = jnp.concatenate([_head_layer_norm_t(hs[h * HEAD_DIM:(h + 1) * HEAD_DIM]) for h in range(N_HEADS)], axis=0)
    bm_ref[...] = (hn * mg_ref[...] * jax.nn.sigmoid(_get_chunks(zo_ref).astype(F32))).astype(BF16)
    y_ref[...] = (_dot_tn(_get_chunks(a_ref), wo_ref[0:GROUP_W, :])
                  + _dot_tn(bm_ref[...], wo_ref[GROUP_W:2 * GROUP_W, :])
                  + _dot_tn(_get_chunks(c_ref), wo_ref[2 * GROUP_W:3 * GROUP_W, :])
                  + _dot(d_ref[...], wo_ref[3 * GROUP_W:4 * GROUP_W, :]))


def _out_ffn(xs, a, hf, hb, vo, c, d, mods, norm_g, mnorm_g, w_out, w_ffn_in, w_ffn_out, layer, n_batch_rows,
             drop_ctx):
    B, S, D = _stream_shape(xs)
    tm = ROW_TILE
    npb = S // tm
    n_tiles = B * npb
    hidden = w_ffn_out.shape[1]
    cur = lambda s: jnp.minimum(s, n_tiles - 1)
    mid = lambda s: jnp.clip(s - 1, 0, n_tiles - 1)
    prv = lambda s: jnp.clip(s - 3, 0, n_tiles - 1)
    row = lambda s: (cur(s) // npb, cur(s) % npb, 0)
    mod_row = lambda t: (layer, jnp.where(t % npb == 0, n_batch_rows, t // npb), 0, 0)
    c2 = lambda s: (0, 0)
    this_layer = lambda s: (layer, 0, 0)
    cpt = tm // CHUNK
    fm = pl.BlockSpec((None, cpt, GROUP_W, CHUNK), lambda s: (cur(s) // npb, cur(s) % npb, 0, 0))
    split_ctx = isinstance(xs, tuple)
    x_specs, x_args = _stream_specs(xs, tm, mid, npb)
    in_specs = x_specs + [
        fm, fm, fm,
        pl.BlockSpec((None, cpt, GROUP_W, CHUNK), lambda s: (cur(s) // npb, cur(s) % npb, 1, 0)),
        fm,
        pl.BlockSpec((None, tm, GROUP_W), row),
        pl.BlockSpec((None, None, 6, D), lambda s: mod_row(mid(s))),
        pl.BlockSpec((None, None, 6, D), lambda s: mod_row(prv(s))),
        pl.BlockSpec((None, 4, D), this_layer),
        pl.BlockSpec((None, GROUP_W, tm), this_layer),
        pl.BlockSpec((None, D, D), this_layer, pipeline_mode=pl.Buffered(1)),
        pl.BlockSpec((None, D, 2 * hidden), this_layer, pipeline_mode=pl.Buffered(1)),
        pl.BlockSpec((None, hidden, D), this_layer, pipeline_mode=pl.Buffered(1)),
    ]
    args = x_args + [a, hf, hb, vo, c, d, mods, mods, norm_g, mnorm_g, w_out, w_ffn_in, w_ffn_out]
    scratch = [pltpu.VMEM((3, tm, D), F32), pltpu.VMEM((tm, D), F32), pltpu.VMEM((2, tm, D), F32),
               pltpu.VMEM((tm, D), BF16), pltpu.VMEM((tm, hidden), BF16), pltpu.VMEM((GROUP_W, tm), BF16)]
    params = pltpu.CompilerParams(dimension_semantics=("arbitrary",), vmem_limit_bytes=VMEM_LIMIT)
    static = dict(hidden=hidden, npb=npb, n_tiles=n_tiles, split_ctx=split_ctx)
    if drop_ctx:
        out_spec = pl.BlockSpec((None, tm, D), lambda s: (prv(s) // npb, jnp.maximum(prv(s) % npb - 1, 0), 0))
        out_shape = jax.ShapeDtypeStruct((B, S - tm, D), F32)
    else:
        out_spec = pl.BlockSpec((None, tm, D), lambda s: (prv(s) // npb, prv(s) % npb, 0))
        out_shape = jax.ShapeDtypeStruct((B, S, D), F32)
    return pl.pallas_call(
        functools.partial(_out_kernel, **static),
        grid=(n_tiles + 3,),
        in_specs=in_specs,
        out_specs=out_spec,
        out_shape=out_shape,
        scratch_shapes=scratch, compiler_params=params, name="out_ffn",
    )(*args)


def _rope_tables(n_tokens, ctx_len, tm):
    rows = n_tokens // GRID_W
    axis_freq = HEAD_DIM // 4
    inv = jnp.power(ROPE_BASE, -jnp.arange(axis_freq, dtype=F32) * 2.0 / (2 * axis_freq))
    ar = jnp.arange(rows, dtype=F32)[:, None] * inv
    ac = jnp.arange(GRID_W, dtype=F32)[:, None] * inv

    def per_token(fn):
        r = jnp.broadcast_to(fn(ar)[:, None, :], (rows, GRID_W, axis_freq))
        c = jnp.broadcast_to(fn(ac)[None, :, :], (rows, GRID_W, axis_freq))
        return jnp.concatenate([r, r, c, c], axis=-1).reshape(n_tokens, HEAD_DIM)

    cos, sin = per_token(jnp.cos), per_token(jnp.sin)
    sign = jnp.where((jnp.arange(HEAD_DIM) % 32) < 16, -1.0, 1.0).astype(F32)
    cos = jnp.concatenate([jnp.ones((ctx_len, HEAD_DIM), F32), cos], axis=0)
    sin = jnp.concatenate([jnp.zeros((ctx_len, HEAD_DIM), F32), sin * sign], axis=0)
    t = jnp.arange(tm)[:, None]
    half = jnp.asarray([w // 2 for w in POOL_WINDOWS])[None, :]
    inv_cnt = []
    for last in (False, True):
        for first in (False, True):
            lo = jnp.maximum(t - half, 0) if first else t - half
            hi = jnp.minimum(t + half, tm) if last else t + half
            inv_cnt.append(jnp.repeat(1.0 / (hi - lo).astype(F32), HEAD_DIM, axis=1))
    q_scale = HEAD_DIM ** -0.5 * LOG2E
    return (jnp.tile(cos, (1, KV_HEADS)), jnp.tile(sin, (1, KV_HEADS)), cos.T * q_scale, sin.T * q_scale,
            jnp.stack(inv_cnt))


def _arrange_w_in(w_in):
    depth = w_in.shape[0]
    ml0 = 2 * GROUP_W
    gate0 = ml0 + 4 * GROUP_W
    att0 = gate0 + 4 * N_HEADS
    pool0 = att0 + GROUP_W + 2 * KV_W
    w_t = jnp.swapaxes(w_in, 1, 2)
    gates = w_t[:, gate0:att0].reshape(depth, 2, 2, N_HEADS, -1)
    gates = gates.transpose(0, 2, 1, 3, 4).reshape(depth, 4 * N_HEADS, -1)
    feat = jnp.concatenate([
        w_t[:, 0:gate0],
        w_t[:, att0:att0 + GROUP_W],
        w_t[:, att0 + GROUP_W + KV_W:pool0],
        gates,
    ], axis=1)
    tok = jnp.concatenate([w_t[:, att0 + GROUP_W:att0 + GROUP_W + KV_W], w_t[:, pool0:pool0 + GROUP_W]], axis=1)
    return feat.astype(BF16), jnp.swapaxes(tok.astype(BF16), 1, 2)


def _block_diag(w):
    g = w.shape[0]
    eye = jnp.eye(g, dtype=w.dtype)
    return (eye[:, None, :, None] * w[:, :, None, :]).reshape(g * w.shape[1], g * w.shape[2])


def kernel(x, c, ctx, c_ctx, w_mod, b_mod, norm_g, w_in, w_out, sgu_w, sgu_b, mlstm_conv_w, mlstm_gate_b,
           mlstm_norm_g, attn_sink, pool_w, pool_scale, w_ffn_in, w_ffn_out):
    B, N, D = x.shape
    ctx_len = ctx.shape[1]
    depth = w_mod.shape[0]
    assert D == 4 * GROUP_W and N % ROW_TILE == 0 and ctx_len == ROW_TILE
    assert w_in.shape[2] == 2 * GROUP_W + 4 * GROUP_W + 4 * N_HEADS + GROUP_W + 2 * KV_W + GROUP_W
    n_ctx_chunks = ctx_len // CHUNK

    xs = (ctx, x)
    rows = -(-(B + 1) // SUBLANES) * SUBLANES
    cc = jnp.zeros((rows, D), F32).at[0:B].set(c).at[B].set(c_ctx)
    mods = _modulation(cc, w_mod, b_mod).reshape(depth, rows, 6, D)
    tables = _rope_tables(N, ctx_len, ROW_TILE)
    masks = _attention_masks(n_ctx_chunks)

    wt, wk = _arrange_w_in(w_in)
    w_out_b, w_ffn_in_b, w_ffn_out_b = w_out.astype(BF16), w_ffn_in.astype(BF16), w_ffn_out.astype(BF16)

    group = N_HEADS // KV_HEADS
    conv_b = jnp.broadcast_to(mlstm_conv_w[:, :, :, None], (depth, 3, 2 * GROUP_W, ROW_TILE))
    proj = (wt, wk, tables, conv_b, jax.vmap(_block_diag)(pool_w).astype(BF16), pool_scale.reshape(depth, 1, GROUP_W))
    gate_bias = jnp.broadcast_to(
        mlstm_gate_b.reshape(depth, 2, 2, N_HEADS).transpose(0, 2, 1, 3).reshape(depth, 4 * N_HEADS, 1),
        (depth, 4 * N_HEADS, CHUNK))
    sink_rows = jnp.zeros((depth, SUBLANES, group * CHUNK), F32).at[:, 0:KV_HEADS].set(
        jnp.repeat(attn_sink.reshape(depth, KV_HEADS, group) * LOG2E, CHUNK, axis=2))
    mnorm_b = jnp.broadcast_to(mlstm_norm_g[:, :, None], (depth, GROUP_W, ROW_TILE))
    sgu_wt = sgu_w.transpose(0, 1, 3, 2).astype(BF16)

    for l in range(depth):
        zs, qk, vo, gt, aq, av, ak, dm = _in_proj(xs, mods, norm_g[:, 0:1], proj, l, B)
        a, cm, hf, hb = _mixers(zs, sgu_wt, sgu_b, aq, ak, av, sink_rows, masks, qk, vo, gt, gate_bias, l,
                                n_ctx_chunks)
        xs = _out_ffn(xs, a, hf, hb, vo, cm, dm, mods, norm_g, mnorm_b, w_out_b, w_ffn_in_b, w_ffn_out_b, l, B,
                      drop_ctx=(l == depth - 1))
    return xs
```

```python
import functools

import jax
import jax.numpy as jnp
from jax import lax
from jax.experimental import pallas as pl
from jax.experimental.pallas import tpu as pltpu

F32 = jnp.float32
BF16 = jnp.bfloat16

GRID_W = 64
ROPE_BASE = 10000.0
EPS = 1e-6
LANES = 128
SUBLANES = 8
BF16_ROWS = 16
CHUNK = 128
HEAD_DIM = 64
N_HEADS = 4
KV_HEADS = 2
GROUP_W = N_HEADS * HEAD_DIM
KV_W = KV_HEADS * HEAD_DIM
POOL_WINDOWS = (2, 4, 8, 16)
HALO = 8
ROW_TILE = 256
NEG = -1e30
VMEM_LIMIT = 56 * 1024 * 1024

R_SGU = 0
R_QK = 512
R_VO = 1024
R_AQ = 1536
R_AV = 1792
R_GATE = 1920
R_END = 1936
T_AK = 0
T_POOL = KV_W
T_END = KV_W + GROUP_W

ST_ROWS = 80
LOG2E = 1.4426950408889634

HI = lax.Precision.HIGHEST


def _dot(a, b):
    return jnp.dot(a, b, preferred_element_type=F32)


def _dot_hi(a, b):
    return jnp.dot(a, b, preferred_element_type=F32, precision=HI)


def _dot_nt(a, b):
    return lax.dot_general(a, b, (((1,), (1,)), ((), ())), preferred_element_type=F32)


def _dot_tn(a, b):
    return lax.dot_general(a, b, (((0,), (0,)), ((), ())), preferred_element_type=F32)


def _rms(x, g):
    return x * lax.rsqrt(jnp.mean(x * x, axis=-1, keepdims=True) + EPS) * g


def _silu(x):
    return x * jax.nn.sigmoid(x)


def _put_chunks(ref, x):
    for i in range(ref.shape[0]):
        ref[i] = x[:, i * CHUNK:(i + 1) * CHUNK].astype(ref.dtype)


def _get_chunks(ref):
    return jnp.concatenate([ref[i] for i in range(ref.shape[0])], axis=1)


def _head_layer_norm_t(v):
    mu = jnp.mean(v, axis=0, keepdims=True)
    vc = v - mu
    return vc * lax.rsqrt(jnp.mean(vc * vc, axis=0, keepdims=True) + EPS)


def _zero_at_first_step(refs):
    @pl.when(pl.program_id(0) == 0)
    def _():
        for ref in refs:
            ref[...] = jnp.zeros_like(ref)


def _mod_kernel(c_ref, w_ref, b_ref, o_ref):
    o_ref[...] = _dot(_silu(c_ref[...]).astype(BF16), w_ref[...].astype(BF16)) + b_ref[...]


def _modulation(cc, w_mod, b_mod):
    depth, d, n6 = w_mod.shape
    rows = cc.shape[0]
    tn = 1536
    return pl.pallas_call(
        _mod_kernel,
        grid=(depth, n6 // tn),
        in_specs=[
            pl.BlockSpec((rows, d), lambda l, j: (0, 0)),
            pl.BlockSpec((None, d, tn), lambda l, j: (l, 0, j)),
            pl.BlockSpec((None, 1, tn), lambda l, j: (l, 0, j)),
        ],
        out_specs=pl.BlockSpec((None, rows, tn), lambda l, j: (l, 0, j)),
        out_shape=jax.ShapeDtypeStruct((depth, rows, n6), F32),
        compiler_params=pltpu.CompilerParams(
            dimension_semantics=("arbitrary", "arbitrary"), vmem_limit_bytes=VMEM_LIMIT),
        name="modulation",
    )(cc, w_mod, b_mod.reshape(depth, 1, n6))


N_IN_SCRATCH = 4


def _first_tile_from_ctx(ctx_ref, x_ref, tile):
    rows = lax.broadcasted_iota(jnp.int32, x_ref.shape, 0)
    return jnp.where(rows < jnp.where(tile == 0, x_ref.shape[0], 0), ctx_ref[...], x_ref[...])


def _project_tile(hb, has_prev, has_next, wt_ref, wk_ref, cos_ref, sin_ref, cost_ref, sint_ref,
                  conv_ref, pw_ref, ps_ref, invc_ref,
                  zs_ref, qk_ref, vo_ref, gt_ref, aq_ref, av_ref, ak_ref, d_ref, z_ref, pcol_ref, zp_ref, prow_ref):
    tm = hb.shape[0]
    zc = _dot_nt(wt_ref[R_QK:R_VO, :], hb)
    _put_chunks(zs_ref, _dot_nt(wt_ref[R_SGU:R_QK, :], hb))

    z = z_ref[...]
    lane = lax.broadcasted_iota(jnp.int32, z.shape, 1)
    before = jnp.where(lane == 0, pcol_ref[:, 0:1] * has_prev, pltpu.roll(z, 1, axis=1))
    after = jnp.where(lane == tm - 1, zc[:, 0:1] * has_next, pltpu.roll(z, tm - 1, axis=1))
    qk = _silu(before * conv_ref[0] + z * conv_ref[1] + after * conv_ref[2])
    _put_chunks(qk_ref, jnp.concatenate([qk[0:GROUP_W], qk[GROUP_W:2 * GROUP_W] * (HEAD_DIM ** -0.5)], axis=0))
    pcol_ref[...] = jnp.broadcast_to(z[:, tm - 1:tm], pcol_ref.shape)
    z_ref[...] = zc

    zp_new = _dot(hb, wk_ref[:, T_POOL:T_END])
    _put_chunks(vo_ref, _dot_nt(wt_ref[R_VO:R_AQ, :], hb))
    zp = zp_ref[...]
    rows = tm + 2 * HALO
    ext = jnp.concatenate([prow_ref[...] * has_prev, zp, zp_new[0:HALO] * has_next], axis=0)
    s2 = ext + pltpu.roll(ext, 1, axis=0)
    s4 = pltpu.roll(s2, 1, axis=0) + pltpu.roll(s2, rows - 1, axis=0)
    s8 = pltpu.roll(s4, 2, axis=0) + pltpu.roll(s4, rows - 2, axis=0)
    s16 = pltpu.roll(s8, 4, axis=0) + pltpu.roll(s8, rows - 4, axis=0)
    sums = [s[HALO:HALO + tm] for s in (s2, s4, s8, s16)]
    plane = lax.broadcasted_iota(jnp.int32, zp.shape, 1)
    pooled = sums[-1]
    for gi in range(len(POOL_WINDOWS) - 2, -1, -1):
        pooled = jnp.where(plane < (gi + 1) * HEAD_DIM, sums[gi], pooled)
    pooled = pooled * invc_ref[...]
    d_ref[...] = (_dot((pooled - zp).astype(BF16), pw_ref[...]) * ps_ref[...]).astype(BF16)
    prow_ref[...] = zp[tm - HALO:tm]
    zp_ref[...] = zp_new

    q = _dot_nt(wt_ref[R_AQ:R_AV, :], hb)
    k = _dot(hb, wk_ref[:, T_AK:T_POOL])
    _put_chunks(av_ref, _dot_nt(wt_ref[R_AV:R_GATE, :], hb))
    _put_chunks(gt_ref, _dot_nt(wt_ref[R_GATE:R_END, :], hb))
    cos_t = jnp.concatenate([cost_ref[...]] * N_HEADS, axis=0)
    sin_t = jnp.concatenate([sint_ref[...]] * N_HEADS, axis=0)
    row = lax.broadcasted_iota(jnp.int32, q.shape, 0)
    rot_q = jnp.where((row & 31) < 16, pltpu.roll(q, GROUP_W - 16, axis=0), pltpu.roll(q, 16, axis=0))
    _put_chunks(aq_ref, q * cos_t + rot_q * sin_t)
    klane = lax.broadcasted_iota(jnp.int32, k.shape, 1)
    rot_k = jnp.where((klane & 31) < 16, pltpu.roll(k, KV_W - 16, axis=1), pltpu.roll(k, 16, axis=1))
    ak_ref[...] = (k * cos_ref[...] + rot_k * sin_ref[...]).astype(BF16)


def _segment_flags(tile, npb):
    j = tile % npb
    return (j >= 2).astype(F32), jnp.logical_and(j >= 1, j < npb - 1).astype(F32)


def _in_kernel(*refs, npb, n_tiles, split_ctx):
    ctx_ref = refs[0] if split_ctx else None
    x_ref, mod_ref, g_ref = (refs[1:] if split_ctx else refs)[0:3]
    project_refs = (refs[1:] if split_ctx else refs)[3:-1]
    hb_ref = refs[-1]
    s = pl.program_id(0)

    _zero_at_first_step((hb_ref,) + tuple(project_refs[-N_IN_SCRATCH:]))
    has_prev, has_next = _segment_flags(jnp.maximum(s - 2, 0), npb)
    gain = g_ref[...] * (1.0 + mod_ref[1:2, :])
    x_tile = _first_tile_from_ctx(ctx_ref, x_ref, jnp.minimum(s, n_tiles - 1) % npb) if split_ctx else x_ref[...]
    hb_next = (_rms(x_tile, gain) + mod_ref[0:1, :]).astype(BF16)
    _project_tile(hb_ref[...], has_prev, has_next, *project_refs)
    hb_ref[...] = hb_next


def _stream_specs(xs, tm, tile_of_step, npb):
    bj = lambda s: (tile_of_step(s) // npb, tile_of_step(s) % npb)
    if isinstance(xs, tuple):
        ctx, x = xs
        D = x.shape[2]
        return [pl.BlockSpec((None, tm, D), lambda s: (bj(s)[0], 0, 0)),
                pl.BlockSpec((None, tm, D), lambda s: (bj(s)[0], jnp.maximum(bj(s)[1] - 1, 0), 0))], [ctx, x]
    return [pl.BlockSpec((None, tm, xs.shape[2]), lambda s: (bj(s)[0], bj(s)[1], 0))], [xs]


def _stream_shape(xs):
    if isinstance(xs, tuple):
        return xs[1].shape[0], xs[0].shape[1] + xs[1].shape[1], xs[1].shape[2]
    return xs.shape


def _project_call_parts(B, S, D, tm, npb, mid, old, proj, layer):
    wt, wk, tables, conv_b, pool_w_bd, pool_scale = proj
    cos_k, sin_k, cos_t, sin_t, inv_cnt = tables
    c2 = lambda s: (0, 0)
    cpt = tm // CHUNK
    pool_kind = lambda j: (j <= 1).astype(jnp.int32) + 2 * ((j == 0) | (j == npb - 1)).astype(jnp.int32)
    in_specs = [
        pl.BlockSpec((None, R_END, D), lambda s: (layer, 0, 0), pipeline_mode=pl.Buffered(1)),
        pl.BlockSpec((None, D, T_END), lambda s: (layer, 0, 0), pipeline_mode=pl.Buffered(1)),
        pl.BlockSpec((tm, KV_W), lambda s: (mid(s) % npb, 0)),
        pl.BlockSpec((tm, KV_W), lambda s: (mid(s) % npb, 0)),
        pl.BlockSpec((HEAD_DIM, tm), lambda s: (0, mid(s) % npb)),
        pl.BlockSpec((HEAD_DIM, tm), lambda s: (0, mid(s) % npb)),
        pl.BlockSpec((None, 3, 2 * GROUP_W, tm), lambda s: (layer, 0, 0, 0)),
        pl.BlockSpec((None, GROUP_W, GROUP_W), lambda s: (layer, 0, 0)),
        pl.BlockSpec((None, 1, GROUP_W), lambda s: (layer, 0, 0)),
        pl.BlockSpec((None, tm, GROUP_W), lambda s: (pool_kind(old(s) % npb), 0, 0)),
    ]
    args = [wt, wk, cos_k, sin_k, cos_t, sin_t, conv_b, pool_w_bd, pool_scale, inv_cnt]

    def fm(rows, dt, tile):
        return (pl.BlockSpec((None, cpt, rows, CHUNK), lambda s: (tile(s) // npb, tile(s) % npb, 0, 0)),
                jax.ShapeDtypeStruct((B, S // CHUNK, rows, CHUNK), dt))

    def tk(cols, dt, tile):
        return (pl.BlockSpec((None, tm, cols), lambda s: (tile(s) // npb, tile(s) % npb, 0)),
                jax.ShapeDtypeStruct((B, S, cols), dt))

    outs = [fm(2 * GROUP_W, BF16, mid), fm(2 * GROUP_W, BF16, old), fm(2 * GROUP_W, BF16, mid),
            fm(4 * N_HEADS, F32, mid), fm(GROUP_W, BF16, mid), fm(KV_W, BF16, mid),
            tk(KV_W, BF16, mid), tk(GROUP_W, BF16, old)]
    scratch = [pltpu.VMEM((2 * GROUP_W, tm), F32), pltpu.VMEM((2 * GROUP_W, LANES), F32),
               pltpu.VMEM((tm, GROUP_W), F32), pltpu.VMEM((HALO, GROUP_W), F32)]
    return in_specs, args, [o[0] for o in outs], [o[1] for o in outs], scratch


def _in_proj(xs, mods, g0, proj, layer, n_batch_rows):
    B, S, D = _stream_shape(xs)
    tm = ROW_TILE
    npb = S // tm
    n_tiles = B * npb
    cur = lambda s: jnp.minimum(s, n_tiles - 1)
    mid = lambda s: jnp.clip(s - 1, 0, n_tiles - 1)
    old = lambda s: jnp.maximum(s - 2, 0)
    kern = functools.partial(_in_kernel, npb=npb, n_tiles=n_tiles, split_ctx=isinstance(xs, tuple))
    x_specs, x_args = _stream_specs(xs, tm, cur, npb)
    p_specs, p_args, out_specs, out_shapes, p_scratch = _project_call_parts(B, S, D, tm, npb, mid, old, proj, layer)
    return pl.pallas_call(
        kern,
        grid=(n_tiles + 2,),
        in_specs=x_specs + [
            pl.BlockSpec((None, None, 6, D),
                         lambda s: (layer, jnp.where(cur(s) % npb == 0, n_batch_rows, cur(s) // npb), 0, 0)),
            pl.BlockSpec((None, 1, D), lambda s: (layer, 0, 0)),
        ] + p_specs,
        out_specs=out_specs,
        out_shape=out_shapes,
        scratch_shapes=p_scratch + [pltpu.VMEM((tm, D), BF16)],
        compiler_params=pltpu.CompilerParams(dimension_semantics=("arbitrary",), vmem_limit_bytes=VMEM_LIMIT),
        name="in_proj",
    )(*x_args, mods, g0, *p_args)


def _log_sigmoid(x):
    return jnp.minimum(x, 0.0) - jnp.log1p(jnp.exp(-jnp.abs(x)))


CPS = 2


def _pair_spec(n_b, rows, pair=lambda p: p, row_block=0):
    return pl.BlockSpec((n_b, CPS, rows, CHUNK), lambda p: (0, pair(p), row_block, 0))


def _mixers_kernel(zs_ref, sw_ref, sb_ref,
                   q_ref, kp_ref, kc_ref, kn_ref, kx_ref, vp_ref, vc_ref, vn_ref, vx_ref, sink_ref, mask0_ref, mask1_ref,
                   qkf_ref, vf_ref, gf_ref, qkb_ref, vb_ref, gb_ref, gbias_ref,
                   a_ref, c_ref, hf_ref, hb_ref, st_ref, m_ref, sc_ref, p_ref, *, n_ctx_chunks):
    n_b = zs_ref.shape[0]
    _zero_at_first_step((st_ref, m_ref))

    group = N_HEADS // KV_HEADS
    zeros = jnp.zeros((HEAD_DIM, group * CHUNK), BF16)
    pairs = [(u, b, kv) for u in range(CPS) for b in range(n_b) for kv in range(KV_HEADS)]
    mask_refs = (mask0_ref, mask1_ref)

    def keys(u, b):
        local = ([kp_ref[b, CHUNK:2 * CHUNK, :], kc_ref[b]] if u == 0 else [kc_ref[b], kn_ref[b, 0:CHUNK, :]])
        return jnp.concatenate(local + [kx_ref[b]], axis=0)

    def values(u, b):
        local = ([vp_ref[b, 1], vc_ref[b, 0], vc_ref[b, 1]] if u == 0 else [vc_ref[b, 0], vc_ref[b, 1], vn_ref[b, 0]])
        return jnp.concatenate(local + [vx_ref[b, i] for i in range(n_ctx_chunks)], axis=1)

    for i, (u, b, kv) in enumerate(pairs):
        q2 = jnp.concatenate([q_ref[b, u, (kv * group + g) * HEAD_DIM:(kv * group + g + 1) * HEAD_DIM, :]
                              for g in range(group)], axis=1)
        qm = jnp.concatenate([q2, zeros] if kv == 0 else [zeros, q2], axis=0)
        sc_ref[i] = _dot(keys(u, b), qm)

    s_i = lax.broadcasted_iota(jnp.int32, (CHUNK, CHUNK), 0)
    t_i = lax.broadcasted_iota(jnp.int32, (CHUNK, CHUNK), 1)
    causal = {True: s_i <= t_i, False: s_i >= t_i}
    ones_row = jnp.where(lax.broadcasted_iota(jnp.int32, (ST_ROWS - HEAD_DIM, CHUNK), 0) == 0, 1.0, 0.0).astype(BF16)
    all_ones = jnp.ones((CHUNK, CHUNK), F32)

    def start_chains(t):
        chains = []
        for b in range(n_b):
            for forward, qk_ref, v_ref, g_ref, o_ref in ((True, qkf_ref, vf_ref, gf_ref, hf_ref),
                                                          (False, qkb_ref, vb_ref, gb_ref, hb_ref)):
                u = t if forward else CPS - 1 - t
                gates = g_ref[b, u] + gbias_ref[...]
                li8 = gates[0:2 * N_HEADS] * LOG2E
                lf8 = _log_sigmoid(gates[2 * N_HEADS:4 * N_HEADS]) * LOG2E
                cum8 = _dot_hi(lf8, jnp.where(causal[forward], 1.0, 0.0).astype(F32))
                tot8 = _dot_hi(lf8, all_ones)
                a8 = li8 - cum8
                for h in range(N_HEADS):
                    r = (0 if forward else N_HEADS) + h
                    ch = dict(b=b, u=u, h=h, mask=causal[forward], o_ref=o_ref,
                              slot=(b * 2 + (0 if forward else 1)) * N_HEADS + h,
                              a=a8[r:r + 1], bcum=cum8[r:r + 1], b_end=tot8[r:r + 1],
                              q=qk_ref[b, u, h * HEAD_DIM:(h + 1) * HEAD_DIM, :],
                              k=qk_ref[b, u, GROUP_W + h * HEAD_DIM:GROUP_W + (h + 1) * HEAD_DIM, :],
                              v=v_ref[b, u, h * HEAD_DIM:(h + 1) * HEAD_DIM, :])
                    ch["kq"] = _dot_tn(ch["k"], ch["q"])
                    chains.append(ch)
        return chains

    def read_state(chains, prev=None):
        for i, ch in enumerate(chains):
            ch["st"] = st_ref[ch["slot"]][:, 0:HEAD_DIM] if prev is None else prev[i]["st_new"]
            ch["m"] = m_ref[ch["slot"]:ch["slot"] + 1, :] if prev is None else prev[i]["m_new"]
            ch["cq"] = _dot(ch["st"].astype(BF16), ch["q"])

    def gate_math(chains):
        for ch in chains:
            a, m, mask = ch["a"], ch["m"], ch["mask"]
            a_col = jnp.broadcast_to(a, (CHUNK, CHUNK)).T
            a_max = jnp.max(a, axis=1, keepdims=True)
            big_m = jnp.maximum(jnp.max(jnp.where(mask, a_col, NEG), axis=0, keepdims=True), m)
            ch["inter"] = jnp.exp2(m - big_m)
            ch["floor"] = jnp.exp2(-(ch["bcum"] + big_m))
            s = ch["kq"] * jnp.exp2(jnp.where(mask, a_col - big_m, NEG))
            ch["rowsum"] = jnp.sum(s, axis=0, keepdims=True)
            ch["s"] = s.astype(BF16)
            m_new = jnp.maximum(ch["b_end"] + m, ch["b_end"] + a_max)
            ch["decay"] = jnp.exp2(ch["b_end"] + m - m_new)
            ch["kw"] = (ch["k"].astype(F32) * jnp.exp2(ch["b_end"] + a - m_new)).astype(BF16)
            ch["m_new"] = m_new

    def finish(chains):
        for ch in chains:
            ch["vs"] = _dot(ch["v"], ch["s"])
            v_ext = jnp.concatenate([ch["v"], ones_row], axis=0)
            ch["upd"] = _dot_nt(v_ext, ch["kw"])
        for ch in chains:
            h, cq, inter = ch["h"], ch["cq"], ch["inter"]
            num = cq[0:HEAD_DIM] * inter + ch["vs"]
            den = inter * cq[HEAD_DIM:HEAD_DIM + 1] + ch["rowsum"]
            ch["o_ref"][ch["b"], ch["u"], h * HEAD_DIM:(h + 1) * HEAD_DIM, :] = (
                num / jnp.maximum(jnp.abs(den), ch["floor"]))
            ch["st_new"] = ch["st"] * ch["decay"][:, 0:HEAD_DIM] + ch["upd"]

    first = start_chains(0)
    read_state(first)
    second = start_chains(1)

    for u in range(CPS):
        for b in range(n_b):
            g = jax.nn.gelu(zs_ref[b, u].astype(F32))
            for h in range(N_HEADS):
                gu = g[h * HEAD_DIM:(h + 1) * HEAD_DIM]
                vh = _head_layer_norm_t(g[GROUP_W + h * HEAD_DIM:GROUP_W + (h + 1) * HEAD_DIM]).astype(BF16)
                mixed = _dot(vh, sw_ref[h]) + sb_ref[h:h + 1, :]
                a_ref[b, u, h * HEAD_DIM:(h + 1) * HEAD_DIM, :] = (gu * mixed).astype(BF16)

    p_sinks = []
    for i, (u, b, kv) in enumerate(pairs):
        sink = sink_ref[kv:kv + 1, :]
        blocks = [sc_ref[i, r:r + CHUNK, :] + mask_refs[u][r:r + CHUNK, :] for r in range(0, sc_ref.shape[1], CHUNK)]
        m = jnp.maximum(jnp.max(functools.reduce(jnp.maximum, blocks), axis=0, keepdims=True), sink)
        for r in range(0, sc_ref.shape[1], CHUNK // 2):
            rows = slice(r, r + CHUNK // 2)
            p_ref[i, rows, :] = jnp.exp2(sc_ref[i, rows, :] + mask_refs[u][rows, :] - m).astype(BF16)
        p_sinks.append(jnp.exp2(sink - m))

    gate_math(first)
    finish(first)

    ones_keys = jnp.where(lax.broadcasted_iota(jnp.int32, (BF16_ROWS, (3 + n_ctx_chunks) * CHUNK), 0) == 0,
                          1.0, 0.0).astype(BF16)
    for i, ((u, b, kv), p_sink) in enumerate(zip(pairs, p_sinks)):
        o_ext = _dot(jnp.concatenate([values(u, b)[kv * HEAD_DIM:(kv + 1) * HEAD_DIM, :], ones_keys], axis=0),
                     p_ref[i])
        o = o_ext[0:HEAD_DIM] / (o_ext[HEAD_DIM:HEAD_DIM + 1] + p_sink)
        for g in range(group):
            h = kv * group + g
            c_ref[b, u, h * HEAD_DIM:(h + 1) * HEAD_DIM, :] = o[:, g * CHUNK:(g + 1) * CHUNK].astype(BF16)

    read_state(second, prev=first)
    gate_math(second)
    finish(second)
    for ch in second:
        st_ref[ch["slot"], :, 0:HEAD_DIM] = ch["st_new"]
        m_ref[ch["slot"]:ch["slot"] + 1, :] = ch["m_new"]


def _attention_masks(n_ctx_chunks):
    group = N_HEADS // KV_HEADS
    n_keys = (3 + n_ctx_chunks) * CHUNK
    j = jnp.arange(n_keys)[:, None]
    i = (jnp.arange(group * CHUNK) & (CHUNK - 1))[None, :]
    masks = []
    for cur_ok, prev_ok, next_ok in ((False, False, False), (True, True, True), (True, False, True),
                                     (True, True, False), (True, False, False)):
        lo = (0 if prev_ok else CHUNK) if cur_ok else 3 * CHUNK
        hi = 3 * CHUNK if next_ok else 2 * CHUNK
        valid = ((j >= i) & (j <= i + 2 * CHUNK) & (j >= lo) & (j < hi)) | (j >= 3 * CHUNK)
        masks.append(jnp.where(valid, 0.0, NEG).astype(F32))
    return jnp.stack(masks)


def _mixers(zs, sgu_wt, sgu_b, aq, ak, av, sink_rows, masks, qk, vo, gt, gate_bias, layer, n_ctx_chunks):
    B, n_chunks = zs.shape[:2]
    assert n_chunks % CPS == 0 and n_ctx_chunks == CPS
    n_pairs = n_chunks // CPS
    ctx_w = n_ctx_chunks * CHUNK
    group = N_HEADS // KV_HEADS
    n_keys = 3 * CHUNK + ctx_w
    kern = functools.partial(_mixers_kernel, n_ctx_chunks=n_ctx_chunks)
    ident = lambda p: p
    prev = lambda p: jnp.maximum(p - 1, 0)
    nxt = lambda p: jnp.minimum(p + 1, n_pairs - 1)
    bwd = lambda p: jnp.where(p == 0, 0, n_pairs - p)
    kspec = lambda f: pl.BlockSpec((B, CPS * CHUNK, KV_W), lambda p: (0, f(p), 0))
    vspec = lambda f: _pair_spec(B, KV_W, f)

    def mask_spec(u):
        def kind(p):
            c = CPS * p + u
            no_prev = (c == n_ctx_chunks).astype(jnp.int32)
            no_next = (c == n_chunks - 1).astype(jnp.int32)
            return jnp.where(c < n_ctx_chunks, 0, 1 + no_prev + 2 * no_next)
        return pl.BlockSpec((None, n_keys, group * CHUNK), lambda p: (kind(p), 0, 0))

    def scan_specs(order):
        return [_pair_spec(B, 2 * GROUP_W, order), _pair_spec(B, GROUP_W, order),
                _pair_spec(B, 4 * N_HEADS, order)]

    n_slots = B * 2 * N_HEADS
    return pl.pallas_call(
        kern,
        grid=(n_pairs,),
        in_specs=[
            _pair_spec(B, 2 * GROUP_W),
            pl.BlockSpec((None, N_HEADS, CHUNK, CHUNK), lambda p: (layer, 0, 0, 0)),
            pl.BlockSpec((None, N_HEADS, CHUNK), lambda p: (layer, 0, 0)),
            _pair_spec(B, GROUP_W),
            kspec(prev), kspec(ident), kspec(nxt), pl.BlockSpec((B, ctx_w, KV_W), lambda p: (0, 0, 0)),
            vspec(prev), vspec(ident), vspec(nxt),
            pl.BlockSpec((B, n_ctx_chunks, KV_W, CHUNK), lambda p: (0, 0, 0, 0)),
            pl.BlockSpec((None, SUBLANES, group * CHUNK), lambda p: (layer, 0, 0)),
            mask_spec(0), mask_spec(1),
        ] + scan_specs(ident) + scan_specs(bwd) + [
            pl.BlockSpec((None, 4 * N_HEADS, CHUNK), lambda p: (layer, 0, 0))],
        out_specs=[_pair_spec(B, GROUP_W), _pair_spec(B, GROUP_W),
                   _pair_spec(B, GROUP_W, ident), _pair_spec(B, GROUP_W, bwd)],
        out_shape=[jax.ShapeDtypeStruct((B, n_chunks, GROUP_W, CHUNK), BF16)] * 2
                  + [jax.ShapeDtypeStruct((B, n_chunks, GROUP_W, CHUNK), F32)] * 2,
        scratch_shapes=[
            pltpu.VMEM((n_slots, ST_ROWS, LANES), F32),
            pltpu.VMEM((n_slots, LANES), F32),
            pltpu.VMEM((CPS * B * KV_HEADS, n_keys, group * CHUNK), F32),
            pltpu.VMEM((CPS * B * KV_HEADS, n_keys, group * CHUNK), BF16),
        ],
        compiler_params=pltpu.CompilerParams(dimension_semantics=("arbitrary",)),
        name="mixers",
    )(zs, sgu_wt, sgu_b, aq, ak, ak, ak, ak, av, av, av, av, sink_rows, masks, masks, qk, vo, gt, qk, vo, gt,
      gate_bias)


N_OUT_SCRATCH = 6


def _hidden_chunks(hidden):
    step = 3 * 2 * LANES
    return [(lo, min(lo + step, hidden)) for lo in range(0, hidden, step)]


def _out_kernel(*refs, hidden, npb, n_tiles, split_ctx):
    _zero_at_first_step(refs[-N_OUT_SCRATCH:-1])
    ctx_ref = refs[0] if split_ctx else None
    (x_ref, a_ref, hf_ref, hb_ref, zo_ref, c_ref, d_ref, moda_ref, modc_ref, g_ref, mg_ref,
     wo_ref, wi_ref, wf_ref, o_ref, x1_ref, y_ref, f_ref, tb_ref, act_ref, bm_ref) = refs[1:] if split_ctx else refs
    s = pl.program_id(0)
    f_ref[s % 2] = _dot(act_ref[...], wf_ref[...])

    x_tile = (_first_tile_from_ctx(ctx_ref, x_ref, jnp.clip(s - 1, 0, n_tiles - 1) % npb)
              if split_ctx else x_ref[...])
    x1 = x_tile + _rms(y_ref[...], moda_ref[2:3, :] * g_ref[1:2, :])
    x1_ref[s % 3] = x1
    tb_ref[...] = (_rms(x1, g_ref[2:3, :] * (1.0 + moda_ref[4:5, :])) + moda_ref[3:4, :]).astype(BF16)

    o_ref[...] = x1_ref[(s + 1) % 3] + _rms(f_ref[(s + 1) % 2], modc_ref[5:6, :] * g_ref[3:4, :])

    for lo, hi in _hidden_chunks(hidden):
        gate = _dot(tb_ref[...], wi_ref[:, lo:hi])
        up = _dot(tb_ref[...], wi_ref[:, hidden + lo:hidden + hi])
        act_ref[:, lo:hi] = (_silu(gate) * up).astype(BF16)

    hs = _get_chunks(hf_ref) + _get_chunks(hb_ref)
    hn = jnp.concatenate([_head_layer_norm_t(hs[h * HEAD_DIM:(h + 1) * HEAD_DIM]) for h in range(N_HEADS)], axis=0)
    bm_ref[...] = (hn * mg_ref[...] * jax.nn.sigmoid(_get_chunks(zo_ref).astype(F32))).astype(BF16)
    y_ref[...] = (_dot_tn(_get_chunks(a_ref), wo_ref[0:GROUP_W, :])
                  + _dot_tn(bm_ref[...], wo_ref[GROUP_W:2 * GROUP_W, :])
                  + _dot_tn(_get_chunks(c_ref), wo_ref[2 * GROUP_W:3 * GROUP_W, :])
                  + _dot(d_ref[...], wo_ref[3 * GROUP_W:4 * GROUP_W, :]))


def _out_ffn(xs, a, hf, hb, vo, c, d, mods, norm_g, mnorm_g, w_out, w_ffn_in, w_ffn_out, layer, n_batch_rows,
             drop_ctx):
    B, S, D = _stream_shape(xs)
    tm = ROW_TILE
    npb = S // tm
    n_tiles = B * npb
    hidden = w_ffn_out.shape[1]
    cur = lambda s: jnp.minimum(s, n_tiles - 1)
    mid = lambda s: jnp.clip(s - 1, 0, n_tiles - 1)
    prv = lambda s: jnp.clip(s - 3, 0, n_tiles - 1)
    row = lambda s: (cur(s) // npb, cur(s) % npb, 0)
    mod_row = lambda t: (layer, jnp.where(t % npb == 0, n_batch_rows, t // npb), 0, 0)
    c2 = lambda s: (0, 0)
    this_layer = lambda s: (layer, 0, 0)
    cpt = tm // CHUNK
    fm = pl.BlockSpec((None, cpt, GROUP_W, CHUNK), lambda s: (cur(s) // npb, cur(s) % npb, 0, 0))
    split_ctx = isinstance(xs, tuple)
    x_specs, x_args = _stream_specs(xs, tm, mid, npb)
    in_specs = x_specs + [
        fm, fm, fm,
        pl.BlockSpec((None, cpt, GROUP_W, CHUNK), lambda s: (cur(s) // npb, cur(s) % npb, 1, 0)),
        fm,
        pl.BlockSpec((None, tm, GROUP_W), row),
        pl.BlockSpec((None, None, 6, D), lambda s: mod_row(mid(s))),
        pl.BlockSpec((None, None, 6, D), lambda s: mod_row(prv(s))),
        pl.BlockSpec((None, 4, D), this_layer),
        pl.BlockSpec((None, GROUP_W, tm), this_layer),
        pl.BlockSpec((None, D, D), this_layer, pipeline_mode=pl.Buffered(1)),
        pl.BlockSpec((None, D, 2 * hidden), this_layer, pipeline_mode=pl.Buffered(1)),
        pl.BlockSpec((None, hidden, D), this_layer, pipeline_mode=pl.Buffered(1)),
    ]
    args = x_args + [a, hf, hb, vo, c, d, mods, mods, norm_g, mnorm_g, w_out, w_ffn_in, w_ffn_out]
    scratch = [pltpu.VMEM((3, tm, D), F32), pltpu.VMEM((tm, D), F32), pltpu.VMEM((2, tm, D), F32),
               pltpu.VMEM((tm, D), BF16), pltpu.VMEM((tm, hidden), BF16), pltpu.VMEM((GROUP_W, tm), BF16)]
    params = pltpu.CompilerParams(dimension_semantics=("arbitrary",), vmem_limit_bytes=VMEM_LIMIT)
    static = dict(hidden=hidden, npb=npb, n_tiles=n_tiles, split_ctx=split_ctx)
    if drop_ctx:
        out_spec = pl.BlockSpec((None, tm, D), lambda s: (prv(s) // npb, jnp.maximum(prv(s) % npb - 1, 0), 0))
        out_shape = jax.ShapeDtypeStruct((B, S - tm, D), F32)
    else:
        out_spec = pl.BlockSpec((None, tm, D), lambda s: (prv(s) // npb, prv(s) % npb, 0))
        out_shape = jax.ShapeDtypeStruct((B, S, D), F32)
    return pl.pallas_call(
        functools.partial(_out_kernel, **static),
        grid=(n_tiles + 3,),
        in_specs=in_specs,
        out_specs=out_spec,
        out_shape=out_shape,
        scratch_shapes=scratch, compiler_params=params, name="out_ffn",
    )(*args)


def _rope_tables(n_tokens, ctx_len, tm):
    rows = n_tokens // GRID_W
    axis_freq = HEAD_DIM // 4
    inv = jnp.power(ROPE_BASE, -jnp.arange(axis_freq, dtype=F32) * 2.0 / (2 * axis_freq))
    ar = jnp.arange(rows, dtype=F32)[:, None] * inv
    ac = jnp.arange(GRID_W, dtype=F32)[:, None] * inv

    def per_token(fn):
        r = jnp.broadcast_to(fn(ar)[:, None, :], (rows, GRID_W, axis_freq))
        c = jnp.broadcast_to(fn(ac)[None, :, :], (rows, GRID_W, axis_freq))
        return jnp.concatenate([r, r, c, c] * KV_HEADS, axis=-1).reshape(n_tokens, KV_HEADS * HEAD_DIM)

    cos, sin = per_token(jnp.cos), per_token(jnp.sin)
    sign = jnp.where((jnp.arange(KV_HEADS * HEAD_DIM) % 32) < 16, -1.0, 1.0).astype(F32)
    cos = jnp.concatenate([jnp.ones((ctx_len, KV_HEADS * HEAD_DIM), F32), cos], axis=0)
    sin = jnp.concatenate([jnp.zeros((ctx_len, KV_HEADS * HEAD_DIM), F32), sin * sign], axis=0)
    t = jnp.arange(tm)[:, None]
    half = jnp.asarray([w // 2 for w in POOL_WINDOWS])[None, :]
    inv_cnt = []
    for last in (False, True):
        for first in (False, True):
            lo = jnp.maximum(t - half, 0) if first else t - half
            hi = jnp.minimum(t + half, tm) if last else t + half
            inv_cnt.append(jnp.repeat(1.0 / (hi - lo).astype(F32), HEAD_DIM, axis=1))
    q_scale = HEAD_DIM ** -0.5 * LOG2E
    return cos, sin, cos[:, 0:HEAD_DIM].T * q_scale, sin[:, 0:HEAD_DIM].T * q_scale, jnp.stack(inv_cnt)


def _arrange_w_in(w_in):
    depth = w_in.shape[0]
    ml0 = 2 * GROUP_W
    gate0 = ml0 + 4 * GROUP_W
    att0 = gate0 + 4 * N_HEADS
    pool0 = att0 + GROUP_W + 2 * KV_W
    w_t = jnp.swapaxes(w_in, 1, 2)
    gates = w_t[:, gate0:att0].reshape(depth, 2, 2, N_HEADS, -1)
    gates = gates.transpose(0, 2, 1, 3, 4).reshape(depth, 4 * N_HEADS, -1)
    feat = jnp.concatenate([
        w_t[:, 0:gate0],
        w_t[:, att0:att0 + GROUP_W],
        w_t[:, att0 + GROUP_W + KV_W:pool0],
        gates,
    ], axis=1)
    tok = jnp.concatenate([w_t[:, att0 + GROUP_W:att0 + GROUP_W + KV_W], w_t[:, pool0:pool0 + GROUP_W]], axis=1)
    return feat.astype(BF16), jnp.swapaxes(tok.astype(BF16), 1, 2)


def _block_diag(w):
    g = w.shape[0]
    eye = jnp.eye(g, dtype=w.dtype)
    return (eye[:, None, :, None] * w[:, :, None, :]).reshape(g * w.shape[1], g * w.shape[2])


def kernel(x, c, ctx, c_ctx, w_mod, b_mod, norm_g, w_in, w_out, sgu_w, sgu_b, mlstm_conv_w, mlstm_gate_b,
           mlstm_norm_g, attn_sink, pool_w, pool_scale, w_ffn_in, w_ffn_out):
    B, N, D = x.shape
    ctx_len = ctx.shape[1]
    depth = w_mod.shape[0]
    assert D == 4 * GROUP_W and N % ROW_TILE == 0 and ctx_len == ROW_TILE
    assert w_in.shape[2] == 2 * GROUP_W + 4 * GROUP_W + 4 * N_HEADS + GROUP_W + 2 * KV_W + GROUP_W
    n_ctx_chunks = ctx_len // CHUNK

    xs = (ctx, x)
    rows = -(-(B + 1) // SUBLANES) * SUBLANES
    cc = jnp.zeros((rows, D), F32).at[0:B].set(c).at[B].set(c_ctx)
    mods = _modulation(cc, w_mod, b_mod).reshape(depth, rows, 6, D)
    tables = _rope_tables(N, ctx_len, ROW_TILE)
    masks = _attention_masks(n_ctx_chunks)

    wt, wk = _arrange_w_in(w_in)
    w_out_b, w_ffn_in_b, w_ffn_out_b = w_out.astype(BF16), w_ffn_in.astype(BF16), w_ffn_out.astype(BF16)

    group = N_HEADS // KV_HEADS
    conv_b = jnp.broadcast_to(mlstm_conv_w[:, :, :, None], (depth, 3, 2 * GROUP_W, ROW_TILE))
    proj = (wt, wk, tables, conv_b, jax.vmap(_block_diag)(pool_w).astype(BF16), pool_scale.reshape(depth, 1, GROUP_W))
    gate_bias = jnp.broadcast_to(
        mlstm_gate_b.reshape(depth, 2, 2, N_HEADS).transpose(0, 2, 1, 3).reshape(depth, 4 * N_HEADS, 1),
        (depth, 4 * N_HEADS, CHUNK))
    sink_rows = jnp.zeros((depth, SUBLANES, group * CHUNK), F32).at[:, 0:KV_HEADS].set(
        jnp.repeat(attn_sink.reshape(depth, KV_HEADS, group) * LOG2E, CHUNK, axis=2))
    mnorm_b = jnp.broadcast_to(mlstm_norm_g[:, :, None], (depth, GROUP_W, ROW_TILE))
    sgu_wt = sgu_w.transpose(0, 1, 3, 2).astype(BF16)

    for l in range(depth):
        zs, qk, vo, gt, aq, av, ak, dm = _in_proj(xs, mods, norm_g[:, 0:1], proj, l, B)
        a, cm, hf, hb = _mixers(zs, sgu_wt, sgu_b, aq, ak, av, sink_rows, masks, qk, vo, gt, gate_bias, l,
                                n_ctx_chunks)
        xs = _out_ffn(xs, a, hf, hb, vo, cm, dm, mods, norm_g, mnorm_b, w_out_b, w_ffn_in_b, w_ffn_out_b, l, B,
                      drop_ctx=(l == depth - 1))
    return xs
```

```python
import functools

import jax
import jax.numpy as jnp
from jax import lax
from jax.experimental import pallas as pl
from jax.experimental.pallas import tpu as pltpu

F32 = jnp.float32
BF16 = jnp.bfloat16

GRID_W = 64
ROPE_BASE = 10000.0
EPS = 1e-6
LANES = 128
SUBLANES = 8
BF16_ROWS = 16
CHUNK = 128
HEAD_DIM = 64
N_HEADS = 4
KV_HEADS = 2
GROUP_W = N_HEADS * HEAD_DIM
KV_W = KV_HEADS * HEAD_DIM
POOL_WINDOWS = (2, 4, 8, 16)
HALO = 8
ROW_TILE = 256
NEG = -1e30
VMEM_LIMIT = 56 * 1024 * 1024

R_SGU = 0
R_QK = 512
R_VO = 1024
R_AQ = 1536
R_AV = 1792
R_GATE = 1920
R_END = 1936
T_AK = 0
T_POOL = KV_W
T_END = KV_W + GROUP_W

ST_ROWS = 80
LOG2E = 1.4426950408889634

HI = lax.Precision.HIGHEST


def _dot(a, b):
    return jnp.dot(a, b, preferred_element_type=F32)


def _dot_hi(a, b):
    return jnp.dot(a, b, preferred_element_type=F32, precision=HI)


def _dot_nt(a, b):
    return lax.dot_general(a, b, (((1,), (1,)), ((), ())), preferred_element_type=F32)


def _dot_tn(a, b):
    return lax.dot_general(a, b, (((0,), (0,)), ((), ())), preferred_element_type=F32)


def _rms(x, g):
    return x * lax.rsqrt(jnp.mean(x * x, axis=-1, keepdims=True) + EPS) * g


def _silu(x):
    return x * jax.nn.sigmoid(x)


def _put_chunks(ref, x):
    for i in range(ref.shape[0]):
        ref[i] = x[:, i * CHUNK:(i + 1) * CHUNK].astype(ref.dtype)


def _get_chunks(ref):
    return jnp.concatenate([ref[i] for i in range(ref.shape[0])], axis=1)


def _head_layer_norm_t(v):
    mu = jnp.mean(v, axis=0, keepdims=True)
    vc = v - mu
    return vc * lax.rsqrt(jnp.mean(vc * vc, axis=0, keepdims=True) + EPS)


def _zero_at_first_step(refs):
    @pl.when(pl.program_id(0) == 0)
    def _():
        for ref in refs:
            ref[...] = jnp.zeros_like(ref)


def _mod_kernel(c_ref, w_ref, b_ref, o_ref):
    o_ref[...] = _dot(_silu(c_ref[...]).astype(BF16), w_ref[...].astype(BF16)) + b_ref[...]


def _modulation(cc, w_mod, b_mod):
    depth, d, n6 = w_mod.shape
    rows = cc.shape[0]
    tn = 1536
    return pl.pallas_call(
        _mod_kernel,
        grid=(depth, n6 // tn),
        in_specs=[
            pl.BlockSpec((rows, d), lambda l, j: (0, 0)),
            pl.BlockSpec((None, d, tn), lambda l, j: (l, 0, j)),
            pl.BlockSpec((None, 1, tn), lambda l, j: (l, 0, j)),
        ],
        out_specs=pl.BlockSpec((None, rows, tn), lambda l, j: (l, 0, j)),
        out_shape=jax.ShapeDtypeStruct((depth, rows, n6), F32),
        compiler_params=pltpu.CompilerParams(
            dimension_semantics=("arbitrary", "arbitrary"), vmem_limit_bytes=VMEM_LIMIT),
        name="modulation",
    )(cc, w_mod, b_mod.reshape(depth, 1, n6))


N_IN_SCRATCH = 4


def _first_tile_from_ctx(ctx_ref, x_ref, tile):
    rows = lax.broadcasted_iota(jnp.int32, x_ref.shape, 0)
    return jnp.where(rows < jnp.where(tile == 0, x_ref.shape[0], 0), ctx_ref[...], x_ref[...])


def _project_tile(hb, has_prev, has_next, wt_ref, wk_ref, cos_ref, sin_ref, cost_ref, sint_ref,
                  conv_ref, pw_ref, ps_ref, invc_ref,
                  zs_ref, qk_ref, vo_ref, gt_ref, aq_ref, av_ref, ak_ref, d_ref, z_ref, pcol_ref, zp_ref, prow_ref):
    tm = hb.shape[0]
    zc = _dot_nt(wt_ref[R_QK:R_VO, :], hb)
    _put_chunks(zs_ref, _dot_nt(wt_ref[R_SGU:R_QK, :], hb))

    z = z_ref[...]
    lane = lax.broadcasted_iota(jnp.int32, z.shape, 1)
    before = jnp.where(lane == 0, pcol_ref[:, 0:1] * has_prev, pltpu.roll(z, 1, axis=1))
    after = jnp.where(lane == tm - 1, zc[:, 0:1] * has_next, pltpu.roll(z, tm - 1, axis=1))
    qk = _silu(before * conv_ref[0] + z * conv_ref[1] + after * conv_ref[2])
    _put_chunks(qk_ref, jnp.concatenate([qk[0:GROUP_W], qk[GROUP_W:2 * GROUP_W] * (HEAD_DIM ** -0.5)], axis=0))
    pcol_ref[...] = jnp.broadcast_to(z[:, tm - 1:tm], pcol_ref.shape)
    z_ref[...] = zc

    zp_new = _dot(hb, wk_ref[:, T_POOL:T_END])
    _put_chunks(vo_ref, _dot_nt(wt_ref[R_VO:R_AQ, :], hb))
    zp = zp_ref[...]
    rows = tm + 2 * HALO
    ext = jnp.concatenate([prow_ref[...] * has_prev, zp, zp_new[0:HALO] * has_next], axis=0)
    s2 = ext + pltpu.roll(ext, 1, axis=0)
    s4 = pltpu.roll(s2, 1, axis=0) + pltpu.roll(s2, rows - 1, axis=0)
    s8 = pltpu.roll(s4, 2, axis=0) + pltpu.roll(s4, rows - 2, axis=0)
    s16 = pltpu.roll(s8, 4, axis=0) + pltpu.roll(s8, rows - 4, axis=0)
    sums = [s[HALO:HALO + tm] for s in (s2, s4, s8, s16)]
    plane = lax.broadcasted_iota(jnp.int32, zp.shape, 1)
    pooled = sums[-1]
    for gi in range(len(POOL_WINDOWS) - 2, -1, -1):
        pooled = jnp.where(plane < (gi + 1) * HEAD_DIM, sums[gi], pooled)
    pooled = pooled * invc_ref[...]
    d_ref[...] = (_dot((pooled - zp).astype(BF16), pw_ref[...]) * ps_ref[...]).astype(BF16)
    prow_ref[...] = zp[tm - HALO:tm]
    zp_ref[...] = zp_new

    q = _dot_nt(wt_ref[R_AQ:R_AV, :], hb)
    k = _dot(hb, wk_ref[:, T_AK:T_POOL])
    _put_chunks(av_ref, _dot_nt(wt_ref[R_AV:R_GATE, :], hb))
    _put_chunks(gt_ref, _dot_nt(wt_ref[R_GATE:R_END, :], hb))
    cos_t = jnp.concatenate([cost_ref[...]] * N_HEADS, axis=0)
    sin_t = jnp.concatenate([sint_ref[...]] * N_HEADS, axis=0)
    row = lax.broadcasted_iota(jnp.int32, q.shape, 0)
    rot_q = jnp.where((row & 31) < 16, pltpu.roll(q, GROUP_W - 16, axis=0), pltpu.roll(q, 16, axis=0))
    _put_chunks(aq_ref, q * cos_t + rot_q * sin_t)
    klane = lax.broadcasted_iota(jnp.int32, k.shape, 1)
    rot_k = jnp.where((klane & 31) < 16, pltpu.roll(k, KV_W - 16, axis=1), pltpu.roll(k, 16, axis=1))
    ak_ref[...] = (k * cos_ref[...] + rot_k * sin_ref[...]).astype(BF16)


def _segment_flags(tile, npb):
    j = tile % npb
    return (j >= 2).astype(F32), jnp.logical_and(j >= 1, j < npb - 1).astype(F32)


def _in_kernel(*refs, npb, n_tiles, split_ctx):
    ctx_ref = refs[0] if split_ctx else None
    x_ref, mod_ref, g_ref = (refs[1:] if split_ctx else refs)[0:3]
    project_refs = (refs[1:] if split_ctx else refs)[3:-1]
    hb_ref = refs[-1]
    s = pl.program_id(0)

    _zero_at_first_step((hb_ref,) + tuple(project_refs[-N_IN_SCRATCH:]))
    has_prev, has_next = _segment_flags(jnp.maximum(s - 2, 0), npb)
    gain = g_ref[...] * (1.0 + mod_ref[1:2, :])
    x_tile = _first_tile_from_ctx(ctx_ref, x_ref, jnp.minimum(s, n_tiles - 1) % npb) if split_ctx else x_ref[...]
    hb_next = (_rms(x_tile, gain) + mod_ref[0:1, :]).astype(BF16)
    _project_tile(hb_ref[...], has_prev, has_next, *project_refs)
    hb_ref[...] = hb_next


def _stream_specs(xs, tm, tile_of_step, npb):
    bj = lambda s: (tile_of_step(s) // npb, tile_of_step(s) % npb)
    if isinstance(xs, tuple):
        ctx, x = xs
        D = x.shape[2]
        return [pl.BlockSpec((None, tm, D), lambda s: (bj(s)[0], 0, 0)),
                pl.BlockSpec((None, tm, D), lambda s: (bj(s)[0], jnp.maximum(bj(s)[1] - 1, 0), 0))], [ctx, x]
    return [pl.BlockSpec((None, tm, xs.shape[2]), lambda s: (bj(s)[0], bj(s)[1], 0))], [xs]


def _stream_shape(xs):
    if isinstance(xs, tuple):
        return xs[1].shape[0], xs[0].shape[1] + xs[1].shape[1], xs[1].shape[2]
    return xs.shape


def _project_call_parts(B, S, D, tm, npb, mid, old, proj, layer):
    wt, wk, tables, conv_b, pool_w_bd, pool_scale = proj
    cos_k, sin_k, cos_t, sin_t, inv_cnt = tables
    c2 = lambda s: (0, 0)
    cpt = tm // CHUNK
    pool_kind = lambda j: (j <= 1).astype(jnp.int32) + 2 * ((j == 0) | (j == npb - 1)).astype(jnp.int32)
    in_specs = [
        pl.BlockSpec((None, R_END, D), lambda s: (layer, 0, 0), pipeline_mode=pl.Buffered(1)),
        pl.BlockSpec((None, D, T_END), lambda s: (layer, 0, 0), pipeline_mode=pl.Buffered(1)),
        pl.BlockSpec((tm, KV_W), lambda s: (mid(s) % npb, 0)),
        pl.BlockSpec((tm, KV_W), lambda s: (mid(s) % npb, 0)),
        pl.BlockSpec((HEAD_DIM, tm), lambda s: (0, mid(s) % npb)),
        pl.BlockSpec((HEAD_DIM, tm), lambda s: (0, mid(s) % npb)),
        pl.BlockSpec((None, 3, 2 * GROUP_W, tm), lambda s: (layer, 0, 0, 0)),
        pl.BlockSpec((None, GROUP_W, GROUP_W), lambda s: (layer, 0, 0)),
        pl.BlockSpec((None, 1, GROUP_W), lambda s: (layer, 0, 0)),
        pl.BlockSpec((None, tm, GROUP_W), lambda s: (pool_kind(old(s) % npb), 0, 0)),
    ]
    args = [wt, wk, cos_k, sin_k, cos_t, sin_t, conv_b, pool_w_bd, pool_scale, inv_cnt]

    def fm(rows, dt, tile):
        return (pl.BlockSpec((None, cpt, rows, CHUNK), lambda s: (tile(s) // npb, tile(s) % npb, 0, 0)),
                jax.ShapeDtypeStruct((B, S // CHUNK, rows, CHUNK), dt))

    def tk(cols, dt, tile):
        return (pl.BlockSpec((None, tm, cols), lambda s: (tile(s) // npb, tile(s) % npb, 0)),
                jax.ShapeDtypeStruct((B, S, cols), dt))

    outs = [fm(2 * GROUP_W, BF16, mid), fm(2 * GROUP_W, BF16, old), fm(2 * GROUP_W, BF16, mid),
            fm(4 * N_HEADS, F32, mid), fm(GROUP_W, BF16, mid), fm(KV_W, BF16, mid),
            tk(KV_W, BF16, mid), tk(GROUP_W, BF16, old)]
    scratch = [pltpu.VMEM((2 * GROUP_W, tm), F32), pltpu.VMEM((2 * GROUP_W, LANES), F32),
               pltpu.VMEM((tm, GROUP_W), F32), pltpu.VMEM((HALO, GROUP_W), F32)]
    return in_specs, args, [o[0] for o in outs], [o[1] for o in outs], scratch


def _in_proj(xs, mods, g0, proj, layer, n_batch_rows):
    B, S, D = _stream_shape(xs)
    tm = ROW_TILE
    npb = S // tm
    n_tiles = B * npb
    cur = lambda s: jnp.minimum(s, n_tiles - 1)
    mid = lambda s: jnp.clip(s - 1, 0, n_tiles - 1)
    old = lambda s: jnp.maximum(s - 2, 0)
    kern = functools.partial(_in_kernel, npb=npb, n_tiles=n_tiles, split_ctx=isinstance(xs, tuple))
    x_specs, x_args = _stream_specs(xs, tm, cur, npb)
    p_specs, p_args, out_specs, out_shapes, p_scratch = _project_call_parts(B, S, D, tm, npb, mid, old, proj, layer)
    return pl.pallas_call(
        kern,
        grid=(n_tiles + 2,),
        in_specs=x_specs + [
            pl.BlockSpec((None, None, 6, D),
                         lambda s: (layer, jnp.where(cur(s) % npb == 0, n_batch_rows, cur(s) // npb), 0, 0)),
            pl.BlockSpec((None, 1, D), lambda s: (layer, 0, 0)),
        ] + p_specs,
        out_specs=out_specs,
        out_shape=out_shapes,
        scratch_shapes=p_scratch + [pltpu.VMEM((tm, D), BF16)],
        compiler_params=pltpu.CompilerParams(dimension_semantics=("arbitrary",), vmem_limit_bytes=VMEM_LIMIT),
        name="in_proj",
    )(*x_args, mods, g0, *p_args)


def _log_sigmoid(x):
    return jnp.minimum(x, 0.0) - jnp.log1p(jnp.exp(-jnp.abs(x)))


CPS = 2


def _pair_spec(n_b, rows, pair=lambda p: p, row_block=0):
    return pl.BlockSpec((n_b, CPS, rows, CHUNK), lambda p: (0, pair(p), row_block, 0))


def _mixers_kernel(zs_ref, sw_ref, sb_ref,
                   q_ref, kp_ref, kc_ref, kn_ref, kx_ref, vp_ref, vc_ref, vn_ref, vx_ref, sink_ref, mask0_ref, mask1_ref,
                   qkf_ref, vf_ref, gf_ref, qkb_ref, vb_ref, gb_ref, gbias_ref,
                   a_ref, c_ref, hf_ref, hb_ref, st_ref, m_ref, sc_ref, p_ref, *, n_ctx_chunks):
    n_b = zs_ref.shape[0]
    _zero_at_first_step((st_ref, m_ref))

    group = N_HEADS // KV_HEADS
    zeros = jnp.zeros((HEAD_DIM, CHUNK), BF16)
    pairs = [(u, b, h) for u in range(CPS) for b in range(n_b) for h in range(N_HEADS)]
    mask_refs = (mask0_ref, mask1_ref)

    def keys(u, b):
        local = ([kp_ref[b, CHUNK:2 * CHUNK, :], kc_ref[b]] if u == 0 else [kc_ref[b], kn_ref[b, 0:CHUNK, :]])
        return jnp.concatenate(local + [kx_ref[b]], axis=0)

    def values(u, b):
        local = ([vp_ref[b, 1], vc_ref[b, 0], vc_ref[b, 1]] if u == 0 else [vc_ref[b, 0], vc_ref[b, 1], vn_ref[b, 0]])
        return jnp.concatenate(local + [vx_ref[b, i] for i in range(n_ctx_chunks)], axis=1)

    for i, (u, b, h) in enumerate(pairs):
        q1 = q_ref[b, u, h * HEAD_DIM:(h + 1) * HEAD_DIM, :]
        qm = jnp.concatenate([q1, zeros] if h // group == 0 else [zeros, q1], axis=0)
        sc_ref[i] = _dot(keys(u, b), qm)

    s_i = lax.broadcasted_iota(jnp.int32, (CHUNK, CHUNK), 0)
    t_i = lax.broadcasted_iota(jnp.int32, (CHUNK, CHUNK), 1)
    causal = {True: s_i <= t_i, False: s_i >= t_i}
    ones_row = jnp.where(lax.broadcasted_iota(jnp.int32, (ST_ROWS - HEAD_DIM, CHUNK), 0) == 0, 1.0, 0.0).astype(BF16)
    all_ones = jnp.ones((CHUNK, CHUNK), F32)

    def start_chains(t):
        chains = []
        for b in range(n_b):
            for forward, qk_ref, v_ref, g_ref, o_ref in ((True, qkf_ref, vf_ref, gf_ref, hf_ref),
                                                          (False, qkb_ref, vb_ref, gb_ref, hb_ref)):
                u = t if forward else CPS - 1 - t
                gates = g_ref[b, u] + gbias_ref[...]
                li8 = gates[0:2 * N_HEADS] * LOG2E
                lf8 = _log_sigmoid(gates[2 * N_HEADS:4 * N_HEADS]) * LOG2E
                cum8 = _dot_hi(lf8, jnp.where(causal[forward], 1.0, 0.0).astype(F32))
                tot8 = _dot_hi(lf8, all_ones)
                a8 = li8 - cum8
                for h in range(N_HEADS):
                    r = (0 if forward else N_HEADS) + h
                    ch = dict(b=b, u=u, h=h, mask=causal[forward], o_ref=o_ref,
                              slot=(b * 2 + (0 if forward else 1)) * N_HEADS + h,
                              a=a8[r:r + 1], bcum=cum8[r:r + 1], b_end=tot8[r:r + 1],
                              q=qk_ref[b, u, h * HEAD_DIM:(h + 1) * HEAD_DIM, :],
                              k=qk_ref[b, u, GROUP_W + h * HEAD_DIM:GROUP_W + (h + 1) * HEAD_DIM, :],
                              v=v_ref[b, u, h * HEAD_DIM:(h + 1) * HEAD_DIM, :])
                    ch["kq"] = _dot_tn(ch["k"], ch["q"])
                    chains.append(ch)
        return chains

    def read_state(chains, prev=None):
        for i, ch in enumerate(chains):
            ch["st"] = st_ref[ch["slot"]][:, 0:HEAD_DIM] if prev is None else prev[i]["st_new"]
            ch["m"] = m_ref[ch["slot"]:ch["slot"] + 1, :] if prev is None else prev[i]["m_new"]
            ch["cq"] = _dot(ch["st"].astype(BF16), ch["q"])

    def gate_math(chains):
        for ch in chains:
            a, m, mask = ch["a"], ch["m"], ch["mask"]
            a_col = jnp.broadcast_to(a, (CHUNK, CHUNK)).T
            a_max = jnp.max(a, axis=1, keepdims=True)
            big_m = jnp.maximum(jnp.max(jnp.where(mask, a_col, NEG), axis=0, keepdims=True), m)
            ch["inter"] = jnp.exp2(m - big_m)
            ch["floor"] = jnp.exp2(-(ch["bcum"] + big_m))
            s = ch["kq"] * jnp.exp2(jnp.where(mask, a_col - big_m, NEG))
            ch["rowsum"] = jnp.sum(s, axis=0, keepdims=True)
            ch["s"] = s.astype(BF16)
            m_new = jnp.maximum(ch["b_end"] + m, ch["b_end"] + a_max)
            ch["decay"] = jnp.exp2(ch["b_end"] + m - m_new)
            ch["kw"] = (ch["k"].astype(F32) * jnp.exp2(ch["b_end"] + a - m_new)).astype(BF16)
            ch["m_new"] = m_new

    def finish(chains):
        for ch in chains:
            ch["vs"] = _dot(ch["v"], ch["s"])
            v_ext = jnp.concatenate([ch["v"], ones_row], axis=0)
            ch["upd"] = _dot_nt(v_ext, ch["kw"])
        for ch in chains:
            h, cq, inter = ch["h"], ch["cq"], ch["inter"]
            num = cq[0:HEAD_DIM] * inter + ch["vs"]
            den = inter * cq[HEAD_DIM:HEAD_DIM + 1] + ch["rowsum"]
            ch["o_ref"][ch["b"], ch["u"], h * HEAD_DIM:(h + 1) * HEAD_DIM, :] = (
                num / jnp.maximum(jnp.abs(den), ch["floor"]))
            ch["st_new"] = ch["st"] * ch["decay"][:, 0:HEAD_DIM] + ch["upd"]

    first = start_chains(0)
    read_state(first)
    second = start_chains(1)

    for u in range(CPS):
        for b in range(n_b):
            g = jax.nn.gelu(zs_ref[b, u].astype(F32))
            for h in range(N_HEADS):
                gu = g[h * HEAD_DIM:(h + 1) * HEAD_DIM]
                vh = _head_layer_norm_t(g[GROUP_W + h * HEAD_DIM:GROUP_W + (h + 1) * HEAD_DIM]).astype(BF16)
                mixed = _dot(vh, sw_ref[h]) + sb_ref[h:h + 1, :]
                a_ref[b, u, h * HEAD_DIM:(h + 1) * HEAD_DIM, :] = (gu * mixed).astype(BF16)

    p_sinks = []
    for i, (u, b, h) in enumerate(pairs):
        kv, g = divmod(h, group)
        sink = sink_ref[kv:kv + 1, g * CHUNK:(g + 1) * CHUNK]
        blocks = [sc_ref[i, r:r + CHUNK, :] + mask_refs[u][r:r + CHUNK, 0:CHUNK] for r in range(0, sc_ref.shape[1], CHUNK)]
        m = jnp.maximum(jnp.max(functools.reduce(jnp.maximum, blocks), axis=0, keepdims=True), sink)
        for r in range(0, sc_ref.shape[1], CHUNK // 2):
            rows = slice(r, r + CHUNK // 2)
            p_ref[i, rows, :] = jnp.exp2(sc_ref[i, rows, :] + mask_refs[u][rows, 0:CHUNK] - m).astype(BF16)
        p_sinks.append(jnp.exp2(sink - m))

    gate_math(first)
    finish(first)

    ones_keys = jnp.where(lax.broadcasted_iota(jnp.int32, (BF16_ROWS, (3 + n_ctx_chunks) * CHUNK), 0) == 0,
                          1.0, 0.0).astype(BF16)
    for i, ((u, b, h), p_sink) in enumerate(zip(pairs, p_sinks)):
        kv = h // group
        o_ext = _dot(jnp.concatenate([values(u, b)[kv * HEAD_DIM:(kv + 1) * HEAD_DIM, :], ones_keys], axis=0),
                     p_ref[i])
        o = o_ext[0:HEAD_DIM] / (o_ext[HEAD_DIM:HEAD_DIM + 1] + p_sink)
        c_ref[b, u, h * HEAD_DIM:(h + 1) * HEAD_DIM, :] = o.astype(BF16)

    read_state(second, prev=first)
    gate_math(second)
    finish(second)
    for ch in second:
        st_ref[ch["slot"], :, 0:HEAD_DIM] = ch["st_new"]
        m_ref[ch["slot"]:ch["slot"] + 1, :] = ch["m_new"]


def _attention_masks(n_ctx_chunks):
    group = N_HEADS // KV_HEADS
    n_keys = (3 + n_ctx_chunks) * CHUNK
    j = jnp.arange(n_keys)[:, None]
    i = (jnp.arange(group * CHUNK) & (CHUNK - 1))[None, :]
    masks = []
    for cur_ok, prev_ok, next_ok in ((False, False, False), (True, True, True), (True, False, True),
                                     (True, True, False), (True, False, False)):
        lo = (0 if prev_ok else CHUNK) if cur_ok else 3 * CHUNK
        hi = 3 * CHUNK if next_ok else 2 * CHUNK
        valid = ((j >= i) & (j <= i + 2 * CHUNK) & (j >= lo) & (j < hi)) | (j >= 3 * CHUNK)
        masks.append(jnp.where(valid, 0.0, NEG).astype(F32))
    return jnp.stack(masks)


def _mixers(zs, sgu_wt, sgu_b, aq, ak, av, sink_rows, masks, qk, vo, gt, gate_bias, layer, n_ctx_chunks):
    B, n_chunks = zs.shape[:2]
    assert n_chunks % CPS == 0 and n_ctx_chunks == CPS
    n_pairs = n_chunks // CPS
    ctx_w = n_ctx_chunks * CHUNK
    group = N_HEADS // KV_HEADS
    n_keys = 3 * CHUNK + ctx_w
    kern = functools.partial(_mixers_kernel, n_ctx_chunks=n_ctx_chunks)
    ident = lambda p: p
    prev = lambda p: jnp.maximum(p - 1, 0)
    nxt = lambda p: jnp.minimum(p + 1, n_pairs - 1)
    bwd = lambda p: jnp.where(p == 0, 0, n_pairs - p)
    kspec = lambda f: pl.BlockSpec((B, CPS * CHUNK, KV_W), lambda p: (0, f(p), 0))
    vspec = lambda f: _pair_spec(B, KV_W, f)

    def mask_spec(u):
        def kind(p):
            c = CPS * p + u
            no_prev = (c == n_ctx_chunks).astype(jnp.int32)
            no_next = (c == n_chunks - 1).astype(jnp.int32)
            return jnp.where(c < n_ctx_chunks, 0, 1 + no_prev + 2 * no_next)
        return pl.BlockSpec((None, n_keys, group * CHUNK), lambda p: (kind(p), 0, 0))

    def scan_specs(order):
        return [_pair_spec(B, 2 * GROUP_W, order), _pair_spec(B, GROUP_W, order),
                _pair_spec(B, 4 * N_HEADS, order)]

    n_slots = B * 2 * N_HEADS
    return pl.pallas_call(
        kern,
        grid=(n_pairs,),
        in_specs=[
            _pair_spec(B, 2 * GROUP_W),
            pl.BlockSpec((None, N_HEADS, CHUNK, CHUNK), lambda p: (layer, 0, 0, 0)),
            pl.BlockSpec((None, N_HEADS, CHUNK), lambda p: (layer, 0, 0)),
            _pair_spec(B, GROUP_W),
            kspec(prev), kspec(ident), kspec(nxt), pl.BlockSpec((B, ctx_w, KV_W), lambda p: (0, 0, 0)),
            vspec(prev), vspec(ident), vspec(nxt),
            pl.BlockSpec((B, n_ctx_chunks, KV_W, CHUNK), lambda p: (0, 0, 0, 0)),
            pl.BlockSpec((None, SUBLANES, group * CHUNK), lambda p: (layer, 0, 0)),
            mask_spec(0), mask_spec(1),
        ] + scan_specs(ident) + scan_specs(bwd) + [
            pl.BlockSpec((None, 4 * N_HEADS, CHUNK), lambda p: (layer, 0, 0))],
        out_specs=[_pair_spec(B, GROUP_W), _pair_spec(B, GROUP_W),
                   _pair_spec(B, GROUP_W, ident), _pair_spec(B, GROUP_W, bwd)],
        out_shape=[jax.ShapeDtypeStruct((B, n_chunks, GROUP_W, CHUNK), BF16)] * 2
                  + [jax.ShapeDtypeStruct((B, n_chunks, GROUP_W, CHUNK), F32)] * 2,
        scratch_shapes=[
            pltpu.VMEM((n_slots, ST_ROWS, LANES), F32),
            pltpu.VMEM((n_slots, LANES), F32),
            pltpu.VMEM((CPS * B * N_HEADS, n_keys, CHUNK), F32),
            pltpu.VMEM((CPS * B * N_HEADS, n_keys, CHUNK), BF16),
        ],
        compiler_params=pltpu.CompilerParams(dimension_semantics=("arbitrary",)),
        name="mixers",
    )(zs, sgu_wt, sgu_b, aq, ak, ak, ak, ak, av, av, av, av, sink_rows, masks, masks, qk, vo, gt, qk, vo, gt,
      gate_bias)


N_OUT_SCRATCH = 6


def _hidden_chunks(hidden):
    step = 3 * 2 * LANES
    return [(lo, min(lo + step, hidden)) for lo in range(0, hidden, step)]


def _out_kernel(*refs, hidden, npb, n_tiles, split_ctx):
    _zero_at_first_step(refs[-N_OUT_SCRATCH:-1])
    ctx_ref = refs[0] if split_ctx else None
    (x_ref, a_ref, hf_ref, hb_ref, zo_ref, c_ref, d_ref, moda_ref, modc_ref, g_ref, mg_ref,
     wo_ref, wi_ref, wf_ref, o_ref, x1_ref, y_ref, f_ref, tb_ref, act_ref, bm_ref) = refs[1:] if split_ctx else refs
    s = pl.program_id(0)
    f_ref[s % 2] = _dot(act_ref[...], wf_ref[...])

    x_tile = (_first_tile_from_ctx(ctx_ref, x_ref, jnp.clip(s - 1, 0, n_tiles - 1) % npb)
              if split_ctx else x_ref[...])
    x1 = x_tile + _rms(y_ref[...], moda_ref[2:3, :] * g_ref[1:2, :])
    x1_ref[s % 3] = x1
    tb_ref[...] = (_rms(x1, g_ref[2:3, :] * (1.0 + moda_ref[4:5, :])) + moda_ref[3:4, :]).astype(BF16)

    o_ref[...] = x1_ref[(s + 1) % 3] + _rms(f_ref[(s + 1) % 2], modc_ref[5:6, :] * g_ref[3:4, :])

    for lo, hi in _hidden_chunks(hidden):
        gate = _dot(tb_ref[...], wi_ref[:, lo:hi])
        up = _dot(tb_ref[...], wi_ref[:, hidden + lo:hidden + hi])
        act_ref[:, lo:hi] = (_silu(gate) * up).astype(BF16)

    hs = _get_chunks(hf_ref) + _get_chunks(hb_ref)
    hn = jnp.concatenate([_head_layer_norm_t(hs[h * HEAD_DIM:(h + 1) * HEAD_DIM]) for h in range(N_HEADS)], axis=0)
    bm_ref[...] = (hn * mg_ref[...] * jax.nn.sigmoid(_get_chunks(zo_ref).astype(F32))).astype(BF16)
    y_ref[...] = (_dot_tn(_get_chunks(a_ref), wo_ref[0:GROUP_W, :])
                  + _dot_tn(bm_ref[...], wo_ref[GROUP_W:2 * GROUP_W, :])
                  + _dot_tn(_get_chunks(c_ref), wo_ref[2 * GROUP_W:3 * GROUP_W, :])
                  + _dot(d_ref[...], wo_ref[3 * GROUP_W:4 * GROUP_W, :]))


def _out_ffn(xs, a, hf, hb, vo, c, d, mods, norm_g, mnorm_g, w_out, w_ffn_in, w_ffn_out, layer, n_batch_rows,
             drop_ctx):
    B, S, D = _stream_shape(xs)
    tm = ROW_TILE
    npb = S // tm
    n_tiles = B * npb
    hidden = w_ffn_out.shape[1]
    cur = lambda s: jnp.minimum(s, n_tiles - 1)
    mid = lambda s: jnp.clip(s - 1, 0, n_tiles - 1)
    prv = lambda s: jnp.clip(s - 3, 0, n_tiles - 1)
    row = lambda s: (cur(s) // npb, cur(s) % npb, 0)
    mod_row = lambda t: (layer, jnp.where(t % npb == 0, n_batch_rows, t // npb), 0, 0)
    c2 = lambda s: (0, 0)
    this_layer = lambda s: (layer, 0, 0)
    cpt = tm // CHUNK
    fm = pl.BlockSpec((None, cpt, GROUP_W, CHUNK), lambda s: (cur(s) // npb, cur(s) % npb, 0, 0))
    split_ctx = isinstance(xs, tuple)
    x_specs, x_args = _stream_specs(xs, tm, mid, npb)
    in_specs = x_specs + [
        fm, fm, fm,
        pl.BlockSpec((None, cpt, GROUP_W, CHUNK), lambda s: (cur(s) // npb, cur(s) % npb, 1, 0)),
        fm,
        pl.BlockSpec((None, tm, GROUP_W), row),
        pl.BlockSpec((None, None, 6, D), lambda s: mod_row(mid(s))),
        pl.BlockSpec((None, None, 6, D), lambda s: mod_row(prv(s))),
        pl.BlockSpec((None, 4, D), this_layer),
        pl.BlockSpec((None, GROUP_W, tm), this_layer),
        pl.BlockSpec((None, D, D), this_layer, pipeline_mode=pl.Buffered(1)),
        pl.BlockSpec((None, D, 2 * hidden), this_layer, pipeline_mode=pl.Buffered(1)),
        pl.BlockSpec((None, hidden, D), this_layer, pipeline_mode=pl.Buffered(1)),
    ]
    args = x_args + [a, hf, hb, vo, c, d, mods, mods, norm_g, mnorm_g, w_out, w_ffn_in, w_ffn_out]
    scratch = [pltpu.VMEM((3, tm, D), F32), pltpu.VMEM((tm, D), F32), pltpu.VMEM((2, tm, D), F32),
               pltpu.VMEM((tm, D), BF16), pltpu.VMEM((tm, hidden), BF16), pltpu.VMEM((GROUP_W, tm), BF16)]
    params = pltpu.CompilerParams(dimension_semantics=("arbitrary",), vmem_limit_bytes=VMEM_LIMIT)
    static = dict(hidden=hidden, npb=npb, n_tiles=n_tiles, split_ctx=split_ctx)
    if drop_ctx:
        out_spec = pl.BlockSpec((None, tm, D), lambda s: (prv(s) // npb, jnp.maximum(prv(s) % npb - 1, 0), 0))
        out_shape = jax.ShapeDtypeStruct((B, S - tm, D), F32)
    else:
        out_spec = pl.BlockSpec((None, tm, D), lambda s: (prv(s) // npb, prv(s) % npb, 0))
        out_shape = jax.ShapeDtypeStruct((B, S, D), F32)
    return pl.pallas_call(
        functools.partial(_out_kernel, **static),
        grid=(n_tiles + 3,),
        in_specs=in_specs,
        out_specs=out_spec,
        out_shape=out_shape,
        scratch_shapes=scratch, compiler_params=params, name="out_ffn",
    )(*args)


def _rope_tables(n_tokens, ctx_len, tm):
    rows = n_tokens // GRID_W
    axis_freq = HEAD_DIM // 4
    inv = jnp.power(ROPE_BASE, -jnp.arange(axis_freq, dtype=F32) * 2.0 / (2 * axis_freq))
    ar = jnp.arange(rows, dtype=F32)[:, None] * inv
    ac = jnp.arange(GRID_W, dtype=F32)[:, None] * inv

    def per_token(fn):
        r = jnp.broadcast_to(fn(ar)[:, None, :], (rows, GRID_W, axis_freq))
        c = jnp.broadcast_to(fn(ac)[None, :, :], (rows, GRID_W, axis_freq))
        return jnp.concatenate([r, r, c, c], axis=-1).reshape(n_tokens, HEAD_DIM)

    cos, sin = per_token(jnp.cos), per_token(jnp.sin)
    sign = jnp.where((jnp.arange(HEAD_DIM) % 32) < 16, -1.0, 1.0).astype(F32)
    cos = jnp.concatenate([jnp.ones((ctx_len, HEAD_DIM), F32), cos], axis=0)
    sin = jnp.concatenate([jnp.zeros((ctx_len, HEAD_DIM), F32), sin * sign], axis=0)
    t = jnp.arange(tm)[:, None]
    half = jnp.asarray([w // 2 for w in POOL_WINDOWS])[None, :]
    inv_cnt = []
    for last in (False, True):
        for first in (False, True):
            lo = jnp.maximum(t - half, 0) if first else t - half
            hi = jnp.minimum(t + half, tm) if last else t + half
            inv_cnt.append(jnp.repeat(1.0 / (hi - lo).astype(F32), HEAD_DIM, axis=1))
    q_scale = HEAD_DIM ** -0.5 * LOG2E
    return (jnp.tile(cos, (1, KV_HEADS)), jnp.tile(sin, (1, KV_HEADS)), cos.T * q_scale, sin.T * q_scale,
            jnp.stack(inv_cnt))


def _arrange_w_in(w_in):
    depth = w_in.shape[0]
    ml0 = 2 * GROUP_W
    gate0 = ml0 + 4 * GROUP_W
    att0 = gate0 + 4 * N_HEADS
    pool0 = att0 + GROUP_W + 2 * KV_W
    w_t = jnp.swapaxes(w_in, 1, 2)
    gates = w_t[:, gate0:att0].reshape(depth, 2, 2, N_HEADS, -1)
    gates = gates.transpose(0, 2, 1, 3, 4).reshape(depth, 4 * N_HEADS, -1)
    feat = jnp.concatenate([
        w_t[:, 0:gate0],
        w_t[:, att0:att0 + GROUP_W],
        w_t[:, att0 + GROUP_W + KV_W:pool0],
        gates,
    ], axis=1)
    tok = jnp.concatenate([w_t[:, att0 + GROUP_W:att0 + GROUP_W + KV_W], w_t[:, pool0:pool0 + GROUP_W]], axis=1)
    return feat.astype(BF16), jnp.swapaxes(tok.astype(BF16), 1, 2)


def _block_diag(w):
    g = w.shape[0]
    eye = jnp.eye(g, dtype=w.dtype)
    return (eye[:, None, :, None] * w[:, :, None, :]).reshape(g * w.shape[1], g * w.shape[2])


def kernel(x, c, ctx, c_ctx, w_mod, b_mod, norm_g, w_in, w_out, sgu_w, sgu_b, mlstm_conv_w, mlstm_gate_b,
           mlstm_norm_g, attn_sink, pool_w, pool_scale, w_ffn_in, w_ffn_out):
    B, N, D = x.shape
    ctx_len = ctx.shape[1]
    depth = w_mod.shape[0]
    assert D == 4 * GROUP_W and N % ROW_TILE == 0 and ctx_len == ROW_TILE
    assert w_in.shape[2] == 2 * GROUP_W + 4 * GROUP_W + 4 * N_HEADS + GROUP_W + 2 * KV_W + GROUP_W
    n_ctx_chunks = ctx_len // CHUNK

    xs = (ctx, x)
    rows = -(-(B + 1) // SUBLANES) * SUBLANES
    cc = jnp.zeros((rows, D), F32).at[0:B].set(c).at[B].set(c_ctx)
    mods = _modulation(cc, w_mod, b_mod).reshape(depth, rows, 6, D)
    tables = _rope_tables(N, ctx_len, ROW_TILE)
    masks = _attention_masks(n_ctx_chunks)

    wt, wk = _arrange_w_in(w_in)
    w_out_b, w_ffn_in_b, w_ffn_out_b = w_out.astype(BF16), w_ffn_in.astype(BF16), w_ffn_out.astype(BF16)

    group = N_HEADS // KV_HEADS
    conv_b = jnp.broadcast_to(mlstm_conv_w[:, :, :, None], (depth, 3, 2 * GROUP_W, ROW_TILE))
    proj = (wt, wk, tables, conv_b, jax.vmap(_block_diag)(pool_w).astype(BF16), pool_scale.reshape(depth, 1, GROUP_W))
    gate_bias = jnp.broadcast_to(
        mlstm_gate_b.reshape(depth, 2, 2, N_HEADS).transpose(0, 2, 1, 3).reshape(depth, 4 * N_HEADS, 1),
        (depth, 4 * N_HEADS, CHUNK))
    sink_rows = jnp.zeros((depth, SUBLANES, group * CHUNK), F32).at[:, 0:KV_HEADS].set(
        jnp.repeat(attn_sink.reshape(depth, KV_HEADS, group) * LOG2E, CHUNK, axis=2))
    mnorm_b = jnp.broadcast_to(mlstm_norm_g[:, :, None], (depth, GROUP_W, ROW_TILE))
    sgu_wt = sgu_w.transpose(0, 1, 3, 2).astype(BF16)

    for l in range(depth):
        zs, qk, vo, gt, aq, av, ak, dm = _in_proj(xs, mods, norm_g[:, 0:1], proj, l, B)
        a, cm, hf, hb = _mixers(zs, sgu_wt, sgu_b, aq, ak, av, sink_rows, masks, qk, vo, gt, gate_bias, l,
                                n_ctx_chunks)
        xs = _out_ffn(xs, a, hf, hb, vo, cm, dm, mods, norm_g, mnorm_b, w_out_b, w_ffn_in_b, w_ffn_out_b, l, B,
                      drop_ctx=(l == depth - 1))
    return xs
```

```python
import functools

import jax
import jax.numpy as jnp
from jax import lax
from jax.experimental import pallas as pl
from jax.experimental.pallas import tpu as pltpu

F32 = jnp.float32
BF16 = jnp.bfloat16

GRID_W = 64
ROPE_BASE = 10000.0
EPS = 1e-6
LANES = 128
SUBLANES = 8
BF16_ROWS = 16
CHUNK = 128
HEAD_DIM = 64
N_HEADS = 4
KV_HEADS = 2
GROUP_W = N_HEADS * HEAD_DIM
KV_W = KV_HEADS * HEAD_DIM
POOL_WINDOWS = (2, 4, 8, 16)
HALO = 8
ROW_TILE = 256
NEG = -1e30
VMEM_LIMIT = 56 * 1024 * 1024

R_SGU = 0
R_QK = 512
R_VO = 1024
R_AQ = 1536
R_AV = 1792
R_GATE = 1920
R_END = 1936
T_AK = 0
T_POOL = KV_W
T_END = KV_W + GROUP_W

ST_ROWS = 80
LOG2E = 1.4426950408889634

HI = lax.Precision.HIGHEST


def _dot(a, b):
    return jnp.dot(a, b, preferred_element_type=F32)


def _dot_hi(a, b):
    return jnp.dot(a, b, preferred_element_type=F32, precision=HI)


def _dot_nt(a, b):
    return lax.dot_general(a, b, (((1,), (1,)), ((), ())), preferred_element_type=F32)


def _dot_tn(a, b):
    return lax.dot_general(a, b, (((0,), (0,)), ((), ())), preferred_element_type=F32)


def _rms(x, g):
    return x * lax.rsqrt(jnp.mean(x * x, axis=-1, keepdims=True) + EPS) * g


def _silu(x):
    return x * jax.nn.sigmoid(x)


def _put_chunks(ref, x):
    for i in range(ref.shape[0]):
        ref[i] = x[:, i * CHUNK:(i + 1) * CHUNK].astype(ref.dtype)


def _get_chunks(ref):
    return jnp.concatenate([ref[i] for i in range(ref.shape[0])], axis=1)


def _head_layer_norm_t(v):
    mu = jnp.mean(v, axis=0, keepdims=True)
    vc = v - mu
    return vc * lax.rsqrt(jnp.mean(vc * vc, axis=0, keepdims=True) + EPS)


def _zero_at_first_step(refs):
    @pl.when(pl.program_id(0) == 0)
    def _():
        for ref in refs:
            ref[...] = jnp.zeros_like(ref)


def _mod_kernel(c_ref, w_ref, b_ref, o_ref):
    o_ref[...] = _dot(_silu(c_ref[...]).astype(BF16), w_ref[...].astype(BF16)) + b_ref[...]


def _modulation(cc, w_mod, b_mod):
    depth, d, n6 = w_mod.shape
    rows = cc.shape[0]
    tn = 1536
    return pl.pallas_call(
        _mod_kernel,
        grid=(depth, n6 // tn),
        in_specs=[
            pl.BlockSpec((rows, d), lambda l, j: (0, 0)),
            pl.BlockSpec((None, d, tn), lambda l, j: (l, 0, j)),
            pl.BlockSpec((None, 1, tn), lambda l, j: (l, 0, j)),
        ],
        out_specs=pl.BlockSpec((None, rows, tn), lambda l, j: (l, 0, j)),
        out_shape=jax.ShapeDtypeStruct((depth, rows, n6), F32),
        compiler_params=pltpu.CompilerParams(
            dimension_semantics=("arbitrary", "arbitrary"), vmem_limit_bytes=VMEM_LIMIT),
        name="modulation",
    )(cc, w_mod, b_mod.reshape(depth, 1, n6))


N_IN_SCRATCH = 4


def _first_tile_from_ctx(ctx_ref, x_ref, tile):
    rows = lax.broadcasted_iota(jnp.int32, x_ref.shape, 0)
    return jnp.where(rows < jnp.where(tile == 0, x_ref.shape[0], 0), ctx_ref[...], x_ref[...])


def _project_tile(hb, has_prev, has_next, wt_ref, wk_ref, cos_ref, sin_ref, cost_ref, sint_ref,
                  conv_ref, pw_ref, ps_ref, invc_ref,
                  zs_ref, qk_ref, vo_ref, gt_ref, aq_ref, av_ref, ak_ref, d_ref, z_ref, pcol_ref, zp_ref, prow_ref):
    tm = hb.shape[0]
    zc = _dot_nt(wt_ref[R_QK:R_VO, :], hb)
    _put_chunks(zs_ref, _dot_nt(wt_ref[R_SGU:R_QK, :], hb))

    z = z_ref[...]
    lane = lax.broadcasted_iota(jnp.int32, z.shape, 1)
    before = jnp.where(lane == 0, pcol_ref[:, 0:1] * has_prev, pltpu.roll(z, 1, axis=1))
    after = jnp.where(lane == tm - 1, zc[:, 0:1] * has_next, pltpu.roll(z, tm - 1, axis=1))
    qk = _silu(before * conv_ref[0] + z * conv_ref[1] + after * conv_ref[2])
    _put_chunks(qk_ref, jnp.concatenate([qk[0:GROUP_W], qk[GROUP_W:2 * GROUP_W] * (HEAD_DIM ** -0.5)], axis=0))
    pcol_ref[...] = jnp.broadcast_to(z[:, tm - 1:tm], pcol_ref.shape)
    z_ref[...] = zc

    zp_new = _dot(hb, wk_ref[:, T_POOL:T_END])
    _put_chunks(vo_ref, _dot_nt(wt_ref[R_VO:R_AQ, :], hb))
    zp = zp_ref[...]
    rows = tm + 2 * HALO
    ext = jnp.concatenate([prow_ref[...] * has_prev, zp, zp_new[0:HALO] * has_next], axis=0)
    s2 = ext + pltpu.roll(ext, 1, axis=0)
    s4 = pltpu.roll(s2, 1, axis=0) + pltpu.roll(s2, rows - 1, axis=0)
    s8 = pltpu.roll(s4, 2, axis=0) + pltpu.roll(s4, rows - 2, axis=0)
    s16 = pltpu.roll(s8, 4, axis=0) + pltpu.roll(s8, rows - 4, axis=0)
    sums = [s[HALO:HALO + tm] for s in (s2, s4, s8, s16)]
    plane = lax.broadcasted_iota(jnp.int32, zp.shape, 1)
    pooled = sums[-1]
    for gi in range(len(POOL_WINDOWS) - 2, -1, -1):
        pooled = jnp.where(plane < (gi + 1) * HEAD_DIM, sums[gi], pooled)
    pooled = pooled * invc_ref[...]
    d_ref[...] = (_dot((pooled - zp).astype(BF16), pw_ref[...]) * ps_ref[...]).astype(BF16)
    prow_ref[...] = zp[tm - HALO:tm]
    zp_ref[...] = zp_new

    q = _dot_nt(wt_ref[R_AQ:R_AV, :], hb)
    k = _dot(hb, wk_ref[:, T_AK:T_POOL])
    _put_chunks(av_ref, _dot_nt(wt_ref[R_AV:R_GATE, :], hb))
    _put_chunks(gt_ref, _dot_nt(wt_ref[R_GATE:R_END, :], hb))
    cos_t = jnp.concatenate([cost_ref[...]] * N_HEADS, axis=0)
    sin_t = jnp.concatenate([sint_ref[...]] * N_HEADS, axis=0)
    row = lax.broadcasted_iota(jnp.int32, q.shape, 0)
    rot_q = jnp.where((row & 31) < 16, pltpu.roll(q, GROUP_W - 16, axis=0), pltpu.roll(q, 16, axis=0))
    _put_chunks(aq_ref, q * cos_t + rot_q * sin_t)
    klane = lax.broadcasted_iota(jnp.int32, k.shape, 1)
    rot_k = jnp.where((klane & 31) < 16, pltpu.roll(k, KV_W - 16, axis=1), pltpu.roll(k, 16, axis=1))
    ak_ref[...] = (k * cos_ref[...] + rot_k * sin_ref[...]).astype(BF16)


def _segment_flags(tile, npb):
    j = tile % npb
    return (j >= 2).astype(F32), jnp.logical_and(j >= 1, j < npb - 1).astype(F32)


def _in_kernel(*refs, npb, n_tiles, split_ctx):
    ctx_ref = refs[0] if split_ctx else None
    x_ref, mod_ref, g_ref = (refs[1:] if split_ctx else refs)[0:3]
    project_refs = (refs[1:] if split_ctx else refs)[3:-1]
    hb_ref = refs[-1]
    s = pl.program_id(0)

    _zero_at_first_step((hb_ref,) + tuple(project_refs[-N_IN_SCRATCH:]))
    has_prev, has_next = _segment_flags(jnp.maximum(s - 2, 0), npb)
    gain = g_ref[...] * (1.0 + mod_ref[1:2, :])
    x_tile = _first_tile_from_ctx(ctx_ref, x_ref, jnp.minimum(s, n_tiles - 1) % npb) if split_ctx else x_ref[...]
    hb_next = (_rms(x_tile, gain) + mod_ref[0:1, :]).astype(BF16)
    _project_tile(hb_ref[...], has_prev, has_next, *project_refs)
    hb_ref[...] = hb_next


def _stream_specs(xs, tm, tile_of_step, npb):
    bj = lambda s: (tile_of_step(s) // npb, tile_of_step(s) % npb)
    if isinstance(xs, tuple):
        ctx, x = xs
        D = x.shape[2]
        return [pl.BlockSpec((None, tm, D), lambda s: (bj(s)[0], 0, 0)),
                pl.BlockSpec((None, tm, D), lambda s: (bj(s)[0], jnp.maximum(bj(s)[1] - 1, 0), 0))], [ctx, x]
    return [pl.BlockSpec((None, tm, xs.shape[2]), lambda s: (bj(s)[0], bj(s)[1], 0))], [xs]


def _stream_shape(xs):
    if isinstance(xs, tuple):
        return xs[1].shape[0], xs[0].shape[1] + xs[1].shape[1], xs[1].shape[2]
    return xs.shape


def _project_call_parts(B, S, D, tm, npb, mid, old, proj, layer):
    wt, wk, tables, conv_b, pool_w_bd, pool_scale = proj
    cos_k, sin_k, cos_t, sin_t, inv_cnt = tables
    c2 = lambda s: (0, 0)
    cpt = tm // CHUNK
    pool_kind = lambda j: (j <= 1).astype(jnp.int32) + 2 * ((j == 0) | (j == npb - 1)).astype(jnp.int32)
    in_specs = [
        pl.BlockSpec((None, R_END, D), lambda s: (layer, 0, 0), pipeline_mode=pl.Buffered(1)),
        pl.BlockSpec((None, D, T_END), lambda s: (layer, 0, 0), pipeline_mode=pl.Buffered(1)),
        pl.BlockSpec((tm, KV_W), lambda s: (mid(s) % npb, 0)),
        pl.BlockSpec((tm, KV_W), lambda s: (mid(s) % npb, 0)),
        pl.BlockSpec((HEAD_DIM, tm), lambda s: (0, mid(s) % npb)),
        pl.BlockSpec((HEAD_DIM, tm), lambda s: (0, mid(s) % npb)),
        pl.BlockSpec((None, 3, 2 * GROUP_W, tm), lambda s: (layer, 0, 0, 0)),
        pl.BlockSpec((None, GROUP_W, GROUP_W), lambda s: (layer, 0, 0)),
        pl.BlockSpec((None, 1, GROUP_W), lambda s: (layer, 0, 0)),
        pl.BlockSpec((None, tm, GROUP_W), lambda s: (pool_kind(old(s) % npb), 0, 0)),
    ]
    args = [wt, wk, cos_k, sin_k, cos_t, sin_t, conv_b, pool_w_bd, pool_scale, inv_cnt]

    def fm(rows, dt, tile):
        return (pl.BlockSpec((None, cpt, rows, CHUNK), lambda s: (tile(s) // npb, tile(s) % npb, 0, 0)),
                jax.ShapeDtypeStruct((B, S // CHUNK, rows, CHUNK), dt))

    def tk(cols, dt, tile):
        return (pl.BlockSpec((None, tm, cols), lambda s: (tile(s) // npb, tile(s) % npb, 0)),
                jax.ShapeDtypeStruct((B, S, cols), dt))

    outs = [fm(2 * GROUP_W, BF16, mid), fm(2 * GROUP_W, BF16, old), fm(2 * GROUP_W, BF16, mid),
            fm(4 * N_HEADS, F32, mid), fm(GROUP_W, BF16, mid), fm(KV_W, BF16, mid),
            tk(KV_W, BF16, mid), tk(GROUP_W, BF16, old)]
    scratch = [pltpu.VMEM((2 * GROUP_W, tm), F32), pltpu.VMEM((2 * GROUP_W, LANES), F32),
               pltpu.VMEM((tm, GROUP_W), F32), pltpu.VMEM((HALO, GROUP_W), F32)]
    return in_specs, args, [o[0] for o in outs], [o[1] for o in outs], scratch


def _in_proj(xs, mods, g0, proj, layer, n_batch_rows):
    B, S, D = _stream_shape(xs)
    tm = ROW_TILE
    npb = S // tm
    n_tiles = B * npb
    cur = lambda s: jnp.minimum(s, n_tiles - 1)
    mid = lambda s: jnp.clip(s - 1, 0, n_tiles - 1)
    old = lambda s: jnp.maximum(s - 2, 0)
    kern = functools.partial(_in_kernel, npb=npb, n_tiles=n_tiles, split_ctx=isinstance(xs, tuple))
    x_specs, x_args = _stream_specs(xs, tm, cur, npb)
    p_specs, p_args, out_specs, out_shapes, p_scratch = _project_call_parts(B, S, D, tm, npb, mid, old, proj, layer)
    return pl.pallas_call(
        kern,
        grid=(n_tiles + 2,),
        in_specs=x_specs + [
            pl.BlockSpec((None, None, 6, D),
                         lambda s: (layer, jnp.where(cur(s) % npb == 0, n_batch_rows, cur(s) // npb), 0, 0)),
            pl.BlockSpec((None, 1, D), lambda s: (layer, 0, 0)),
        ] + p_specs,
        out_specs=out_specs,
        out_shape=out_shapes,
        scratch_shapes=p_scratch + [pltpu.VMEM((tm, D), BF16)],
        compiler_params=pltpu.CompilerParams(dimension_semantics=("arbitrary",), vmem_limit_bytes=VMEM_LIMIT),
        name="in_proj",
    )(*x_args, mods, g0, *p_args)


def _log_sigmoid(x):
    return jnp.minimum(x, 0.0) - jnp.log1p(jnp.exp(-jnp.abs(x)))


CPS = 2


def _pair_spec(n_b, rows, pair=lambda p: p, row_block=0):
    return pl.BlockSpec((n_b, CPS, rows, CHUNK), lambda p: (0, pair(p), row_block, 0))


def _mixers_kernel(zs_ref, sw_ref, sb_ref,
                   q_ref, kp_ref, kc_ref, kn_ref, kx_ref, vp_ref, vc_ref, vn_ref, vx_ref, sink_ref, mask0_ref, mask1_ref,
                   qkf_ref, vf_ref, gf_ref, qkb_ref, vb_ref, gb_ref, gbias_ref,
                   a_ref, c_ref, hf_ref, hb_ref, st_ref, m_ref, sc_ref, p_ref, *, n_ctx_chunks):
    n_b = zs_ref.shape[0]
    _zero_at_first_step((st_ref, m_ref))

    group = N_HEADS // KV_HEADS
    zeros = jnp.zeros((HEAD_DIM, group * CHUNK), BF16)
    pairs = [(u, b, kv) for u in range(CPS) for b in range(n_b) for kv in range(KV_HEADS)]
    mask_refs = (mask0_ref, mask1_ref)

    def key_blocks(u, b):
        local = ([kp_ref[b, CHUNK:2 * CHUNK, :], kc_ref[b]] if u == 0 else [kc_ref[b], kn_ref[b, 0:CHUNK, :]])
        return local + [kx_ref[b]]

    def values(u, b):
        local = ([vp_ref[b, 1], vc_ref[b, 0], vc_ref[b, 1]] if u == 0 else [vc_ref[b, 0], vc_ref[b, 1], vn_ref[b, 0]])
        return jnp.concatenate(local + [vx_ref[b, i] for i in range(n_ctx_chunks)], axis=1)

    for i, (u, b, kv) in enumerate(pairs):
        q2 = jnp.concatenate([q_ref[b, u, (kv * group + g) * HEAD_DIM:(kv * group + g + 1) * HEAD_DIM, :]
                              for g in range(group)], axis=1)
        qm = jnp.concatenate([q2, zeros] if kv == 0 else [zeros, q2], axis=0)
        r = 0
        for kb in key_blocks(u, b):
            sc_ref[i, r:r + kb.shape[0], :] = _dot(kb, qm)
            r += kb.shape[0]

    s_i = lax.broadcasted_iota(jnp.int32, (CHUNK, CHUNK), 0)
    t_i = lax.broadcasted_iota(jnp.int32, (CHUNK, CHUNK), 1)
    causal = {True: s_i <= t_i, False: s_i >= t_i}
    ones_row = jnp.where(lax.broadcasted_iota(jnp.int32, (ST_ROWS - HEAD_DIM, CHUNK), 0) == 0, 1.0, 0.0).astype(BF16)
    all_ones = jnp.ones((CHUNK, CHUNK), F32)

    def start_chains(t):
        chains = []
        for b in range(n_b):
            for forward, qk_ref, v_ref, g_ref, o_ref in ((True, qkf_ref, vf_ref, gf_ref, hf_ref),
                                                          (False, qkb_ref, vb_ref, gb_ref, hb_ref)):
                u = t if forward else CPS - 1 - t
                gates = g_ref[b, u] + gbias_ref[...]
                li8 = gates[0:2 * N_HEADS] * LOG2E
                lf8 = _log_sigmoid(gates[2 * N_HEADS:4 * N_HEADS]) * LOG2E
                cum8 = _dot_hi(lf8, jnp.where(causal[forward], 1.0, 0.0).astype(F32))
                tot8 = _dot_hi(lf8, all_ones)
                a8 = li8 - cum8
                for h in range(N_HEADS):
                    r = (0 if forward else N_HEADS) + h
                    ch = dict(b=b, u=u, h=h, mask=causal[forward], o_ref=o_ref,
                              slot=(b * 2 + (0 if forward else 1)) * N_HEADS + h,
                              a=a8[r:r + 1], bcum=cum8[r:r + 1], b_end=tot8[r:r + 1],
                              q=qk_ref[b, u, h * HEAD_DIM:(h + 1) * HEAD_DIM, :],
                              k=qk_ref[b, u, GROUP_W + h * HEAD_DIM:GROUP_W + (h + 1) * HEAD_DIM, :],
                              v=v_ref[b, u, h * HEAD_DIM:(h + 1) * HEAD_DIM, :])
                    ch["kq"] = _dot_tn(ch["k"], ch["q"])
                    chains.append(ch)
        return chains

    def read_state(chains, prev=None):
        for i, ch in enumerate(chains):
            ch["st"] = st_ref[ch["slot"]][:, 0:HEAD_DIM] if prev is None else prev[i]["st_new"]
            ch["m"] = m_ref[ch["slot"]:ch["slot"] + 1, :] if prev is None else prev[i]["m_new"]
            ch["cq"] = _dot(ch["st"].astype(BF16), ch["q"])

    def gate_math(chains):
        for ch in chains:
            a, m, mask = ch["a"], ch["m"], ch["mask"]
            a_col = jnp.broadcast_to(a, (CHUNK, CHUNK)).T
            a_max = jnp.max(a, axis=1, keepdims=True)
            big_m = jnp.maximum(jnp.max(jnp.where(mask, a_col, NEG), axis=0, keepdims=True), m)
            ch["inter"] = jnp.exp2(m - big_m)
            ch["floor"] = jnp.exp2(-(ch["bcum"] + big_m))
            s = ch["kq"] * jnp.exp2(jnp.where(mask, a_col - big_m, NEG))
            ch["rowsum"] = jnp.sum(s, axis=0, keepdims=True)
            ch["s"] = s.astype(BF16)
            m_new = jnp.maximum(ch["b_end"] + m, ch["b_end"] + a_max)
            ch["decay"] = jnp.exp2(ch["b_end"] + m - m_new)
            ch["kw"] = (ch["k"].astype(F32) * jnp.exp2(ch["b_end"] + a - m_new)).astype(BF16)
            ch["m_new"] = m_new

    def finish(chains):
        for ch in chains:
            ch["vs"] = _dot(ch["v"], ch["s"])
            v_ext = jnp.concatenate([ch["v"], ones_row], axis=0)
            ch["upd"] = _dot_nt(v_ext, ch["kw"])
        for ch in chains:
            h, cq, inter = ch["h"], ch["cq"], ch["inter"]
            num = cq[0:HEAD_DIM] * inter + ch["vs"]
            den = inter * cq[HEAD_DIM:HEAD_DIM + 1] + ch["rowsum"]
            ch["o_ref"][ch["b"], ch["u"], h * HEAD_DIM:(h + 1) * HEAD_DIM, :] = (
                num / jnp.maximum(jnp.abs(den), ch["floor"]))
            ch["st_new"] = ch["st"] * ch["decay"][:, 0:HEAD_DIM] + ch["upd"]

    first = start_chains(0)
    read_state(first)
    second = start_chains(1)

    for u in range(CPS):
        for b in range(n_b):
            g = jax.nn.gelu(zs_ref[b, u].astype(F32))
            for h in range(N_HEADS):
                gu = g[h * HEAD_DIM:(h + 1) * HEAD_DIM]
                vh = _head_layer_norm_t(g[GROUP_W + h * HEAD_DIM:GROUP_W + (h + 1) * HEAD_DIM]).astype(BF16)
                mixed = _dot(vh, sw_ref[h]) + sb_ref[h:h + 1, :]
                a_ref[b, u, h * HEAD_DIM:(h + 1) * HEAD_DIM, :] = (gu * mixed).astype(BF16)

    p_sinks = []
    for i, (u, b, kv) in enumerate(pairs):
        sink = sink_ref[kv:kv + 1, :]
        blocks = [sc_ref[i, r:r + CHUNK, :] + mask_refs[u][r:r + CHUNK, :] for r in range(0, sc_ref.shape[1], CHUNK)]
        m = jnp.maximum(jnp.max(functools.reduce(jnp.maximum, blocks), axis=0, keepdims=True), sink)
        for r in range(0, sc_ref.shape[1], CHUNK // 2):
            rows = slice(r, r + CHUNK // 2)
            p_ref[i, rows, :] = jnp.exp2(sc_ref[i, rows, :] + mask_refs[u][rows, :] - m).astype(BF16)
        p_sinks.append(jnp.exp2(sink - m))

    gate_math(first)
    finish(first)

    ones_keys = jnp.where(lax.broadcasted_iota(jnp.int32, (BF16_ROWS, (3 + n_ctx_chunks) * CHUNK), 0) == 0,
                          1.0, 0.0).astype(BF16)
    for i, ((u, b, kv), p_sink) in enumerate(zip(pairs, p_sinks)):
        o_ext = _dot(jnp.concatenate([values(u, b)[kv * HEAD_DIM:(kv + 1) * HEAD_DIM, :], ones_keys], axis=0),
                     p_ref[i])
        o = o_ext[0:HEAD_DIM] / (o_ext[HEAD_DIM:HEAD_DIM + 1] + p_sink)
        for g in range(group):
            h = kv * group + g
            c_ref[b, u, h * HEAD_DIM:(h + 1) * HEAD_DIM, :] = o[:, g * CHUNK:(g + 1) * CHUNK].astype(BF16)

    read_state(second, prev=first)
    gate_math(second)
    finish(second)
    for ch in second:
        st_ref[ch["slot"], :, 0:HEAD_DIM] = ch["st_new"]
        m_ref[ch["slot"]:ch["slot"] + 1, :] = ch["m_new"]


def _attention_masks(n_ctx_chunks):
    group = N_HEADS // KV_HEADS
    n_keys = (3 + n_ctx_chunks) * CHUNK
    j = jnp.arange(n_keys)[:, None]
    i = (jnp.arange(group * CHUNK) & (CHUNK - 1))[None, :]
    masks = []
    for cur_ok, prev_ok, next_ok in ((False, False, False), (True, True, True), (True, False, True),
                                     (True, True, False), (True, False, False)):
        lo = (0 if prev_ok else CHUNK) if cur_ok else 3 * CHUNK
        hi = 3 * CHUNK if next_ok else 2 * CHUNK
        valid = ((j >= i) & (j <= i + 2 * CHUNK) & (j >= lo) & (j < hi)) | (j >= 3 * CHUNK)
        masks.append(jnp.where(valid, 0.0, NEG).astype(F32))
    return jnp.stack(masks)


def _mixers(zs, sgu_wt, sgu_b, aq, ak, av, sink_rows, masks, qk, vo, gt, gate_bias, layer, n_ctx_chunks):
    B, n_chunks = zs.shape[:2]
    assert n_chunks % CPS == 0 and n_ctx_chunks == CPS
    n_pairs = n_chunks // CPS
    ctx_w = n_ctx_chunks * CHUNK
    group = N_HEADS // KV_HEADS
    n_keys = 3 * CHUNK + ctx_w
    kern = functools.partial(_mixers_kernel, n_ctx_chunks=n_ctx_chunks)
    ident = lambda p: p
    prev = lambda p: jnp.maximum(p - 1, 0)
    nxt = lambda p: jnp.minimum(p + 1, n_pairs - 1)
    bwd = lambda p: jnp.where(p == 0, 0, n_pairs - p)
    kspec = lambda f: pl.BlockSpec((B, CPS * CHUNK, KV_W), lambda p: (0, f(p), 0))
    vspec = lambda f: _pair_spec(B, KV_W, f)

    def mask_spec(u):
        def kind(p):
            c = CPS * p + u
            no_prev = (c == n_ctx_chunks).astype(jnp.int32)
            no_next = (c == n_chunks - 1).astype(jnp.int32)
            return jnp.where(c < n_ctx_chunks, 0, 1 + no_prev + 2 * no_next)
        return pl.BlockSpec((None, n_keys, group * CHUNK), lambda p: (kind(p), 0, 0))

    def scan_specs(order):
        return [_pair_spec(B, 2 * GROUP_W, order), _pair_spec(B, GROUP_W, order),
                _pair_spec(B, 4 * N_HEADS, order)]

    n_slots = B * 2 * N_HEADS
    return pl.pallas_call(
        kern,
        grid=(n_pairs,),
        in_specs=[
            _pair_spec(B, 2 * GROUP_W),
            pl.BlockSpec((None, N_HEADS, CHUNK, CHUNK), lambda p: (layer, 0, 0, 0)),
            pl.BlockSpec((None, N_HEADS, CHUNK), lambda p: (layer, 0, 0)),
            _pair_spec(B, GROUP_W),
            kspec(prev), kspec(ident), kspec(nxt), pl.BlockSpec((B, ctx_w, KV_W), lambda p: (0, 0, 0)),
            vspec(prev), vspec(ident), vspec(nxt),
            pl.BlockSpec((B, n_ctx_chunks, KV_W, CHUNK), lambda p: (0, 0, 0, 0)),
            pl.BlockSpec((None, SUBLANES, group * CHUNK), lambda p: (layer, 0, 0)),
            mask_spec(0), mask_spec(1),
        ] + scan_specs(ident) + scan_specs(bwd) + [
            pl.BlockSpec((None, 4 * N_HEADS, CHUNK), lambda p: (layer, 0, 0))],
        out_specs=[_pair_spec(B, GROUP_W), _pair_spec(B, GROUP_W),
                   _pair_spec(B, GROUP_W, ident), _pair_spec(B, GROUP_W, bwd)],
        out_shape=[jax.ShapeDtypeStruct((B, n_chunks, GROUP_W, CHUNK), BF16)] * 2
                  + [jax.ShapeDtypeStruct((B, n_chunks, GROUP_W, CHUNK), F32)] * 2,
        scratch_shapes=[
            pltpu.VMEM((n_slots, ST_ROWS, LANES), F32),
            pltpu.VMEM((n_slots, LANES), F32),
            pltpu.VMEM((CPS * B * KV_HEADS, n_keys, group * CHUNK), F32),
            pltpu.VMEM((CPS * B * KV_HEADS, n_keys, group * CHUNK), BF16),
        ],
        compiler_params=pltpu.CompilerParams(dimension_semantics=("arbitrary",)),
        name="mixers",
    )(zs, sgu_wt, sgu_b, aq, ak, ak, ak, ak, av, av, av, av, sink_rows, masks, masks, qk, vo, gt, qk, vo, gt,
      gate_bias)


N_OUT_SCRATCH = 6


def _hidden_chunks(hidden):
    step = 3 * 2 * LANES
    return [(lo, min(lo + step, hidden)) for lo in range(0, hidden, step)]


def _out_kernel(*refs, hidden, npb, n_tiles, split_ctx):
    _zero_at_first_step(refs[-N_OUT_SCRATCH:-1])
    ctx_ref = refs[0] if split_ctx else None
    (x_ref, a_ref, hf_ref, hb_ref, zo_ref, c_ref, d_ref, moda_ref, modc_ref, g_ref, mg_ref,
     wo_ref, wi_ref, wf_ref, o_ref, x1_ref, y_ref, f_ref, tb_ref, act_ref, bm_ref) = refs[1:] if split_ctx else refs
    s = pl.program_id(0)
    f_ref[s % 2] = _dot(act_ref[...], wf_ref[...])

    x_tile = (_first_tile_from_ctx(ctx_ref, x_ref, jnp.clip(s - 1, 0, n_tiles - 1) % npb)
              if split_ctx else x_ref[...])
    x1 = x_tile + _rms(y_ref[...], moda_ref[2:3, :] * g_ref[1:2, :])
    x1_ref[s % 3] = x1
    tb_ref[...] = (_rms(x1, g_ref[2:3, :] * (1.0 + moda_ref[4:5, :])) + moda_ref[3:4, :]).astype(BF16)

    o_ref[...] = x1_ref[(s + 1) % 3] + _rms(f_ref[(s + 1) % 2], modc_ref[5:6, :] * g_ref[3:4, :])

    for lo, hi in _hidden_chunks(hidden):
        gate = _dot(tb_ref[...], wi_ref[:, lo:hi])
        up = _dot(tb_ref[...], wi_ref[:, hidden + lo:hidden + hi])
        act_ref[:, lo:hi] = (_silu(gate) * up).astype(BF16)

    hs = _get_chunks(hf_ref) + _get_chunks(hb_ref)
    hn = jnp.concatenate([_head_layer_norm_t(hs[h * HEAD_DIM:(h + 1) * HEAD_DIM]) for h in range(N_HEADS)], axis=0)
    bm_ref[...] = (hn * mg_ref[...] * jax.nn.sigmoid(_get_chunks(zo_ref).astype(F32))).astype(BF16)
    y_ref[...] = (_dot_tn(_get_chunks(a_ref), wo_ref[0:GROUP_W, :])
                  + _dot_tn(bm_ref[...], wo_ref[GROUP_W:2 * GROUP_W, :])
                  + _dot_tn(_get_chunks(c_ref), wo_ref[2 * GROUP_W:3 * GROUP_W, :])
                  + _dot(d_ref[...], wo_ref[3 * GROUP_W:4 * GROUP_W, :]))


def _out_ffn(xs, a, hf, hb, vo, c, d, mods, norm_g, mnorm_g, w_out, w_ffn_in, w_ffn_out, layer, n_batch_rows,
             drop_ctx):
    B, S, D = _stream_shape(xs)
    tm = ROW_TILE
    npb = S // tm
    n_tiles = B * npb
    hidden = w_ffn_out.shape[1]
    cur = lambda s: jnp.minimum(s, n_tiles - 1)
    mid = lambda s: jnp.clip(s - 1, 0, n_tiles - 1)
    prv = lambda s: jnp.clip(s - 3, 0, n_tiles - 1)
    row = lambda s: (cur(s) // npb, cur(s) % npb, 0)
    mod_row = lambda t: (layer, jnp.where(t % npb == 0, n_batch_rows, t // npb), 0, 0)
    c2 = lambda s: (0, 0)
    this_layer = lambda s: (layer, 0, 0)
    cpt = tm // CHUNK
    fm = pl.BlockSpec((None, cpt, GROUP_W, CHUNK), lambda s: (cur(s) // npb, cur(s) % npb, 0, 0))
    split_ctx = isinstance(xs, tuple)
    x_specs, x_args = _stream_specs(xs, tm, mid, npb)
    in_specs = x_specs + [
        fm, fm, fm,
        pl.BlockSpec((None, cpt, GROUP_W, CHUNK), lambda s: (cur(s) // npb, cur(s) % npb, 1, 0)),
        fm,
        pl.BlockSpec((None, tm, GROUP_W), row),
        pl.BlockSpec((None, None, 6, D), lambda s: mod_row(mid(s))),
        pl.BlockSpec((None, None, 6, D), lambda s: mod_row(prv(s))),
        pl.BlockSpec((None, 4, D), this_layer),
        pl.BlockSpec((None, GROUP_W, tm), this_layer),
        pl.BlockSpec((None, D, D), this_layer, pipeline_mode=pl.Buffered(1)),
        pl.BlockSpec((None, D, 2 * hidden), this_layer, pipeline_mode=pl.Buffered(1)),
        pl.BlockSpec((None, hidden, D), this_layer, pipeline_mode=pl.Buffered(1)),
    ]
    args = x_args + [a, hf, hb, vo, c, d, mods, mods, norm_g, mnorm_g, w_out, w_ffn_in, w_ffn_out]
    scratch = [pltpu.VMEM((3, tm, D), F32), pltpu.VMEM((tm, D), F32), pltpu.VMEM((2, tm, D), F32),
               pltpu.VMEM((tm, D), BF16), pltpu.VMEM((tm, hidden), BF16), pltpu.VMEM((GROUP_W, tm), BF16)]
    params = pltpu.CompilerParams(dimension_semantics=("arbitrary",), vmem_limit_bytes=VMEM_LIMIT)
    static = dict(hidden=hidden, npb=npb, n_tiles=n_tiles, split_ctx=split_ctx)
    if drop_ctx:
        out_spec = pl.BlockSpec((None, tm, D), lambda s: (prv(s) // npb, jnp.maximum(prv(s) % npb - 1, 0), 0))
        out_shape = jax.ShapeDtypeStruct((B, S - tm, D), F32)
    else:
        out_spec = pl.BlockSpec((None, tm, D), lambda s: (prv(s) // npb, prv(s) % npb, 0))
        out_shape = jax.ShapeDtypeStruct((B, S, D), F32)
    return pl.pallas_call(
        functools.partial(_out_kernel, **static),
        grid=(n_tiles + 3,),
        in_specs=in_specs,
        out_specs=out_spec,
        out_shape=out_shape,
        scratch_shapes=scratch, compiler_params=params, name="out_ffn",
    )(*args)


def _rope_tables(n_tokens, ctx_len, tm):
    rows = n_tokens // GRID_W
    axis_freq = HEAD_DIM // 4
    inv = jnp.power(ROPE_BASE, -jnp.arange(axis_freq, dtype=F32) * 2.0 / (2 * axis_freq))
    ar = jnp.arange(rows, dtype=F32)[:, None] * inv
    ac = jnp.arange(GRID_W, dtype=F32)[:, None] * inv

    def per_token(fn):
        r = jnp.broadcast_to(fn(ar)[:, None, :], (rows, GRID_W, axis_freq))
        c = jnp.broadcast_to(fn(ac)[None, :, :], (rows, GRID_W, axis_freq))
        return jnp.concatenate([r, r, c, c], axis=-1).reshape(n_tokens, HEAD_DIM)

    cos, sin = per_token(jnp.cos), per_token(jnp.sin)
    sign = jnp.where((jnp.arange(HEAD_DIM) % 32) < 16, -1.0, 1.0).astype(F32)
    cos = jnp.concatenate([jnp.ones((ctx_len, HEAD_DIM), F32), cos], axis=0)
    sin = jnp.concatenate([jnp.zeros((ctx_len, HEAD_DIM), F32), sin * sign], axis=0)
    t = jnp.arange(tm)[:, None]
    half = jnp.asarray([w // 2 for w in POOL_WINDOWS])[None, :]
    inv_cnt = []
    for last in (False, True):
        for first in (False, True):
            lo = jnp.maximum(t - half, 0) if first else t - half
            hi = jnp.minimum(t + half, tm) if last else t + half
            inv_cnt.append(jnp.repeat(1.0 / (hi - lo).astype(F32), HEAD_DIM, axis=1))
    q_scale = HEAD_DIM ** -0.5 * LOG2E
    return (jnp.tile(cos, (1, KV_HEADS)), jnp.tile(sin, (1, KV_HEADS)), cos.T * q_scale, sin.T * q_scale,
            jnp.stack(inv_cnt))


def _arrange_w_in(w_in):
    depth = w_in.shape[0]
    ml0 = 2 * GROUP_W
    gate0 = ml0 + 4 * GROUP_W
    att0 = gate0 + 4 * N_HEADS
    pool0 = att0 + GROUP_W + 2 * KV_W
    w_t = jnp.swapaxes(w_in, 1, 2)
    gates = w_t[:, gate0:att0].reshape(depth, 2, 2, N_HEADS, -1)
    gates = gates.transpose(0, 2, 1, 3, 4).reshape(depth, 4 * N_HEADS, -1)
    feat = jnp.concatenate([
        w_t[:, 0:gate0],
        w_t[:, att0:att0 + GROUP_W],
        w_t[:, att0 + GROUP_W + KV_W:pool0],
        gates,
    ], axis=1)
    tok = jnp.concatenate([w_t[:, att0 + GROUP_W:att0 + GROUP_W + KV_W], w_t[:, pool0:pool0 + GROUP_W]], axis=1)
    return feat.astype(BF16), jnp.swapaxes(tok.astype(BF16), 1, 2)


def _block_diag(w):
    g = w.shape[0]
    eye = jnp.eye(g, dtype=w.dtype)
    return (eye[:, None, :, None] * w[:, :, None, :]).reshape(g * w.shape[1], g * w.shape[2])


def kernel(x, c, ctx, c_ctx, w_mod, b_mod, norm_g, w_in, w_out, sgu_w, sgu_b, mlstm_conv_w, mlstm_gate_b,
           mlstm_norm_g, attn_sink, pool_w, pool_scale, w_ffn_in, w_ffn_out):
    B, N, D = x.shape
    ctx_len = ctx.shape[1]
    depth = w_mod.shape[0]
    assert D == 4 * GROUP_W and N % ROW_TILE == 0 and ctx_len == ROW_TILE
    assert w_in.shape[2] == 2 * GROUP_W + 4 * GROUP_W + 4 * N_HEADS + GROUP_W + 2 * KV_W + GROUP_W
    n_ctx_chunks = ctx_len // CHUNK

    xs = (ctx, x)
    rows = -(-(B + 1) // SUBLANES) * SUBLANES
    cc = jnp.zeros((rows, D), F32).at[0:B].set(c).at[B].set(c_ctx)
    mods = _modulation(cc, w_mod, b_mod).reshape(depth, rows, 6, D)
    tables = _rope_tables(N, ctx_len, ROW_TILE)
    masks = _attention_masks(n_ctx_chunks)

    wt, wk = _arrange_w_in(w_in)
    w_out_b, w_ffn_in_b, w_ffn_out_b = w_out.astype(BF16), w_ffn_in.astype(BF16), w_ffn_out.astype(BF16)

    group = N_HEADS // KV_HEADS
    conv_b = jnp.broadcast_to(mlstm_conv_w[:, :, :, None], (depth, 3, 2 * GROUP_W, ROW_TILE))
    proj = (wt, wk, tables, conv_b, jax.vmap(_block_diag)(pool_w).astype(BF16), pool_scale.reshape(depth, 1, GROUP_W))
    gate_bias = jnp.broadcast_to(
        mlstm_gate_b.reshape(depth, 2, 2, N_HEADS).transpose(0, 2, 1, 3).reshape(depth, 4 * N_HEADS, 1),
        (depth, 4 * N_HEADS, CHUNK))
    sink_rows = jnp.zeros((depth, SUBLANES, group * CHUNK), F32).at[:, 0:KV_HEADS].set(
        jnp.repeat(attn_sink.reshape(depth, KV_HEADS, group) * LOG2E, CHUNK, axis=2))
    mnorm_b = jnp.broadcast_to(mlstm_norm_g[:, :, None], (depth, GROUP_W, ROW_TILE))
    sgu_wt = sgu_w.transpose(0, 1, 3, 2).astype(BF16)

    for l in range(depth):
        zs, qk, vo, gt, aq, av, ak, dm = _in_proj(xs, mods, norm_g[:, 0:1], proj, l, B)
        a, cm, hf, hb = _mixers(zs, sgu_wt, sgu_b, aq, ak, av, sink_rows, masks, qk, vo, gt, gate_bias, l,
                                n_ctx_chunks)
        xs = _out_ffn(xs, a, hf, hb, vo, cm, dm, mods, norm_g, mnorm_b, w_out_b, w_ffn_in_b, w_ffn_out_b, l, B,
                      drop_ctx=(l == depth - 1))
    return xs
```

```python
import functools

import jax
import jax.numpy as jnp
from jax import lax
from jax.experimental import pallas as pl
from jax.experimental.pallas import tpu as pltpu

F32 = jnp.float32
BF16 = jnp.bfloat16

GRID_W = 64
ROPE_BASE = 10000.0
EPS = 1e-6
LANES = 128
SUBLANES = 8
BF16_ROWS = 16
CHUNK = 128
HEAD_DIM = 64
N_HEADS = 4
KV_HEADS = 2
GROUP_W = N_HEADS * HEAD_DIM
KV_W = KV_HEADS * HEAD_DIM
POOL_WINDOWS = (2, 4, 8, 16)
HALO = 8
ROW_TILE = 256
NEG = -1e30
VMEM_LIMIT = 56 * 1024 * 1024

R_SGU = 0
R_QK = 512
R_VO = 1024
R_AQ = 1536
R_AV = 1792
R_GATE = 1920
R_END = 1936
T_AK = 0
T_POOL = KV_W
T_END = KV_W + GROUP_W

ST_ROWS = 80
LOG2E = 1.4426950408889634

HI = lax.Precision.HIGHEST


def _dot(a, b):
    return jnp.dot(a, b, preferred_element_type=F32)


def _dot_hi(a, b):
    return jnp.dot(a, b, preferred_element_type=F32, precision=HI)


def _dot_nt(a, b):
    return lax.dot_general(a, b, (((1,), (1,)), ((), ())), preferred_element_type=F32)


def _dot_tn(a, b):
    return lax.dot_general(a, b, (((0,), (0,)), ((), ())), preferred_element_type=F32)


def _rms(x, g):
    return x * lax.rsqrt(jnp.mean(x * x, axis=-1, keepdims=True) + EPS) * g


def _silu(x):
    return x * jax.nn.sigmoid(x)


def _put_chunks(ref, x):
    for i in range(ref.shape[0]):
        ref[i] = x[:, i * CHUNK:(i + 1) * CHUNK].astype(ref.dtype)


def _get_chunks(ref):
    return jnp.concatenate([ref[i] for i in range(ref.shape[0])], axis=1)


def _head_layer_norm_t(v):
    mu = jnp.mean(v, axis=0, keepdims=True)
    vc = v - mu
    return vc * lax.rsqrt(jnp.mean(vc * vc, axis=0, keepdims=True) + EPS)


def _zero_at_first_step(refs):
    @pl.when(pl.program_id(0) == 0)
    def _():
        for ref in refs:
            ref[...] = jnp.zeros_like(ref)


def _mod_kernel(c_ref, w_ref, b_ref, o_ref):
    o_ref[...] = _dot(_silu(c_ref[...]).astype(BF16), w_ref[...].astype(BF16)) + b_ref[...]


def _modulation(cc, w_mod, b_mod):
    depth, d, n6 = w_mod.shape
    rows = cc.shape[0]
    tn = 1536
    return pl.pallas_call(
        _mod_kernel,
        grid=(depth, n6 // tn),
        in_specs=[
            pl.BlockSpec((rows, d), lambda l, j: (0, 0)),
            pl.BlockSpec((None, d, tn), lambda l, j: (l, 0, j)),
            pl.BlockSpec((None, 1, tn), lambda l, j: (l, 0, j)),
        ],
        out_specs=pl.BlockSpec((None, rows, tn), lambda l, j: (l, 0, j)),
        out_shape=jax.ShapeDtypeStruct((depth, rows, n6), F32),
        compiler_params=pltpu.CompilerParams(
            dimension_semantics=("arbitrary", "arbitrary"), vmem_limit_bytes=VMEM_LIMIT),
        name="modulation",
    )(cc, w_mod, b_mod.reshape(depth, 1, n6))


N_IN_SCRATCH = 4


def _first_tile_from_ctx(ctx_ref, x_ref, tile):
    rows = lax.broadcasted_iota(jnp.int32, x_ref.shape, 0)
    return jnp.where(rows < jnp.where(tile == 0, x_ref.shape[0], 0), ctx_ref[...], x_ref[...])


def _project_tile(hb, has_prev, has_next, wt_ref, wk_ref, cos_ref, sin_ref, cost_ref, sint_ref,
                  conv_ref, pw_ref, ps_ref, invc_ref,
                  zs_ref, qk_ref, vo_ref, gt_ref, aq_ref, av_ref, ak_ref, d_ref, z_ref, pcol_ref, zp_ref, prow_ref):
    tm = hb.shape[0]
    zc = _dot_nt(wt_ref[R_QK:R_VO, :], hb)
    _put_chunks(zs_ref, _dot_nt(wt_ref[R_SGU:R_QK, :], hb))

    z = z_ref[...]
    lane = lax.broadcasted_iota(jnp.int32, z.shape, 1)
    before = jnp.where(lane == 0, pcol_ref[:, 0:1] * has_prev, pltpu.roll(z, 1, axis=1))
    after = jnp.where(lane == tm - 1, zc[:, 0:1] * has_next, pltpu.roll(z, tm - 1, axis=1))
    qk = _silu(before * conv_ref[0] + z * conv_ref[1] + after * conv_ref[2])
    _put_chunks(qk_ref, jnp.concatenate([qk[0:GROUP_W], qk[GROUP_W:2 * GROUP_W] * (HEAD_DIM ** -0.5)], axis=0))
    pcol_ref[...] = jnp.broadcast_to(z[:, tm - 1:tm], pcol_ref.shape)
    z_ref[...] = zc

    zp_new = _dot(hb, wk_ref[:, T_POOL:T_END])
    _put_chunks(vo_ref, _dot_nt(wt_ref[R_VO:R_AQ, :], hb))
    zp = zp_ref[...]
    rows = tm + 2 * HALO
    ext = jnp.concatenate([prow_ref[...] * has_prev, zp, zp_new[0:HALO] * has_next], axis=0)
    s2 = ext + pltpu.roll(ext, 1, axis=0)
    s4 = pltpu.roll(s2, 1, axis=0) + pltpu.roll(s2, rows - 1, axis=0)
    s8 = pltpu.roll(s4, 2, axis=0) + pltpu.roll(s4, rows - 2, axis=0)
    s16 = pltpu.roll(s8, 4, axis=0) + pltpu.roll(s8, rows - 4, axis=0)
    sums = [s[HALO:HALO + tm] for s in (s2, s4, s8, s16)]
    plane = lax.broadcasted_iota(jnp.int32, zp.shape, 1)
    pooled = sums[-1]
    for gi in range(len(POOL_WINDOWS) - 2, -1, -1):
        pooled = jnp.where(plane < (gi + 1) * HEAD_DIM, sums[gi], pooled)
    pooled = pooled * invc_ref[...]
    d_ref[...] = (_dot((pooled - zp).astype(BF16), pw_ref[...]) * ps_ref[...]).astype(BF16)
    prow_ref[...] = zp[tm - HALO:tm]
    zp_ref[...] = zp_new

    q = _dot_nt(wt_ref[R_AQ:R_AV, :], hb)
    k = _dot(hb, wk_ref[:, T_AK:T_POOL])
    _put_chunks(av_ref, _dot_nt(wt_ref[R_AV:R_GATE, :], hb))
    _put_chunks(gt_ref, _dot_nt(wt_ref[R_GATE:R_END, :], hb))
    cos_t = jnp.concatenate([cost_ref[...]] * N_HEADS, axis=0)
    sin_t = jnp.concatenate([sint_ref[...]] * N_HEADS, axis=0)
    row = lax.broadcasted_iota(jnp.int32, q.shape, 0)
    rot_q = jnp.where((row & 31) < 16, pltpu.roll(q, GROUP_W - 16, axis=0), pltpu.roll(q, 16, axis=0))
    _put_chunks(aq_ref, q * cos_t + rot_q * sin_t)
    klane = lax.broadcasted_iota(jnp.int32, k.shape, 1)
    rot_k = jnp.where((klane & 31) < 16, pltpu.roll(k, KV_W - 16, axis=1), pltpu.roll(k, 16, axis=1))
    ak_ref[...] = (k * cos_ref[...] + rot_k * sin_ref[...]).astype(BF16)


def _segment_flags(tile, npb):
    j = tile % npb
    return (j >= 2).astype(F32), jnp.logical_and(j >= 1, j < npb - 1).astype(F32)


def _in_kernel(*refs, npb, n_tiles, split_ctx):
    ctx_ref = refs[0] if split_ctx else None
    x_ref, mod_ref, g_ref = (refs[1:] if split_ctx else refs)[0:3]
    project_refs = (refs[1:] if split_ctx else refs)[3:-1]
    hb_ref = refs[-1]
    s = pl.program_id(0)

    _zero_at_first_step((hb_ref,) + tuple(project_refs[-N_IN_SCRATCH:]))
    has_prev, has_next = _segment_flags(jnp.maximum(s - 2, 0), npb)
    gain = g_ref[...] * (1.0 + mod_ref[1:2, :])
    x_tile = _first_tile_from_ctx(ctx_ref, x_ref, jnp.minimum(s, n_tiles - 1) % npb) if split_ctx else x_ref[...]
    hb_next = (_rms(x_tile, gain) + mod_ref[0:1, :]).astype(BF16)
    _project_tile(hb_ref[...], has_prev, has_next, *project_refs)
    hb_ref[...] = hb_next


def _stream_specs(xs, tm, tile_of_step, npb):
    bj = lambda s: (tile_of_step(s) // npb, tile_of_step(s) % npb)
    if isinstance(xs, tuple):
        ctx, x = xs
        D = x.shape[2]
        return [pl.BlockSpec((None, tm, D), lambda s: (bj(s)[0], 0, 0)),
                pl.BlockSpec((None, tm, D), lambda s: (bj(s)[0], jnp.maximum(bj(s)[1] - 1, 0), 0))], [ctx, x]
    return [pl.BlockSpec((None, tm, xs.shape[2]), lambda s: (bj(s)[0], bj(s)[1], 0))], [xs]


def _stream_shape(xs):
    if isinstance(xs, tuple):
        return xs[1].shape[0], xs[0].shape[1] + xs[1].shape[1], xs[1].shape[2]
    return xs.shape


def _project_call_parts(B, S, D, tm, npb, mid, old, proj, layer):
    wt, wk, tables, conv_b, pool_w_bd, pool_scale = proj
    cos_k, sin_k, cos_t, sin_t, inv_cnt = tables
    c2 = lambda s: (0, 0)
    cpt = tm // CHUNK
    pool_kind = lambda j: (j <= 1).astype(jnp.int32) + 2 * ((j == 0) | (j == npb - 1)).astype(jnp.int32)
    in_specs = [
        pl.BlockSpec((None, R_END, D), lambda s: (layer, 0, 0), pipeline_mode=pl.Buffered(1)),
        pl.BlockSpec((None, D, T_END), lambda s: (layer, 0, 0), pipeline_mode=pl.Buffered(1)),
        pl.BlockSpec((tm, KV_W), lambda s: (mid(s) % npb, 0)),
        pl.BlockSpec((tm, KV_W), lambda s: (mid(s) % npb, 0)),
        pl.BlockSpec((HEAD_DIM, tm), lambda s: (0, mid(s) % npb)),
        pl.BlockSpec((HEAD_DIM, tm), lambda s: (0, mid(s) % npb)),
        pl.BlockSpec((None, 3, 2 * GROUP_W, tm), lambda s: (layer, 0, 0, 0)),
        pl.BlockSpec((None, GROUP_W, GROUP_W), lambda s: (layer, 0, 0)),
        pl.BlockSpec((None, 1, GROUP_W), lambda s: (layer, 0, 0)),
        pl.BlockSpec((None, tm, GROUP_W), lambda s: (pool_kind(old(s) % npb), 0, 0)),
    ]
    args = [wt, wk, cos_k, sin_k, cos_t, sin_t, conv_b, pool_w_bd, pool_scale, inv_cnt]

    def fm(rows, dt, tile):
        return (pl.BlockSpec((None, cpt, rows, CHUNK), lambda s: (tile(s) // npb, tile(s) % npb, 0, 0)),
                jax.ShapeDtypeStruct((B, S // CHUNK, rows, CHUNK), dt))

    def tk(cols, dt, tile):
        return (pl.BlockSpec((None, tm, cols), lambda s: (tile(s) // npb, tile(s) % npb, 0)),
                jax.ShapeDtypeStruct((B, S, cols), dt))

    outs = [fm(2 * GROUP_W, BF16, mid), fm(2 * GROUP_W, BF16, old), fm(2 * GROUP_W, BF16, mid),
            fm(4 * N_HEADS, F32, mid), fm(GROUP_W, BF16, mid), fm(KV_W, BF16, mid),
            tk(KV_W, BF16, mid), tk(GROUP_W, BF16, old)]
    scratch = [pltpu.VMEM((2 * GROUP_W, tm), F32), pltpu.VMEM((2 * GROUP_W, LANES), F32),
               pltpu.VMEM((tm, GROUP_W), F32), pltpu.VMEM((HALO, GROUP_W), F32)]
    return in_specs, args, [o[0] for o in outs], [o[1] for o in outs], scratch


def _in_proj(xs, mods, g0, proj, layer, n_batch_rows):
    B, S, D = _stream_shape(xs)
    tm = ROW_TILE
    npb = S // tm
    n_tiles = B * npb
    cur = lambda s: jnp.minimum(s, n_tiles - 1)
    mid = lambda s: jnp.clip(s - 1, 0, n_tiles - 1)
    old = lambda s: jnp.maximum(s - 2, 0)
    kern = functools.partial(_in_kernel, npb=npb, n_tiles=n_tiles, split_ctx=isinstance(xs, tuple))
    x_specs, x_args = _stream_specs(xs, tm, cur, npb)
    p_specs, p_args, out_specs, out_shapes, p_scratch = _project_call_parts(B, S, D, tm, npb, mid, old, proj, layer)
    return pl.pallas_call(
        kern,
        grid=(n_tiles + 2,),
        in_specs=x_specs + [
            pl.BlockSpec((None, None, 6, D),
                         lambda s: (layer, jnp.where(cur(s) % npb == 0, n_batch_rows, cur(s) // npb), 0, 0)),
            pl.BlockSpec((None, 1, D), lambda s: (layer, 0, 0)),
        ] + p_specs,
        out_specs=out_specs,
        out_shape=out_shapes,
        scratch_shapes=p_scratch + [pltpu.VMEM((tm, D), BF16)],
        compiler_params=pltpu.CompilerParams(dimension_semantics=("arbitrary",), vmem_limit_bytes=VMEM_LIMIT),
        name="in_proj",
    )(*x_args, mods, g0, *p_args)


def _log_sigmoid(x):
    return jnp.minimum(x, 0.0) - jnp.log1p(jnp.exp(-jnp.abs(x)))


CPS = 2


def _pair_spec(n_b, rows, pair=lambda p: p, row_block=0):
    return pl.BlockSpec((n_b, CPS, rows, CHUNK), lambda p: (0, pair(p), row_block, 0))


def _mixers_sample(samples, zs_ref, sw_ref, sb_ref,
                   q_ref, kp_ref, kc_ref, kn_ref, kx_ref, vp_ref, vc_ref, vn_ref, vx_ref, sink_ref, mask0_ref, mask1_ref,
                   qkf_ref, vf_ref, gf_ref, qkb_ref, vb_ref, gb_ref, gbias_ref,
                   a_ref, c_ref, hf_ref, hb_ref, st_ref, m_ref, sc_ref, p_ref, *, n_ctx_chunks):

    group = N_HEADS // KV_HEADS
    zeros = jnp.zeros((HEAD_DIM, group * CHUNK), BF16)
    pairs = [(u, b, kv) for u in range(CPS) for b in samples for kv in range(KV_HEADS)]
    mask_refs = (mask0_ref, mask1_ref)

    def keys(u, b):
        local = ([kp_ref[b, CHUNK:2 * CHUNK, :], kc_ref[b]] if u == 0 else [kc_ref[b], kn_ref[b, 0:CHUNK, :]])
        return jnp.concatenate(local + [kx_ref[b]], axis=0)

    def values(u, b):
        local = ([vp_ref[b, 1], vc_ref[b, 0], vc_ref[b, 1]] if u == 0 else [vc_ref[b, 0], vc_ref[b, 1], vn_ref[b, 0]])
        return jnp.concatenate(local + [vx_ref[b, i] for i in range(n_ctx_chunks)], axis=1)

    for i, (u, b, kv) in enumerate(pairs):
        q2 = jnp.concatenate([q_ref[b, u, (kv * group + g) * HEAD_DIM:(kv * group + g + 1) * HEAD_DIM, :]
                              for g in range(group)], axis=1)
        qm = jnp.concatenate([q2, zeros] if kv == 0 else [zeros, q2], axis=0)
        sc_ref[i] = _dot(keys(u, b), qm)

    s_i = lax.broadcasted_iota(jnp.int32, (CHUNK, CHUNK), 0)
    t_i = lax.broadcasted_iota(jnp.int32, (CHUNK, CHUNK), 1)
    causal = {True: s_i <= t_i, False: s_i >= t_i}
    ones_row = jnp.where(lax.broadcasted_iota(jnp.int32, (ST_ROWS - HEAD_DIM, CHUNK), 0) == 0, 1.0, 0.0).astype(BF16)
    all_ones = jnp.ones((CHUNK, CHUNK), F32)

    def start_chains(t):
        chains = []
        for b in samples:
            for forward, qk_ref, v_ref, g_ref, o_ref in ((True, qkf_ref, vf_ref, gf_ref, hf_ref),
                                                          (False, qkb_ref, vb_ref, gb_ref, hb_ref)):
                u = t if forward else CPS - 1 - t
                gates = g_ref[b, u] + gbias_ref[...]
                li8 = gates[0:2 * N_HEADS] * LOG2E
                lf8 = _log_sigmoid(gates[2 * N_HEADS:4 * N_HEADS]) * LOG2E
                cum8 = _dot_hi(lf8, jnp.where(causal[forward], 1.0, 0.0).astype(F32))
                tot8 = _dot_hi(lf8, all_ones)
                a8 = li8 - cum8
                for h in range(N_HEADS):
                    r = (0 if forward else N_HEADS) + h
                    ch = dict(b=b, u=u, h=h, mask=causal[forward], o_ref=o_ref,
                              slot=(b * 2 + (0 if forward else 1)) * N_HEADS + h,
                              a=a8[r:r + 1], bcum=cum8[r:r + 1], b_end=tot8[r:r + 1],
                              q=qk_ref[b, u, h * HEAD_DIM:(h + 1) * HEAD_DIM, :],
                              k=qk_ref[b, u, GROUP_W + h * HEAD_DIM:GROUP_W + (h + 1) * HEAD_DIM, :],
                              v=v_ref[b, u, h * HEAD_DIM:(h + 1) * HEAD_DIM, :])
                    ch["kq"] = _dot_tn(ch["k"], ch["q"])
                    chains.append(ch)
        return chains

    def read_state(chains, prev=None):
        for i, ch in enumerate(chains):
            ch["st"] = st_ref[ch["slot"]][:, 0:HEAD_DIM] if prev is None else prev[i]["st_new"]
            ch["m"] = m_ref[ch["slot"]] if prev is None else prev[i]["m_new"]
            ch["cq"] = _dot(ch["st"].astype(BF16), ch["q"])

    def gate_math(chains):
        for ch in chains:
            a, m, mask = ch["a"], ch["m"], ch["mask"]
            a_col = jnp.broadcast_to(a, (CHUNK, CHUNK)).T
            a_max = jnp.max(a, axis=1, keepdims=True)
            big_m = jnp.maximum(jnp.max(jnp.where(mask, a_col, NEG), axis=0, keepdims=True), m)
            ch["inter"] = jnp.exp2(m - big_m)
            ch["floor"] = jnp.exp2(-(ch["bcum"] + big_m))
            s = ch["kq"] * jnp.exp2(jnp.where(mask, a_col - big_m, NEG))
            ch["rowsum"] = jnp.sum(s, axis=0, keepdims=True)
            ch["s"] = s.astype(BF16)
            m_new = jnp.maximum(ch["b_end"] + m, ch["b_end"] + a_max)
            ch["decay"] = jnp.exp2(ch["b_end"] + m - m_new)
            ch["kw"] = (ch["k"].astype(F32) * jnp.exp2(ch["b_end"] + a - m_new)).astype(BF16)
            ch["m_new"] = m_new

    def finish(chains):
        for ch in chains:
            ch["vs"] = _dot(ch["v"], ch["s"])
            v_ext = jnp.concatenate([ch["v"], ones_row], axis=0)
            ch["upd"] = _dot_nt(v_ext, ch["kw"])
        for ch in chains:
            h, cq, inter = ch["h"], ch["cq"], ch["inter"]
            num = cq[0:HEAD_DIM] * inter + ch["vs"]
            den = inter * cq[HEAD_DIM:HEAD_DIM + 1] + ch["rowsum"]
            ch["o_ref"][ch["b"], ch["u"], h * HEAD_DIM:(h + 1) * HEAD_DIM, :] = (
                num / jnp.maximum(jnp.abs(den), ch["floor"]))
            ch["st_new"] = ch["st"] * ch["decay"][:, 0:HEAD_DIM] + ch["upd"]

    first = start_chains(0)
    read_state(first)
    second = start_chains(1)

    for u in range(CPS):
        for b in samples:
            g = jax.nn.gelu(zs_ref[b, u].astype(F32))
            for h in range(N_HEADS):
                gu = g[h * HEAD_DIM:(h + 1) * HEAD_DIM]
                vh = _head_layer_norm_t(g[GROUP_W + h * HEAD_DIM:GROUP_W + (h + 1) * HEAD_DIM]).astype(BF16)
                mixed = _dot(vh, sw_ref[h]) + sb_ref[h:h + 1, :]
                a_ref[b, u, h * HEAD_DIM:(h + 1) * HEAD_DIM, :] = (gu * mixed).astype(BF16)

    p_sinks = []
    for i, (u, b, kv) in enumerate(pairs):
        sink = sink_ref[kv:kv + 1, :]
        blocks = [sc_ref[i, r:r + CHUNK, :] + mask_refs[u][r:r + CHUNK, :] for r in range(0, sc_ref.shape[1], CHUNK)]
        m = jnp.maximum(jnp.max(functools.reduce(jnp.maximum, blocks), axis=0, keepdims=True), sink)
        for r in range(0, sc_ref.shape[1], CHUNK // 2):
            rows = slice(r, r + CHUNK // 2)
            p_ref[i, rows, :] = jnp.exp2(sc_ref[i, rows, :] + mask_refs[u][rows, :] - m).astype(BF16)
        p_sinks.append(jnp.exp2(sink - m))

    gate_math(first)
    finish(first)

    ones_keys = jnp.where(lax.broadcasted_iota(jnp.int32, (BF16_ROWS, (3 + n_ctx_chunks) * CHUNK), 0) == 0,
                          1.0, 0.0).astype(BF16)
    for i, ((u, b, kv), p_sink) in enumerate(zip(pairs, p_sinks)):
        o_ext = _dot(jnp.concatenate([values(u, b)[kv * HEAD_DIM:(kv + 1) * HEAD_DIM, :], ones_keys], axis=0),
                     p_ref[i])
        o = o_ext[0:HEAD_DIM] / (o_ext[HEAD_DIM:HEAD_DIM + 1] + p_sink)
        for g in range(group):
            h = kv * group + g
            c_ref[b, u, h * HEAD_DIM:(h + 1) * HEAD_DIM, :] = o[:, g * CHUNK:(g + 1) * CHUNK].astype(BF16)

    read_state(second, prev=first)
    gate_math(second)
    finish(second)
    for ch in second:
        st_ref[ch["slot"], :, 0:HEAD_DIM] = ch["st_new"]
        m_ref[ch["slot"]] = ch["m_new"]


def _mixers_kernel(*refs, n_ctx_chunks):
    _zero_at_first_step(refs[-4:-2])

    def one_sample(b, carry):
        _mixers_sample((b,), *refs, n_ctx_chunks=n_ctx_chunks)
        return carry

    lax.fori_loop(0, refs[0].shape[0], one_sample, 0)


def _attention_masks(n_ctx_chunks):
    group = N_HEADS // KV_HEADS
    n_keys = (3 + n_ctx_chunks) * CHUNK
    j = jnp.arange(n_keys)[:, None]
    i = (jnp.arange(group * CHUNK) & (CHUNK - 1))[None, :]
    masks = []
    for cur_ok, prev_ok, next_ok in ((False, False, False), (True, True, True), (True, False, True),
                                     (True, True, False), (True, False, False)):
        lo = (0 if prev_ok else CHUNK) if cur_ok else 3 * CHUNK
        hi = 3 * CHUNK if next_ok else 2 * CHUNK
        valid = ((j >= i) & (j <= i + 2 * CHUNK) & (j >= lo) & (j < hi)) | (j >= 3 * CHUNK)
        masks.append(jnp.where(valid, 0.0, NEG).astype(F32))
    return jnp.stack(masks)


def _mixers(zs, sgu_wt, sgu_b, aq, ak, av, sink_rows, masks, qk, vo, gt, gate_bias, layer, n_ctx_chunks):
    B, n_chunks = zs.shape[:2]
    assert n_chunks % CPS == 0 and n_ctx_chunks == CPS
    n_pairs = n_chunks // CPS
    ctx_w = n_ctx_chunks * CHUNK
    group = N_HEADS // KV_HEADS
    n_keys = 3 * CHUNK + ctx_w
    kern = functools.partial(_mixers_kernel, n_ctx_chunks=n_ctx_chunks)
    ident = lambda p: p
    prev = lambda p: jnp.maximum(p - 1, 0)
    nxt = lambda p: jnp.minimum(p + 1, n_pairs - 1)
    bwd = lambda p: jnp.where(p == 0, 0, n_pairs - p)
    kspec = lambda f: pl.BlockSpec((B, CPS * CHUNK, KV_W), lambda p: (0, f(p), 0))
    vspec = lambda f: _pair_spec(B, KV_W, f)

    def mask_spec(u):
        def kind(p):
            c = CPS * p + u
            no_prev = (c == n_ctx_chunks).astype(jnp.int32)
            no_next = (c == n_chunks - 1).astype(jnp.int32)
            return jnp.where(c < n_ctx_chunks, 0, 1 + no_prev + 2 * no_next)
        return pl.BlockSpec((None, n_keys, group * CHUNK), lambda p: (kind(p), 0, 0))

    def scan_specs(order):
        return [_pair_spec(B, 2 * GROUP_W, order), _pair_spec(B, GROUP_W, order),
                _pair_spec(B, 4 * N_HEADS, order)]

    n_slots = B * 2 * N_HEADS
    return pl.pallas_call(
        kern,
        grid=(n_pairs,),
        in_specs=[
            _pair_spec(B, 2 * GROUP_W),
            pl.BlockSpec((None, N_HEADS, CHUNK, CHUNK), lambda p: (layer, 0, 0, 0)),
            pl.BlockSpec((None, N_HEADS, CHUNK), lambda p: (layer, 0, 0)),
            _pair_spec(B, GROUP_W),
            kspec(prev), kspec(ident), kspec(nxt), pl.BlockSpec((B, ctx_w, KV_W), lambda p: (0, 0, 0)),
            vspec(prev), vspec(ident), vspec(nxt),
            pl.BlockSpec((B, n_ctx_chunks, KV_W, CHUNK), lambda p: (0, 0, 0, 0)),
            pl.BlockSpec((None, SUBLANES, group * CHUNK), lambda p: (layer, 0, 0)),
            mask_spec(0), mask_spec(1),
        ] + scan_specs(ident) + scan_specs(bwd) + [
            pl.BlockSpec((None, 4 * N_HEADS, CHUNK), lambda p: (layer, 0, 0))],
        out_specs=[_pair_spec(B, GROUP_W), _pair_spec(B, GROUP_W),
                   _pair_spec(B, GROUP_W, ident), _pair_spec(B, GROUP_W, bwd)],
        out_shape=[jax.ShapeDtypeStruct((B, n_chunks, GROUP_W, CHUNK), BF16)] * 2
                  + [jax.ShapeDtypeStruct((B, n_chunks, GROUP_W, CHUNK), F32)] * 2,
        scratch_shapes=[
            pltpu.VMEM((n_slots, ST_ROWS, LANES), F32),
            pltpu.VMEM((n_slots, 1, LANES), F32),
            pltpu.VMEM((CPS * KV_HEADS, n_keys, group * CHUNK), F32),
            pltpu.VMEM((CPS * KV_HEADS, n_keys, group * CHUNK), BF16),
        ],
        compiler_params=pltpu.CompilerParams(dimension_semantics=("arbitrary",)),
        name="mixers",
    )(zs, sgu_wt, sgu_b, aq, ak, ak, ak, ak, av, av, av, av, sink_rows, masks, masks, qk, vo, gt, qk, vo, gt,
      gate_bias)


N_OUT_SCRATCH = 6


def _hidden_chunks(hidden):
    step = 3 * 2 * LANES
    return [(lo, min(lo + step, hidden)) for lo in range(0, hidden, step)]


def _out_kernel(*refs, hidden, npb, n_tiles, split_ctx):
    _zero_at_first_step(refs[-N_OUT_SCRATCH:-1])
    ctx_ref = refs[0] if split_ctx else None
    (x_ref, a_ref, hf_ref, hb_ref, zo_ref, c_ref, d_ref, moda_ref, modc_ref, g_ref, mg_ref,
     wo_ref, wi_ref, wf_ref, o_ref, x1_ref, y_ref, f_ref, tb_ref, act_ref, bm_ref) = refs[1:] if split_ctx else refs
    s = pl.program_id(0)
    f_ref[s % 2] = _dot(act_ref[...], wf_ref[...])

    x_tile = (_first_tile_from_ctx(ctx_ref, x_ref, jnp.clip(s - 1, 0, n_tiles - 1) % npb)
              if split_ctx else x_ref[...])
    x1 = x_tile + _rms(y_ref[...], moda_ref[2:3, :] * g_ref[1:2, :])
    x1_ref[s % 3] = x1
    tb_ref[...] = (_rms(x1, g_ref[2:3, :] * (1.0 + moda_ref[4:5, :])) + moda_ref[3:4, :]).astype(BF16)

    o_ref[...] = x1_ref[(s + 1) % 3] + _rms(f_ref[(s + 1) % 2], modc_ref[5:6, :] * g_ref[3:4, :])

    for lo, hi in _hidden_chunks(hidden):
        gate = _dot(tb_ref[...], wi_ref[:, lo:hi])
        up = _dot(tb_ref[...], wi_ref[:, hidden + lo:hidden + hi])
        act_ref[:, lo:hi] = (_silu(gate) * up).astype(BF16)

    hs = _get_chunks(hf_ref) + _get_chunks(hb_ref)
    hn = jnp.concatenate([_head_layer_norm_t(hs[h * HEAD_DIM:(h + 1) * HEAD_DIM]) for h in range(N_HEADS)], axis=0)
    bm_ref[...] = (hn * mg_ref[...] * jax.nn.sigmoid(_get_chunks(zo_ref).astype(F32))).astype(BF16)
    y_ref[...] = (_dot_tn(_get_chunks(a_ref), wo_ref[0:GROUP_W, :])
                  + _dot_tn(bm_ref[...], wo_ref[GROUP_W:2 * GROUP_W, :])
                  + _dot_tn(_get_chunks(c_ref), wo_ref[2 * GROUP_W:3 * GROUP_W, :])
                  + _dot(d_ref[...], wo_ref[3 * GROUP_W:4 * GROUP_W, :]))


def _out_ffn(xs, a, hf, hb, vo, c, d, mods, norm_g, mnorm_g, w_out, w_ffn_in, w_ffn_out, layer, n_batch_rows,
             drop_ctx):
    B, S, D = _stream_shape(xs)
    tm = ROW_TILE
    npb = S // tm
    n_tiles = B * npb
    hidden = w_ffn_out.shape[1]
    cur = lambda s: jnp.minimum(s, n_tiles - 1)
    mid = lambda s: jnp.clip(s - 1, 0, n_tiles - 1)
    prv = lambda s: jnp.clip(s - 3, 0, n_tiles - 1)
    row = lambda s: (cur(s) // npb, cur(s) % npb, 0)
    mod_row = lambda t: (layer, jnp.where(t % npb == 0, n_batch_rows, t // npb), 0, 0)
    c2 = lambda s: (0, 0)
    this_layer = lambda s: (layer, 0, 0)
    cpt = tm // CHUNK
    fm = pl.BlockSpec((None, cpt, GROUP_W, CHUNK), lambda s: (cur(s) // npb, cur(s) % npb, 0, 0))
    split_ctx = isinstance(xs, tuple)
    x_specs, x_args = _stream_specs(xs, tm, mid, npb)
    in_specs = x_specs + [
        fm, fm, fm,
        pl.BlockSpec((None, cpt, GROUP_W, CHUNK), lambda s: (cur(s) // npb, cur(s) % npb, 1, 0)),
        fm,
        pl.BlockSpec((None, tm, GROUP_W), row),
        pl.BlockSpec((None, None, 6, D), lambda s: mod_row(mid(s))),
        pl.BlockSpec((None, None, 6, D), lambda s: mod_row(prv(s))),
        pl.BlockSpec((None, 4, D), this_layer),
        pl.BlockSpec((None, GROUP_W, tm), this_layer),
        pl.BlockSpec((None, D, D), this_layer, pipeline_mode=pl.Buffered(1)),
        pl.BlockSpec((None, D, 2 * hidden), this_layer, pipeline_mode=pl.Buffered(1)),
        pl.BlockSpec((None, hidden, D), this_layer, pipeline_mode=pl.Buffered(1)),
    ]
    args = x_args + [a, hf, hb, vo, c, d, mods, mods, norm_g, mnorm_g, w_out, w_ffn_in, w_ffn_out]
    scratch = [pltpu.VMEM((3, tm, D), F32), pltpu.VMEM((tm, D), F32), pltpu.VMEM((2, tm, D), F32),
               pltpu.VMEM((tm, D), BF16), pltpu.VMEM((tm, hidden), BF16), pltpu.VMEM((GROUP_W, tm), BF16)]
    params = pltpu.CompilerParams(dimension_semantics=("arbitrary",), vmem_limit_bytes=VMEM_LIMIT)
    static = dict(hidden=hidden, npb=npb, n_tiles=n_tiles, split_ctx=split_ctx)
    if drop_ctx:
        out_spec = pl.BlockSpec((None, tm, D), lambda s: (prv(s) // npb, jnp.maximum(prv(s) % npb - 1, 0), 0))
        out_shape = jax.ShapeDtypeStruct((B, S - tm, D), F32)
    else:
        out_spec = pl.BlockSpec((None, tm, D), lambda s: (prv(s) // npb, prv(s) % npb, 0))
        out_shape = jax.ShapeDtypeStruct((B, S, D), F32)
    return pl.pallas_call(
        functools.partial(_out_kernel, **static),
        grid=(n_tiles + 3,),
        in_specs=in_specs,
        out_specs=out_spec,
        out_shape=out_shape,
        scratch_shapes=scratch, compiler_params=params, name="out_ffn",
    )(*args)


def _rope_tables(n_tokens, ctx_len, tm):
    rows = n_tokens // GRID_W
    axis_freq = HEAD_DIM // 4
    inv = jnp.power(ROPE_BASE, -jnp.arange(axis_freq, dtype=F32) * 2.0 / (2 * axis_freq))
    ar = jnp.arange(rows, dtype=F32)[:, None] * inv
    ac = jnp.arange(GRID_W, dtype=F32)[:, None] * inv

    def per_token(fn):
        r = jnp.broadcast_to(fn(ar)[:, None, :], (rows, GRID_W, axis_freq))
        c = jnp.broadcast_to(fn(ac)[None, :, :], (rows, GRID_W, axis_freq))
        return jnp.concatenate([r, r, c, c], axis=-1).reshape(n_tokens, HEAD_DIM)

    cos, sin = per_token(jnp.cos), per_token(jnp.sin)
    sign = jnp.where((jnp.arange(HEAD_DIM) % 32) < 16, -1.0, 1.0).astype(F32)
    cos = jnp.concatenate([jnp.ones((ctx_len, HEAD_DIM), F32), cos], axis=0)
    sin = jnp.concatenate([jnp.zeros((ctx_len, HEAD_DIM), F32), sin * sign], axis=0)
    t = jnp.arange(tm)[:, None]
    half = jnp.asarray([w // 2 for w in POOL_WINDOWS])[None, :]
    inv_cnt = []
    for last in (False, True):
        for first in (False, True):
            lo = jnp.maximum(t - half, 0) if first else t - half
            hi = jnp.minimum(t + half, tm) if last else t + half
            inv_cnt.append(jnp.repeat(1.0 / (hi - lo).astype(F32), HEAD_DIM, axis=1))
    q_scale = HEAD_DIM ** -0.5 * LOG2E
    return (jnp.tile(cos, (1, KV_HEADS)), jnp.tile(sin, (1, KV_HEADS)), cos.T * q_scale, sin.T * q_scale,
            jnp.stack(inv_cnt))


def _arrange_w_in(w_in):
    depth = w_in.shape[0]
    ml0 = 2 * GROUP_W
    gate0 = ml0 + 4 * GROUP_W
    att0 = gate0 + 4 * N_HEADS
    pool0 = att0 + GROUP_W + 2 * KV_W
    w_t = jnp.swapaxes(w_in, 1, 2)
    gates = w_t[:, gate0:att0].reshape(depth, 2, 2, N_HEADS, -1)
    gates = gates.transpose(0, 2, 1, 3, 4).reshape(depth, 4 * N_HEADS, -1)
    feat = jnp.concatenate([
        w_t[:, 0:gate0],
        w_t[:, att0:att0 + GROUP_W],
        w_t[:, att0 + GROUP_W + KV_W:pool0],
        gates,
    ], axis=1)
    tok = jnp.concatenate([w_t[:, att0 + GROUP_W:att0 + GROUP_W + KV_W], w_t[:, pool0:pool0 + GROUP_W]], axis=1)
    return feat.astype(BF16), jnp.swapaxes(tok.astype(BF16), 1, 2)


def _block_diag(w):
    g = w.shape[0]
    eye = jnp.eye(g, dtype=w.dtype)
    return (eye[:, None, :, None] * w[:, :, None, :]).reshape(g * w.shape[1], g * w.shape[2])


def kernel(x, c, ctx, c_ctx, w_mod, b_mod, norm_g, w_in, w_out, sgu_w, sgu_b, mlstm_conv_w, mlstm_gate_b,
           mlstm_norm_g, attn_sink, pool_w, pool_scale, w_ffn_in, w_ffn_out):
    B, N, D = x.shape
    ctx_len = ctx.shape[1]
    depth = w_mod.shape[0]
    assert D == 4 * GROUP_W and N % ROW_TILE == 0 and ctx_len == ROW_TILE
    assert w_in.shape[2] == 2 * GROUP_W + 4 * GROUP_W + 4 * N_HEADS + GROUP_W + 2 * KV_W + GROUP_W
    n_ctx_chunks = ctx_len // CHUNK

    xs = (ctx, x)
    rows = -(-(B + 1) // SUBLANES) * SUBLANES
    cc = jnp.zeros((rows, D), F32).at[0:B].set(c).at[B].set(c_ctx)
    mods = _modulation(cc, w_mod, b_mod).reshape(depth, rows, 6, D)
    tables = _rope_tables(N, ctx_len, ROW_TILE)
    masks = _attention_masks(n_ctx_chunks)

    wt, wk = _arrange_w_in(w_in)
    w_out_b, w_ffn_in_b, w_ffn_out_b = w_out.astype(BF16), w_ffn_in.astype(BF16), w_ffn_out.astype(BF16)

    group = N_HEADS // KV_HEADS
    conv_b = jnp.broadcast_to(mlstm_conv_w[:, :, :, None], (depth, 3, 2 * GROUP_W, ROW_TILE))
    proj = (wt, wk, tables, conv_b, jax.vmap(_block_diag)(pool_w).astype(BF16), pool_scale.reshape(depth, 1, GROUP_W))
    gate_bias = jnp.broadcast_to(
        mlstm_gate_b.reshape(depth, 2, 2, N_HEADS).transpose(0, 2, 1, 3).reshape(depth, 4 * N_HEADS, 1),
        (depth, 4 * N_HEADS, CHUNK))
    sink_rows = jnp.zeros((depth, SUBLANES, group * CHUNK), F32).at[:, 0:KV_HEADS].set(
        jnp.repeat(attn_sink.reshape(depth, KV_HEADS, group) * LOG2E, CHUNK, axis=2))
    mnorm_b = jnp.broadcast_to(mlstm_norm_g[:, :, None], (depth, GROUP_W, ROW_TILE))
    sgu_wt = sgu_w.transpose(0, 1, 3, 2).astype(BF16)

    for l in range(depth):
        zs, qk, vo, gt, aq, av, ak, dm = _in_proj(xs, mods, norm_g[:, 0:1], proj, l, B)
        a, cm, hf, hb = _mixers(zs, sgu_wt, sgu_b, aq, ak, av, sink_rows, masks, qk, vo, gt, gate_bias, l,
                                n_ctx_chunks)
        xs = _out_ffn(xs, a, hf, hb, vo, cm, dm, mods, norm_g, mnorm_b, w_out_b, w_ffn_in_b, w_ffn_out_b, l, B,
                      drop_ctx=(l == depth - 1))
    return xs
```
